```python
import jax, jax.numpy as jnp
from jax import lax
import numpy as np

D_MODEL = 1024
BATCH = 8
SEQ = 2048
DEPTH = 1

GLA_HEADS = 4
GLA_DK = 64
GLA_DV = 128
GLA_GATE_RANK = 16
GLA_GATE_NORMALIZER = 16.0
GLA_CHUNK = 64
MLA_HEADS = 8
MLA_NOPE = 64
MLA_ROPE = 32
MLA_QK = MLA_NOPE + MLA_ROPE
MLA_V = 64
MLA_Q_RANK = 256
MLA_KV_RANK = 128
ROPE_BASE = 10000.0
Q_BLOCK = 128
GLA_WIDTH = GLA_HEADS * GLA_DV
MLA_WIDTH = MLA_HEADS * MLA_V
MIX_WIDTH = GLA_WIDTH + MLA_WIDTH
IN_SPLITS = (GLA_HEADS * GLA_DK, GLA_HEADS * GLA_DK, GLA_WIDTH, GLA_WIDTH,
             GLA_GATE_RANK, GLA_GATE_RANK, MLA_Q_RANK, MLA_KV_RANK, MLA_ROPE)
IN_WIDTH = sum(IN_SPLITS)
N_GROUPS = 4
EXPERTS_PER_GROUP = 8
TOP_K_IN_GROUP = 2
D_EXPERT = 256
EPS = 1e-6

kernel_name = 'hybrid_gla_mla_hmoe_encoder_block'


def rms_norm(x, gain):
    xf = x.astype(jnp.float32)
    y = xf * lax.rsqrt(jnp.mean(xf * xf, axis=-1, keepdims=True) + EPS)
    return (y * gain.astype(jnp.float32)).astype(x.dtype)


def gla_chunked(q, k, v, g, strict):
    B, H, S, DK = q.shape
    DV = v.shape[-1]
    C = GLA_CHUNK
    n = S // C
    qc = q.reshape(B, H, n, C, DK)
    kc = k.reshape(B, H, n, C, DK)
    vc = v.reshape(B, H, n, C, DV)
    b = jnp.cumsum(g.reshape(B, H, n, C, DK), axis=3)
    b_ref = b[:, :, :, C // 2 - 1:C // 2, :]
    q_rel = qc * jnp.exp(b - b_ref)
    k_rel = kc * jnp.exp(b_ref - b)
    scores = jnp.einsum('bhnid,bhnjd->bhnij', q_rel, k_rel)
    mask = jnp.tril(jnp.ones((C, C), dtype=bool), k=-1 if strict else 0)
    scores = jnp.where(mask, scores, 0.0)
    o_intra = jnp.einsum('bhnij,bhnjv->bhniv', scores, vc)
    b_last = b[:, :, :, -1:, :]
    kv_chunk = jnp.einsum('bhncd,bhncv->nbhdv', kc * jnp.exp(b_last - b), vc)
    decay_chunk = jnp.moveaxis(jnp.exp(b_last[:, :, :, 0, :]), 2, 0)

    def step(state, inp):
        dec, kv = inp
        return dec[..., None] * state + kv, state

    state0 = jnp.zeros((B, H, DK, DV), dtype=q.dtype)
    _, states = lax.scan(step, state0, (decay_chunk, kv_chunk))
    o_inter = jnp.einsum('bhnid,nbhdv->bhniv', qc * jnp.exp(b), states)
    return (o_intra + o_inter).reshape(B, H, S, DV)


def rope_tables(positions):
    inv_freq = ROPE_BASE ** (-jnp.arange(0, MLA_ROPE, 2, dtype=jnp.float32) / MLA_ROPE)
    ang = positions.astype(jnp.float32)[..., None] * inv_freq
    return jnp.cos(ang)[:, :, None, :], jnp.sin(ang)[:, :, None, :]


def apply_rope(t, cos, sin):
    t1, t2 = jnp.split(t, 2, axis=-1)
    return jnp.concatenate([t1 * cos - t2 * sin, t1 * sin + t2 * cos], axis=-1)


def dense_block_attention(q, k, v):
    B, S, H, Dq = q.shape
    Dv = v.shape[-1]
    n = S // Q_BLOCK
    qb = q.reshape(B, n, Q_BLOCK, H, Dq).transpose(1, 0, 3, 2, 4)
    kh = k.transpose(0, 2, 1, 3).astype(jnp.float32)
    vh = v.transpose(0, 2, 1, 3).astype(jnp.float32)
    scale = Dq ** -0.5

    def one_block(q_blk):
        s = jnp.einsum('bhqd,bhkd->bhqk', q_blk.astype(jnp.float32), kh) * scale
        p = jax.nn.softmax(s, axis=-1)
        return jnp.einsum('bhqk,bhkv->bhqv', p, vh)

    o = lax.map(one_block, qb)
    return o.transpose(1, 0, 3, 2, 4).reshape(B, S, H * Dv)


def hybrid_mixer(h, positions, w_in, gla_gk_fwd_w, gla_gk_fwd_b, gla_gk_bwd_w, gla_gk_bwd_b,
                 gla_out_gain, mla_q_gain, mla_w_qb, mla_kv_gain, mla_w_kvb,
                 q_norm_gain, k_norm_gain, w_out):
    B, S, _ = h.shape
    f32 = jnp.float32
    proj = h @ w_in
    split_idx = [int(i) for i in np.cumsum(IN_SPLITS)[:-1]]
    g_q, g_k, g_v, g_gate, g_lr_f, g_lr_b, m_q_a, m_kv_a, m_k_rope = jnp.split(proj, split_idx, axis=-1)

    def heads(t, d):
        return t.reshape(B, S, GLA_HEADS, d).transpose(0, 2, 1, 3).astype(f32)

    q = heads(g_q, GLA_DK) * GLA_DK ** -0.5
    k = heads(g_k, GLA_DK)
    v = heads(g_v, GLA_DV)
    lg_f = jax.nn.log_sigmoid((g_lr_f @ gla_gk_fwd_w + gla_gk_fwd_b).astype(f32)) / GLA_GATE_NORMALIZER
    lg_b = jax.nn.log_sigmoid((g_lr_b @ gla_gk_bwd_w + gla_gk_bwd_b).astype(f32)) / GLA_GATE_NORMALIZER
    g_f = heads(lg_f, GLA_DK)
    g_b = heads(lg_b, GLA_DK)
    flip = lambda t: t[:, :, ::-1]
    o_fwd = gla_chunked(q, k, v, g_f, False)
    o_bwd = flip(gla_chunked(flip(q), flip(k), flip(v), flip(g_b), True))
    o_gla = (o_fwd + o_bwd).transpose(0, 2, 1, 3)
    o_gla = rms_norm(o_gla, gla_out_gain) * jax.nn.silu(g_gate.reshape(B, S, GLA_HEADS, GLA_DV).astype(f32))
    gla_out = o_gla.reshape(B, S, GLA_WIDTH).astype(h.dtype)

    mq = (rms_norm(m_q_a, mla_q_gain) @ mla_w_qb).reshape(B, S, MLA_HEADS, MLA_QK)
    mkv = (rms_norm(m_kv_a, mla_kv_gain) @ mla_w_kvb).reshape(B, S, MLA_HEADS, MLA_NOPE + MLA_V)
    mk_nope, mv = mkv[..., :MLA_NOPE], mkv[..., MLA_NOPE:]
    mk = jnp.concatenate([mk_nope, jnp.broadcast_to(m_k_rope[:, :, None, :], (B, S, MLA_HEADS, MLA_ROPE))], axis=-1)
    mq = rms_norm(mq, q_norm_gain).astype(f32)
    mk = rms_norm(mk, k_norm_gain).astype(f32)
    cos, sin = rope_tables(positions)
    mq = jnp.concatenate([mq[..., :MLA_NOPE], apply_rope(mq[..., MLA_NOPE:], cos, sin)], axis=-1)
    mk = jnp.concatenate([mk[..., :MLA_NOPE], apply_rope(mk[..., MLA_NOPE:], cos, sin)], axis=-1)
    mla_out = dense_block_attention(mq, mk, mv).astype(h.dtype)

    return jnp.concatenate([gla_out, mla_out], axis=-1) @ w_out


def hierarchical_moe(h, w_router_group, b_router_group, w_router_expert, b_router_expert,
                     w_expert_gate, w_expert_up, w_expert_down):
    B, S, D = h.shape
    f32 = jnp.float32
    t = h.reshape(B * S, D)
    p_group = jax.nn.softmax((t @ w_router_group).astype(f32) + b_router_group.astype(f32), axis=-1)
    g_idx = jnp.argmax(p_group, axis=-1)
    g_onehot = jax.nn.one_hot(g_idx, N_GROUPS, dtype=f32)
    p_top_group = jnp.sum(p_group * g_onehot, axis=-1)
    logits_e = ((t @ w_router_expert).astype(f32) + b_router_expert.astype(f32)).reshape(-1, N_GROUPS, EXPERTS_PER_GROUP)
    logits_sel = jnp.take_along_axis(logits_e, g_idx[:, None, None], axis=1)[:, 0]
    p_e = jax.nn.softmax(logits_sel, axis=-1)
    top_v, top_i = lax.top_k(p_e, TOP_K_IN_GROUP)
    top_v = top_v / jnp.sum(top_v, axis=-1, keepdims=True)
    w_in_group = jnp.sum(jax.nn.one_hot(top_i, EXPERTS_PER_GROUP, dtype=f32) * top_v[..., None], axis=1)
    gate = g_onehot[:, :, None] * (p_top_group[:, None] * w_in_group)[:, None, :]
    out = jnp.zeros_like(t)
    for gi in range(N_GROUPS):
        hid = jax.nn.silu(jnp.einsum('td,edf->tef', t, w_expert_gate[gi])) * jnp.einsum('td,edf->tef', t, w_expert_up[gi])
        hid = hid * gate[:, gi, :, None].astype(t.dtype)
        out = out + jnp.einsum('tef,efd->td', hid, w_expert_down[gi])
    return out.reshape(B, S, D)


def setup_inputs(seed: int = 0) -> dict:
    key = jax.random.key(seed)
    ks = jax.random.split(key, 24)
    L = DEPTH
    f32 = jnp.float32

    def nrm(k, shape, fan_in):
        return jax.random.normal(k, shape, f32) * fan_in ** -0.5

    def gain(k, shape):
        return 1.0 + 0.05 * jax.random.normal(k, shape, f32)

    def bias(k, shape):
        return 0.01 * jax.random.normal(k, shape, f32)

    return {
        'x': jax.random.normal(ks[0], (BATCH, SEQ, D_MODEL), f32),
        'positions': jnp.broadcast_to(jnp.arange(SEQ, dtype=jnp.int32)[None, :], (BATCH, SEQ)),
        'norm1_gain': gain(ks[1], (L, D_MODEL)),
        'w_in': nrm(ks[2], (L, D_MODEL, IN_WIDTH), D_MODEL),
        'gla_gk_fwd_w': nrm(ks[3], (L, GLA_GATE_RANK, GLA_HEADS * GLA_DK), GLA_GATE_RANK),
        'gla_gk_fwd_b': bias(ks[4], (L, GLA_HEADS * GLA_DK)),
        'gla_gk_bwd_w': nrm(ks[5], (L, GLA_GATE_RANK, GLA_HEADS * GLA_DK), GLA_GATE_RANK),
        'gla_gk_bwd_b': bias(ks[6], (L, GLA_HEADS * GLA_DK)),
        'gla_out_gain': gain(ks[7], (L, GLA_DV)),
        'mla_q_gain': gain(ks[8], (L, MLA_Q_RANK)),
        'mla_w_qb': nrm(ks[9], (L, MLA_Q_RANK, MLA_HEADS * MLA_QK), MLA_Q_RANK),
        'mla_kv_gain': gain(ks[10], (L, MLA_KV_RANK)),
        'mla_w_kvb': nrm(ks[11], (L, MLA_KV_RANK, MLA_HEADS * (MLA_NOPE + MLA_V)), MLA_KV_RANK),
        'q_norm_gain': gain(ks[12], (L, MLA_QK)),
        'k_norm_gain': gain(ks[13], (L, MLA_QK)),
        'w_out': nrm(ks[14], (L, MIX_WIDTH, D_MODEL), MIX_WIDTH),
        'norm2_gain': gain(ks[15], (L, D_MODEL)),
        'w_router_group': nrm(ks[16], (L, D_MODEL, N_GROUPS), D_MODEL),
        'b_router_group': bias(ks[17], (L, N_GROUPS)),
        'w_router_expert': nrm(ks[18], (L, D_MODEL, N_GROUPS * EXPERTS_PER_GROUP), D_MODEL),
        'b_router_expert': bias(ks[19], (L, N_GROUPS * EXPERTS_PER_GROUP)),
        'w_expert_gate': nrm(ks[20], (L, N_GROUPS, EXPERTS_PER_GROUP, D_MODEL, D_EXPERT), D_MODEL),
        'w_expert_up': nrm(ks[21], (L, N_GROUPS, EXPERTS_PER_GROUP, D_MODEL, D_EXPERT), D_MODEL),
        'w_expert_down': nrm(ks[22], (L, N_GROUPS, EXPERTS_PER_GROUP, D_EXPERT, D_MODEL), D_EXPERT),
    }


def reference(x, positions, norm1_gain, w_in, gla_gk_fwd_w, gla_gk_fwd_b, gla_gk_bwd_w, gla_gk_bwd_b,
              gla_out_gain, mla_q_gain, mla_w_qb, mla_kv_gain, mla_w_kvb, q_norm_gain, k_norm_gain,
              w_out, norm2_gain, w_router_group, b_router_group, w_router_expert, b_router_expert,
              w_expert_gate, w_expert_up, w_expert_down):
    for l in range(DEPTH):
        h = rms_norm(x, norm1_gain[l])
        x = x + hybrid_mixer(h, positions, w_in[l], gla_gk_fwd_w[l], gla_gk_fwd_b[l], gla_gk_bwd_w[l],
                             gla_gk_bwd_b[l], gla_out_gain[l], mla_q_gain[l], mla_w_qb[l], mla_kv_gain[l],
                             mla_w_kvb[l], q_norm_gain[l], k_norm_gain[l], w_out[l])
        h = rms_norm(x, norm2_gain[l])
        x = x + hierarchical_moe(h, w_router_group[l], b_router_group[l], w_router_expert[l],
                                 b_router_expert[l], w_expert_gate[l], w_expert_up[l], w_expert_down[l])
    return x
```

```python
import functools

import numpy as np
import jax
import jax.numpy as jnp
from jax import lax
from jax.experimental import pallas as pl
from jax.experimental.pallas import tpu as pltpu

F32 = jnp.float32
BF16 = jnp.bfloat16

D_MODEL = 1024
GLA_HEADS = 4
GLA_DK = 64
GLA_DV = 128
GLA_GATE_RANK = 16
GLA_GATE_NORMALIZER = 16.0
GLA_CHUNK = 64
MLA_HEADS = 8
MLA_NOPE = 64
MLA_ROPE = 32
MLA_QK = MLA_NOPE + MLA_ROPE
MLA_V = 64
MLA_Q_RANK = 256
MLA_KV_RANK = 128
ROPE_BASE = 10000.0
GLA_QK_WIDTH = GLA_HEADS * GLA_DK
GLA_WIDTH = GLA_HEADS * GLA_DV
MLA_WIDTH = MLA_HEADS * MLA_V
N_GROUPS = 4
EXPERTS_PER_GROUP = 8
N_EXPERTS = N_GROUPS * EXPERTS_PER_GROUP
D_EXPERT = 256
EPS = 1e-6

LANES = 128
HEAD_TILE = LANES
ROPE_LO = MLA_NOPE
ROPE_HALF = MLA_ROPE // 2

COL_GQ = 0
COL_GK = COL_GQ + GLA_QK_WIDTH
COL_GV = COL_GK + GLA_QK_WIDTH
COL_GG = COL_GV + GLA_WIDTH
COL_MQ = COL_GG + GLA_WIDTH
COL_MKV = COL_MQ + MLA_Q_RANK
COL_MISC = COL_MKV + MLA_KV_RANK
PROJ_WIDTH = COL_MISC + LANES
MISC_LRB = GLA_GATE_RANK

ROUTER_ROWS = LANES
ROUTER_EXPERT_ROW = 8

VMEM_LIMIT = 56 * 1024 * 1024


def _dot(a, b):
    return jnp.dot(a, b, preferred_element_type=F32)


def _dot_nt(a, b):
    return lax.dot_general(a, b, (((1,), (1,)), ((), ())), preferred_element_type=F32)


def _dot_tn(a, b):
    return lax.dot_general(a, b, (((0,), (0,)), ((), ())), preferred_element_type=F32)


def _split_bf16(x):
    hi = x.astype(BF16)
    lo = (x - hi.astype(F32)).astype(BF16)
    return hi, lo


def _proj_kernel(x_ref, pos_ref, n1_ref, win_ref, wg_ref, bg_ref, qgain_ref, wqb_ref, kvgain_ref,
                 wkvk_ref, wkvv_ref, qng_ref, kng_ref, freq_ref,
                 gq_ref, gk_ref, gv_ref, gg_ref, lg_ref, mq_ref, mk_ref, mv_ref):
    x = x_ref[...]
    h = x * lax.rsqrt(jnp.mean(x * x, axis=-1, keepdims=True) + EPS) * n1_ref[...]
    proj = _dot(h.astype(BF16), win_ref[...])

    gq_ref[...] = (proj[:, COL_GQ:COL_GK] * GLA_DK ** -0.5).astype(BF16)
    gk_ref[...] = proj[:, COL_GK:COL_GV].astype(BF16)
    gv_ref[...] = proj[:, COL_GV:COL_GG].astype(BF16)
    gg_ref[...] = proj[:, COL_GG:COL_MQ].astype(BF16)

    misc = proj[:, COL_MISC:PROJ_WIDTH]
    z = _dot(misc.astype(BF16), wg_ref[...]) + bg_ref[...]
    log_sig = -(jnp.maximum(-z, 0.0) + jnp.log1p(jnp.exp(-jnp.abs(z))))
    lg_ref[...] = log_sig / GLA_GATE_NORMALIZER

    lane = lax.broadcasted_iota(jnp.int32, (1, LANES), 1)
    ang = pos_ref[...].astype(F32) * freq_ref[...]
    cos = jnp.cos(ang)
    sin = jnp.sin(ang)
    in_rope = (lane >= ROPE_LO) & (lane < ROPE_LO + MLA_ROPE)
    c_tab = jnp.where(lane < ROPE_LO, 1.0, jnp.where(in_rope, cos, 0.0))
    s_up = jnp.where((lane >= ROPE_LO + ROPE_HALF) & (lane < ROPE_LO + MLA_ROPE), sin, 0.0)
    s_dn = jnp.where((lane >= ROPE_LO) & (lane < ROPE_LO + ROPE_HALF), -sin, 0.0)

    def norm_rope(t, gain):
        ss = jnp.sum(t * t, axis=-1, keepdims=True) * (1.0 / MLA_QK)
        tn = t * lax.rsqrt(ss + EPS) * gain
        return (tn * c_tab + pltpu.roll(tn, ROPE_HALF, 1) * s_up
                + pltpu.roll(tn, LANES - ROPE_HALF, 1) * s_dn)

    qa = proj[:, COL_MQ:COL_MKV]
    qn = qa * lax.rsqrt(jnp.mean(qa * qa, axis=-1, keepdims=True) + EPS) * qgain_ref[...]
    mq = _dot(qn.astype(BF16), wqb_ref[...])
    kva = proj[:, COL_MKV:COL_MISC]
    kvn = (kva * lax.rsqrt(jnp.mean(kva * kva, axis=-1, keepdims=True) + EPS) * kvgain_ref[...]).astype(BF16)
    kn = _dot(kvn, wkvk_ref[...])
    mv_ref[...] = _dot(kvn, wkvv_ref[...]).astype(BF16)
    rope_tile = jnp.where(in_rope, misc, 0.0)
    qng = qng_ref[...]
    kng = kng_ref[...]
    scale = MLA_QK ** -0.5
    for hd in range(MLA_HEADS):
        sl = slice(hd * HEAD_TILE, (hd + 1) * HEAD_TILE)
        mq_ref[0, hd] = (norm_rope(mq[:, sl], qng) * scale).astype(BF16)
        mk_ref[0, hd] = norm_rope(kn[:, sl] + rope_tile, kng).astype(BF16)


def _gla_kernel(q_ref, k_ref, v_ref, lg_ref, gate_ref, gain_ref, o_ref, acc_ref, st_ref):
    seq = q_ref.shape[0]
    C = GLA_CHUNK
    n_chunks = seq // C
    HK = GLA_QK_WIDTH

    row = lax.broadcasted_iota(jnp.int32, (C, C), 0)
    col = lax.broadcasted_iota(jnp.int32, (C, C), 1)
    tri_prefix = (col <= row).astype(BF16)
    tri_suffix = (col >= row).astype(BF16)
    lane_head = lax.broadcasted_iota(jnp.int32, (1, HK), 1) // GLA_DK
    srow = lax.broadcasted_iota(jnp.int32, (GLA_HEADS * C, C), 0) % C
    scol = lax.broadcasted_iota(jnp.int32, (GLA_HEADS * C, C), 1)
    gain = gain_ref[...]

    def stack_heads(t):
        return jnp.concatenate([jnp.where(lane_head == hd, t, 0.0) for hd in range(GLA_HEADS)], axis=0)

    def chunk_step(c, tri, ref_i, last_i, keep, lg_off, finish):
        r0 = pl.multiple_of(c * C, C)
        rows = pl.ds(r0, C)
        g = lg_ref[rows, lg_off:lg_off + HK]
        g_hi, g_lo = _split_bf16(g)
        b = _dot(tri, g_hi) + _dot(tri, g_lo)
        b_ref = b[ref_i:ref_i + 1, :]
        b_last = b[last_i:last_i + 1, :]
        q = q_ref[rows, :].astype(F32)
        k = k_ref[rows, :].astype(F32)
        v = v_ref[rows, :]
        q_rel = q * jnp.exp(b - b_ref)
        k_rel = (k * jnp.exp(b_ref - b)).astype(BF16)
        k_dec = (k * jnp.exp(b_last - b)).astype(BF16)
        q_dec = q * jnp.exp(b)
        st = st_ref[...]
        sc = _dot_nt(stack_heads(q_rel).astype(BF16), k_rel)
        sc = jnp.where(keep, sc, 0.0).astype(BF16)
        o_inter = _dot_nt(stack_heads(q_dec).astype(BF16), st.astype(BF16))
        kv_t = _dot_tn(v, k_dec)
        new_st = jnp.exp(b_last) * st
        for hd in range(GLA_HEADS):
            vs = slice(hd * GLA_DV, (hd + 1) * GLA_DV)
            rs = slice(hd * C, (hd + 1) * C)
            o_h = _dot(sc[rs, :], v[:, vs]) + o_inter[rs, :]
            new_st = new_st + jnp.where(lane_head == hd, kv_t[vs, :], 0.0)
            finish(rows, vs, o_h)
        st_ref[...] = new_st

    def store_fwd(rows, vs, o_h):
        acc_ref[rows, vs] = o_h

    def store_final(rows, vs, o_h):
        o = acc_ref[rows, vs] + o_h
        on = o * lax.rsqrt(jnp.mean(o * o, axis=-1, keepdims=True) + EPS) * gain
        gt = gate_ref[rows, vs].astype(F32)
        o_ref[rows, vs] = (on * (gt * jax.nn.sigmoid(gt))).astype(BF16)

    st_ref[...] = jnp.zeros_like(st_ref)

    def fwd_body(c, carry):
        chunk_step(c, tri_prefix, C // 2 - 1, C - 1, scol <= srow, 0, store_fwd)
        return carry

    lax.fori_loop(0, n_chunks, fwd_body, 0)

    st_ref[...] = jnp.zeros_like(st_ref)

    def bwd_body(i, carry):
        chunk_step(n_chunks - 1 - i, tri_suffix, C // 2, 0, scol > srow, HK, store_final)
        return carry

    lax.fori_loop(0, n_chunks, bwd_body, 0)


def _attn_kernel(q_ref, k_ref, v_ref, o_ref):
    outs = []
    for j in range(2):
        s = _dot_nt(q_ref[0, j], k_ref[0, j])
        m = jnp.max(s, axis=-1, keepdims=True)
        p = jnp.exp(s - m)
        l = jnp.sum(p, axis=-1, keepdims=True)
        outs.append(_dot(p.astype(BF16), v_ref[...]) / l)
    lane = lax.broadcasted_iota(jnp.int32, (1, LANES), 1)
    o_ref[...] = jnp.where(lane < MLA_V, outs[0], outs[1]).astype(BF16)


def _outproj_kernel(x_ref, gla_ref, mla_ref, wo_ref, n2_ref, wr_hi_ref, wr_lo_ref, br_ref,
                    x1_ref, h2_ref, route_ref):
    mix = _dot(gla_ref[...], wo_ref[0:GLA_WIDTH, :]) + _dot(mla_ref[...], wo_ref[GLA_WIDTH:, :])
    x1 = x_ref[...] + mix
    x1_ref[...] = x1
    h2 = x1 * lax.rsqrt(jnp.mean(x1 * x1, axis=-1, keepdims=True) + EPS) * n2_ref[...]
    h2_ref[...] = h2.astype(BF16)

    h_hi, h_lo = _split_bf16(h2)
    w_hi = wr_hi_ref[...]
    logits = (_dot_nt(w_hi, h_hi) + _dot_nt(w_hi, h_lo) + _dot_nt(wr_lo_ref[...], h_hi)) + br_ref[...]
    tm = logits.shape[1]
    gl = logits[0:N_GROUPS, :]
    ge = jnp.exp(gl - jnp.max(gl, axis=0, keepdims=True))
    pg = ge / jnp.sum(ge, axis=0, keepdims=True)
    p_top = jnp.max(pg, axis=0, keepdims=True)
    gi = lax.broadcasted_iota(jnp.int32, (N_GROUPS, tm), 0)
    g_idx = jnp.min(jnp.where(pg == p_top, gi, N_GROUPS), axis=0, keepdims=True)
    sel = jnp.zeros((EXPERTS_PER_GROUP, tm), F32)
    for g in range(N_GROUPS):
        r0 = ROUTER_EXPERT_ROW + g * EXPERTS_PER_GROUP
        sel = sel + jnp.where(g_idx == g, logits[r0:r0 + EXPERTS_PER_GROUP, :], 0.0)
    se = jnp.exp(sel - jnp.max(sel, axis=0, keepdims=True))
    pe = se / jnp.sum(se, axis=0, keepdims=True)
    ei = lax.broadcasted_iota(jnp.int32, (EXPERTS_PER_GROUP, tm), 0)
    m1 = jnp.max(pe, axis=0, keepdims=True)
    i1 = jnp.min(jnp.where(pe == m1, ei, EXPERTS_PER_GROUP), axis=0, keepdims=True)
    pe2 = jnp.where(ei == i1, -1.0, pe)
    m2 = jnp.max(pe2, axis=0, keepdims=True)
    i2 = jnp.min(jnp.where(pe2 == m2, ei, EXPERTS_PER_GROUP), axis=0, keepdims=True)
    den = m1 + m2
    e1 = (g_idx * EXPERTS_PER_GROUP + i1).astype(F32)
    e2 = (g_idx * EXPERTS_PER_GROUP + i2).astype(F32)
    zeros = jnp.zeros((4, tm), F32)
    route_ref[...] = jnp.concatenate([e1, e2, p_top * (m1 / den), p_top * (m2 / den), zeros], axis=0)


def _moe_kernel(h_ref, x1_ref, gate_ref, wg_ref, wu_ref, wd_ref, o_ref):
    e = pl.program_id(1)

    @pl.when(e == 0)
    def _():
        o_ref[...] = x1_ref[...]

    h = h_ref[...]
    a = _dot(h, wg_ref[0])
    u = _dot(h, wu_ref[0])
    hid = (a * jax.nn.sigmoid(a)) * u * gate_ref[0]
    o_ref[...] += _dot(hid.astype(BF16), wd_ref[0])


def _full(shape):
    return pl.BlockSpec(shape, lambda *_: (0,) * len(shape))


def _prep_weights(w_in, gk_fwd_w, gk_fwd_b, gk_bwd_w, gk_bwd_b, mla_w_qb, mla_w_kvb, q_norm_gain, k_norm_gain):
    splits = np.cumsum([0, GLA_QK_WIDTH, GLA_QK_WIDTH, GLA_WIDTH, GLA_WIDTH, GLA_GATE_RANK, GLA_GATE_RANK,
                        MLA_Q_RANK, MLA_KV_RANK, MLA_ROPE])
    seg = [w_in[:, splits[i]:splits[i + 1]] for i in range(9)]
    d = w_in.shape[0]
    misc = jnp.zeros((d, LANES), F32)
    misc = misc.at[:, 0:GLA_GATE_RANK].set(seg[4])
    misc = misc.at[:, MISC_LRB:MISC_LRB + GLA_GATE_RANK].set(seg[5])
    misc = misc.at[:, ROPE_LO:ROPE_LO + MLA_ROPE].set(seg[8])
    win = jnp.concatenate([seg[0], seg[1], seg[2], seg[3], seg[6], seg[7], misc], axis=1).astype(BF16)

    wg = jnp.zeros((LANES, 2 * GLA_QK_WIDTH), F32)
    wg = wg.at[0:GLA_GATE_RANK, 0:GLA_QK_WIDTH].set(gk_fwd_w)
    wg = wg.at[MISC_LRB:MISC_LRB + GLA_GATE_RANK, GLA_QK_WIDTH:].set(gk_bwd_w)
    bg = jnp.concatenate([gk_fwd_b, gk_bwd_b])[None, :]

    pad = HEAD_TILE - MLA_QK
    wqb = jnp.pad(mla_w_qb.reshape(MLA_Q_RANK, MLA_HEADS, MLA_QK), ((0, 0), (0, 0), (0, pad)))
    wqb = wqb.reshape(MLA_Q_RANK, MLA_HEADS * HEAD_TILE).astype(BF16)
    wkv = mla_w_kvb.reshape(MLA_KV_RANK, MLA_HEADS, MLA_NOPE + MLA_V)
    wkvk = jnp.pad(wkv[:, :, :MLA_NOPE], ((0, 0), (0, 0), (0, HEAD_TILE - MLA_NOPE)))
    wkvk = wkvk.reshape(MLA_KV_RANK, MLA_HEADS * HEAD_TILE).astype(BF16)
    wkvv = wkv[:, :, MLA_NOPE:].reshape(MLA_KV_RANK, MLA_WIDTH).astype(BF16)
    qng = jnp.pad(q_norm_gain, (0, pad))[None, :]
    kng = jnp.pad(k_norm_gain, (0, pad))[None, :]
    return win, wg.astype(BF16), bg, wqb, wkvk, wkvv, qng, kng


def _rope_freq():
    inv = ROPE_BASE ** (-np.arange(0, MLA_ROPE, 2, dtype=np.float32) / MLA_ROPE)
    tab = np.zeros((1, LANES), np.float32)
    tab[0, ROPE_LO:ROPE_LO + ROPE_HALF] = inv
    tab[0, ROPE_LO + ROPE_HALF:ROPE_LO + MLA_ROPE] = inv
    return jnp.asarray(tab)


def _layer(x2, pos2, batch, seq, norm1_gain, w_in, gk_fwd_w, gk_fwd_b, gk_bwd_w, gk_bwd_b, gla_out_gain,
           mla_q_gain, mla_w_qb, mla_kv_gain, mla_w_kvb, q_norm_gain, k_norm_gain, w_out, norm2_gain,
           w_router_group, b_router_group, w_router_expert, b_router_expert,
           w_expert_gate, w_expert_up, w_expert_down):
    T, D = x2.shape
    win, wg, bg, wqb, wkvk, wkvv, qng, kng = _prep_weights(
        w_in, gk_fwd_w, gk_fwd_b, gk_bwd_w, gk_bwd_b, mla_w_qb, mla_w_kvb, q_norm_gain, k_norm_gain)

    TM = 512
    spb = seq // TM
    tok = lambda w: pl.BlockSpec((TM, w), lambda i: (i, 0))
    head_spec = pl.BlockSpec((1, MLA_HEADS, TM, HEAD_TILE), lambda i: (i // spb, 0, i % spb, 0))
    gq, gk, gv, gg, lg, mq, mk, mv = pl.pallas_call(
        _proj_kernel,
        grid=(T // TM,),
        in_specs=[tok(D), tok(1), _full((1, D)), _full((D, PROJ_WIDTH)), _full((LANES, 2 * GLA_QK_WIDTH)),
                  _full((1, 2 * GLA_QK_WIDTH)), _full((1, MLA_Q_RANK)), _full((MLA_Q_RANK, MLA_HEADS * HEAD_TILE)),
                  _full((1, MLA_KV_RANK)), _full((MLA_KV_RANK, MLA_HEADS * HEAD_TILE)),
                  _full((MLA_KV_RANK, MLA_WIDTH)), _full((1, HEAD_TILE)), _full((1, HEAD_TILE)),
                  _full((1, LANES))],
        out_specs=[tok(GLA_QK_WIDTH), tok(GLA_QK_WIDTH), tok(GLA_WIDTH), tok(GLA_WIDTH), tok(2 * GLA_QK_WIDTH),
                   head_spec, head_spec, tok(MLA_WIDTH)],
        out_shape=[jax.ShapeDtypeStruct((T, GLA_QK_WIDTH), BF16), jax.ShapeDtypeStruct((T, GLA_QK_WIDTH), BF16),
                   jax.ShapeDtypeStruct((T, GLA_WIDTH), BF16), jax.ShapeDtypeStruct((T, GLA_WIDTH), BF16),
                   jax.ShapeDtypeStruct((T, 2 * GLA_QK_WIDTH), F32),
                   jax.ShapeDtypeStruct((batch, MLA_HEADS, seq, HEAD_TILE), BF16),
                   jax.ShapeDtypeStruct((batch, MLA_HEADS, seq, HEAD_TILE), BF16),
                   jax.ShapeDtypeStruct((T, MLA_WIDTH), BF16)],
        compiler_params=pltpu.CompilerParams(dimension_semantics=("parallel",), vmem_limit_bytes=VMEM_LIMIT),
        name="proj",
    )(x2, pos2, norm1_gain[None, :], win, wg, bg, mla_q_gain[None, :], wqb, mla_kv_gain[None, :],
      wkvk, wkvv, qng, kng, _rope_freq())

    seqspec = lambda w: pl.BlockSpec((seq, w), lambda b: (b, 0))
    gla_out = pl.pallas_call(
        _gla_kernel,
        grid=(batch,),
        in_specs=[seqspec(GLA_QK_WIDTH), seqspec(GLA_QK_WIDTH), seqspec(GLA_WIDTH), seqspec(2 * GLA_QK_WIDTH),
                  seqspec(GLA_WIDTH), _full((1, GLA_DV))],
        out_specs=seqspec(GLA_WIDTH),
        out_shape=jax.ShapeDtypeStruct((T, GLA_WIDTH), BF16),
        scratch_shapes=[pltpu.VMEM((seq, GLA_WIDTH), F32), pltpu.VMEM((GLA_DV, GLA_QK_WIDTH), F32)],
        compiler_params=pltpu.CompilerParams(dimension_semantics=("parallel",), vmem_limit_bytes=VMEM_LIMIT),
        name="gla",
    )(gq, gk, gv, lg, gg, gla_out_gain[None, :])

    TQ = 256
    nq = seq // TQ
    mla_out = pl.pallas_call(
        _attn_kernel,
        grid=(batch, MLA_HEADS // 2, nq),
        in_specs=[pl.BlockSpec((1, 2, TQ, HEAD_TILE), lambda b, hp, i: (b, hp, i, 0)),
                  pl.BlockSpec((1, 2, seq, HEAD_TILE), lambda b, hp, i: (b, hp, 0, 0)),
                  pl.BlockSpec((seq, LANES), lambda b, hp, i: (b, hp))],
        out_specs=pl.BlockSpec((TQ, LANES), lambda b, hp, i: (b * nq + i, hp)),
        out_shape=jax.ShapeDtypeStruct((T, MLA_WIDTH), BF16),
        compiler_params=pltpu.CompilerParams(dimension_semantics=("parallel", "parallel", "parallel"),
                                             vmem_limit_bytes=VMEM_LIMIT),
        name="attn",
    )(mq, mk, mv)

    wr = jnp.zeros((ROUTER_ROWS, D), F32)
    wr = wr.at[0:N_GROUPS].set(w_router_group.T)
    wr = wr.at[ROUTER_EXPERT_ROW:ROUTER_EXPERT_ROW + N_EXPERTS].set(w_router_expert.T)
    wr_hi, wr_lo = _split_bf16(wr)
    br = jnp.zeros((ROUTER_ROWS, 1), F32)
    br = br.at[0:N_GROUPS, 0].set(b_router_group)
    br = br.at[ROUTER_EXPERT_ROW:ROUTER_EXPERT_ROW + N_EXPERTS, 0].set(b_router_expert)
    x1, h2, route = pl.pallas_call(
        _outproj_kernel,
        grid=(T // TM,),
        in_specs=[tok(D), tok(GLA_WIDTH), tok(MLA_WIDTH), _full((D, D)), _full((1, D)),
                  _full((ROUTER_ROWS, D)), _full((ROUTER_ROWS, D)), _full((ROUTER_ROWS, 1))],
        out_specs=[tok(D), tok(D), pl.BlockSpec((8, TM), lambda i: (0, i))],
        out_shape=[jax.ShapeDtypeStruct((T, D), F32), jax.ShapeDtypeStruct((T, D), BF16),
                   jax.ShapeDtypeStruct((8, T), F32)],
        compiler_params=pltpu.CompilerParams(dimension_semantics=("parallel",), vmem_limit_bytes=VMEM_LIMIT),
        name="outproj",
    )(x2, gla_out, mla_out, w_out.astype(BF16), norm2_gain[None, :], wr_hi, wr_lo, br)

    e1 = route[0].astype(jnp.int32)
    e2 = route[1].astype(jnp.int32)
    eid = jnp.arange(N_EXPERTS, dtype=jnp.int32)[:, None]
    gate = jnp.where(eid == e1[None, :], route[2][None, :], 0.0) + jnp.where(eid == e2[None, :], route[3][None, :], 0.0)
    gate = gate[:, :, None]
    TE = 1024
    wge = w_expert_gate.reshape(N_EXPERTS, D, D_EXPERT).astype(BF16)
    wue = w_expert_up.reshape(N_EXPERTS, D, D_EXPERT).astype(BF16)
    wde = w_expert_down.reshape(N_EXPERTS, D_EXPERT, D).astype(BF16)
    out = pl.pallas_call(
        _moe_kernel,
        grid=(T // TE, N_EXPERTS),
        in_specs=[pl.BlockSpec((TE, D), lambda i, e: (i, 0)), pl.BlockSpec((TE, D), lambda i, e: (i, 0)),
                  pl.BlockSpec((1, TE, 1), lambda i, e: (e, i, 0)),
                  pl.BlockSpec((1, D, D_EXPERT), lambda i, e: (e, 0, 0)),
                  pl.BlockSpec((1, D, D_EXPERT), lambda i, e: (e, 0, 0)),
                  pl.BlockSpec((1, D_EXPERT, D), lambda i, e: (e, 0, 0))],
        out_specs=pl.BlockSpec((TE, D), lambda i, e: (i, 0)),
        out_shape=jax.ShapeDtypeStruct((T, D), F32),
        compiler_params=pltpu.CompilerParams(dimension_semantics=("parallel", "arbitrary"),
                                             vmem_limit_bytes=VMEM_LIMIT),
        name="moe",
    )(h2, x1, gate, wge, wue, wde)
    return out


def kernel(x, positions, norm1_gain, w_in, gla_gk_fwd_w, gla_gk_fwd_b, gla_gk_bwd_w, gla_gk_bwd_b, gla_out_gain, mla_q_gain, mla_w_qb, mla_kv_gain, mla_w_kvb, q_norm_gain, k_norm_gain, w_out, norm2_gain, w_router_group, b_router_group, w_router_expert, b_router_expert, w_expert_gate, w_expert_up, w_expert_down):
    batch, seq, d = x.shape
    x2 = x.reshape(batch * seq, d)
    pos2 = positions.reshape(batch * seq, 1)
    for l in range(norm1_gain.shape[0]):
        x2 = _layer(x2, pos2, batch, seq, norm1_gain[l], w_in[l], gla_gk_fwd_w[l], gla_gk_fwd_b[l],
                    gla_gk_bwd_w[l], gla_gk_bwd_b[l], gla_out_gain[l], mla_q_gain[l], mla_w_qb[l],
                    mla_kv_gain[l], mla_w_kvb[l], q_norm_gain[l], k_norm_gain[l], w_out[l], norm2_gain[l],
                    w_router_group[l], b_router_group[l], w_router_expert[l], b_router_expert[l],
                    w_expert_gate[l], w_expert_up[l], w_expert_down[l])
    return x2.reshape(batch, seq, d)
```

```python
import functools

import numpy as np
import jax
import jax.numpy as jnp
from jax import lax
from jax.experimental import pallas as pl
from jax.experimental.pallas import tpu as pltpu

F32 = jnp.float32
BF16 = jnp.bfloat16

D_MODEL = 1024
GLA_HEADS = 4
GLA_DK = 64
GLA_DV = 128
GLA_GATE_RANK = 16
GLA_GATE_NORMALIZER = 16.0
GLA_CHUNK = 64
MLA_HEADS = 8
MLA_NOPE = 64
MLA_ROPE = 32
MLA_QK = MLA_NOPE + MLA_ROPE
MLA_V = 64
MLA_Q_RANK = 256
MLA_KV_RANK = 128
ROPE_BASE = 10000.0
GLA_QK_WIDTH = GLA_HEADS * GLA_DK
GLA_WIDTH = GLA_HEADS * GLA_DV
MLA_WIDTH = MLA_HEADS * MLA_V
N_GROUPS = 4
EXPERTS_PER_GROUP = 8
N_EXPERTS = N_GROUPS * EXPERTS_PER_GROUP
D_EXPERT = 256
EPS = 1e-6

LANES = 128
HEAD_TILE = LANES
ROPE_LO = MLA_NOPE
ROPE_HALF = MLA_ROPE // 2

COL_GQ = 0
COL_GK = COL_GQ + GLA_QK_WIDTH
COL_GV = COL_GK + GLA_QK_WIDTH
COL_GG = COL_GV + GLA_WIDTH
COL_MQ = COL_GG + GLA_WIDTH
COL_MKV = COL_MQ + MLA_Q_RANK
COL_MISC = COL_MKV + MLA_KV_RANK
PROJ_WIDTH = COL_MISC + LANES
MISC_LRB = GLA_GATE_RANK

ROUTER_ROWS = LANES
ROUTER_EXPERT_ROW = 8

U32 = jnp.uint32
HI_HALF_MASK = np.uint32(0xFFFF0000)
PACK_ROWS = D_MODEL // (2 * LANES)
MOE_SUPER_BLOCK = 4096
MOE_CHUNK = 320
MOE_COMBINE_TILE = 512
MOE_ROW_UNROLL = 8

VMEM_LIMIT = 56 * 1024 * 1024


def _dot(a, b):
    return jnp.dot(a, b, preferred_element_type=F32)


def _dot_nt(a, b):
    return lax.dot_general(a, b, (((1,), (1,)), ((), ())), preferred_element_type=F32)


def _dot_tn(a, b):
    return lax.dot_general(a, b, (((0,), (0,)), ((), ())), preferred_element_type=F32)


def _split_bf16(x):
    hi = x.astype(BF16)
    lo = (x - hi.astype(F32)).astype(BF16)
    return hi, lo


def _proj_kernel(x_ref, pos_ref, n1_ref, win_ref, wg_ref, bg_ref, qgain_ref, wqb_ref, kvgain_ref,
                 wkvk_ref, wkvv_ref, qng_ref, kng_ref, freq_ref,
                 gq_ref, gk_ref, gv_ref, gg_ref, lg_ref, mq_ref, mk_ref, mv_ref):
    x = x_ref[...]
    h = x * lax.rsqrt(jnp.mean(x * x, axis=-1, keepdims=True) + EPS) * n1_ref[...]
    proj = _dot(h.astype(BF16), win_ref[...])

    gq_ref[...] = (proj[:, COL_GQ:COL_GK] * GLA_DK ** -0.5).astype(BF16)
    gk_ref[...] = proj[:, COL_GK:COL_GV].astype(BF16)
    gv_ref[...] = proj[:, COL_GV:COL_GG].astype(BF16)
    gg_ref[...] = proj[:, COL_GG:COL_MQ].astype(BF16)

    misc = proj[:, COL_MISC:PROJ_WIDTH]
    z = _dot(misc.astype(BF16), wg_ref[...]) + bg_ref[...]
    log_sig = -(jnp.maximum(-z, 0.0) + jnp.log1p(jnp.exp(-jnp.abs(z))))
    lg_ref[...] = log_sig / GLA_GATE_NORMALIZER

    lane = lax.broadcasted_iota(jnp.int32, (1, LANES), 1)
    ang = pos_ref[...].astype(F32) * freq_ref[...]
    cos = jnp.cos(ang)
    sin = jnp.sin(ang)
    in_rope = (lane >= ROPE_LO) & (lane < ROPE_LO + MLA_ROPE)
    c_tab = jnp.where(lane < ROPE_LO, 1.0, jnp.where(in_rope, cos, 0.0))
    s_up = jnp.where((lane >= ROPE_LO + ROPE_HALF) & (lane < ROPE_LO + MLA_ROPE), sin, 0.0)
    s_dn = jnp.where((lane >= ROPE_LO) & (lane < ROPE_LO + ROPE_HALF), -sin, 0.0)

    def norm_rope(t, gain):
        ss = jnp.sum(t * t, axis=-1, keepdims=True) * (1.0 / MLA_QK)
        tn = t * lax.rsqrt(ss + EPS) * gain
        return (tn * c_tab + pltpu.roll(tn, ROPE_HALF, 1) * s_up
                + pltpu.roll(tn, LANES - ROPE_HALF, 1) * s_dn)

    qa = proj[:, COL_MQ:COL_MKV]
    qn = qa * lax.rsqrt(jnp.mean(qa * qa, axis=-1, keepdims=True) + EPS) * qgain_ref[...]
    mq = _dot(qn.astype(BF16), wqb_ref[...])
    kva = proj[:, COL_MKV:COL_MISC]
    kvn = (kva * lax.rsqrt(jnp.mean(kva * kva, axis=-1, keepdims=True) + EPS) * kvgain_ref[...]).astype(BF16)
    kn = _dot(kvn, wkvk_ref[...])
    mv_ref[...] = _dot(kvn, wkvv_ref[...]).astype(BF16)
    rope_tile = jnp.where(in_rope, misc, 0.0)
    qng = qng_ref[...]
    kng = kng_ref[...]
    scale = MLA_QK ** -0.5
    for hd in range(MLA_HEADS):
        sl = slice(hd * HEAD_TILE, (hd + 1) * HEAD_TILE)
        mq_ref[0, hd] = (norm_rope(mq[:, sl], qng) * scale).astype(BF16)
        mk_ref[0, hd] = norm_rope(kn[:, sl] + rope_tile, kng).astype(BF16)


def _gla_kernel(q_ref, k_ref, v_ref, lg_ref, gate_ref, gain_ref, o_ref, acc_ref, st_ref):
    seq = q_ref.shape[0]
    C = GLA_CHUNK
    n_chunks = seq // C
    HK = GLA_QK_WIDTH

    row = lax.broadcasted_iota(jnp.int32, (C, C), 0)
    col = lax.broadcasted_iota(jnp.int32, (C, C), 1)
    tri_prefix = (col <= row).astype(BF16)
    tri_suffix = (col >= row).astype(BF16)
    lane_head = lax.broadcasted_iota(jnp.int32, (1, HK), 1) // GLA_DK
    srow = lax.broadcasted_iota(jnp.int32, (GLA_HEADS * C, C), 0) % C
    scol = lax.broadcasted_iota(jnp.int32, (GLA_HEADS * C, C), 1)
    gain = gain_ref[...]

    def stack_heads(t):
        return jnp.concatenate([jnp.where(lane_head == hd, t, 0.0) for hd in range(GLA_HEADS)], axis=0)

    def chunk_step(c, tri, ref_i, last_i, keep, lg_off, finish):
        r0 = pl.multiple_of(c * C, C)
        rows = pl.ds(r0, C)
        g = lg_ref[rows, lg_off:lg_off + HK]
        g_hi, g_lo = _split_bf16(g)
        b = _dot(tri, g_hi) + _dot(tri, g_lo)
        b_ref = b[ref_i:ref_i + 1, :]
        b_last = b[last_i:last_i + 1, :]
        q = q_ref[rows, :].astype(F32)
        k = k_ref[rows, :].astype(F32)
        v = v_ref[rows, :]
        q_rel = q * jnp.exp(b - b_ref)
        k_rel = (k * jnp.exp(b_ref - b)).astype(BF16)
        k_dec = (k * jnp.exp(b_last - b)).astype(BF16)
        q_dec = q * jnp.exp(b)
        st = st_ref[...]
        sc = _dot_nt(stack_heads(q_rel).astype(BF16), k_rel)
        sc = jnp.where(keep, sc, 0.0).astype(BF16)
        o_inter = _dot_nt(stack_heads(q_dec).astype(BF16), st.astype(BF16))
        kv_t = _dot_tn(v, k_dec)
        new_st = jnp.exp(b_last) * st
        for hd in range(GLA_HEADS):
            vs = slice(hd * GLA_DV, (hd + 1) * GLA_DV)
            rs = slice(hd * C, (hd + 1) * C)
            o_h = _dot(sc[rs, :], v[:, vs]) + o_inter[rs, :]
            new_st = new_st + jnp.where(lane_head == hd, kv_t[vs, :], 0.0)
            finish(rows, vs, o_h)
        st_ref[...] = new_st

    def store_fwd(rows, vs, o_h):
        acc_ref[rows, vs] = o_h

    def store_final(rows, vs, o_h):
        o = acc_ref[rows, vs] + o_h
        on = o * lax.rsqrt(jnp.mean(o * o, axis=-1, keepdims=True) + EPS) * gain
        gt = gate_ref[rows, vs].astype(F32)
        o_ref[rows, vs] = (on * (gt * jax.nn.sigmoid(gt))).astype(BF16)

    st_ref[...] = jnp.zeros_like(st_ref)

    def fwd_body(c, carry):
        chunk_step(c, tri_prefix, C // 2 - 1, C - 1, scol <= srow, 0, store_fwd)
        return carry

    lax.fori_loop(0, n_chunks, fwd_body, 0)

    st_ref[...] = jnp.zeros_like(st_ref)

    def bwd_body(i, carry):
        chunk_step(n_chunks - 1 - i, tri_suffix, C // 2, 0, scol > srow, HK, store_final)
        return carry

    lax.fori_loop(0, n_chunks, bwd_body, 0)


def _attn_kernel(q_ref, k_ref, v_ref, o_ref):
    outs = []
    for j in range(2):
        s = _dot_nt(q_ref[0, j], k_ref[0, j])
        m = jnp.max(s, axis=-1, keepdims=True)
        p = jnp.exp(s - m)
        l = jnp.sum(p, axis=-1, keepdims=True)
        outs.append(_dot(p.astype(BF16), v_ref[...]) / l)
    lane = lax.broadcasted_iota(jnp.int32, (1, LANES), 1)
    o_ref[...] = jnp.where(lane < MLA_V, outs[0], outs[1]).astype(BF16)


def _unpack_bf16_pair(w):
    lo = lax.bitcast_convert_type(w << 16, F32)
    hi = lax.bitcast_convert_type(w & HI_HALF_MASK, F32)
    return lo, hi


def _pack_bf16_pair(lo, hi):
    lo_b = lax.bitcast_convert_type(lo.astype(BF16).astype(F32), U32)
    hi_b = lax.bitcast_convert_type(hi.astype(BF16).astype(F32), U32)
    return (hi_b & HI_HALF_MASK) | (lo_b >> 16)


def _outproj_kernel(tiles_per_sb, x_ref, gla_ref, mla_ref, wo_ref, n2_ref, wr_hi_ref, wr_lo_ref, br_ref, tri_ref,
                    x1_ref, h2p_ref, route_ref, cnt_ref, carry_ref):
    mix = _dot(gla_ref[...], wo_ref[0:GLA_WIDTH, :]) + _dot(mla_ref[...], wo_ref[GLA_WIDTH:, :])
    x1 = x_ref[...] + mix
    x1_ref[...] = x1
    h2 = x1 * lax.rsqrt(jnp.mean(x1 * x1, axis=-1, keepdims=True) + EPS) * n2_ref[...]
    rows = h2.shape[0]
    for r in range(PACK_ROWS):
        c0 = r * 2 * LANES
        h2p_ref[pl.ds(r, rows, stride=PACK_ROWS), :] = _pack_bf16_pair(h2[:, c0:c0 + LANES],
                                                                      h2[:, c0 + LANES:c0 + 2 * LANES])

    h_hi, h_lo = _split_bf16(h2)
    w_hi = wr_hi_ref[...]
    logits = (_dot_nt(w_hi, h_hi) + _dot_nt(w_hi, h_lo) + _dot_nt(wr_lo_ref[...], h_hi)) + br_ref[...]
    tm = logits.shape[1]
    gl = logits[0:N_GROUPS, :]
    ge = jnp.exp(gl - jnp.max(gl, axis=0, keepdims=True))
    pg = ge / jnp.sum(ge, axis=0, keepdims=True)
    p_top = jnp.max(pg, axis=0, keepdims=True)
    gi = lax.broadcasted_iota(jnp.int32, (N_GROUPS, tm), 0)
    g_idx = jnp.min(jnp.where(pg == p_top, gi, N_GROUPS), axis=0, keepdims=True)
    sel = jnp.zeros((EXPERTS_PER_GROUP, tm), F32)
    for g in range(N_GROUPS):
        r0 = ROUTER_EXPERT_ROW + g * EXPERTS_PER_GROUP
        sel = sel + jnp.where(g_idx == g, logits[r0:r0 + EXPERTS_PER_GROUP, :], 0.0)
    se = jnp.exp(sel - jnp.max(sel, axis=0, keepdims=True))
    pe = se / jnp.sum(se, axis=0, keepdims=True)
    ei = lax.broadcasted_iota(jnp.int32, (EXPERTS_PER_GROUP, tm), 0)
    m1 = jnp.max(pe, axis=0, keepdims=True)
    i1 = jnp.min(jnp.where(pe == m1, ei, EXPERTS_PER_GROUP), axis=0, keepdims=True)
    pe2 = jnp.where(ei == i1, -1.0, pe)
    m2 = jnp.max(pe2, axis=0, keepdims=True)
    i2 = jnp.min(jnp.where(pe2 == m2, ei, EXPERTS_PER_GROUP), axis=0, keepdims=True)
    den = m1 + m2
    e1 = g_idx * EXPERTS_PER_GROUP + i1
    e2 = g_idx * EXPERTS_PER_GROUP + i2

    @pl.when(pl.program_id(0) % tiles_per_sb == 0)
    def _():
        carry_ref[...] = jnp.zeros_like(carry_ref)

    eall = lax.broadcasted_iota(jnp.int32, (N_EXPERTS, tm), 0)
    tri = tri_ref[...]
    carry = carry_ref[...]
    ranks = []
    for eid in (e1, e2):
        hot = eall == eid
        prefix = _dot(hot.astype(BF16), tri)
        ranks.append(jnp.sum(jnp.where(hot, carry + prefix - 1.0, 0.0), axis=0, keepdims=True))
        carry = carry + prefix[:, tm - 1:tm]
    carry_ref[...] = carry
    cnt_ref[0] = jnp.broadcast_to(carry, (N_EXPERTS, LANES))
    zeros = jnp.zeros((2, tm), F32)
    route_ref[...] = jnp.concatenate([e1.astype(F32), e2.astype(F32), p_top * (m1 / den), p_top * (m2 / den),
                                      ranks[0], ranks[1], zeros], axis=0)


def _moe_kernel(tb, off_ref, cnt_ref, h2p_ref, pos_ref, wg_ref, wu_ref, wd_ref, x1_ref, wt_ref, o_ref,
                xy_ref, cb_ref):
    sb = pl.program_id(0)
    j = pl.program_id(1)

    @pl.when(j == 0)
    def _scatter():
        xy_ref[...] = jnp.zeros_like(xy_ref)

        def body(g, c):
            for i in range(MOE_ROW_UNROLL // 2):
                pair = g * (MOE_ROW_UNROLL // 2) + i
                v = h2p_ref[pl.ds(pl.multiple_of(pair * 2 * PACK_ROWS, 2 * PACK_ROWS), 2 * PACK_ROWS), :]
                for half in range(2):
                    t = pair * 2 + half
                    piece = v[half * PACK_ROWS:(half + 1) * PACK_ROWS, :]
                    for slot in range(2):
                        p = pos_ref[slot * tb + t]
                        xy_ref[pl.ds(pl.multiple_of(p * PACK_ROWS, PACK_ROWS), PACK_ROWS), :] = piece
            return c

        lax.fori_loop(0, tb // MOE_ROW_UNROLL, body, 0)

    @pl.when((j >= 1) & (j <= N_EXPERTS))
    def _experts():
        e = sb * N_EXPERTS + j - 1
        n = cnt_ref[e]
        base = off_ref[e]

        def body(i, c):
            r0 = base + i * MOE_CHUNK
            row0 = pl.multiple_of(r0 * PACK_ROWS, 2 * PACK_ROWS)
            words = [xy_ref[pl.ds(row0 + r, MOE_CHUNK, stride=PACK_ROWS), :] for r in range(PACK_ROWS)]
            a = jnp.zeros((MOE_CHUNK, D_EXPERT), F32)
            u = jnp.zeros((MOE_CHUNK, D_EXPERT), F32)
            for r in range(PACK_ROWS):
                lo, hi = _unpack_bf16_pair(words[r])
                xb = jnp.concatenate([lo, hi], axis=1).astype(BF16)
                a = a + _dot(xb, wg_ref[0, r * 2 * LANES:(r + 1) * 2 * LANES, :])
                u = u + _dot(xb, wu_ref[0, r * 2 * LANES:(r + 1) * 2 * LANES, :])
            hid = ((a * jax.nn.sigmoid(a)) * u).astype(BF16)
            y = _dot(hid, wd_ref[0])
            valid = (r0 + lax.broadcasted_iota(jnp.int32, (MOE_CHUNK, 1), 0)) < (base + n)
            for r in range(PACK_ROWS):
                c0 = r * 2 * LANES
                new = _pack_bf16_pair(y[:, c0:c0 + LANES], y[:, c0 + LANES:c0 + 2 * LANES])
                xy_ref[pl.ds(row0 + r, MOE_CHUNK, stride=PACK_ROWS), :] = jnp.where(valid, new, words[r])
            return c

        lax.fori_loop(0, (n + MOE_CHUNK - 1) // MOE_CHUNK, body, 0)

    @pl.when(j > N_EXPERTS)
    def _combine():
        t0 = (j - N_EXPERTS - 1) * MOE_COMBINE_TILE

        def body(g, c):
            for i in range(MOE_ROW_UNROLL):
                t = g * MOE_ROW_UNROLL + i
                for slot in range(2):
                    p = pos_ref[slot * tb + t0 + t]
                    cb_ref[slot, pl.ds(pl.multiple_of(t * PACK_ROWS, PACK_ROWS), PACK_ROWS), :] = (
                        xy_ref[pl.ds(pl.multiple_of(p * PACK_ROWS, PACK_ROWS), PACK_ROWS), :])
            return c

        lax.fori_loop(0, MOE_COMBINE_TILE // MOE_ROW_UNROLL, body, 0)
        w1 = wt_ref[:, 0:1]
        w2 = wt_ref[:, 1:2]
        for r in range(PACK_ROWS):
            lo1, hi1 = _unpack_bf16_pair(cb_ref[0, pl.ds(r, MOE_COMBINE_TILE, stride=PACK_ROWS), :])
            lo2, hi2 = _unpack_bf16_pair(cb_ref[1, pl.ds(r, MOE_COMBINE_TILE, stride=PACK_ROWS), :])
            c0 = r * 2 * LANES
            o_ref[:, c0:c0 + LANES] = x1_ref[:, c0:c0 + LANES] + w1 * lo1 + w2 * lo2
            o_ref[:, c0 + LANES:c0 + 2 * LANES] = x1_ref[:, c0 + LANES:c0 + 2 * LANES] + w1 * hi1 + w2 * hi2


def _full(shape):
    return pl.BlockSpec(shape, lambda *_: (0,) * len(shape))


def _prep_weights(w_in, gk_fwd_w, gk_fwd_b, gk_bwd_w, gk_bwd_b, mla_w_qb, mla_w_kvb, q_norm_gain, k_norm_gain):
    splits = np.cumsum([0, GLA_QK_WIDTH, GLA_QK_WIDTH, GLA_WIDTH, GLA_WIDTH, GLA_GATE_RANK, GLA_GATE_RANK,
                        MLA_Q_RANK, MLA_KV_RANK, MLA_ROPE])
    seg = [w_in[:, splits[i]:splits[i + 1]] for i in range(9)]
    d = w_in.shape[0]
    misc = jnp.zeros((d, LANES), F32)
    misc = misc.at[:, 0:GLA_GATE_RANK].set(seg[4])
    misc = misc.at[:, MISC_LRB:MISC_LRB + GLA_GATE_RANK].set(seg[5])
    misc = misc.at[:, ROPE_LO:ROPE_LO + MLA_ROPE].set(seg[8])
    win = jnp.concatenate([seg[0], seg[1], seg[2], seg[3], seg[6], seg[7], misc], axis=1).astype(BF16)

    wg = jnp.zeros((LANES, 2 * GLA_QK_WIDTH), F32)
    wg = wg.at[0:GLA_GATE_RANK, 0:GLA_QK_WIDTH].set(gk_fwd_w)
    wg = wg.at[MISC_LRB:MISC_LRB + GLA_GATE_RANK, GLA_QK_WIDTH:].set(gk_bwd_w)
    bg = jnp.concatenate([gk_fwd_b, gk_bwd_b])[None, :]

    pad = HEAD_TILE - MLA_QK
    wqb = jnp.pad(mla_w_qb.reshape(MLA_Q_RANK, MLA_HEADS, MLA_QK), ((0, 0), (0, 0), (0, pad)))
    wqb = wqb.reshape(MLA_Q_RANK, MLA_HEADS * HEAD_TILE).astype(BF16)
    wkv = mla_w_kvb.reshape(MLA_KV_RANK, MLA_HEADS, MLA_NOPE + MLA_V)
    wkvk = jnp.pad(wkv[:, :, :MLA_NOPE], ((0, 0), (0, 0), (0, HEAD_TILE - MLA_NOPE)))
    wkvk = wkvk.reshape(MLA_KV_RANK, MLA_HEADS * HEAD_TILE).astype(BF16)
    wkvv = wkv[:, :, MLA_NOPE:].reshape(MLA_KV_RANK, MLA_WIDTH).astype(BF16)
    qng = jnp.pad(q_norm_gain, (0, pad))[None, :]
    kng = jnp.pad(k_norm_gain, (0, pad))[None, :]
    return win, wg.astype(BF16), bg, wqb, wkvk, wkvv, qng, kng


def _rope_freq():
    inv = ROPE_BASE ** (-np.arange(0, MLA_ROPE, 2, dtype=np.float32) / MLA_ROPE)
    tab = np.zeros((1, LANES), np.float32)
    tab[0, ROPE_LO:ROPE_LO + ROPE_HALF] = inv
    tab[0, ROPE_LO + ROPE_HALF:ROPE_LO + MLA_ROPE] = inv
    return jnp.asarray(tab)


def _layer(x2, pos2, batch, seq, norm1_gain, w_in, gk_fwd_w, gk_fwd_b, gk_bwd_w, gk_bwd_b, gla_out_gain,
           mla_q_gain, mla_w_qb, mla_kv_gain, mla_w_kvb, q_norm_gain, k_norm_gain, w_out, norm2_gain,
           w_router_group, b_router_group, w_router_expert, b_router_expert,
           w_expert_gate, w_expert_up, w_expert_down):
    T, D = x2.shape
    win, wg, bg, wqb, wkvk, wkvv, qng, kng = _prep_weights(
        w_in, gk_fwd_w, gk_fwd_b, gk_bwd_w, gk_bwd_b, mla_w_qb, mla_w_kvb, q_norm_gain, k_norm_gain)

    TM = 512
    spb = seq // TM
    tok = lambda w: pl.BlockSpec((TM, w), lambda i: (i, 0))
    head_spec = pl.BlockSpec((1, MLA_HEADS, TM, HEAD_TILE), lambda i: (i // spb, 0, i % spb, 0))
    gq, gk, gv, gg, lg, mq, mk, mv = pl.pallas_call(
        _proj_kernel,
        grid=(T // TM,),
        in_specs=[tok(D), tok(1), _full((1, D)), _full((D, PROJ_WIDTH)), _full((LANES, 2 * GLA_QK_WIDTH)),
                  _full((1, 2 * GLA_QK_WIDTH)), _full((1, MLA_Q_RANK)), _full((MLA_Q_RANK, MLA_HEADS * HEAD_TILE)),
                  _full((1, MLA_KV_RANK)), _full((MLA_KV_RANK, MLA_HEADS * HEAD_TILE)),
                  _full((MLA_KV_RANK, MLA_WIDTH)), _full((1, HEAD_TILE)), _full((1, HEAD_TILE)),
                  _full((1, LANES))],
        out_specs=[tok(GLA_QK_WIDTH), tok(GLA_QK_WIDTH), tok(GLA_WIDTH), tok(GLA_WIDTH), tok(2 * GLA_QK_WIDTH),
                   head_spec, head_spec, tok(MLA_WIDTH)],
        out_shape=[jax.ShapeDtypeStruct((T, GLA_QK_WIDTH), BF16), jax.ShapeDtypeStruct((T, GLA_QK_WIDTH), BF16),
                   jax.ShapeDtypeStruct((T, GLA_WIDTH), BF16), jax.ShapeDtypeStruct((T, GLA_WIDTH), BF16),
                   jax.ShapeDtypeStruct((T, 2 * GLA_QK_WIDTH), F32),
                   jax.ShapeDtypeStruct((batch, MLA_HEADS, seq, HEAD_TILE), BF16),
                   jax.ShapeDtypeStruct((batch, MLA_HEADS, seq, HEAD_TILE), BF16),
                   jax.ShapeDtypeStruct((T, MLA_WIDTH), BF16)],
        compiler_params=pltpu.CompilerParams(dimension_semantics=("parallel",), vmem_limit_bytes=VMEM_LIMIT),
        name="proj",
    )(x2, pos2, norm1_gain[None, :], win, wg, bg, mla_q_gain[None, :], wqb, mla_kv_gain[None, :],
      wkvk, wkvv, qng, kng, _rope_freq())

    seqspec = lambda w: pl.BlockSpec((seq, w), lambda b: (b, 0))
    gla_out = pl.pallas_call(
        _gla_kernel,
        grid=(batch,),
        in_specs=[seqspec(GLA_QK_WIDTH), seqspec(GLA_QK_WIDTH), seqspec(GLA_WIDTH), seqspec(2 * GLA_QK_WIDTH),
                  seqspec(GLA_WIDTH), _full((1, GLA_DV))],
        out_specs=seqspec(GLA_WIDTH),
        out_shape=jax.ShapeDtypeStruct((T, GLA_WIDTH), BF16),
        scratch_shapes=[pltpu.VMEM((seq, GLA_WIDTH), F32), pltpu.VMEM((GLA_DV, GLA_QK_WIDTH), F32)],
        compiler_params=pltpu.CompilerParams(dimension_semantics=("parallel",), vmem_limit_bytes=VMEM_LIMIT),
        name="gla",
    )(gq, gk, gv, lg, gg, gla_out_gain[None, :])

    TQ = 256
    nq = seq // TQ
    mla_out = pl.pallas_call(
        _attn_kernel,
        grid=(batch, MLA_HEADS // 2, nq),
        in_specs=[pl.BlockSpec((1, 2, TQ, HEAD_TILE), lambda b, hp, i: (b, hp, i, 0)),
                  pl.BlockSpec((1, 2, seq, HEAD_TILE), lambda b, hp, i: (b, hp, 0, 0)),
                  pl.BlockSpec((seq, LANES), lambda b, hp, i: (b, hp))],
        out_specs=pl.BlockSpec((TQ, LANES), lambda b, hp, i: (b * nq + i, hp)),
        out_shape=jax.ShapeDtypeStruct((T, MLA_WIDTH), BF16),
        compiler_params=pltpu.CompilerParams(dimension_semantics=("parallel", "parallel", "parallel"),
                                             vmem_limit_bytes=VMEM_LIMIT),
        name="attn",
    )(mq, mk, mv)

    wr = jnp.zeros((ROUTER_ROWS, D), F32)
    wr = wr.at[0:N_GROUPS].set(w_router_group.T)
    wr = wr.at[ROUTER_EXPERT_ROW:ROUTER_EXPERT_ROW + N_EXPERTS].set(w_router_expert.T)
    wr_hi, wr_lo = _split_bf16(wr)
    br = jnp.zeros((ROUTER_ROWS, 1), F32)
    br = br.at[0:N_GROUPS, 0].set(b_router_group)
    br = br.at[ROUTER_EXPERT_ROW:ROUTER_EXPERT_ROW + N_EXPERTS, 0].set(b_router_expert)
    tb = min(MOE_SUPER_BLOCK, T)
    nsb = T // tb
    tiles_per_sb = tb // TM
    tri = jnp.asarray(np.triu(np.ones((TM, TM), np.float32)), BF16)
    x1, h2p, route, cnt = pl.pallas_call(
        functools.partial(_outproj_kernel, tiles_per_sb),
        grid=(T // TM,),
        in_specs=[tok(D), tok(GLA_WIDTH), tok(MLA_WIDTH), _full((D, D)), _full((1, D)),
                  _full((ROUTER_ROWS, D)), _full((ROUTER_ROWS, D)), _full((ROUTER_ROWS, 1)), _full((TM, TM))],
        out_specs=[tok(D), pl.BlockSpec((TM * PACK_ROWS, LANES), lambda i: (i, 0)),
                   pl.BlockSpec((8, TM), lambda i: (0, i)),
                   pl.BlockSpec((1, N_EXPERTS, LANES), lambda i: (i // tiles_per_sb, 0, 0))],
        out_shape=[jax.ShapeDtypeStruct((T, D), F32), jax.ShapeDtypeStruct((T * PACK_ROWS, LANES), U32),
                   jax.ShapeDtypeStruct((8, T), F32), jax.ShapeDtypeStruct((nsb, N_EXPERTS, LANES), F32)],
        scratch_shapes=[pltpu.VMEM((N_EXPERTS, 1), F32)],
        compiler_params=pltpu.CompilerParams(dimension_semantics=("arbitrary",), vmem_limit_bytes=VMEM_LIMIT),
        name="outproj",
    )(x2, gla_out, mla_out, w_out.astype(BF16), norm2_gain[None, :], wr_hi, wr_lo, br, tri)

    counts = cnt[:, :, 0].astype(jnp.int32)
    padded = (counts + 1) // 2 * 2
    offsets = jnp.cumsum(padded, axis=1) - padded
    sb_of_tok = jnp.arange(T, dtype=jnp.int32) // tb
    flat_off = offsets.reshape(-1)
    pos = [jnp.take(flat_off, sb_of_tok * N_EXPERTS + route[s].astype(jnp.int32)) + route[4 + s].astype(jnp.int32)
           for s in range(2)]
    pos = jnp.stack([p.reshape(nsb, tb) for p in pos], axis=1).reshape(-1)
    wts = jnp.stack([route[2], route[3]], axis=1)
    wge = w_expert_gate.reshape(N_EXPERTS, D, D_EXPERT).astype(BF16)
    wue = w_expert_up.reshape(N_EXPERTS, D, D_EXPERT).astype(BF16)
    wde = w_expert_down.reshape(N_EXPERTS, D_EXPERT, D).astype(BF16)
    nct = tb // MOE_COMBINE_TILE
    slots = (2 * tb + N_EXPERTS + MOE_CHUNK + 7) // 8 * 8
    eidx = lambda sb, j, off, cn: (jnp.clip(j - 1, 0, N_EXPERTS - 1), 0, 0)
    tidx = lambda sb, j, off, cn: (sb * nct + jnp.clip(j - N_EXPERTS - 1, 0, nct - 1), 0)
    out = pl.pallas_call(
        functools.partial(_moe_kernel, tb),
        grid_spec=pltpu.PrefetchScalarGridSpec(
            num_scalar_prefetch=2,
            grid=(nsb, 1 + N_EXPERTS + nct),
            in_specs=[pl.BlockSpec((tb * PACK_ROWS, LANES), lambda sb, j, off, cn: (sb, 0)),
                      pl.BlockSpec((2 * tb,), lambda sb, j, off, cn: (sb,), memory_space=pltpu.SMEM),
                      pl.BlockSpec((1, D, D_EXPERT), eidx), pl.BlockSpec((1, D, D_EXPERT), eidx),
                      pl.BlockSpec((1, D_EXPERT, D), eidx),
                      pl.BlockSpec((MOE_COMBINE_TILE, D), tidx), pl.BlockSpec((MOE_COMBINE_TILE, 2), tidx)],
            out_specs=pl.BlockSpec((MOE_COMBINE_TILE, D), tidx),
            scratch_shapes=[pltpu.VMEM((slots * PACK_ROWS, LANES), U32),
                            pltpu.VMEM((2, MOE_COMBINE_TILE * PACK_ROWS, LANES), U32)]),
        out_shape=jax.ShapeDtypeStruct((T, D), F32),
        compiler_params=pltpu.CompilerParams(dimension_semantics=("arbitrary", "arbitrary"),
                                             vmem_limit_bytes=VMEM_LIMIT),
        name="moe",
    )(flat_off, counts.reshape(-1), h2p, pos, wge, wue, wde, x1, wts)
    return out


def kernel(x, positions, norm1_gain, w_in, gla_gk_fwd_w, gla_gk_fwd_b, gla_gk_bwd_w, gla_gk_bwd_b, gla_out_gain, mla_q_gain, mla_w_qb, mla_kv_gain, mla_w_kvb, q_norm_gain, k_norm_gain, w_out, norm2_gain, w_router_group, b_router_group, w_router_expert, b_router_expert, w_expert_gate, w_expert_up, w_expert_down):
    batch, seq, d = x.shape
    x2 = x.reshape(batch * seq, d)
    pos2 = positions.reshape(batch * seq, 1)
    for l in range(norm1_gain.shape[0]):
        x2 = _layer(x2, pos2, batch, seq, norm1_gain[l], w_in[l], gla_gk_fwd_w[l], gla_gk_fwd_b[l],
                    gla_gk_bwd_w[l], gla_gk_bwd_b[l], gla_out_gain[l], mla_q_gain[l], mla_w_qb[l],
                    mla_kv_gain[l], mla_w_kvb[l], q_norm_gain[l], k_norm_gain[l], w_out[l], norm2_gain[l],
                    w_router_group[l], b_router_group[l], w_router_expert[l], b_router_expert[l],
                    w_expert_gate[l], w_expert_up[l], w_expert_down[l])
    return x2.reshape(batch, seq, d)
```

```python
import functools

import numpy as np
import jax
import jax.numpy as jnp
from jax import lax
from jax.experimental import pallas as pl
from jax.experimental.pallas import tpu as pltpu

F32 = jnp.float32
BF16 = jnp.bfloat16

D_MODEL = 1024
GLA_HEADS = 4
GLA_DK = 64
GLA_DV = 128
GLA_GATE_RANK = 16
GLA_GATE_NORMALIZER = 16.0
GLA_CHUNK = 64
MLA_HEADS = 8
MLA_NOPE = 64
MLA_ROPE = 32
MLA_QK = MLA_NOPE + MLA_ROPE
MLA_V = 64
MLA_Q_RANK = 256
MLA_KV_RANK = 128
ROPE_BASE = 10000.0
GLA_QK_WIDTH = GLA_HEADS * GLA_DK
GLA_WIDTH = GLA_HEADS * GLA_DV
MLA_WIDTH = MLA_HEADS * MLA_V
N_GROUPS = 4
EXPERTS_PER_GROUP = 8
N_EXPERTS = N_GROUPS * EXPERTS_PER_GROUP
D_EXPERT = 256
EPS = 1e-6

LANES = 128
HEAD_TILE = LANES
ROPE_LO = MLA_NOPE
ROPE_HALF = MLA_ROPE // 2

COL_GQ = 0
COL_GK = COL_GQ + GLA_QK_WIDTH
COL_GV = COL_GK + GLA_QK_WIDTH
COL_GG = COL_GV + GLA_WIDTH
COL_MQ = COL_GG + GLA_WIDTH
COL_MKV = COL_MQ + MLA_Q_RANK
COL_MISC = COL_MKV + MLA_KV_RANK
PROJ_WIDTH = COL_MISC + LANES
MISC_LRB = GLA_GATE_RANK

ROUTER_ROWS = LANES
ROUTER_EXPERT_ROW = 8

U32 = jnp.uint32
HI_HALF_MASK = np.uint32(0xFFFF0000)
PACK_ROWS = D_MODEL // (2 * LANES)
MOE_SUPER_BLOCK = 4096
MOE_CHUNK = 320
MOE_COMBINE_TILE = 512
MOE_ROW_UNROLL = 8

VMEM_LIMIT = 56 * 1024 * 1024


def _dot(a, b):
    return jnp.dot(a, b, preferred_element_type=F32)


def _dot_nt(a, b):
    return lax.dot_general(a, b, (((1,), (1,)), ((), ())), preferred_element_type=F32)


def _dot_tn(a, b):
    return lax.dot_general(a, b, (((0,), (0,)), ((), ())), preferred_element_type=F32)


def _split_bf16(x):
    hi = x.astype(BF16)
    lo = (x - hi.astype(F32)).astype(BF16)
    return hi, lo


def _proj_kernel(x_ref, pos_ref, n1_ref, win_ref, wg_ref, bg_ref, qgain_ref, wqb_ref, kvgain_ref,
                 wkvk_ref, wkvv_ref, qng_ref, kng_ref, freq_ref,
                 gq_ref, gk_ref, gv_ref, gg_ref, lg_ref, mq_ref, mk_ref, mv_ref):
    x = x_ref[...]
    h = x * lax.rsqrt(jnp.mean(x * x, axis=-1, keepdims=True) + EPS) * n1_ref[...]
    proj = _dot(h.astype(BF16), win_ref[...])

    gq_ref[...] = (proj[:, COL_GQ:COL_GK] * GLA_DK ** -0.5).astype(BF16)
    gk_ref[...] = proj[:, COL_GK:COL_GV].astype(BF16)
    gv_ref[...] = proj[:, COL_GV:COL_GG].astype(BF16)
    gg_ref[...] = proj[:, COL_GG:COL_MQ].astype(BF16)

    misc = proj[:, COL_MISC:PROJ_WIDTH]
    z = _dot(misc.astype(BF16), wg_ref[...]) + bg_ref[...]
    log_sig = -(jnp.maximum(-z, 0.0) + jnp.log1p(jnp.exp(-jnp.abs(z))))
    lg_ref[...] = log_sig / GLA_GATE_NORMALIZER

    lane = lax.broadcasted_iota(jnp.int32, (1, LANES), 1)
    ang = pos_ref[...].astype(F32) * freq_ref[...]
    cos = jnp.cos(ang)
    sin = jnp.sin(ang)
    in_rope = (lane >= ROPE_LO) & (lane < ROPE_LO + MLA_ROPE)
    c_tab = jnp.where(lane < ROPE_LO, 1.0, jnp.where(in_rope, cos, 0.0))
    s_up = jnp.where((lane >= ROPE_LO + ROPE_HALF) & (lane < ROPE_LO + MLA_ROPE), sin, 0.0)
    s_dn = jnp.where((lane >= ROPE_LO) & (lane < ROPE_LO + ROPE_HALF), -sin, 0.0)

    def norm_rope(t, gain):
        ss = jnp.sum(t * t, axis=-1, keepdims=True) * (1.0 / MLA_QK)
        tn = t * lax.rsqrt(ss + EPS) * gain
        return (tn * c_tab + pltpu.roll(tn, ROPE_HALF, 1) * s_up
                + pltpu.roll(tn, LANES - ROPE_HALF, 1) * s_dn)

    qa = proj[:, COL_MQ:COL_MKV]
    qn = qa * lax.rsqrt(jnp.mean(qa * qa, axis=-1, keepdims=True) + EPS) * qgain_ref[...]
    mq = _dot(qn.astype(BF16), wqb_ref[...])
    kva = proj[:, COL_MKV:COL_MISC]
    kvn = (kva * lax.rsqrt(jnp.mean(kva * kva, axis=-1, keepdims=True) + EPS) * kvgain_ref[...]).astype(BF16)
    kn = _dot(kvn, wkvk_ref[...])
    mv_ref[...] = _dot(kvn, wkvv_ref[...]).astype(BF16)
    rope_tile = jnp.where(in_rope, misc, 0.0)
    qng = qng_ref[...]
    kng = kng_ref[...]
    scale = MLA_QK ** -0.5
    for hd in range(MLA_HEADS):
        sl = slice(hd * HEAD_TILE, (hd + 1) * HEAD_TILE)
        mq_ref[0, hd] = (norm_rope(mq[:, sl], qng) * scale).astype(BF16)
        mk_ref[0, hd] = norm_rope(kn[:, sl] + rope_tile, kng).astype(BF16)


def _gla_kernel(q_ref, k_ref, v_ref, lg_ref, gate_ref, gain_ref, o_ref, acc_ref, st_ref):
    seq = q_ref.shape[0]
    C = GLA_CHUNK
    n_chunks = seq // C
    HK = GLA_QK_WIDTH

    row = lax.broadcasted_iota(jnp.int32, (C, C), 0)
    col = lax.broadcasted_iota(jnp.int32, (C, C), 1)
    tri_prefix = (col <= row).astype(BF16)
    tri_suffix = (col >= row).astype(BF16)
    lane_head = lax.broadcasted_iota(jnp.int32, (1, HK), 1) // GLA_DK
    srow = lax.broadcasted_iota(jnp.int32, (GLA_HEADS * C, C), 0) % C
    scol = lax.broadcasted_iota(jnp.int32, (GLA_HEADS * C, C), 1)
    gain = gain_ref[...]

    def stack_heads(t):
        return jnp.concatenate([jnp.where(lane_head == hd, t, 0.0) for hd in range(GLA_HEADS)], axis=0)

    def chunk_step(c, tri, ref_i, last_i, keep, lg_off, finish):
        r0 = pl.multiple_of(c * C, C)
        rows = pl.ds(r0, C)
        g = lg_ref[rows, lg_off:lg_off + HK]
        g_hi, g_lo = _split_bf16(g)
        b = _dot(tri, g_hi) + _dot(tri, g_lo)
        b_ref = b[ref_i:ref_i + 1, :]
        b_last = b[last_i:last_i + 1, :]
        q = q_ref[rows, :].astype(F32)
        k = k_ref[rows, :].astype(F32)
        v = v_ref[rows, :]
        q_rel = q * jnp.exp(b - b_ref)
        k_rel = (k * jnp.exp(b_ref - b)).astype(BF16)
        k_dec = (k * jnp.exp(b_last - b)).astype(BF16)
        q_dec = q * jnp.exp(b)
        st = st_ref[...]
        sc = _dot_nt(stack_heads(q_rel).astype(BF16), k_rel)
        sc = jnp.where(keep, sc, 0.0).astype(BF16)
        o_inter = _dot_nt(stack_heads(q_dec).astype(BF16), st.astype(BF16))
        kv_t = _dot_tn(v, k_dec)
        new_st = jnp.exp(b_last) * st
        for hd in range(GLA_HEADS):
            vs = slice(hd * GLA_DV, (hd + 1) * GLA_DV)
            rs = slice(hd * C, (hd + 1) * C)
            o_h = _dot(sc[rs, :], v[:, vs]) + o_inter[rs, :]
            new_st = new_st + jnp.where(lane_head == hd, kv_t[vs, :], 0.0)
            finish(rows, vs, o_h)
        st_ref[...] = new_st

    def store_fwd(rows, vs, o_h):
        acc_ref[rows, vs] = o_h

    def store_final(rows, vs, o_h):
        o = acc_ref[rows, vs] + o_h
        on = o * lax.rsqrt(jnp.mean(o * o, axis=-1, keepdims=True) + EPS) * gain
        gt = gate_ref[rows, vs].astype(F32)
        o_ref[rows, vs] = (on * (gt * jax.nn.sigmoid(gt))).astype(BF16)

    st_ref[...] = jnp.zeros_like(st_ref)

    def fwd_body(c, carry):
        chunk_step(c, tri_prefix, C // 2 - 1, C - 1, scol <= srow, 0, store_fwd)
        return carry

    lax.fori_loop(0, n_chunks, fwd_body, 0)

    st_ref[...] = jnp.zeros_like(st_ref)

    def bwd_body(i, carry):
        chunk_step(n_chunks - 1 - i, tri_suffix, C // 2, 0, scol > srow, HK, store_final)
        return carry

    lax.fori_loop(0, n_chunks, bwd_body, 0)


def _attn_kernel(q_ref, k_ref, v_ref, o_ref):
    outs = []
    for j in range(2):
        s = _dot_nt(q_ref[0, j], k_ref[0, j])
        m = jnp.max(s, axis=-1, keepdims=True)
        p = jnp.exp(s - m)
        l = jnp.sum(p, axis=-1, keepdims=True)
        outs.append(_dot(p.astype(BF16), v_ref[...]) / l)
    lane = lax.broadcasted_iota(jnp.int32, (1, LANES), 1)
    o_ref[...] = jnp.where(lane < MLA_V, outs[0], outs[1]).astype(BF16)


def _unpack_bf16_pair(w):
    lo = pltpu.unpack_elementwise(w, index=0, packed_dtype=BF16, unpacked_dtype=F32)
    hi = pltpu.unpack_elementwise(w, index=1, packed_dtype=BF16, unpacked_dtype=F32)
    return lo, hi


def _pack_bf16_pair(lo, hi):
    return pltpu.pack_elementwise([lo, hi], packed_dtype=BF16)


def _outproj_kernel(tiles_per_sb, x_ref, gla_ref, mla_ref, wo_ref, n2_ref, wr_hi_ref, wr_lo_ref, br_ref, tri_ref,
                    x1_ref, h2_ref, route_ref, cnt_ref, carry_ref):
    mix = _dot(gla_ref[...], wo_ref[0:GLA_WIDTH, :]) + _dot(mla_ref[...], wo_ref[GLA_WIDTH:, :])
    x1 = x_ref[...] + mix
    x1_ref[...] = x1
    h2 = x1 * lax.rsqrt(jnp.mean(x1 * x1, axis=-1, keepdims=True) + EPS) * n2_ref[...]
    h2_ref[...] = h2.astype(BF16)

    h_hi, h_lo = _split_bf16(h2)
    w_hi = wr_hi_ref[...]
    logits = (_dot_nt(w_hi, h_hi) + _dot_nt(w_hi, h_lo) + _dot_nt(wr_lo_ref[...], h_hi)) + br_ref[...]
    tm = logits.shape[1]
    gl = logits[0:N_GROUPS, :]
    ge = jnp.exp(gl - jnp.max(gl, axis=0, keepdims=True))
    pg = ge / jnp.sum(ge, axis=0, keepdims=True)
    p_top = jnp.max(pg, axis=0, keepdims=True)
    gi = lax.broadcasted_iota(jnp.int32, (N_GROUPS, tm), 0)
    g_idx = jnp.min(jnp.where(pg == p_top, gi, N_GROUPS), axis=0, keepdims=True)
    sel = jnp.zeros((EXPERTS_PER_GROUP, tm), F32)
    for g in range(N_GROUPS):
        r0 = ROUTER_EXPERT_ROW + g * EXPERTS_PER_GROUP
        sel = sel + jnp.where(g_idx == g, logits[r0:r0 + EXPERTS_PER_GROUP, :], 0.0)
    se = jnp.exp(sel - jnp.max(sel, axis=0, keepdims=True))
    pe = se / jnp.sum(se, axis=0, keepdims=True)
    ei = lax.broadcasted_iota(jnp.int32, (EXPERTS_PER_GROUP, tm), 0)
    m1 = jnp.max(pe, axis=0, keepdims=True)
    i1 = jnp.min(jnp.where(pe == m1, ei, EXPERTS_PER_GROUP), axis=0, keepdims=True)
    pe2 = jnp.where(ei == i1, -1.0, pe)
    m2 = jnp.max(pe2, axis=0, keepdims=True)
    i2 = jnp.min(jnp.where(pe2 == m2, ei, EXPERTS_PER_GROUP), axis=0, keepdims=True)
    den = m1 + m2
    e1 = g_idx * EXPERTS_PER_GROUP + i1
    e2 = g_idx * EXPERTS_PER_GROUP + i2

    @pl.when(pl.program_id(0) % tiles_per_sb == 0)
    def _():
        carry_ref[...] = jnp.zeros_like(carry_ref)

    eall = lax.broadcasted_iota(jnp.int32, (N_EXPERTS, tm), 0)
    tri = tri_ref[...]
    carry = carry_ref[...]
    ranks = []
    for eid in (e1, e2):
        hot = eall == eid
        prefix = _dot(hot.astype(BF16), tri)
        ranks.append(jnp.sum(jnp.where(hot, carry + prefix - 1.0, 0.0), axis=0, keepdims=True))
        carry = carry + prefix[:, tm - 1:tm]
    carry_ref[...] = carry
    cnt_ref[0] = jnp.broadcast_to(carry, (N_EXPERTS, LANES))
    zeros = jnp.zeros((2, tm), F32)
    route_ref[...] = jnp.concatenate([e1.astype(F32), e2.astype(F32), p_top * (m1 / den), p_top * (m2 / den),
                                      ranks[0], ranks[1], zeros], axis=0)


def _slots_kernel(route_ref, cnt_ref, strict_ref, pos_ref, off_ref):
    counts = cnt_ref[0]
    padded = 2.0 * jnp.floor(counts * 0.5 + 0.5)
    hi = jnp.floor(padded * (1.0 / 256.0))
    lo = padded - 256.0 * hi
    strict = strict_ref[...]
    off = 256.0 * _dot(strict, hi.astype(BF16)) + _dot(strict, lo.astype(BF16))
    off_ref[0] = off.astype(jnp.int32)
    tb = route_ref.shape[1]
    eall = lax.broadcasted_iota(jnp.int32, (N_EXPERTS, tb), 0)
    off_col = off[:, 0:1]
    for s in range(2):
        hot = eall == route_ref[s:s + 1, :].astype(jnp.int32)
        base = jnp.sum(jnp.where(hot, off_col, 0.0), axis=0, keepdims=True)
        pos_ref[0, s:s + 1, :] = (base + route_ref[4 + s:5 + s, :]).astype(jnp.int32)


def _moe_kernel(tb, off_ref, cnt_ref, h2_ref, pos_ref, wg_ref, wu_ref, wd_ref, x1_ref, wt_ref, o_ref,
                xy_ref, cb_ref):
    sb = pl.program_id(0)
    j = pl.program_id(1)

    @pl.when(j == 0)
    def _scatter():
        xy_ref[...] = jnp.zeros_like(xy_ref)

        def tile_body(ti, c):
            t0 = pl.multiple_of(ti * MOE_COMBINE_TILE, MOE_COMBINE_TILE)
            h = h2_ref[pl.ds(t0, MOE_COMBINE_TILE), :].astype(F32)
            for r in range(PACK_ROWS):
                c0 = r * 2 * LANES
                cb_ref[0, pl.ds(r, MOE_COMBINE_TILE, stride=PACK_ROWS), :] = _pack_bf16_pair(
                    h[:, c0:c0 + LANES], h[:, c0 + LANES:c0 + 2 * LANES])

            def body(g, c2):
                for i in range(MOE_ROW_UNROLL // 2):
                    pair = g * (MOE_ROW_UNROLL // 2) + i
                    v = cb_ref[0, pl.ds(pl.multiple_of(pair * 2 * PACK_ROWS, 2 * PACK_ROWS), 2 * PACK_ROWS), :]
                    for half in range(2):
                        t = t0 + pair * 2 + half
                        piece = v[half * PACK_ROWS:(half + 1) * PACK_ROWS, :]
                        for slot in range(2):
                            p = pos_ref[slot * tb + t]
                            xy_ref[pl.ds(pl.multiple_of(p * PACK_ROWS, PACK_ROWS), PACK_ROWS), :] = piece
                return c2

            lax.fori_loop(0, MOE_COMBINE_TILE // MOE_ROW_UNROLL, body, 0)
            return c

        lax.fori_loop(0, tb // MOE_COMBINE_TILE, tile_body, 0)

    @pl.when((j >= 1) & (j <= N_EXPERTS))
    def _experts():
        e = sb * N_EXPERTS + j - 1
        n = cnt_ref[e]
        base = off_ref[e]

        def body(i, c):
            r0 = base + i * MOE_CHUNK
            row0 = pl.multiple_of(r0 * PACK_ROWS, 2 * PACK_ROWS)
            halves = [_unpack_bf16_pair(xy_ref[pl.ds(row0 + r, MOE_CHUNK, stride=PACK_ROWS), :])
                      for r in range(PACK_ROWS)]
            a = jnp.zeros((MOE_CHUNK, D_EXPERT), F32)
            u = jnp.zeros((MOE_CHUNK, D_EXPERT), F32)
            for r in range(PACK_ROWS):
                lo, hi = halves[r]
                xb = jnp.concatenate([lo, hi], axis=1).astype(BF16)
                a = a + _dot(xb, wg_ref[0, r * 2 * LANES:(r + 1) * 2 * LANES, :])
                u = u + _dot(xb, wu_ref[0, r * 2 * LANES:(r + 1) * 2 * LANES, :])
            hid = ((a * jax.nn.sigmoid(a)) * u).astype(BF16)
            y = _dot(hid, wd_ref[0])
            valid = (r0 + lax.broadcasted_iota(jnp.int32, (MOE_CHUNK, 1), 0)) < (base + n)
            for r in range(PACK_ROWS):
                c0 = r * 2 * LANES
                lo, hi = halves[r]
                xy_ref[pl.ds(row0 + r, MOE_CHUNK, stride=PACK_ROWS), :] = _pack_bf16_pair(
                    jnp.where(valid, y[:, c0:c0 + LANES], lo), jnp.where(valid, y[:, c0 + LANES:c0 + 2 * LANES], hi))
            return c

        lax.fori_loop(0, (n + MOE_CHUNK - 1) // MOE_CHUNK, body, 0)

    @pl.when(j > N_EXPERTS)
    def _combine():
        t0 = (j - N_EXPERTS - 1) * MOE_COMBINE_TILE

        def body(g, c):
            for i in range(MOE_ROW_UNROLL):
                t = g * MOE_ROW_UNROLL + i
                for slot in range(2):
                    p = pos_ref[slot * tb + t0 + t]
                    cb_ref[slot, pl.ds(pl.multiple_of(t * PACK_ROWS, PACK_ROWS), PACK_ROWS), :] = (
                        xy_ref[pl.ds(pl.multiple_of(p * PACK_ROWS, PACK_ROWS), PACK_ROWS), :])
            return c

        lax.fori_loop(0, MOE_COMBINE_TILE // MOE_ROW_UNROLL, body, 0)
        w1 = wt_ref[:, 0:1]
        w2 = wt_ref[:, 1:2]
        for r in range(PACK_ROWS):
            lo1, hi1 = _unpack_bf16_pair(cb_ref[0, pl.ds(r, MOE_COMBINE_TILE, stride=PACK_ROWS), :])
            lo2, hi2 = _unpack_bf16_pair(cb_ref[1, pl.ds(r, MOE_COMBINE_TILE, stride=PACK_ROWS), :])
            c0 = r * 2 * LANES
            o_ref[:, c0:c0 + LANES] = x1_ref[:, c0:c0 + LANES] + w1 * lo1 + w2 * lo2
            o_ref[:, c0 + LANES:c0 + 2 * LANES] = x1_ref[:, c0 + LANES:c0 + 2 * LANES] + w1 * hi1 + w2 * hi2


def _full(shape):
    return pl.BlockSpec(shape, lambda *_: (0,) * len(shape))


def _prep_weights(w_in, gk_fwd_w, gk_fwd_b, gk_bwd_w, gk_bwd_b, mla_w_qb, mla_w_kvb, q_norm_gain, k_norm_gain):
    splits = np.cumsum([0, GLA_QK_WIDTH, GLA_QK_WIDTH, GLA_WIDTH, GLA_WIDTH, GLA_GATE_RANK, GLA_GATE_RANK,
                        MLA_Q_RANK, MLA_KV_RANK, MLA_ROPE])
    seg = [w_in[:, splits[i]:splits[i + 1]] for i in range(9)]
    d = w_in.shape[0]
    misc = jnp.zeros((d, LANES), F32)
    misc = misc.at[:, 0:GLA_GATE_RANK].set(seg[4])
    misc = misc.at[:, MISC_LRB:MISC_LRB + GLA_GATE_RANK].set(seg[5])
    misc = misc.at[:, ROPE_LO:ROPE_LO + MLA_ROPE].set(seg[8])
    win = jnp.concatenate([seg[0], seg[1], seg[2], seg[3], seg[6], seg[7], misc], axis=1).astype(BF16)

    wg = jnp.zeros((LANES, 2 * GLA_QK_WIDTH), F32)
    wg = wg.at[0:GLA_GATE_RANK, 0:GLA_QK_WIDTH].set(gk_fwd_w)
    wg = wg.at[MISC_LRB:MISC_LRB + GLA_GATE_RANK, GLA_QK_WIDTH:].set(gk_bwd_w)
    bg = jnp.concatenate([gk_fwd_b, gk_bwd_b])[None, :]

    pad = HEAD_TILE - MLA_QK
    wqb = jnp.pad(mla_w_qb.reshape(MLA_Q_RANK, MLA_HEADS, MLA_QK), ((0, 0), (0, 0), (0, pad)))
    wqb = wqb.reshape(MLA_Q_RANK, MLA_HEADS * HEAD_TILE).astype(BF16)
    wkv = mla_w_kvb.reshape(MLA_KV_RANK, MLA_HEADS, MLA_NOPE + MLA_V)
    wkvk = jnp.pad(wkv[:, :, :MLA_NOPE], ((0, 0), (0, 0), (0, HEAD_TILE - MLA_NOPE)))
    wkvk = wkvk.reshape(MLA_KV_RANK, MLA_HEADS * HEAD_TILE).astype(BF16)
    wkvv = wkv[:, :, MLA_NOPE:].reshape(MLA_KV_RANK, MLA_WIDTH).astype(BF16)
    qng = jnp.pad(q_norm_gain, (0, pad))[None, :]
    kng = jnp.pad(k_norm_gain, (0, pad))[None, :]
    return win, wg.astype(BF16), bg, wqb, wkvk, wkvv, qng, kng


def _rope_freq():
    inv = ROPE_BASE ** (-np.arange(0, MLA_ROPE, 2, dtype=np.float32) / MLA_ROPE)
    tab = np.zeros((1, LANES), np.float32)
    tab[0, ROPE_LO:ROPE_LO + ROPE_HALF] = inv
    tab[0, ROPE_LO + ROPE_HALF:ROPE_LO + MLA_ROPE] = inv
    return jnp.asarray(tab)


def _layer(x2, pos2, batch, seq, norm1_gain, w_in, gk_fwd_w, gk_fwd_b, gk_bwd_w, gk_bwd_b, gla_out_gain,
           mla_q_gain, mla_w_qb, mla_kv_gain, mla_w_kvb, q_norm_gain, k_norm_gain, w_out, norm2_gain,
           w_router_group, b_router_group, w_router_expert, b_router_expert,
           w_expert_gate, w_expert_up, w_expert_down):
    T, D = x2.shape
    win, wg, bg, wqb, wkvk, wkvv, qng, kng = _prep_weights(
        w_in, gk_fwd_w, gk_fwd_b, gk_bwd_w, gk_bwd_b, mla_w_qb, mla_w_kvb, q_norm_gain, k_norm_gain)

    TM = 512
    spb = seq // TM
    tok = lambda w: pl.BlockSpec((TM, w), lambda i: (i, 0))
    head_spec = pl.BlockSpec((1, MLA_HEADS, TM, HEAD_TILE), lambda i: (i // spb, 0, i % spb, 0))
    gq, gk, gv, gg, lg, mq, mk, mv = pl.pallas_call(
        _proj_kernel,
        grid=(T // TM,),
        in_specs=[tok(D), tok(1), _full((1, D)), _full((D, PROJ_WIDTH)), _full((LANES, 2 * GLA_QK_WIDTH)),
                  _full((1, 2 * GLA_QK_WIDTH)), _full((1, MLA_Q_RANK)), _full((MLA_Q_RANK, MLA_HEADS * HEAD_TILE)),
                  _full((1, MLA_KV_RANK)), _full((MLA_KV_RANK, MLA_HEADS * HEAD_TILE)),
                  _full((MLA_KV_RANK, MLA_WIDTH)), _full((1, HEAD_TILE)), _full((1, HEAD_TILE)),
                  _full((1, LANES))],
        out_specs=[tok(GLA_QK_WIDTH), tok(GLA_QK_WIDTH), tok(GLA_WIDTH), tok(GLA_WIDTH), tok(2 * GLA_QK_WIDTH),
                   head_spec, head_spec, tok(MLA_WIDTH)],
        out_shape=[jax.ShapeDtypeStruct((T, GLA_QK_WIDTH), BF16), jax.ShapeDtypeStruct((T, GLA_QK_WIDTH), BF16),
                   jax.ShapeDtypeStruct((T, GLA_WIDTH), BF16), jax.ShapeDtypeStruct((T, GLA_WIDTH), BF16),
                   jax.ShapeDtypeStruct((T, 2 * GLA_QK_WIDTH), F32),
                   jax.ShapeDtypeStruct((batch, MLA_HEADS, seq, HEAD_TILE), BF16),
                   jax.ShapeDtypeStruct((batch, MLA_HEADS, seq, HEAD_TILE), BF16),
                   jax.ShapeDtypeStruct((T, MLA_WIDTH), BF16)],
        compiler_params=pltpu.CompilerParams(dimension_semantics=("parallel",), vmem_limit_bytes=VMEM_LIMIT),
        name="proj",
    )(x2, pos2, norm1_gain[None, :], win, wg, bg, mla_q_gain[None, :], wqb, mla_kv_gain[None, :],
      wkvk, wkvv, qng, kng, _rope_freq())

    seqspec = lambda w: pl.BlockSpec((seq, w), lambda b: (b, 0))
    gla_out = pl.pallas_call(
        _gla_kernel,
        grid=(batch,),
        in_specs=[seqspec(GLA_QK_WIDTH), seqspec(GLA_QK_WIDTH), seqspec(GLA_WIDTH), seqspec(2 * GLA_QK_WIDTH),
                  seqspec(GLA_WIDTH), _full((1, GLA_DV))],
        out_specs=seqspec(GLA_WIDTH),
        out_shape=jax.ShapeDtypeStruct((T, GLA_WIDTH), BF16),
        scratch_shapes=[pltpu.VMEM((seq, GLA_WIDTH), F32), pltpu.VMEM((GLA_DV, GLA_QK_WIDTH), F32)],
        compiler_params=pltpu.CompilerParams(dimension_semantics=("parallel",), vmem_limit_bytes=VMEM_LIMIT),
        name="gla",
    )(gq, gk, gv, lg, gg, gla_out_gain[None, :])

    TQ = 256
    nq = seq // TQ
    mla_out = pl.pallas_call(
        _attn_kernel,
        grid=(batch, MLA_HEADS // 2, nq),
        in_specs=[pl.BlockSpec((1, 2, TQ, HEAD_TILE), lambda b, hp, i: (b, hp, i, 0)),
                  pl.BlockSpec((1, 2, seq, HEAD_TILE), lambda b, hp, i: (b, hp, 0, 0)),
                  pl.BlockSpec((seq, LANES), lambda b, hp, i: (b, hp))],
        out_specs=pl.BlockSpec((TQ, LANES), lambda b, hp, i: (b * nq + i, hp)),
        out_shape=jax.ShapeDtypeStruct((T, MLA_WIDTH), BF16),
        compiler_params=pltpu.CompilerParams(dimension_semantics=("parallel", "parallel", "parallel"),
                                             vmem_limit_bytes=VMEM_LIMIT),
        name="attn",
    )(mq, mk, mv)

    wr = jnp.zeros((ROUTER_ROWS, D), F32)
    wr = wr.at[0:N_GROUPS].set(w_router_group.T)
    wr = wr.at[ROUTER_EXPERT_ROW:ROUTER_EXPERT_ROW + N_EXPERTS].set(w_router_expert.T)
    wr_hi, wr_lo = _split_bf16(wr)
    br = jnp.zeros((ROUTER_ROWS, 1), F32)
    br = br.at[0:N_GROUPS, 0].set(b_router_group)
    br = br.at[ROUTER_EXPERT_ROW:ROUTER_EXPERT_ROW + N_EXPERTS, 0].set(b_router_expert)
    tb = min(MOE_SUPER_BLOCK, T)
    nsb = T // tb
    tiles_per_sb = tb // TM
    tri = jnp.asarray(np.triu(np.ones((TM, TM), np.float32)), BF16)
    x1, h2, route, cnt = pl.pallas_call(
        functools.partial(_outproj_kernel, tiles_per_sb),
        grid=(T // TM,),
        in_specs=[tok(D), tok(GLA_WIDTH), tok(MLA_WIDTH), _full((D, D)), _full((1, D)),
                  _full((ROUTER_ROWS, D)), _full((ROUTER_ROWS, D)), _full((ROUTER_ROWS, 1)), _full((TM, TM))],
        out_specs=[tok(D), tok(D), pl.BlockSpec((8, TM), lambda i: (0, i)),
                   pl.BlockSpec((1, N_EXPERTS, LANES), lambda i: (i // tiles_per_sb, 0, 0))],
        out_shape=[jax.ShapeDtypeStruct((T, D), F32), jax.ShapeDtypeStruct((T, D), BF16),
                   jax.ShapeDtypeStruct((8, T), F32), jax.ShapeDtypeStruct((nsb, N_EXPERTS, LANES), F32)],
        scratch_shapes=[pltpu.VMEM((N_EXPERTS, 1), F32)],
        compiler_params=pltpu.CompilerParams(dimension_semantics=("arbitrary",), vmem_limit_bytes=VMEM_LIMIT),
        name="outproj",
    )(x2, gla_out, mla_out, w_out.astype(BF16), norm2_gain[None, :], wr_hi, wr_lo, br, tri)

    strict = jnp.asarray(np.tril(np.ones((N_EXPERTS, N_EXPERTS), np.float32), -1), BF16)
    pos, offs = pl.pallas_call(
        _slots_kernel,
        grid=(nsb,),
        in_specs=[pl.BlockSpec((8, tb), lambda s: (0, s)), pl.BlockSpec((1, N_EXPERTS, LANES), lambda s: (s, 0, 0)),
                  _full((N_EXPERTS, N_EXPERTS))],
        out_specs=[pl.BlockSpec((1, 2, tb), lambda s: (s, 0, 0)),
                   pl.BlockSpec((1, N_EXPERTS, LANES), lambda s: (s, 0, 0))],
        out_shape=[jax.ShapeDtypeStruct((nsb, 2, tb), jnp.int32),
                   jax.ShapeDtypeStruct((nsb, N_EXPERTS, LANES), jnp.int32)],
        compiler_params=pltpu.CompilerParams(dimension_semantics=("parallel",)),
        name="slots",
    )(route, cnt, strict)
    pos = pos.reshape(-1)
    flat_off = offs[:, :, 0].reshape(-1)
    counts = cnt[:, :, 0].astype(jnp.int32)
    wts = jnp.stack([route[2], route[3]], axis=1)
    wge = w_expert_gate.reshape(N_EXPERTS, D, D_EXPERT).astype(BF16)
    wue = w_expert_up.reshape(N_EXPERTS, D, D_EXPERT).astype(BF16)
    wde = w_expert_down.reshape(N_EXPERTS, D_EXPERT, D).astype(BF16)
    nct = tb // MOE_COMBINE_TILE
    slots = (2 * tb + N_EXPERTS + MOE_CHUNK + 7) // 8 * 8
    eidx = lambda sb, j, off, cn: (jnp.clip(j - 1, 0, N_EXPERTS - 1), 0, 0)
    tidx = lambda sb, j, off, cn: (sb * nct + jnp.clip(j - N_EXPERTS - 1, 0, nct - 1), 0)
    out = pl.pallas_call(
        functools.partial(_moe_kernel, tb),
        grid_spec=pltpu.PrefetchScalarGridSpec(
            num_scalar_prefetch=2,
            grid=(nsb, 1 + N_EXPERTS + nct),
            in_specs=[pl.BlockSpec((tb, D), lambda sb, j, off, cn: (sb, 0)),
                      pl.BlockSpec((2 * tb,), lambda sb, j, off, cn: (sb,), memory_space=pltpu.SMEM),
                      pl.BlockSpec((1, D, D_EXPERT), eidx), pl.BlockSpec((1, D, D_EXPERT), eidx),
                      pl.BlockSpec((1, D_EXPERT, D), eidx),
                      pl.BlockSpec((MOE_COMBINE_TILE, D), tidx), pl.BlockSpec((MOE_COMBINE_TILE, 2), tidx)],
            out_specs=pl.BlockSpec((MOE_COMBINE_TILE, D), tidx),
            scratch_shapes=[pltpu.VMEM((slots * PACK_ROWS, LANES), U32),
                            pltpu.VMEM((2, MOE_COMBINE_TILE * PACK_ROWS, LANES), U32)]),
        out_shape=jax.ShapeDtypeStruct((T, D), F32),
        compiler_params=pltpu.CompilerParams(dimension_semantics=("arbitrary", "arbitrary"),
                                             vmem_limit_bytes=VMEM_LIMIT),
        name="moe",
    )(flat_off, counts.reshape(-1), h2, pos, wge, wue, wde, x1, wts)
    return out


def kernel(x, positions, norm1_gain, w_in, gla_gk_fwd_w, gla_gk_fwd_b, gla_gk_bwd_w, gla_gk_bwd_b, gla_out_gain, mla_q_gain, mla_w_qb, mla_kv_gain, mla_w_kvb, q_norm_gain, k_norm_gain, w_out, norm2_gain, w_router_group, b_router_group, w_router_expert, b_router_expert, w_expert_gate, w_expert_up, w_expert_down):
    batch, seq, d = x.shape
    x2 = x.reshape(batch * seq, d)
    pos2 = positions.reshape(batch * seq, 1)
    for l in range(norm1_gain.shape[0]):
        x2 = _layer(x2, pos2, batch, seq, norm1_gain[l], w_in[l], gla_gk_fwd_w[l], gla_gk_fwd_b[l],
                    gla_gk_bwd_w[l], gla_gk_bwd_b[l], gla_out_gain[l], mla_q_gain[l], mla_w_qb[l],
                    mla_kv_gain[l], mla_w_kvb[l], q_norm_gain[l], k_norm_gain[l], w_out[l], norm2_gain[l],
                    w_router_group[l], b_router_group[l], w_router_expert[l], b_router_expert[l],
                    w_expert_gate[l], w_expert_up[l], w_expert_down[l])
    return x2.reshape(batch, seq, d)
```

```python
import functools

import numpy as np
import jax
import jax.numpy as jnp
from jax import lax
from jax.experimental import pallas as pl
from jax.experimental.pallas import tpu as pltpu

F32 = jnp.float32
BF16 = jnp.bfloat16

D_MODEL = 1024
GLA_HEADS = 4
GLA_DK = 64
GLA_DV = 128
GLA_GATE_RANK = 16
GLA_GATE_NORMALIZER = 16.0
GLA_CHUNK = 64
GLA_CHUNK_UNROLL = 4
GLA_FINISH_ROWS = 256
MLA_HEADS = 8
MLA_NOPE = 64
MLA_ROPE = 32
MLA_QK = MLA_NOPE + MLA_ROPE
MLA_V = 64
MLA_Q_RANK = 256
MLA_KV_RANK = 128
ROPE_BASE = 10000.0
GLA_QK_WIDTH = GLA_HEADS * GLA_DK
GLA_WIDTH = GLA_HEADS * GLA_DV
MLA_WIDTH = MLA_HEADS * MLA_V
N_GROUPS = 4
EXPERTS_PER_GROUP = 8
N_EXPERTS = N_GROUPS * EXPERTS_PER_GROUP
D_EXPERT = 256
EPS = 1e-6

LANES = 128
HEAD_TILE = LANES
ROPE_LO = MLA_NOPE
ROPE_HALF = MLA_ROPE // 2
LOG2_E = 1.4426950408889634

COL_GQ = 0
COL_GK = COL_GQ + GLA_QK_WIDTH
COL_GV = COL_GK + GLA_QK_WIDTH
COL_GG = COL_GV + GLA_WIDTH
COL_MQ = COL_GG + GLA_WIDTH
COL_MKV = COL_MQ + MLA_Q_RANK
COL_MISC = COL_MKV + MLA_KV_RANK
PROJ_WIDTH = COL_MISC + LANES
MISC_LRB = GLA_GATE_RANK

ROUTER_ROWS = LANES
ROUTER_EXPERT_ROW = 8

U32 = jnp.uint32
HI_HALF_MASK = np.uint32(0xFFFF0000)
PACK_ROWS = D_MODEL // (2 * LANES)
MOE_SUPER_BLOCK = 4096
MOE_CHUNK = 320
MOE_COMBINE_TILE = 512
MOE_ROW_UNROLL = 8

VMEM_LIMIT = 56 * 1024 * 1024


def _dot(a, b):
    return jnp.dot(a, b, preferred_element_type=F32)


def _dot_nt(a, b):
    return lax.dot_general(a, b, (((1,), (1,)), ((), ())), preferred_element_type=F32)


def _dot_tn(a, b):
    return lax.dot_general(a, b, (((0,), (0,)), ((), ())), preferred_element_type=F32)


def _split_bf16(x):
    hi = x.astype(BF16)
    lo = (x - hi.astype(F32)).astype(BF16)
    return hi, lo


def _proj_kernel(x_ref, pos_ref, n1_ref, win_ref, wg_ref, bg_ref, qgain_ref, wqb_ref, wqbr_ref, kvgain_ref,
                 wkvk_ref, wkvv_ref, qng_ref, qngr_ref, kng_ref, kngr_ref, freq_ref, place_ref, ones_ref,
                 gq_ref, gk_ref, gv_ref, gg_ref, lg_ref, mq_ref, mk_ref, mv_ref):
    x = x_ref[...]
    h = x * lax.rsqrt(jnp.mean(x * x, axis=-1, keepdims=True) + EPS) * n1_ref[...]
    proj = _dot(h.astype(BF16), win_ref[...])

    gq_ref[...] = (proj[:, COL_GQ:COL_GK] * GLA_DK ** -0.5).astype(BF16)
    gk_ref[...] = proj[:, COL_GK:COL_GV].astype(BF16)
    gv_ref[...] = proj[:, COL_GV:COL_GG].astype(BF16)
    gg_ref[...] = proj[:, COL_GG:COL_MQ].astype(BF16)

    misc = proj[:, COL_MISC:PROJ_WIDTH]
    z = _dot(misc.astype(BF16), wg_ref[...]) + bg_ref[...]
    log_sig = -(jnp.maximum(-z, 0.0) + jnp.log(1.0 + jnp.exp(-jnp.abs(z))))
    lg_ref[...] = log_sig / GLA_GATE_NORMALIZER

    lane = lax.broadcasted_iota(jnp.int32, (1, LANES), 1)
    ang_t = freq_ref[...] * pos_ref[0].astype(F32)
    place = place_ref[...]

    def to_rows(tab_t):
        hi, lo = _split_bf16(tab_t)
        return _dot_tn(hi, place) + _dot_tn(lo, place)

    in_rope = (lane >= ROPE_LO) & (lane < ROPE_LO + MLA_ROPE)
    c_tab = jnp.where(lane < ROPE_LO, 1.0, to_rows(jnp.cos(ang_t)))
    s_tab = to_rows(jnp.sin(ang_t))
    ones_bd = ones_ref[...]

    def norm_rope_pair(t2, r2, gc, gs):
        ss = _dot((t2 * t2).astype(BF16), ones_bd) * (1.0 / MLA_QK)
        rs = lax.rsqrt(ss + EPS)
        outs = []
        for i in range(2):
            sl = slice(i * LANES, (i + 1) * LANES)
            outs.append(rs[:, sl] * (t2[:, sl] * gc + r2[:, sl] * gs))
        return outs

    qa = proj[:, COL_MQ:COL_MKV]
    qn = qa * lax.rsqrt(jnp.mean(qa * qa, axis=-1, keepdims=True) + EPS) * qgain_ref[...]
    qn = qn.astype(BF16)
    mq = _dot(qn, wqb_ref[...])
    mq_rot = _dot(qn, wqbr_ref[...])
    kva = proj[:, COL_MKV:COL_MISC]
    kvn = (kva * lax.rsqrt(jnp.mean(kva * kva, axis=-1, keepdims=True) + EPS) * kvgain_ref[...]).astype(BF16)
    kn = _dot(kvn, wkvk_ref[...])
    mv = _dot(kvn, wkvv_ref[...])
    for hd in range(MLA_HEADS):
        pair = mv[:, (hd // 2) * LANES:(hd // 2 + 1) * LANES]
        if hd % 2 == 0:
            tile = jnp.where(lane < MLA_V, pair, jnp.where(lane == MLA_V, 1.0, 0.0))
        else:
            tile = jnp.where(lane >= MLA_V, pair, jnp.where(lane == 0, 1.0, 0.0))
        mv_ref[0, hd] = tile.astype(BF16)
    rope_tile = jnp.where(in_rope, misc, 0.0)
    up = (lane >= ROPE_LO + ROPE_HALF) & (lane < ROPE_LO + MLA_ROPE)
    dn = (lane >= ROPE_LO) & (lane < ROPE_LO + ROPE_HALF)
    rope_rot = (jnp.where(up, pltpu.roll(rope_tile, ROPE_HALF, 1), 0.0)
                - jnp.where(dn, pltpu.roll(rope_tile, LANES - ROPE_HALF, 1), 0.0))
    rope2 = jnp.concatenate([rope_tile, rope_tile], axis=1)
    rope_rot2 = jnp.concatenate([rope_rot, rope_rot], axis=1)
    scale = MLA_QK ** -0.5 * LOG2_E
    q_gc, q_gs = (qng_ref[...] * scale) * c_tab, (qngr_ref[...] * scale) * s_tab
    k_gc, k_gs = kng_ref[...] * c_tab, kngr_ref[...] * s_tab
    for hp in range(MLA_HEADS // 2):
        sl = slice(hp * 2 * HEAD_TILE, (hp + 1) * 2 * HEAD_TILE)
        q2 = norm_rope_pair(mq[:, sl], mq_rot[:, sl], q_gc, q_gs)
        k2 = norm_rope_pair(kn[:, sl] + rope2, rope_rot2, k_gc, k_gs)
        for i in range(2):
            mq_ref[0, 2 * hp + i] = q2[i].astype(BF16)
            mk_ref[0, 2 * hp + i] = k2[i].astype(BF16)


def _gla_kernel(q_ref, k_ref, v_ref, lg_ref, gate_ref, gain_ref, o_ref, accf_ref, accb_ref, stf_ref, stb_ref):
    seq = q_ref.shape[0]
    C = GLA_CHUNK
    n_chunks = seq // C
    HK = GLA_QK_WIDTH

    lane_head = lax.broadcasted_iota(jnp.int32, (1, HK), 1) // GLA_DK
    rowi = lax.broadcasted_iota(jnp.int32, (C, 1), 0)
    srow = lax.broadcasted_iota(jnp.int32, (GLA_HEADS * C, C), 0) % C
    scol = lax.broadcasted_iota(jnp.int32, (GLA_HEADS * C, C), 1)
    fwd_cfg = (True, C // 2 - 1, C - 1, scol <= srow, 0)
    bwd_cfg = (False, C // 2, 0, scol > srow, HK)

    def stack_heads(t):
        return jnp.concatenate([jnp.where(lane_head == hd, t, 0.0) for hd in range(GLA_HEADS)], axis=0)

    def chunk_step(c, st, cfg, acc_ref):
        prefix, ref_i, last_i, keep, lg_off = cfg
        rows = pl.ds(pl.multiple_of(c * C, C), C)
        b = lg_ref[rows, lg_off:lg_off + HK]
        shift = 1
        while shift < C:
            if prefix:
                b = b + jnp.where(rowi >= shift, pltpu.roll(b, shift, 0), 0.0)
            else:
                b = b + jnp.where(rowi < C - shift, pltpu.roll(b, C - shift, 0), 0.0)
            shift *= 2
        b_ref = b[ref_i:ref_i + 1, :]
        b_last = b[last_i:last_i + 1, :]
        q = q_ref[rows, :].astype(F32)
        k = k_ref[rows, :].astype(F32)
        v = v_ref[rows, :]
        q_rel = q * jnp.exp(b - b_ref)
        k_rel = (k * jnp.exp(b_ref - b)).astype(BF16)
        k_dec = (k * jnp.exp(b_last - b)).astype(BF16)
        q_dec = q * jnp.exp(b)
        sc = _dot_nt(stack_heads(q_rel).astype(BF16), k_rel)
        sc = jnp.where(keep, sc, 0.0).astype(BF16)
        o_inter = _dot_nt(stack_heads(q_dec).astype(BF16), st.astype(BF16))
        kv_t = _dot_tn(v, k_dec)
        new_st = jnp.exp(b_last) * st
        for hd in range(GLA_HEADS):
            vs = slice(hd * GLA_DV, (hd + 1) * GLA_DV)
            rs = slice(hd * C, (hd + 1) * C)
            acc_ref[rows, vs] = _dot(sc[rs, :], v[:, vs]) + o_inter[rs, :]
            new_st = new_st + jnp.where(lane_head == hd, kv_t[vs, :], 0.0)
        return new_st

    stf_ref[...] = jnp.zeros_like(stf_ref)
    stb_ref[...] = jnp.zeros_like(stb_ref)

    def body(i, carry):
        st_f = stf_ref[...]
        st_b = stb_ref[...]
        for u in range(GLA_CHUNK_UNROLL):
            cf = i * GLA_CHUNK_UNROLL + u
            st_f = chunk_step(cf, st_f, fwd_cfg, accf_ref)
            st_b = chunk_step(n_chunks - 1 - cf, st_b, bwd_cfg, accb_ref)
        stf_ref[...] = st_f
        stb_ref[...] = st_b
        return carry

    lax.fori_loop(0, n_chunks // GLA_CHUNK_UNROLL, body, 0)

    gain = gain_ref[...]

    def finish(r, carry):
        rows = pl.ds(pl.multiple_of(r * GLA_FINISH_ROWS, GLA_FINISH_ROWS), GLA_FINISH_ROWS)
        for hd in range(GLA_HEADS):
            vs = slice(hd * GLA_DV, (hd + 1) * GLA_DV)
            o = accf_ref[rows, vs] + accb_ref[rows, vs]
            on = o * lax.rsqrt(jnp.mean(o * o, axis=-1, keepdims=True) + EPS) * gain
            gt = gate_ref[rows, vs].astype(F32)
            o_ref[rows, vs] = (on * (gt * jax.nn.sigmoid(gt))).astype(BF16)
        return carry

    lax.fori_loop(0, seq // GLA_FINISH_ROWS, finish, 0)


def _attn_kernel(q_ref, k_ref, v_ref, o_ref):
    lane = lax.broadcasted_iota(jnp.int32, (1, LANES), 1)
    for hp in range(MLA_HEADS // 2):
        outs = []
        for j in range(2):
            hd = 2 * hp + j
            s = _dot_nt(q_ref[0, hd], k_ref[0, hd])
            p = jnp.exp2(s - jnp.max(s, axis=-1, keepdims=True))
            o = _dot(p.astype(BF16), v_ref[0, hd])
            den_lane = MLA_V if j == 0 else 0
            outs.append(o / o[:, den_lane:den_lane + 1])
        o_ref[:, hp * LANES:(hp + 1) * LANES] = jnp.where(lane < MLA_V, outs[0], outs[1]).astype(BF16)


def _unpack_bf16_pair(w):
    lo = pltpu.unpack_elementwise(w, index=0, packed_dtype=BF16, unpacked_dtype=F32)
    hi = pltpu.unpack_elementwise(w, index=1, packed_dtype=BF16, unpacked_dtype=F32)
    return lo, hi


def _pack_bf16_pair(lo, hi):
    return pltpu.pack_elementwise([lo, hi], packed_dtype=BF16)


def _outproj_kernel(tiles_per_sb, x_ref, gla_ref, mla_ref, wo_ref, n2_ref, wr_hi_ref, wr_lo_ref, br_ref, tri_ref,
                    x1_ref, h2_ref, route_ref, cnt_ref, carry_ref):
    mix = _dot(gla_ref[...], wo_ref[0:GLA_WIDTH, :]) + _dot(mla_ref[...], wo_ref[GLA_WIDTH:, :])
    x1 = x_ref[...] + mix
    x1_ref[...] = x1
    h2 = x1 * lax.rsqrt(jnp.mean(x1 * x1, axis=-1, keepdims=True) + EPS) * n2_ref[...]
    h2_ref[...] = h2.astype(BF16)

    h_hi, h_lo = _split_bf16(h2)
    w_hi = wr_hi_ref[...]
    logits = (_dot_nt(w_hi, h_hi) + _dot_nt(w_hi, h_lo) + _dot_nt(wr_lo_ref[...], h_hi)) + br_ref[...]
    tm = logits.shape[1]
    gl = logits[0:N_GROUPS, :]
    ge = jnp.exp(gl - jnp.max(gl, axis=0, keepdims=True))
    pg = ge / jnp.sum(ge, axis=0, keepdims=True)
    p_top = jnp.max(pg, axis=0, keepdims=True)
    gi = lax.broadcasted_iota(jnp.int32, (N_GROUPS, tm), 0)
    g_idx = jnp.min(jnp.where(pg == p_top, gi, N_GROUPS), axis=0, keepdims=True)
    sel = jnp.zeros((EXPERTS_PER_GROUP, tm), F32)
    for g in range(N_GROUPS):
        r0 = ROUTER_EXPERT_ROW + g * EXPERTS_PER_GROUP
        sel = sel + jnp.where(g_idx == g, logits[r0:r0 + EXPERTS_PER_GROUP, :], 0.0)
    se = jnp.exp(sel - jnp.max(sel, axis=0, keepdims=True))
    pe = se / jnp.sum(se, axis=0, keepdims=True)
    ei = lax.broadcasted_iota(jnp.int32, (EXPERTS_PER_GROUP, tm), 0)
    m1 = jnp.max(pe, axis=0, keepdims=True)
    i1 = jnp.min(jnp.where(pe == m1, ei, EXPERTS_PER_GROUP), axis=0, keepdims=True)
    pe2 = jnp.where(ei == i1, -1.0, pe)
    m2 = jnp.max(pe2, axis=0, keepdims=True)
    i2 = jnp.min(jnp.where(pe2 == m2, ei, EXPERTS_PER_GROUP), axis=0, keepdims=True)
    den = m1 + m2
    e1 = g_idx * EXPERTS_PER_GROUP + i1
    e2 = g_idx * EXPERTS_PER_GROUP + i2

    @pl.when(pl.program_id(0) % tiles_per_sb == 0)
    def _():
        carry_ref[...] = jnp.zeros_like(carry_ref)

    eall = lax.broadcasted_iota(jnp.int32, (N_EXPERTS, tm), 0)
    tri = tri_ref[...]
    carry = carry_ref[...]
    ranks = []
    for eid in (e1, e2):
        hot = eall == eid
        prefix = _dot(hot.astype(BF16), tri)
        ranks.append(jnp.sum(jnp.where(hot, carry + prefix - 1.0, 0.0), axis=0, keepdims=True))
        carry = carry + prefix[:, tm - 1:tm]
    carry_ref[...] = carry
    cnt_ref[0] = jnp.broadcast_to(carry, (N_EXPERTS, LANES))
    zeros = jnp.zeros((2, tm), F32)
    route_ref[...] = jnp.concatenate([e1.astype(F32), e2.astype(F32), p_top * (m1 / den), p_top * (m2 / den),
                                      ranks[0], ranks[1], zeros], axis=0)


def _slots_kernel(route_ref, cnt_ref, strict_ref, pos_ref, off_ref):
    counts = cnt_ref[0]
    padded = 2.0 * jnp.floor(counts * 0.5 + 0.5)
    hi = jnp.floor(padded * (1.0 / 256.0))
    lo = padded - 256.0 * hi
    strict = strict_ref[...]
    off = 256.0 * _dot(strict, hi.astype(BF16)) + _dot(strict, lo.astype(BF16))
    off_ref[0] = off.astype(jnp.int32)
    tb = route_ref.shape[1]
    eall = lax.broadcasted_iota(jnp.int32, (N_EXPERTS, tb), 0)
    off_col = off[:, 0:1]
    for s in range(2):
        hot = eall == route_ref[s:s + 1, :].astype(jnp.int32)
        base = jnp.sum(jnp.where(hot, off_col, 0.0), axis=0, keepdims=True)
        pos_ref[0, s:s + 1, :] = (base + route_ref[4 + s:5 + s, :]).astype(jnp.int32)


def _moe_kernel(tb, off_ref, cnt_ref, h2_ref, pos_ref, wg_ref, wu_ref, wd_ref, x1_ref, wt_ref, o_ref,
                xy_ref, cb_ref):
    sb = pl.program_id(0)
    j = pl.program_id(1)

    @pl.when(j == 0)
    def _scatter():
        xy_ref[...] = jnp.zeros_like(xy_ref)

        def tile_body(ti, c):
            t0 = pl.multiple_of(ti * MOE_COMBINE_TILE, MOE_COMBINE_TILE)
            h = h2_ref[pl.ds(t0, MOE_COMBINE_TILE), :].astype(F32)
            for r in range(PACK_ROWS):
                c0 = r * 2 * LANES
                cb_ref[0, pl.ds(r, MOE_COMBINE_TILE, stride=PACK_ROWS), :] = _pack_bf16_pair(
                    h[:, c0:c0 + LANES], h[:, c0 + LANES:c0 + 2 * LANES])

            def body(g, c2):
                for i in range(MOE_ROW_UNROLL // 2):
                    pair = g * (MOE_ROW_UNROLL // 2) + i
                    v = cb_ref[0, pl.ds(pl.multiple_of(pair * 2 * PACK_ROWS, 2 * PACK_ROWS), 2 * PACK_ROWS), :]
                    for half in range(2):
                        t = t0 + pair * 2 + half
                        piece = v[half * PACK_ROWS:(half + 1) * PACK_ROWS, :]
                        for slot in range(2):
                            p = pos_ref[slot * tb + t]
                            xy_ref[pl.ds(pl.multiple_of(p * PACK_ROWS, PACK_ROWS), PACK_ROWS), :] = piece
                return c2

            lax.fori_loop(0, MOE_COMBINE_TILE // MOE_ROW_UNROLL, body, 0)
            return c

        lax.fori_loop(0, tb // MOE_COMBINE_TILE, tile_body, 0)

    @pl.when((j >= 1) & (j <= N_EXPERTS))
    def _experts():
        e = sb * N_EXPERTS + j - 1
        n = cnt_ref[e]
        base = off_ref[e]

        def body(i, c):
            r0 = base + i * MOE_CHUNK
            row0 = pl.multiple_of(r0 * PACK_ROWS, 2 * PACK_ROWS)
            halves = [_unpack_bf16_pair(xy_ref[pl.ds(row0 + r, MOE_CHUNK, stride=PACK_ROWS), :])
                      for r in range(PACK_ROWS)]
            a = jnp.zeros((MOE_CHUNK, D_EXPERT), F32)
            u = jnp.zeros((MOE_CHUNK, D_EXPERT), F32)
            for r in range(PACK_ROWS):
                lo, hi = halves[r]
                xb = jnp.concatenate([lo, hi], axis=1).astype(BF16)
                a = a + _dot(xb, wg_ref[0, r * 2 * LANES:(r + 1) * 2 * LANES, :])
                u = u + _dot(xb, wu_ref[0, r * 2 * LANES:(r + 1) * 2 * LANES, :])
            hid = ((a * jax.nn.sigmoid(a)) * u).astype(BF16)
            y = _dot(hid, wd_ref[0])
            valid = (r0 + lax.broadcasted_iota(jnp.int32, (MOE_CHUNK, 1), 0)) < (base + n)
            for r in range(PACK_ROWS):
                c0 = r * 2 * LANES
                lo, hi = halves[r]
                xy_ref[pl.ds(row0 + r, MOE_CHUNK, stride=PACK_ROWS), :] = _pack_bf16_pair(
                    jnp.where(valid, y[:, c0:c0 + LANES], lo), jnp.where(valid, y[:, c0 + LANES:c0 + 2 * LANES], hi))
            return c

        lax.fori_loop(0, (n + MOE_CHUNK - 1) // MOE_CHUNK, body, 0)

    @pl.when(j > N_EXPERTS)
    def _combine():
        t0 = (j - N_EXPERTS - 1) * MOE_COMBINE_TILE

        def body(g, c):
            for i in range(MOE_ROW_UNROLL):
                t = g * MOE_ROW_UNROLL + i
                for slot in range(2):
                    p = pos_ref[slot * tb + t0 + t]
                    cb_ref[slot, pl.ds(pl.multiple_of(t * PACK_ROWS, PACK_ROWS), PACK_ROWS), :] = (
                        xy_ref[pl.ds(pl.multiple_of(p * PACK_ROWS, PACK_ROWS), PACK_ROWS), :])
            return c

        lax.fori_loop(0, MOE_COMBINE_TILE // MOE_ROW_UNROLL, body, 0)
        w1 = wt_ref[:, 0:1]
        w2 = wt_ref[:, 1:2]
        for r in range(PACK_ROWS):
            lo1, hi1 = _unpack_bf16_pair(cb_ref[0, pl.ds(r, MOE_COMBINE_TILE, stride=PACK_ROWS), :])
            lo2, hi2 = _unpack_bf16_pair(cb_ref[1, pl.ds(r, MOE_COMBINE_TILE, stride=PACK_ROWS), :])
            c0 = r * 2 * LANES
            o_ref[:, c0:c0 + LANES] = x1_ref[:, c0:c0 + LANES] + w1 * lo1 + w2 * lo2
            o_ref[:, c0 + LANES:c0 + 2 * LANES] = x1_ref[:, c0 + LANES:c0 + 2 * LANES] + w1 * hi1 + w2 * hi2


def _full(shape):
    return pl.BlockSpec(shape, lambda *_: (0,) * len(shape))


def _prep_weights(w_in, gk_fwd_w, gk_fwd_b, gk_bwd_w, gk_bwd_b, mla_w_qb, mla_w_kvb, q_norm_gain, k_norm_gain):
    splits = np.cumsum([0, GLA_QK_WIDTH, GLA_QK_WIDTH, GLA_WIDTH, GLA_WIDTH, GLA_GATE_RANK, GLA_GATE_RANK,
                        MLA_Q_RANK, MLA_KV_RANK, MLA_ROPE])
    seg = [w_in[:, splits[i]:splits[i + 1]] for i in range(9)]
    d = w_in.shape[0]
    misc = jnp.zeros((d, LANES), F32)
    misc = misc.at[:, 0:GLA_GATE_RANK].set(seg[4])
    misc = misc.at[:, MISC_LRB:MISC_LRB + GLA_GATE_RANK].set(seg[5])
    misc = misc.at[:, ROPE_LO:ROPE_LO + MLA_ROPE].set(seg[8])
    win = jnp.concatenate([seg[0], seg[1], seg[2], seg[3], seg[6], seg[7], misc], axis=1).astype(BF16)

    wg = jnp.zeros((LANES, 2 * GLA_QK_WIDTH), F32)
    wg = wg.at[0:GLA_GATE_RANK, 0:GLA_QK_WIDTH].set(gk_fwd_w)
    wg = wg.at[MISC_LRB:MISC_LRB + GLA_GATE_RANK, GLA_QK_WIDTH:].set(gk_bwd_w)
    bg = jnp.concatenate([gk_fwd_b, gk_bwd_b])[None, :]

    pad = HEAD_TILE - MLA_QK
    wqb = jnp.pad(mla_w_qb.reshape(MLA_Q_RANK, MLA_HEADS, MLA_QK), ((0, 0), (0, 0), (0, pad)))
    wqb = wqb.reshape(MLA_Q_RANK, MLA_HEADS * HEAD_TILE)
    wkv = mla_w_kvb.reshape(MLA_KV_RANK, MLA_HEADS, MLA_NOPE + MLA_V)
    wkvk = jnp.pad(wkv[:, :, :MLA_NOPE], ((0, 0), (0, 0), (0, HEAD_TILE - MLA_NOPE)))
    wkvk = wkvk.reshape(MLA_KV_RANK, MLA_HEADS * HEAD_TILE).astype(BF16)
    wkvv = wkv[:, :, MLA_NOPE:].reshape(MLA_KV_RANK, MLA_WIDTH).astype(BF16)
    qng = jnp.pad(q_norm_gain, (0, pad))[None, :]
    kng = jnp.pad(k_norm_gain, (0, pad))[None, :]
    def partner(w, sign):
        first, second = w[..., ROPE_LO:ROPE_LO + ROPE_HALF], w[..., ROPE_LO + ROPE_HALF:ROPE_LO + MLA_ROPE]
        return jnp.concatenate([jnp.zeros_like(w[..., :ROPE_LO]), sign * second, first,
                                jnp.zeros_like(w[..., ROPE_LO + MLA_ROPE:])], axis=-1)

    wqb_rot = partner(wqb.reshape(MLA_Q_RANK, MLA_HEADS, HEAD_TILE), -1.0).reshape(MLA_Q_RANK, -1).astype(BF16)
    qng_rot = partner(qng, 1.0)
    kng_rot = partner(kng, 1.0)
    return win, wg.astype(BF16), bg, wqb.astype(BF16), wqb_rot, wkvk, wkvv, qng, qng_rot, kng, kng_rot


def _rope_consts():
    inv = ROPE_BASE ** (-np.arange(0, MLA_ROPE, 2, dtype=np.float32) / MLA_ROPE)
    place = np.zeros((ROPE_HALF, LANES), np.float32)
    place[np.arange(ROPE_HALF), ROPE_LO + np.arange(ROPE_HALF)] = 1.0
    place[np.arange(ROPE_HALF), ROPE_LO + ROPE_HALF + np.arange(ROPE_HALF)] = 1.0
    ones_bd = np.kron(np.eye(2, dtype=np.float32), np.ones((LANES, LANES), np.float32))
    return jnp.asarray(inv[:, None]), jnp.asarray(place, BF16), jnp.asarray(ones_bd, BF16)


def _layer(x2, pos2, batch, seq, norm1_gain, w_in, gk_fwd_w, gk_fwd_b, gk_bwd_w, gk_bwd_b, gla_out_gain,
           mla_q_gain, mla_w_qb, mla_kv_gain, mla_w_kvb, q_norm_gain, k_norm_gain, w_out, norm2_gain,
           w_router_group, b_router_group, w_router_expert, b_router_expert,
           w_expert_gate, w_expert_up, w_expert_down):
    T, D = x2.shape
    win, wg, bg, wqb, wqb_rot, wkvk, wkvv, qng, qng_rot, kng, kng_rot = _prep_weights(
        w_in, gk_fwd_w, gk_fwd_b, gk_bwd_w, gk_bwd_b, mla_w_qb, mla_w_kvb, q_norm_gain, k_norm_gain)

    TM = 512
    spb = seq // TM
    tok = lambda w: pl.BlockSpec((TM, w), lambda i: (i, 0))
    head_spec = pl.BlockSpec((1, MLA_HEADS, TM, HEAD_TILE), lambda i: (i // spb, 0, i % spb, 0))
    gq, gk, gv, gg, lg, mq, mk, mv = pl.pallas_call(
        _proj_kernel,
        grid=(T // TM,),
        in_specs=[tok(D), pl.BlockSpec((1, 1, TM), lambda i: (i, 0, 0)), _full((1, D)), _full((D, PROJ_WIDTH)),
                  _full((LANES, 2 * GLA_QK_WIDTH)), _full((1, 2 * GLA_QK_WIDTH)), _full((1, MLA_Q_RANK)),
                  _full((MLA_Q_RANK, MLA_HEADS * HEAD_TILE)), _full((MLA_Q_RANK, MLA_HEADS * HEAD_TILE)),
                  _full((1, MLA_KV_RANK)), _full((MLA_KV_RANK, MLA_HEADS * HEAD_TILE)),
                  _full((MLA_KV_RANK, MLA_WIDTH)), _full((1, HEAD_TILE)), _full((1, HEAD_TILE)),
                  _full((1, HEAD_TILE)), _full((1, HEAD_TILE)),
                  _full((ROPE_HALF, 1)), _full((ROPE_HALF, LANES)), _full((2 * LANES, 2 * LANES))],
        out_specs=[tok(GLA_QK_WIDTH), tok(GLA_QK_WIDTH), tok(GLA_WIDTH), tok(GLA_WIDTH), tok(2 * GLA_QK_WIDTH),
                   head_spec, head_spec, head_spec],
        out_shape=[jax.ShapeDtypeStruct((T, GLA_QK_WIDTH), BF16), jax.ShapeDtypeStruct((T, GLA_QK_WIDTH), BF16),
                   jax.ShapeDtypeStruct((T, GLA_WIDTH), BF16), jax.ShapeDtypeStruct((T, GLA_WIDTH), BF16),
                   jax.ShapeDtypeStruct((T, 2 * GLA_QK_WIDTH), F32),
                   jax.ShapeDtypeStruct((batch, MLA_HEADS, seq, HEAD_TILE), BF16),
                   jax.ShapeDtypeStruct((batch, MLA_HEADS, seq, HEAD_TILE), BF16),
                   jax.ShapeDtypeStruct((batch, MLA_HEADS, seq, HEAD_TILE), BF16)],
        compiler_params=pltpu.CompilerParams(dimension_semantics=("parallel",), vmem_limit_bytes=VMEM_LIMIT),
        name="proj",
    )(x2, pos2.reshape(T // TM, 1, TM), norm1_gain[None, :], win, wg, bg, mla_q_gain[None, :], wqb, wqb_rot,
      mla_kv_gain[None, :], wkvk, wkvv, qng, qng_rot, kng, kng_rot, *_rope_consts())

    seqspec = lambda w: pl.BlockSpec((seq, w), lambda b: (b, 0))
    gla_out = pl.pallas_call(
        _gla_kernel,
        grid=(batch,),
        in_specs=[seqspec(GLA_QK_WIDTH), seqspec(GLA_QK_WIDTH), seqspec(GLA_WIDTH), seqspec(2 * GLA_QK_WIDTH),
                  seqspec(GLA_WIDTH), _full((1, GLA_DV))],
        out_specs=seqspec(GLA_WIDTH),
        out_shape=jax.ShapeDtypeStruct((T, GLA_WIDTH), BF16),
        scratch_shapes=[pltpu.VMEM((seq, GLA_WIDTH), F32), pltpu.VMEM((seq, GLA_WIDTH), F32),
                        pltpu.VMEM((GLA_DV, GLA_QK_WIDTH), F32), pltpu.VMEM((GLA_DV, GLA_QK_WIDTH), F32)],
        compiler_params=pltpu.CompilerParams(dimension_semantics=("parallel",), vmem_limit_bytes=VMEM_LIMIT),
        name="gla",
    )(gq, gk, gv, lg, gg, gla_out_gain[None, :])

    TQ = 256
    nq = seq // TQ
    mla_out = pl.pallas_call(
        _attn_kernel,
        grid=(batch, nq),
        in_specs=[pl.BlockSpec((1, MLA_HEADS, TQ, HEAD_TILE), lambda b, i: (b, 0, i, 0)),
                  pl.BlockSpec((1, MLA_HEADS, seq, HEAD_TILE), lambda b, i: (b, 0, 0, 0)),
                  pl.BlockSpec((1, MLA_HEADS, seq, HEAD_TILE), lambda b, i: (b, 0, 0, 0))],
        out_specs=pl.BlockSpec((TQ, MLA_WIDTH), lambda b, i: (b * nq + i, 0)),
        out_shape=jax.ShapeDtypeStruct((T, MLA_WIDTH), BF16),
        compiler_params=pltpu.CompilerParams(dimension_semantics=("parallel", "parallel"),
                                             vmem_limit_bytes=VMEM_LIMIT),
        name="attn",
    )(mq, mk, mv)

    wr = jnp.zeros((ROUTER_ROWS, D), F32)
    wr = wr.at[0:N_GROUPS].set(w_router_group.T)
    wr = wr.at[ROUTER_EXPERT_ROW:ROUTER_EXPERT_ROW + N_EXPERTS].set(w_router_expert.T)
    wr_hi, wr_lo = _split_bf16(wr)
    br = jnp.zeros((ROUTER_ROWS, 1), F32)
    br = br.at[0:N_GROUPS, 0].set(b_router_group)
    br = br.at[ROUTER_EXPERT_ROW:ROUTER_EXPERT_ROW + N_EXPERTS, 0].set(b_router_expert)
    tb = min(MOE_SUPER_BLOCK, T)
    nsb = T // tb
    tiles_per_sb = tb // TM
    tri = jnp.asarray(np.triu(np.ones((TM, TM), np.float32)), BF16)
    x1, h2, route, cnt = pl.pallas_call(
        functools.partial(_outproj_kernel, tiles_per_sb),
        grid=(T // TM,),
        in_specs=[tok(D), tok(GLA_WIDTH), tok(MLA_WIDTH), _full((D, D)), _full((1, D)),
                  _full((ROUTER_ROWS, D)), _full((ROUTER_ROWS, D)), _full((ROUTER_ROWS, 1)), _full((TM, TM))],
        out_specs=[tok(D), tok(D), pl.BlockSpec((8, TM), lambda i: (0, i)),
                   pl.BlockSpec((1, N_EXPERTS, LANES), lambda i: (i // tiles_per_sb, 0, 0))],
        out_shape=[jax.ShapeDtypeStruct((T, D), F32), jax.ShapeDtypeStruct((T, D), BF16),
                   jax.ShapeDtypeStruct((8, T), F32), jax.ShapeDtypeStruct((nsb, N_EXPERTS, LANES), F32)],
        scratch_shapes=[pltpu.VMEM((N_EXPERTS, 1), F32)],
        compiler_params=pltpu.CompilerParams(dimension_semantics=("arbitrary",), vmem_limit_bytes=VMEM_LIMIT),
        name="outproj",
    )(x2, gla_out, mla_out, w_out.astype(BF16), norm2_gain[None, :], wr_hi, wr_lo, br, tri)

    strict = jnp.asarray(np.tril(np.ones((N_EXPERTS, N_EXPERTS), np.float32), -1), BF16)
    pos, offs = pl.pallas_call(
        _slots_kernel,
        grid=(nsb,),
        in_specs=[pl.BlockSpec((8, tb), lambda s: (0, s)), pl.BlockSpec((1, N_EXPERTS, LANES), lambda s: (s, 0, 0)),
                  _full((N_EXPERTS, N_EXPERTS))],
        out_specs=[pl.BlockSpec((1, 2, tb), lambda s: (s, 0, 0)),
                   pl.BlockSpec((1, N_EXPERTS, LANES), lambda s: (s, 0, 0))],
        out_shape=[jax.ShapeDtypeStruct((nsb, 2, tb), jnp.int32),
                   jax.ShapeDtypeStruct((nsb, N_EXPERTS, LANES), jnp.int32)],
        compiler_params=pltpu.CompilerParams(dimension_semantics=("parallel",)),
        name="slots",
    )(route, cnt, strict)
    pos = pos.reshape(-1)
    flat_off = offs[:, :, 0].reshape(-1)
    counts = cnt[:, :, 0].astype(jnp.int32)
    wts = jnp.stack([route[2], route[3]], axis=1)
    wge = w_expert_gate.reshape(N_EXPERTS, D, D_EXPERT).astype(BF16)
    wue = w_expert_up.reshape(N_EXPERTS, D, D_EXPERT).astype(BF16)
    wde = w_expert_down.reshape(N_EXPERTS, D_EXPERT, D).astype(BF16)
    nct = tb // MOE_COMBINE_TILE
    slots = (2 * tb + N_EXPERTS + MOE_CHUNK + 7) // 8 * 8
    eidx = lambda sb, j, off, cn: (jnp.clip(j - 1, 0, N_EXPERTS - 1), 0, 0)
    tidx = lambda sb, j, off, cn: (sb * nct + jnp.clip(j - N_EXPERTS - 1, 0, nct - 1), 0)
    out = pl.pallas_call(
        functools.partial(_moe_kernel, tb),
        grid_spec=pltpu.PrefetchScalarGridSpec(
            num_scalar_prefetch=2,
            grid=(nsb, 1 + N_EXPERTS + nct),
            in_specs=[pl.BlockSpec((tb, D), lambda sb, j, off, cn: (sb, 0)),
                      pl.BlockSpec((2 * tb,), lambda sb, j, off, cn: (sb,), memory_space=pltpu.SMEM),
                      pl.BlockSpec((1, D, D_EXPERT), eidx), pl.BlockSpec((1, D, D_EXPERT), eidx),
                      pl.BlockSpec((1, D_EXPERT, D), eidx),
                      pl.BlockSpec((MOE_COMBINE_TILE, D), tidx), pl.BlockSpec((MOE_COMBINE_TILE, 2), tidx)],
            out_specs=pl.BlockSpec((MOE_COMBINE_TILE, D), tidx),
            scratch_shapes=[pltpu.VMEM((slots * PACK_ROWS, LANES), U32),
                            pltpu.VMEM((2, MOE_COMBINE_TILE * PACK_ROWS, LANES), U32)]),
        out_shape=jax.ShapeDtypeStruct((T, D), F32),
        compiler_params=pltpu.CompilerParams(dimension_semantics=("arbitrary", "arbitrary"),
                                             vmem_limit_bytes=VMEM_LIMIT),
        name="moe",
    )(flat_off, counts.reshape(-1), h2, pos, wge, wue, wde, x1, wts)
    return out


def kernel(x, positions, norm1_gain, w_in, gla_gk_fwd_w, gla_gk_fwd_b, gla_gk_bwd_w, gla_gk_bwd_b, gla_out_gain, mla_q_gain, mla_w_qb, mla_kv_gain, mla_w_kvb, q_norm_gain, k_norm_gain, w_out, norm2_gain, w_router_group, b_router_group, w_router_expert, b_router_expert, w_expert_gate, w_expert_up, w_expert_down):
    batch, seq, d = x.shape
    x2 = x.reshape(batch * seq, d)
    pos2 = positions.reshape(batch * seq, 1)
    for l in range(norm1_gain.shape[0]):
        x2 = _layer(x2, pos2, batch, seq, norm1_gain[l], w_in[l], gla_gk_fwd_w[l], gla_gk_fwd_b[l],
                    gla_gk_bwd_w[l], gla_gk_bwd_b[l], gla_out_gain[l], mla_q_gain[l], mla_w_qb[l],
                    mla_kv_gain[l], mla_w_kvb[l], q_norm_gain[l], k_norm_gain[l], w_out[l], norm2_gain[l],
                    w_router_group[l], b_router_group[l], w_router_expert[l], b_router_expert[l],
                    w_expert_gate[l], w_expert_up[l], w_expert_down[l])
    return x2.reshape(batch, seq, d)
```

```python
import functools

import numpy as np
import jax
import jax.numpy as jnp
from jax import lax
from jax.experimental import pallas as pl
from jax.experimental.pallas import tpu as pltpu

F32 = jnp.float32
BF16 = jnp.bfloat16

D_MODEL = 1024
GLA_HEADS = 4
GLA_DK = 64
GLA_DV = 128
GLA_GATE_RANK = 16
GLA_GATE_NORMALIZER = 16.0
GLA_CHUNK = 64
GLA_CHUNK_UNROLL = 4
GLA_FINISH_ROWS = 256
MLA_HEADS = 8
MLA_NOPE = 64
MLA_ROPE = 32
MLA_QK = MLA_NOPE + MLA_ROPE
MLA_V = 64
MLA_Q_RANK = 256
MLA_KV_RANK = 128
ROPE_BASE = 10000.0
GLA_QK_WIDTH = GLA_HEADS * GLA_DK
GLA_WIDTH = GLA_HEADS * GLA_DV
MLA_WIDTH = MLA_HEADS * MLA_V
N_GROUPS = 4
EXPERTS_PER_GROUP = 8
N_EXPERTS = N_GROUPS * EXPERTS_PER_GROUP
D_EXPERT = 256
EPS = 1e-6

LANES = 128
HEAD_TILE = LANES
ROPE_LO = MLA_NOPE
ROPE_HALF = MLA_ROPE // 2
LOG2_E = 1.4426950408889634

COL_GQ = 0
COL_GK = COL_GQ + GLA_QK_WIDTH
COL_GV = COL_GK + GLA_QK_WIDTH
COL_GG = COL_GV + GLA_WIDTH
COL_MQ = COL_GG + GLA_WIDTH
COL_MKV = COL_MQ + MLA_Q_RANK
COL_MISC = COL_MKV + MLA_KV_RANK
PROJ_WIDTH = COL_MISC + LANES
MISC_LRB = GLA_GATE_RANK

ROUTER_ROWS = LANES
ROUTER_EXPERT_ROW = 8

U32 = jnp.uint32
HI_HALF_MASK = np.uint32(0xFFFF0000)
PACK_ROWS = D_MODEL // (2 * LANES)
MOE_SUPER_BLOCK = 4096
MOE_CHUNK = 320
MOE_COMBINE_TILE = 512
OUTPROJ_TILE = 1024
MOE_ROW_UNROLL = 8

VMEM_LIMIT = 56 * 1024 * 1024


def _dot(a, b):
    return jnp.dot(a, b, preferred_element_type=F32)


def _dot_nt(a, b):
    return lax.dot_general(a, b, (((1,), (1,)), ((), ())), preferred_element_type=F32)


def _dot_tn(a, b):
    return lax.dot_general(a, b, (((0,), (0,)), ((), ())), preferred_element_type=F32)


def _split_bf16(x):
    hi = x.astype(BF16)
    lo = (x - hi.astype(F32)).astype(BF16)
    return hi, lo


def _proj_kernel(x_ref, pos_ref, n1_ref, win_ref, wg_ref, bg_ref, qgain_ref, wqb_ref, wqbr_ref, kvgain_ref,
                 wkvk_ref, wkvv_ref, qng_ref, qngr_ref, kng_ref, kngr_ref, freq_ref, place_ref, ones_ref,
                 gq_ref, gk_ref, gv_ref, gg_ref, lg_ref, mq_ref, mk_ref, mv_ref):
    x = x_ref[...]
    h = x * lax.rsqrt(jnp.mean(x * x, axis=-1, keepdims=True) + EPS) * n1_ref[...]
    proj = _dot(h.astype(BF16), win_ref[...])

    gq_ref[...] = (proj[:, COL_GQ:COL_GK] * GLA_DK ** -0.5).astype(BF16)
    gk_ref[...] = proj[:, COL_GK:COL_GV].astype(BF16)
    gv_ref[...] = proj[:, COL_GV:COL_GG].astype(BF16)
    gg_ref[...] = proj[:, COL_GG:COL_MQ].astype(BF16)

    misc = proj[:, COL_MISC:PROJ_WIDTH]
    z = _dot(misc.astype(BF16), wg_ref[...]) + bg_ref[...]
    log_sig = -(jnp.maximum(-z, 0.0) + jnp.log(1.0 + jnp.exp(-jnp.abs(z))))
    lg_ref[...] = log_sig / GLA_GATE_NORMALIZER

    lane = lax.broadcasted_iota(jnp.int32, (1, LANES), 1)
    ang_t = freq_ref[...] * pos_ref[0].astype(F32)
    place = place_ref[...]

    def to_rows(tab_t):
        hi, lo = _split_bf16(tab_t)
        return _dot_tn(hi, place) + _dot_tn(lo, place)

    in_rope = (lane >= ROPE_LO) & (lane < ROPE_LO + MLA_ROPE)
    c_tab = jnp.where(lane < ROPE_LO, 1.0, to_rows(jnp.cos(ang_t)))
    s_tab = to_rows(jnp.sin(ang_t))
    ones_bd = ones_ref[...]

    def norm_rope_pair(t2, r2, gc, gs):
        ss = _dot((t2 * t2).astype(BF16), ones_bd) * (1.0 / MLA_QK)
        rs = lax.rsqrt(ss + EPS)
        outs = []
        for i in range(2):
            sl = slice(i * LANES, (i + 1) * LANES)
            outs.append(rs[:, sl] * (t2[:, sl] * gc + r2[:, sl] * gs))
        return outs

    qa = proj[:, COL_MQ:COL_MKV]
    qn = qa * lax.rsqrt(jnp.mean(qa * qa, axis=-1, keepdims=True) + EPS) * qgain_ref[...]
    qn = qn.astype(BF16)
    mq = _dot(qn, wqb_ref[...])
    mq_rot = _dot(qn, wqbr_ref[...])
    kva = proj[:, COL_MKV:COL_MISC]
    kvn = (kva * lax.rsqrt(jnp.mean(kva * kva, axis=-1, keepdims=True) + EPS) * kvgain_ref[...]).astype(BF16)
    kn = _dot(kvn, wkvk_ref[...])
    mv = _dot(kvn, wkvv_ref[...])
    for hd in range(MLA_HEADS):
        pair = mv[:, (hd // 2) * LANES:(hd // 2 + 1) * LANES]
        if hd % 2 == 0:
            tile = jnp.where(lane < MLA_V, pair, jnp.where(lane == MLA_V, 1.0, 0.0))
        else:
            tile = jnp.where(lane >= MLA_V, pair, jnp.where(lane == 0, 1.0, 0.0))
        mv_ref[0, hd] = tile.astype(BF16)
    rope_tile = jnp.where(in_rope, misc, 0.0)
    up = (lane >= ROPE_LO + ROPE_HALF) & (lane < ROPE_LO + MLA_ROPE)
    dn = (lane >= ROPE_LO) & (lane < ROPE_LO + ROPE_HALF)
    rope_rot = (jnp.where(up, pltpu.roll(rope_tile, ROPE_HALF, 1), 0.0)
                - jnp.where(dn, pltpu.roll(rope_tile, LANES - ROPE_HALF, 1), 0.0))
    rope2 = jnp.concatenate([rope_tile, rope_tile], axis=1)
    rope_rot2 = jnp.concatenate([rope_rot, rope_rot], axis=1)
    scale = MLA_QK ** -0.5 * LOG2_E
    q_gc, q_gs = (qng_ref[...] * scale) * c_tab, (qngr_ref[...] * scale) * s_tab
    k_gc, k_gs = kng_ref[...] * c_tab, kngr_ref[...] * s_tab
    for hp in range(MLA_HEADS // 2):
        sl = slice(hp * 2 * HEAD_TILE, (hp + 1) * 2 * HEAD_TILE)
        q2 = norm_rope_pair(mq[:, sl], mq_rot[:, sl], q_gc, q_gs)
        k2 = norm_rope_pair(kn[:, sl] + rope2, rope_rot2, k_gc, k_gs)
        for i in range(2):
            mq_ref[0, 2 * hp + i] = q2[i].astype(BF16)
            mk_ref[0, 2 * hp + i] = k2[i].astype(BF16)


def _gla_kernel(q_ref, k_ref, v_ref, lg_ref, gate_ref, gain_ref, o_ref, accf_ref, accb_ref, stf_ref, stb_ref):
    seq = q_ref.shape[0]
    C = GLA_CHUNK
    n_chunks = seq // C
    HK = GLA_QK_WIDTH

    lane_head = lax.broadcasted_iota(jnp.int32, (1, HK), 1) // GLA_DK
    rowi = lax.broadcasted_iota(jnp.int32, (C, 1), 0)
    srow = lax.broadcasted_iota(jnp.int32, (GLA_HEADS * C, C), 0) % C
    scol = lax.broadcasted_iota(jnp.int32, (GLA_HEADS * C, C), 1)
    fwd_cfg = (True, C // 2 - 1, C - 1, scol <= srow, 0)
    bwd_cfg = (False, C // 2, 0, scol > srow, HK)

    def stack_heads(t):
        return jnp.concatenate([jnp.where(lane_head == hd, t, 0.0) for hd in range(GLA_HEADS)], axis=0)

    def chunk_step(c, st, cfg, acc_ref):
        prefix, ref_i, last_i, keep, lg_off = cfg
        rows = pl.ds(pl.multiple_of(c * C, C), C)
        b = lg_ref[rows, lg_off:lg_off + HK]
        shift = 1
        while shift < C:
            if prefix:
                b = b + jnp.where(rowi >= shift, pltpu.roll(b, shift, 0), 0.0)
            else:
                b = b + jnp.where(rowi < C - shift, pltpu.roll(b, C - shift, 0), 0.0)
            shift *= 2
        b_ref = b[ref_i:ref_i + 1, :]
        b_last = b[last_i:last_i + 1, :]
        q = q_ref[rows, :].astype(F32)
        k = k_ref[rows, :].astype(F32)
        v = v_ref[rows, :]
        q_rel = q * jnp.exp(b - b_ref)
        k_rel = (k * jnp.exp(b_ref - b)).astype(BF16)
        k_dec = (k * jnp.exp(b_last - b)).astype(BF16)
        q_dec = q * jnp.exp(b)
        sc = _dot_nt(stack_heads(q_rel).astype(BF16), k_rel)
        sc = jnp.where(keep, sc, 0.0).astype(BF16)
        o_inter = _dot_nt(stack_heads(q_dec).astype(BF16), st.astype(BF16))
        kv_t = _dot_tn(v, k_dec)
        new_st = jnp.exp(b_last) * st
        for hd in range(GLA_HEADS):
            vs = slice(hd * GLA_DV, (hd + 1) * GLA_DV)
            rs = slice(hd * C, (hd + 1) * C)
            acc_ref[rows, vs] = _dot(sc[rs, :], v[:, vs]) + o_inter[rs, :]
            new_st = new_st + jnp.where(lane_head == hd, kv_t[vs, :], 0.0)
        return new_st

    stf_ref[...] = jnp.zeros_like(stf_ref)
    stb_ref[...] = jnp.zeros_like(stb_ref)

    def body(i, carry):
        st_f = stf_ref[...]
        st_b = stb_ref[...]
        for u in range(GLA_CHUNK_UNROLL):
            cf = i * GLA_CHUNK_UNROLL + u
            st_f = chunk_step(cf, st_f, fwd_cfg, accf_ref)
            st_b = chunk_step(n_chunks - 1 - cf, st_b, bwd_cfg, accb_ref)
        stf_ref[...] = st_f
        stb_ref[...] = st_b
        return carry

    lax.fori_loop(0, n_chunks // GLA_CHUNK_UNROLL, body, 0)

    gain = gain_ref[...]

    def finish(r, carry):
        rows = pl.ds(pl.multiple_of(r * GLA_FINISH_ROWS, GLA_FINISH_ROWS), GLA_FINISH_ROWS)
        for hd in range(GLA_HEADS):
            vs = slice(hd * GLA_DV, (hd + 1) * GLA_DV)
            o = accf_ref[rows, vs] + accb_ref[rows, vs]
            on = o * lax.rsqrt(jnp.mean(o * o, axis=-1, keepdims=True) + EPS) * gain
            gt = gate_ref[rows, vs].astype(F32)
            o_ref[rows, vs] = (on * (gt * jax.nn.sigmoid(gt))).astype(BF16)
        return carry

    lax.fori_loop(0, seq // GLA_FINISH_ROWS, finish, 0)


def _attn_kernel(q_ref, k_ref, v_ref, o_ref):
    lane = lax.broadcasted_iota(jnp.int32, (1, LANES), 1)
    for hp in range(MLA_HEADS // 2):
        outs = []
        for j in range(2):
            hd = 2 * hp + j
            s = _dot_nt(q_ref[0, hd], k_ref[0, hd])
            p = jnp.exp2(s - jnp.max(s, axis=-1, keepdims=True))
            o = _dot(p.astype(BF16), v_ref[0, hd])
            den_lane = MLA_V if j == 0 else 0
            outs.append(o / o[:, den_lane:den_lane + 1])
        o_ref[:, hp * LANES:(hp + 1) * LANES] = jnp.where(lane < MLA_V, outs[0], outs[1]).astype(BF16)


def _unpack_bf16_pair(w):
    lo = pltpu.unpack_elementwise(w, index=0, packed_dtype=BF16, unpacked_dtype=F32)
    hi = pltpu.unpack_elementwise(w, index=1, packed_dtype=BF16, unpacked_dtype=F32)
    return lo, hi


def _pack_bf16_pair(lo, hi):
    return pltpu.pack_elementwise([lo, hi], packed_dtype=BF16)


def _outproj_kernel(tiles_per_sb, x_ref, gla_ref, mla_ref, wo_ref, n2_ref, wr_hi_ref, wr_lo_ref, br_ref, tri_ref,
                    x1_ref, h2_ref, route_ref, cnt_ref, carry_ref):
    mix = _dot(gla_ref[...], wo_ref[0:GLA_WIDTH, :]) + _dot(mla_ref[...], wo_ref[GLA_WIDTH:, :])
    x1 = x_ref[...] + mix
    x1_ref[...] = x1
    h2 = x1 * lax.rsqrt(jnp.mean(x1 * x1, axis=-1, keepdims=True) + EPS) * n2_ref[...]
    h2_ref[...] = h2.astype(BF16)

    h_hi, h_lo = _split_bf16(h2)
    w_hi = wr_hi_ref[...]
    logits = (_dot_nt(w_hi, h_hi) + _dot_nt(w_hi, h_lo) + _dot_nt(wr_lo_ref[...], h_hi)) + br_ref[...]
    tm = logits.shape[1]
    gl = logits[0:N_GROUPS, :]
    ge = jnp.exp(gl - jnp.max(gl, axis=0, keepdims=True))
    pg = ge / jnp.sum(ge, axis=0, keepdims=True)
    p_top = jnp.max(pg, axis=0, keepdims=True)
    gi = lax.broadcasted_iota(jnp.int32, (N_GROUPS, tm), 0)
    g_idx = jnp.min(jnp.where(pg == p_top, gi, N_GROUPS), axis=0, keepdims=True)
    sel = jnp.zeros((EXPERTS_PER_GROUP, tm), F32)
    for g in range(N_GROUPS):
        r0 = ROUTER_EXPERT_ROW + g * EXPERTS_PER_GROUP
        sel = sel + jnp.where(g_idx == g, logits[r0:r0 + EXPERTS_PER_GROUP, :], 0.0)
    se = jnp.exp(sel - jnp.max(sel, axis=0, keepdims=True))
    pe = se / jnp.sum(se, axis=0, keepdims=True)
    ei = lax.broadcasted_iota(jnp.int32, (EXPERTS_PER_GROUP, tm), 0)
    m1 = jnp.max(pe, axis=0, keepdims=True)
    i1 = jnp.min(jnp.where(pe == m1, ei, EXPERTS_PER_GROUP), axis=0, keepdims=True)
    pe2 = jnp.where(ei == i1, -1.0, pe)
    m2 = jnp.max(pe2, axis=0, keepdims=True)
    i2 = jnp.min(jnp.where(pe2 == m2, ei, EXPERTS_PER_GROUP), axis=0, keepdims=True)
    den = m1 + m2
    e1 = g_idx * EXPERTS_PER_GROUP + i1
    e2 = g_idx * EXPERTS_PER_GROUP + i2

    @pl.when(pl.program_id(0) % tiles_per_sb == 0)
    def _():
        carry_ref[...] = jnp.zeros_like(carry_ref)

    eall = lax.broadcasted_iota(jnp.int32, (N_EXPERTS, tm), 0)
    tri = tri_ref[...]
    carry = carry_ref[...]
    ranks = []
    for eid in (e1, e2):
        hot = eall == eid
        prefix = _dot(hot.astype(BF16), tri)
        ranks.append(jnp.sum(jnp.where(hot, carry + prefix - 1.0, 0.0), axis=0, keepdims=True))
        carry = carry + prefix[:, tm - 1:tm]
    carry_ref[...] = carry
    cnt_ref[0] = jnp.broadcast_to(carry, (N_EXPERTS, LANES))
    zeros = jnp.zeros((2, tm), F32)
    route_ref[...] = jnp.concatenate([e1.astype(F32), e2.astype(F32), p_top * (m1 / den), p_top * (m2 / den),
                                      ranks[0], ranks[1], zeros], axis=0)


def _slots_kernel(route_ref, cnt_ref, strict_ref, pos_ref, off_ref):
    counts = cnt_ref[0]
    padded = 2.0 * jnp.floor(counts * 0.5 + 0.5)
    hi = jnp.floor(padded * (1.0 / 256.0))
    lo = padded - 256.0 * hi
    strict = strict_ref[...]
    off = 256.0 * _dot(strict, hi.astype(BF16)) + _dot(strict, lo.astype(BF16))
    off_ref[0] = off.astype(jnp.int32)
    tb = route_ref.shape[1]
    eall = lax.broadcasted_iota(jnp.int32, (N_EXPERTS, tb), 0)
    off_col = off[:, 0:1]
    for s in range(2):
        hot = eall == route_ref[s:s + 1, :].astype(jnp.int32)
        base = jnp.sum(jnp.where(hot, off_col, 0.0), axis=0, keepdims=True)
        pos_ref[0, s:s + 1, :] = ((base + route_ref[4 + s:5 + s, :]) * PACK_ROWS).astype(jnp.int32)


def _moe_kernel(tb, off_ref, cnt_ref, h2_ref, pos_ref, wg_ref, wu_ref, wd_ref, x1_ref, wt_ref, o_ref,
                xy_ref, cb_ref):
    sb = pl.program_id(0)
    j = pl.program_id(1)

    @pl.when(j == 0)
    def _scatter():
        xy_ref[...] = jnp.zeros_like(xy_ref)

        def tile_body(ti, c):
            t0 = pl.multiple_of(ti * MOE_COMBINE_TILE, MOE_COMBINE_TILE)
            h = h2_ref[pl.ds(t0, MOE_COMBINE_TILE), :].astype(F32)
            for r in range(PACK_ROWS):
                c0 = r * 2 * LANES
                cb_ref[0, pl.ds(r, MOE_COMBINE_TILE, stride=PACK_ROWS), :] = _pack_bf16_pair(
                    h[:, c0:c0 + LANES], h[:, c0 + LANES:c0 + 2 * LANES])

            def body(g, c2):
                for i in range(MOE_ROW_UNROLL // 2):
                    pair = g * (MOE_ROW_UNROLL // 2) + i
                    v = cb_ref[0, pl.ds(pl.multiple_of(pair * 2 * PACK_ROWS, 2 * PACK_ROWS), 2 * PACK_ROWS), :]
                    for half in range(2):
                        t = t0 + pair * 2 + half
                        piece = v[half * PACK_ROWS:(half + 1) * PACK_ROWS, :]
                        for slot in range(2):
                            p = pos_ref[slot * tb + t]
                            xy_ref[pl.ds(pl.multiple_of(p, PACK_ROWS), PACK_ROWS), :] = piece
                return c2

            lax.fori_loop(0, MOE_COMBINE_TILE // MOE_ROW_UNROLL, body, 0)
            return c

        lax.fori_loop(0, tb // MOE_COMBINE_TILE, tile_body, 0)

    @pl.when((j >= 1) & (j <= N_EXPERTS))
    def _experts():
        e = sb * N_EXPERTS + j - 1
        n = cnt_ref[e]
        base = off_ref[e]

        def body(i, c):
            r0 = base + i * MOE_CHUNK
            row0 = pl.multiple_of(r0 * PACK_ROWS, 2 * PACK_ROWS)
            halves = [_unpack_bf16_pair(xy_ref[pl.ds(row0 + r, MOE_CHUNK, stride=PACK_ROWS), :])
                      for r in range(PACK_ROWS)]
            a = jnp.zeros((MOE_CHUNK, D_EXPERT), F32)
            u = jnp.zeros((MOE_CHUNK, D_EXPERT), F32)
            for r in range(PACK_ROWS):
                lo, hi = halves[r]
                xb = jnp.concatenate([lo, hi], axis=1).astype(BF16)
                a = a + _dot(xb, wg_ref[0, r * 2 * LANES:(r + 1) * 2 * LANES, :].astype(BF16))
                u = u + _dot(xb, wu_ref[0, r * 2 * LANES:(r + 1) * 2 * LANES, :].astype(BF16))
            hid = ((a * jax.nn.sigmoid(a)) * u).astype(BF16)
            y = _dot(hid, wd_ref[0].astype(BF16))
            valid = (r0 + lax.broadcasted_iota(jnp.int32, (MOE_CHUNK, 1), 0)) < (base + n)
            for r in range(PACK_ROWS):
                c0 = r * 2 * LANES
                lo, hi = halves[r]
                xy_ref[pl.ds(row0 + r, MOE_CHUNK, stride=PACK_ROWS), :] = _pack_bf16_pair(
                    jnp.where(valid, y[:, c0:c0 + LANES], lo), jnp.where(valid, y[:, c0 + LANES:c0 + 2 * LANES], hi))
            return c

        lax.fori_loop(0, (n + MOE_CHUNK - 1) // MOE_CHUNK, body, 0)

    @pl.when(j > N_EXPERTS)
    def _combine():
        t0 = (j - N_EXPERTS - 1) * MOE_COMBINE_TILE

        def body(g, c):
            for i in range(MOE_ROW_UNROLL):
                t = g * MOE_ROW_UNROLL + i
                for slot in range(2):
                    p = pos_ref[slot * tb + t0 + t]
                    cb_ref[slot, pl.ds(pl.multiple_of(t * PACK_ROWS, PACK_ROWS), PACK_ROWS), :] = (
                        xy_ref[pl.ds(pl.multiple_of(p, PACK_ROWS), PACK_ROWS), :])
            return c

        lax.fori_loop(0, MOE_COMBINE_TILE // MOE_ROW_UNROLL, body, 0)
        w1 = wt_ref[:, 0:1]
        w2 = wt_ref[:, 1:2]
        for r in range(PACK_ROWS):
            lo1, hi1 = _unpack_bf16_pair(cb_ref[0, pl.ds(r, MOE_COMBINE_TILE, stride=PACK_ROWS), :])
            lo2, hi2 = _unpack_bf16_pair(cb_ref[1, pl.ds(r, MOE_COMBINE_TILE, stride=PACK_ROWS), :])
            c0 = r * 2 * LANES
            o_ref[:, c0:c0 + LANES] = x1_ref[:, c0:c0 + LANES] + w1 * lo1 + w2 * lo2
            o_ref[:, c0 + LANES:c0 + 2 * LANES] = x1_ref[:, c0 + LANES:c0 + 2 * LANES] + w1 * hi1 + w2 * hi2


def _full(shape):
    return pl.BlockSpec(shape, lambda *_: (0,) * len(shape))


def _prep_weights(w_in, gk_fwd_w, gk_fwd_b, gk_bwd_w, gk_bwd_b, mla_w_qb, mla_w_kvb, q_norm_gain, k_norm_gain):
    splits = np.cumsum([0, GLA_QK_WIDTH, GLA_QK_WIDTH, GLA_WIDTH, GLA_WIDTH, GLA_GATE_RANK, GLA_GATE_RANK,
                        MLA_Q_RANK, MLA_KV_RANK, MLA_ROPE])
    seg = [w_in[:, splits[i]:splits[i + 1]] for i in range(9)]
    d = w_in.shape[0]
    misc = jnp.zeros((d, LANES), F32)
    misc = misc.at[:, 0:GLA_GATE_RANK].set(seg[4])
    misc = misc.at[:, MISC_LRB:MISC_LRB + GLA_GATE_RANK].set(seg[5])
    misc = misc.at[:, ROPE_LO:ROPE_LO + MLA_ROPE].set(seg[8])
    win = jnp.concatenate([seg[0], seg[1], seg[2], seg[3], seg[6], seg[7], misc], axis=1).astype(BF16)

    wg = jnp.zeros((LANES, 2 * GLA_QK_WIDTH), F32)
    wg = wg.at[0:GLA_GATE_RANK, 0:GLA_QK_WIDTH].set(gk_fwd_w)
    wg = wg.at[MISC_LRB:MISC_LRB + GLA_GATE_RANK, GLA_QK_WIDTH:].set(gk_bwd_w)
    bg = jnp.concatenate([gk_fwd_b, gk_bwd_b])[None, :]

    pad = HEAD_TILE - MLA_QK
    wqb = jnp.pad(mla_w_qb.reshape(MLA_Q_RANK, MLA_HEADS, MLA_QK), ((0, 0), (0, 0), (0, pad)))
    wqb = wqb.reshape(MLA_Q_RANK, MLA_HEADS * HEAD_TILE)
    wkv = mla_w_kvb.reshape(MLA_KV_RANK, MLA_HEADS, MLA_NOPE + MLA_V)
    wkvk = jnp.pad(wkv[:, :, :MLA_NOPE], ((0, 0), (0, 0), (0, HEAD_TILE - MLA_NOPE)))
    wkvk = wkvk.reshape(MLA_KV_RANK, MLA_HEADS * HEAD_TILE).astype(BF16)
    wkvv = wkv[:, :, MLA_NOPE:].reshape(MLA_KV_RANK, MLA_WIDTH).astype(BF16)
    qng = jnp.pad(q_norm_gain, (0, pad))[None, :]
    kng = jnp.pad(k_norm_gain, (0, pad))[None, :]
    def partner(w, sign):
        first, second = w[..., ROPE_LO:ROPE_LO + ROPE_HALF], w[..., ROPE_LO + ROPE_HALF:ROPE_LO + MLA_ROPE]
        return jnp.concatenate([jnp.zeros_like(w[..., :ROPE_LO]), sign * second, first,
                                jnp.zeros_like(w[..., ROPE_LO + MLA_ROPE:])], axis=-1)

    wqb_rot = partner(wqb.reshape(MLA_Q_RANK, MLA_HEADS, HEAD_TILE), -1.0).reshape(MLA_Q_RANK, -1).astype(BF16)
    qng_rot = partner(qng, 1.0)
    kng_rot = partner(kng, 1.0)
    return win, wg.astype(BF16), bg, wqb.astype(BF16), wqb_rot, wkvk, wkvv, qng, qng_rot, kng, kng_rot


def _rope_consts():
    inv = ROPE_BASE ** (-np.arange(0, MLA_ROPE, 2, dtype=np.float32) / MLA_ROPE)
    place = np.zeros((ROPE_HALF, LANES), np.float32)
    place[np.arange(ROPE_HALF), ROPE_LO + np.arange(ROPE_HALF)] = 1.0
    place[np.arange(ROPE_HALF), ROPE_LO + ROPE_HALF + np.arange(ROPE_HALF)] = 1.0
    ones_bd = np.kron(np.eye(2, dtype=np.float32), np.ones((LANES, LANES), np.float32))
    return jnp.asarray(inv[:, None]), jnp.asarray(place, BF16), jnp.asarray(ones_bd, BF16)


def _layer(x2, pos2, batch, seq, norm1_gain, w_in, gk_fwd_w, gk_fwd_b, gk_bwd_w, gk_bwd_b, gla_out_gain,
           mla_q_gain, mla_w_qb, mla_kv_gain, mla_w_kvb, q_norm_gain, k_norm_gain, w_out, norm2_gain,
           w_router_group, b_router_group, w_router_expert, b_router_expert,
           w_expert_gate, w_expert_up, w_expert_down):
    T, D = x2.shape
    win, wg, bg, wqb, wqb_rot, wkvk, wkvv, qng, qng_rot, kng, kng_rot = _prep_weights(
        w_in, gk_fwd_w, gk_fwd_b, gk_bwd_w, gk_bwd_b, mla_w_qb, mla_w_kvb, q_norm_gain, k_norm_gain)

    TM = 512
    spb = seq // TM
    tok = lambda w: pl.BlockSpec((TM, w), lambda i: (i, 0))
    head_spec = pl.BlockSpec((1, MLA_HEADS, TM, HEAD_TILE), lambda i: (i // spb, 0, i % spb, 0))
    gq, gk, gv, gg, lg, mq, mk, mv = pl.pallas_call(
        _proj_kernel,
        grid=(T // TM,),
        in_specs=[tok(D), pl.BlockSpec((1, 1, TM), lambda i: (i, 0, 0)), _full((1, D)), _full((D, PROJ_WIDTH)),
                  _full((LANES, 2 * GLA_QK_WIDTH)), _full((1, 2 * GLA_QK_WIDTH)), _full((1, MLA_Q_RANK)),
                  _full((MLA_Q_RANK, MLA_HEADS * HEAD_TILE)), _full((MLA_Q_RANK, MLA_HEADS * HEAD_TILE)),
                  _full((1, MLA_KV_RANK)), _full((MLA_KV_RANK, MLA_HEADS * HEAD_TILE)),
                  _full((MLA_KV_RANK, MLA_WIDTH)), _full((1, HEAD_TILE)), _full((1, HEAD_TILE)),
                  _full((1, HEAD_TILE)), _full((1, HEAD_TILE)),
                  _full((ROPE_HALF, 1)), _full((ROPE_HALF, LANES)), _full((2 * LANES, 2 * LANES))],
        out_specs=[tok(GLA_QK_WIDTH), tok(GLA_QK_WIDTH), tok(GLA_WIDTH), tok(GLA_WIDTH), tok(2 * GLA_QK_WIDTH),
                   head_spec, head_spec, head_spec],
        out_shape=[jax.ShapeDtypeStruct((T, GLA_QK_WIDTH), BF16), jax.ShapeDtypeStruct((T, GLA_QK_WIDTH), BF16),
                   jax.ShapeDtypeStruct((T, GLA_WIDTH), BF16), jax.ShapeDtypeStruct((T, GLA_WIDTH), BF16),
                   jax.ShapeDtypeStruct((T, 2 * GLA_QK_WIDTH), F32),
                   jax.ShapeDtypeStruct((batch, MLA_HEADS, seq, HEAD_TILE), BF16),
                   jax.ShapeDtypeStruct((batch, MLA_HEADS, seq, HEAD_TILE), BF16),
                   jax.ShapeDtypeStruct((batch, MLA_HEADS, seq, HEAD_TILE), BF16)],
        compiler_params=pltpu.CompilerParams(dimension_semantics=("parallel",), vmem_limit_bytes=VMEM_LIMIT),
        name="proj",
    )(x2, pos2.reshape(T // TM, 1, TM), norm1_gain[None, :], win, wg, bg, mla_q_gain[None, :], wqb, wqb_rot,
      mla_kv_gain[None, :], wkvk, wkvv, qng, qng_rot, kng, kng_rot, *_rope_consts())

    seqspec = lambda w: pl.BlockSpec((seq, w), lambda b: (b, 0))
    gla_out = pl.pallas_call(
        _gla_kernel,
        grid=(batch,),
        in_specs=[seqspec(GLA_QK_WIDTH), seqspec(GLA_QK_WIDTH), seqspec(GLA_WIDTH), seqspec(2 * GLA_QK_WIDTH),
                  seqspec(GLA_WIDTH), _full((1, GLA_DV))],
        out_specs=seqspec(GLA_WIDTH),
        out_shape=jax.ShapeDtypeStruct((T, GLA_WIDTH), BF16),
        scratch_shapes=[pltpu.VMEM((seq, GLA_WIDTH), F32), pltpu.VMEM((seq, GLA_WIDTH), F32),
                        pltpu.VMEM((GLA_DV, GLA_QK_WIDTH), F32), pltpu.VMEM((GLA_DV, GLA_QK_WIDTH), F32)],
        compiler_params=pltpu.CompilerParams(dimension_semantics=("parallel",), vmem_limit_bytes=VMEM_LIMIT),
        name="gla",
    )(gq, gk, gv, lg, gg, gla_out_gain[None, :])

    TQ = 256
    nq = seq // TQ
    mla_out = pl.pallas_call(
        _attn_kernel,
        grid=(batch, nq),
        in_specs=[pl.BlockSpec((1, MLA_HEADS, TQ, HEAD_TILE), lambda b, i: (b, 0, i, 0)),
                  pl.BlockSpec((1, MLA_HEADS, seq, HEAD_TILE), lambda b, i: (b, 0, 0, 0)),
                  pl.BlockSpec((1, MLA_HEADS, seq, HEAD_TILE), lambda b, i: (b, 0, 0, 0))],
        out_specs=pl.BlockSpec((TQ, MLA_WIDTH), lambda b, i: (b * nq + i, 0)),
        out_shape=jax.ShapeDtypeStruct((T, MLA_WIDTH), BF16),
        compiler_params=pltpu.CompilerParams(dimension_semantics=("parallel", "parallel"),
                                             vmem_limit_bytes=VMEM_LIMIT),
        name="attn",
    )(mq, mk, mv)

    wr = jnp.zeros((ROUTER_ROWS, D), F32)
    wr = wr.at[0:N_GROUPS].set(w_router_group.T)
    wr = wr.at[ROUTER_EXPERT_ROW:ROUTER_EXPERT_ROW + N_EXPERTS].set(w_router_expert.T)
    wr_hi, wr_lo = _split_bf16(wr)
    br = jnp.zeros((ROUTER_ROWS, 1), F32)
    br = br.at[0:N_GROUPS, 0].set(b_router_group)
    br = br.at[ROUTER_EXPERT_ROW:ROUTER_EXPERT_ROW + N_EXPERTS, 0].set(b_router_expert)
    tb = min(MOE_SUPER_BLOCK, T)
    nsb = T // tb
    TO = min(OUTPROJ_TILE, tb)
    tiles_per_sb = tb // TO
    tok = lambda w: pl.BlockSpec((TO, w), lambda i: (i, 0))
    tri = jnp.asarray(np.triu(np.ones((TO, TO), np.float32)), BF16)
    x1, h2, route, cnt = pl.pallas_call(
        functools.partial(_outproj_kernel, tiles_per_sb),
        grid=(T // TO,),
        in_specs=[tok(D), tok(GLA_WIDTH), tok(MLA_WIDTH), _full((D, D)), _full((1, D)),
                  _full((ROUTER_ROWS, D)), _full((ROUTER_ROWS, D)), _full((ROUTER_ROWS, 1)), _full((TO, TO))],
        out_specs=[tok(D), tok(D), pl.BlockSpec((8, TO), lambda i: (0, i)),
                   pl.BlockSpec((1, N_EXPERTS, LANES), lambda i: (i // tiles_per_sb, 0, 0))],
        out_shape=[jax.ShapeDtypeStruct((T, D), F32), jax.ShapeDtypeStruct((T, D), BF16),
                   jax.ShapeDtypeStruct((8, T), F32), jax.ShapeDtypeStruct((nsb, N_EXPERTS, LANES), F32)],
        scratch_shapes=[pltpu.VMEM((N_EXPERTS, 1), F32)],
        compiler_params=pltpu.CompilerParams(dimension_semantics=("arbitrary",), vmem_limit_bytes=VMEM_LIMIT),
        name="outproj",
    )(x2, gla_out, mla_out, w_out.astype(BF16), norm2_gain[None, :], wr_hi, wr_lo, br, tri)

    strict = jnp.asarray(np.tril(np.ones((N_EXPERTS, N_EXPERTS), np.float32), -1), BF16)
    pos, offs = pl.pallas_call(
        _slots_kernel,
        grid=(nsb,),
        in_specs=[pl.BlockSpec((8, tb), lambda s: (0, s)), pl.BlockSpec((1, N_EXPERTS, LANES), lambda s: (s, 0, 0)),
                  _full((N_EXPERTS, N_EXPERTS))],
        out_specs=[pl.BlockSpec((1, 2, tb), lambda s: (s, 0, 0)),
                   pl.BlockSpec((1, N_EXPERTS, LANES), lambda s: (s, 0, 0))],
        out_shape=[jax.ShapeDtypeStruct((nsb, 2, tb), jnp.int32),
                   jax.ShapeDtypeStruct((nsb, N_EXPERTS, LANES), jnp.int32)],
        compiler_params=pltpu.CompilerParams(dimension_semantics=("parallel",)),
        name="slots",
    )(route, cnt, strict)
    pos = pos.reshape(-1)
    flat_off = offs[:, :, 0].reshape(-1)
    counts = cnt[:, :, 0].astype(jnp.int32)
    wts = jnp.stack([route[2], route[3]], axis=1)
    wge = w_expert_gate.reshape(N_EXPERTS, D, D_EXPERT)
    wue = w_expert_up.reshape(N_EXPERTS, D, D_EXPERT)
    wde = w_expert_down.reshape(N_EXPERTS, D_EXPERT, D)
    nct = tb // MOE_COMBINE_TILE
    slots = (2 * tb + N_EXPERTS + MOE_CHUNK + 7) // 8 * 8
    eidx = lambda sb, j, off, cn: (jnp.clip(j - 1, 0, N_EXPERTS - 1), 0, 0)
    tidx = lambda sb, j, off, cn: (sb * nct + jnp.clip(j - N_EXPERTS - 1, 0, nct - 1), 0)
    out = pl.pallas_call(
        functools.partial(_moe_kernel, tb),
        grid_spec=pltpu.PrefetchScalarGridSpec(
            num_scalar_prefetch=2,
            grid=(nsb, 1 + N_EXPERTS + nct),
            in_specs=[pl.BlockSpec((tb, D), lambda sb, j, off, cn: (sb, 0)),
                      pl.BlockSpec((2 * tb,), lambda sb, j, off, cn: (sb,), memory_space=pltpu.SMEM),
                      pl.BlockSpec((1, D, D_EXPERT), eidx), pl.BlockSpec((1, D, D_EXPERT), eidx),
                      pl.BlockSpec((1, D_EXPERT, D), eidx),
                      pl.BlockSpec((MOE_COMBINE_TILE, D), tidx), pl.BlockSpec((MOE_COMBINE_TILE, 2), tidx)],
            out_specs=pl.BlockSpec((MOE_COMBINE_TILE, D), tidx),
            scratch_shapes=[pltpu.VMEM((slots * PACK_ROWS, LANES), U32),
                            pltpu.VMEM((2, MOE_COMBINE_TILE * PACK_ROWS, LANES), U32)]),
        out_shape=jax.ShapeDtypeStruct((T, D), F32),
        compiler_params=pltpu.CompilerParams(dimension_semantics=("arbitrary", "arbitrary"),
                                             vmem_limit_bytes=VMEM_LIMIT),
        name="moe",
    )(flat_off, counts.reshape(-1), h2, pos, wge, wue, wde, x1, wts)
    return out


def kernel(x, positions, norm1_gain, w_in, gla_gk_fwd_w, gla_gk_fwd_b, gla_gk_bwd_w, gla_gk_bwd_b, gla_out_gain, mla_q_gain, mla_w_qb, mla_kv_gain, mla_w_kvb, q_norm_gain, k_norm_gain, w_out, norm2_gain, w_router_group, b_router_group, w_router_expert, b_router_expert, w_expert_gate, w_expert_up, w_expert_down):
    batch, seq, d = x.shape
    x2 = x.reshape(batch * seq, d)
    pos2 = positions.reshape(batch * seq, 1)
    for l in range(norm1_gain.shape[0]):
        x2 = _layer(x2, pos2, batch, seq, norm1_gain[l], w_in[l], gla_gk_fwd_w[l], gla_gk_fwd_b[l],
                    gla_gk_bwd_w[l], gla_gk_bwd_b[l], gla_out_gain[l], mla_q_gain[l], mla_w_qb[l],
                    mla_kv_gain[l], mla_w_kvb[l], q_norm_gain[l], k_norm_gain[l], w_out[l], norm2_gain[l],
                    w_router_group[l], b_router_group[l], w_router_expert[l], b_router_expert[l],
                    w_expert_gate[l], w_expert_up[l], w_expert_down[l])
    return x2.reshape(batch, seq, d)
```

```python
import functools

import numpy as np
import jax
import jax.numpy as jnp
from jax import lax
from jax.experimental import pallas as pl
from jax.experimental.pallas import tpu as pltpu

F32 = jnp.float32
BF16 = jnp.bfloat16

D_MODEL = 1024
GLA_HEADS = 4
GLA_DK = 64
GLA_DV = 128
GLA_GATE_RANK = 16
GLA_GATE_NORMALIZER = 16.0
GLA_CHUNK = 64
GLA_CHUNK_UNROLL = 4
GLA_FINISH_ROWS = 256
MLA_HEADS = 8
MLA_NOPE = 64
MLA_ROPE = 32
MLA_QK = MLA_NOPE + MLA_ROPE
MLA_V = 64
MLA_Q_RANK = 256
MLA_KV_RANK = 128
ROPE_BASE = 10000.0
GLA_QK_WIDTH = GLA_HEADS * GLA_DK
GLA_WIDTH = GLA_HEADS * GLA_DV
MLA_WIDTH = MLA_HEADS * MLA_V
N_GROUPS = 4
EXPERTS_PER_GROUP = 8
N_EXPERTS = N_GROUPS * EXPERTS_PER_GROUP
D_EXPERT = 256
EPS = 1e-6

LANES = 128
HEAD_TILE = LANES
ROPE_LO = MLA_NOPE
ROPE_HALF = MLA_ROPE // 2
LOG2_E = 1.4426950408889634

COL_GQ = 0
COL_GK = COL_GQ + GLA_QK_WIDTH
COL_GV = COL_GK + GLA_QK_WIDTH
COL_GG = COL_GV + GLA_WIDTH
COL_MQ = COL_GG + GLA_WIDTH
COL_MKV = COL_MQ + MLA_Q_RANK
COL_MISC = COL_MKV + MLA_KV_RANK
PROJ_WIDTH = COL_MISC + LANES
MISC_LRB = GLA_GATE_RANK

ROUTER_ROWS = LANES
ROUTER_EXPERT_ROW = 8

U32 = jnp.uint32
HI_HALF_MASK = np.uint32(0xFFFF0000)
PACK_ROWS = D_MODEL // (2 * LANES)
MOE_SUPER_BLOCK = 8192
MOE_CHUNK = 576
MOE_SCATTER_TILE = 512
MOE_COMBINE_TILE = 256
OUTPROJ_TILE = 1024
MOE_ROW_UNROLL = 8

VMEM_LIMIT = 56 * 1024 * 1024


def _dot(a, b):
    return jnp.dot(a, b, preferred_element_type=F32)


def _dot_nt(a, b):
    return lax.dot_general(a, b, (((1,), (1,)), ((), ())), preferred_element_type=F32)


def _dot_tn(a, b):
    return lax.dot_general(a, b, (((0,), (0,)), ((), ())), preferred_element_type=F32)


def _split_bf16(x):
    hi = x.astype(BF16)
    lo = (x - hi.astype(F32)).astype(BF16)
    return hi, lo


def _proj_kernel(x_ref, pos_ref, n1_ref, win_ref, wg_ref, bg_ref, qgain_ref, wqb_ref, wqbr_ref, kvgain_ref,
                 wkvk_ref, wkvv_ref, qng_ref, qngr_ref, kng_ref, kngr_ref, freq_ref, place_ref, ones_ref,
                 gq_ref, gk_ref, gv_ref, gg_ref, lg_ref, mq_ref, mk_ref, mv_ref):
    x = x_ref[...]
    h = x * lax.rsqrt(jnp.mean(x * x, axis=-1, keepdims=True) + EPS) * n1_ref[...]
    proj = _dot(h.astype(BF16), win_ref[...])

    gq_ref[...] = (proj[:, COL_GQ:COL_GK] * GLA_DK ** -0.5).astype(BF16)
    gk_ref[...] = proj[:, COL_GK:COL_GV].astype(BF16)
    gv_ref[...] = proj[:, COL_GV:COL_GG].astype(BF16)
    gg_ref[...] = proj[:, COL_GG:COL_MQ].astype(BF16)

    misc = proj[:, COL_MISC:PROJ_WIDTH]
    z = _dot(misc.astype(BF16), wg_ref[...]) + bg_ref[...]
    log_sig = -(jnp.maximum(-z, 0.0) + jnp.log(1.0 + jnp.exp(-jnp.abs(z))))
    lg_ref[...] = log_sig / GLA_GATE_NORMALIZER

    lane = lax.broadcasted_iota(jnp.int32, (1, LANES), 1)
    ang_t = freq_ref[...] * pos_ref[0].astype(F32)
    place = place_ref[...]

    def to_rows(tab_t):
        hi, lo = _split_bf16(tab_t)
        return _dot_tn(hi, place) + _dot_tn(lo, place)

    in_rope = (lane >= ROPE_LO) & (lane < ROPE_LO + MLA_ROPE)
    c_tab = jnp.where(lane < ROPE_LO, 1.0, to_rows(jnp.cos(ang_t)))
    s_tab = to_rows(jnp.sin(ang_t))
    ones_bd = ones_ref[...]

    def norm_rope_pair(t2, r2, gc, gs):
        ss = _dot((t2 * t2).astype(BF16), ones_bd) * (1.0 / MLA_QK)
        rs = lax.rsqrt(ss + EPS)
        outs = []
        for i in range(2):
            sl = slice(i * LANES, (i + 1) * LANES)
            outs.append(rs[:, sl] * (t2[:, sl] * gc + r2[:, sl] * gs))
        return outs

    qa = proj[:, COL_MQ:COL_MKV]
    qn = qa * lax.rsqrt(jnp.mean(qa * qa, axis=-1, keepdims=True) + EPS) * qgain_ref[...]
    qn = qn.astype(BF16)
    mq = _dot(qn, wqb_ref[...])
    mq_rot = _dot(qn, wqbr_ref[...])
    kva = proj[:, COL_MKV:COL_MISC]
    kvn = (kva * lax.rsqrt(jnp.mean(kva * kva, axis=-1, keepdims=True) + EPS) * kvgain_ref[...]).astype(BF16)
    kn = _dot(kvn, wkvk_ref[...])
    mv = _dot(kvn, wkvv_ref[...])
    for hd in range(MLA_HEADS):
        pair = mv[:, (hd // 2) * LANES:(hd // 2 + 1) * LANES]
        if hd % 2 == 0:
            tile = jnp.where(lane < MLA_V, pair, jnp.where(lane == MLA_V, 1.0, 0.0))
        else:
            tile = jnp.where(lane >= MLA_V, pair, jnp.where(lane == 0, 1.0, 0.0))
        mv_ref[0, hd] = tile.astype(BF16)
    rope_tile = jnp.where(in_rope, misc, 0.0)
    up = (lane >= ROPE_LO + ROPE_HALF) & (lane < ROPE_LO + MLA_ROPE)
    dn = (lane >= ROPE_LO) & (lane < ROPE_LO + ROPE_HALF)
    rope_rot = (jnp.where(up, pltpu.roll(rope_tile, ROPE_HALF, 1), 0.0)
                - jnp.where(dn, pltpu.roll(rope_tile, LANES - ROPE_HALF, 1), 0.0))
    rope2 = jnp.concatenate([rope_tile, rope_tile], axis=1)
    rope_rot2 = jnp.concatenate([rope_rot, rope_rot], axis=1)
    scale = MLA_QK ** -0.5 * LOG2_E
    q_gc, q_gs = (qng_ref[...] * scale) * c_tab, (qngr_ref[...] * scale) * s_tab
    k_gc, k_gs = kng_ref[...] * c_tab, kngr_ref[...] * s_tab
    for hp in range(MLA_HEADS // 2):
        sl = slice(hp * 2 * HEAD_TILE, (hp + 1) * 2 * HEAD_TILE)
        q2 = norm_rope_pair(mq[:, sl], mq_rot[:, sl], q_gc, q_gs)
        k2 = norm_rope_pair(kn[:, sl] + rope2, rope_rot2, k_gc, k_gs)
        for i in range(2):
            mq_ref[0, 2 * hp + i] = q2[i].astype(BF16)
            mk_ref[0, 2 * hp + i] = k2[i].astype(BF16)


def _gla_kernel(q_ref, k_ref, v_ref, lg_ref, gate_ref, gain_ref, o_ref, accf_ref, accb_ref, stf_ref, stb_ref):
    seq = q_ref.shape[0]
    C = GLA_CHUNK
    n_chunks = seq // C
    HK = GLA_QK_WIDTH

    lane_head = lax.broadcasted_iota(jnp.int32, (1, HK), 1) // GLA_DK
    rowi = lax.broadcasted_iota(jnp.int32, (C, 1), 0)
    srow = lax.broadcasted_iota(jnp.int32, (GLA_HEADS * C, C), 0) % C
    scol = lax.broadcasted_iota(jnp.int32, (GLA_HEADS * C, C), 1)
    fwd_cfg = (True, C // 2 - 1, C - 1, scol <= srow, 0)
    bwd_cfg = (False, C // 2, 0, scol > srow, HK)

    def stack_heads(t):
        return jnp.concatenate([jnp.where(lane_head == hd, t, 0.0) for hd in range(GLA_HEADS)], axis=0)

    def chunk_step(c, st, cfg, acc_ref):
        prefix, ref_i, last_i, keep, lg_off = cfg
        rows = pl.ds(pl.multiple_of(c * C, C), C)
        b = lg_ref[rows, lg_off:lg_off + HK]
        shift = 1
        while shift < C:
            if prefix:
                b = b + jnp.where(rowi >= shift, pltpu.roll(b, shift, 0), 0.0)
            else:
                b = b + jnp.where(rowi < C - shift, pltpu.roll(b, C - shift, 0), 0.0)
            shift *= 2
        b_ref = b[ref_i:ref_i + 1, :]
        b_last = b[last_i:last_i + 1, :]
        q = q_ref[rows, :].astype(F32)
        k = k_ref[rows, :].astype(F32)
        v = v_ref[rows, :]
        q_rel = q * jnp.exp(b - b_ref)
        k_rel = (k * jnp.exp(b_ref - b)).astype(BF16)
        k_dec = (k * jnp.exp(b_last - b)).astype(BF16)
        q_dec = q * jnp.exp(b)
        sc = _dot_nt(stack_heads(q_rel).astype(BF16), k_rel)
        sc = jnp.where(keep, sc, 0.0).astype(BF16)
        o_inter = _dot_nt(stack_heads(q_dec).astype(BF16), st.astype(BF16))
        kv_t = _dot_tn(v, k_dec)
        new_st = jnp.exp(b_last) * st
        for hd in range(GLA_HEADS):
            vs = slice(hd * GLA_DV, (hd + 1) * GLA_DV)
            rs = slice(hd * C, (hd + 1) * C)
            acc_ref[rows, vs] = _dot(sc[rs, :], v[:, vs]) + o_inter[rs, :]
            new_st = new_st + jnp.where(lane_head == hd, kv_t[vs, :], 0.0)
        return new_st

    stf_ref[...] = jnp.zeros_like(stf_ref)
    stb_ref[...] = jnp.zeros_like(stb_ref)

    def body(i, carry):
        st_f = stf_ref[...]
        st_b = stb_ref[...]
        for u in range(GLA_CHUNK_UNROLL):
            cf = i * GLA_CHUNK_UNROLL + u
            st_f = chunk_step(cf, st_f, fwd_cfg, accf_ref)
            st_b = chunk_step(n_chunks - 1 - cf, st_b, bwd_cfg, accb_ref)
        stf_ref[...] = st_f
        stb_ref[...] = st_b
        return carry

    lax.fori_loop(0, n_chunks // GLA_CHUNK_UNROLL, body, 0)

    gain = gain_ref[...]

    def finish(r, carry):
        rows = pl.ds(pl.multiple_of(r * GLA_FINISH_ROWS, GLA_FINISH_ROWS), GLA_FINISH_ROWS)
        for hd in range(GLA_HEADS):
            vs = slice(hd * GLA_DV, (hd + 1) * GLA_DV)
            o = accf_ref[rows, vs] + accb_ref[rows, vs]
            on = o * lax.rsqrt(jnp.mean(o * o, axis=-1, keepdims=True) + EPS) * gain
            gt = gate_ref[rows, vs].astype(F32)
            o_ref[rows, vs] = (on * (gt * jax.nn.sigmoid(gt))).astype(BF16)
        return carry

    lax.fori_loop(0, seq // GLA_FINISH_ROWS, finish, 0)


def _attn_kernel(q_ref, k_ref, v_ref, o_ref):
    lane = lax.broadcasted_iota(jnp.int32, (1, LANES), 1)
    for hp in range(MLA_HEADS // 2):
        outs = []
        for j in range(2):
            hd = 2 * hp + j
            s = _dot_nt(q_ref[0, hd], k_ref[0, hd])
            p = jnp.exp2(s - jnp.max(s, axis=-1, keepdims=True))
            o = _dot(p.astype(BF16), v_ref[0, hd])
            den_lane = MLA_V if j == 0 else 0
            outs.append(o / o[:, den_lane:den_lane + 1])
        o_ref[:, hp * LANES:(hp + 1) * LANES] = jnp.where(lane < MLA_V, outs[0], outs[1]).astype(BF16)


def _unpack_bf16_pair(w):
    lo = pltpu.unpack_elementwise(w, index=0, packed_dtype=BF16, unpacked_dtype=F32)
    hi = pltpu.unpack_elementwise(w, index=1, packed_dtype=BF16, unpacked_dtype=F32)
    return lo, hi


def _pack_bf16_pair(lo, hi):
    return pltpu.pack_elementwise([lo, hi], packed_dtype=BF16)


def _outproj_kernel(tiles_per_sb, x_ref, gla_ref, mla_ref, wo_ref, n2_ref, wr_hi_ref, wr_lo_ref, br_ref, tri_ref,
                    x1_ref, h2_ref, route_ref, cnt_ref, carry_ref):
    mix = _dot(gla_ref[...], wo_ref[0:GLA_WIDTH, :]) + _dot(mla_ref[...], wo_ref[GLA_WIDTH:, :])
    x1 = x_ref[...] + mix
    x1_ref[...] = x1
    h2 = x1 * lax.rsqrt(jnp.mean(x1 * x1, axis=-1, keepdims=True) + EPS) * n2_ref[...]
    h2_ref[...] = h2.astype(BF16)

    h_hi, h_lo = _split_bf16(h2)
    w_hi = wr_hi_ref[...]
    logits = (_dot_nt(w_hi, h_hi) + _dot_nt(w_hi, h_lo) + _dot_nt(wr_lo_ref[...], h_hi)) + br_ref[...]
    tm = logits.shape[1]
    gl = logits[0:N_GROUPS, :]
    ge = jnp.exp(gl - jnp.max(gl, axis=0, keepdims=True))
    pg = ge / jnp.sum(ge, axis=0, keepdims=True)
    p_top = jnp.max(pg, axis=0, keepdims=True)
    gi = lax.broadcasted_iota(jnp.int32, (N_GROUPS, tm), 0)
    g_idx = jnp.min(jnp.where(pg == p_top, gi, N_GROUPS), axis=0, keepdims=True)
    sel = jnp.zeros((EXPERTS_PER_GROUP, tm), F32)
    for g in range(N_GROUPS):
        r0 = ROUTER_EXPERT_ROW + g * EXPERTS_PER_GROUP
        sel = sel + jnp.where(g_idx == g, logits[r0:r0 + EXPERTS_PER_GROUP, :], 0.0)
    se = jnp.exp(sel - jnp.max(sel, axis=0, keepdims=True))
    pe = se / jnp.sum(se, axis=0, keepdims=True)
    ei = lax.broadcasted_iota(jnp.int32, (EXPERTS_PER_GROUP, tm), 0)
    m1 = jnp.max(pe, axis=0, keepdims=True)
    i1 = jnp.min(jnp.where(pe == m1, ei, EXPERTS_PER_GROUP), axis=0, keepdims=True)
    pe2 = jnp.where(ei == i1, -1.0, pe)
    m2 = jnp.max(pe2, axis=0, keepdims=True)
    i2 = jnp.min(jnp.where(pe2 == m2, ei, EXPERTS_PER_GROUP), axis=0, keepdims=True)
    den = m1 + m2
    e1 = g_idx * EXPERTS_PER_GROUP + i1
    e2 = g_idx * EXPERTS_PER_GROUP + i2

    @pl.when(pl.program_id(0) % tiles_per_sb == 0)
    def _():
        carry_ref[...] = jnp.zeros_like(carry_ref)

    eall = lax.broadcasted_iota(jnp.int32, (N_EXPERTS, tm), 0)
    tri = tri_ref[...]
    carry = carry_ref[...]
    ranks = []
    for eid in (e1, e2):
        hot = eall == eid
        prefix = _dot(hot.astype(BF16), tri)
        ranks.append(jnp.sum(jnp.where(hot, carry + prefix - 1.0, 0.0), axis=0, keepdims=True))
        carry = carry + prefix[:, tm - 1:tm]
    carry_ref[...] = carry
    cnt_ref[0] = jnp.broadcast_to(carry, (N_EXPERTS, LANES))
    zeros = jnp.zeros((2, tm), F32)
    route_ref[...] = jnp.concatenate([e1.astype(F32), e2.astype(F32), p_top * (m1 / den), p_top * (m2 / den),
                                      ranks[0], ranks[1], zeros], axis=0)


def _slots_kernel(route_ref, cnt_ref, strict_ref, pos_ref, off_ref):
    counts = cnt_ref[0]
    padded = 2.0 * jnp.floor(counts * 0.5 + 0.5)
    hi = jnp.floor(padded * (1.0 / 256.0))
    lo = padded - 256.0 * hi
    strict = strict_ref[...]
    off = 256.0 * _dot(strict, hi.astype(BF16)) + _dot(strict, lo.astype(BF16))
    off_ref[0] = off.astype(jnp.int32)
    tb = route_ref.shape[1]
    eall = lax.broadcasted_iota(jnp.int32, (N_EXPERTS, tb), 0)
    off_col = off[:, 0:1]
    for s in range(2):
        hot = eall == route_ref[s:s + 1, :].astype(jnp.int32)
        base = jnp.sum(jnp.where(hot, off_col, 0.0), axis=0, keepdims=True)
        pos_ref[0, s:s + 1, :] = ((base + route_ref[4 + s:5 + s, :]) * PACK_ROWS).astype(jnp.int32)


def _moe_kernel(tb, off_ref, cnt_ref, h2_ref, pos_ref, wg_ref, wu_ref, wd_ref, x1_ref, wt_ref, o_ref,
                xy_ref, cb_ref):
    sb = pl.program_id(0)
    j = pl.program_id(1)
    n_scatter = tb // MOE_SCATTER_TILE
    first_expert_step = n_scatter
    first_combine_step = n_scatter + N_EXPERTS

    @pl.when(j == 0)
    def _clear():
        xy_ref[...] = jnp.zeros_like(xy_ref)

    @pl.when(j < n_scatter)
    def _scatter():
        t0 = j * MOE_SCATTER_TILE
        h = h2_ref[...].astype(F32)
        for r in range(PACK_ROWS):
            c0 = r * 2 * LANES
            cb_ref[0, pl.ds(r, MOE_SCATTER_TILE, stride=PACK_ROWS), :] = _pack_bf16_pair(
                h[:, c0:c0 + LANES], h[:, c0 + LANES:c0 + 2 * LANES])

        def body(g, c):
            for i in range(MOE_ROW_UNROLL // 2):
                pair = g * (MOE_ROW_UNROLL // 2) + i
                v = cb_ref[0, pl.ds(pl.multiple_of(pair * 2 * PACK_ROWS, 2 * PACK_ROWS), 2 * PACK_ROWS), :]
                for half in range(2):
                    t = t0 + pair * 2 + half
                    piece = v[half * PACK_ROWS:(half + 1) * PACK_ROWS, :]
                    for slot in range(2):
                        p = pos_ref[slot * tb + t]
                        xy_ref[pl.ds(pl.multiple_of(p, PACK_ROWS), PACK_ROWS), :] = piece
            return c

        lax.fori_loop(0, MOE_SCATTER_TILE // MOE_ROW_UNROLL, body, 0)

    @pl.when((j >= first_expert_step) & (j < first_combine_step))
    def _experts():
        e = sb * N_EXPERTS + j - first_expert_step
        n = cnt_ref[e]
        base = off_ref[e]

        def body(i, c):
            r0 = base + i * MOE_CHUNK
            row0 = pl.multiple_of(r0 * PACK_ROWS, 2 * PACK_ROWS)
            halves = [_unpack_bf16_pair(xy_ref[pl.ds(row0 + r, MOE_CHUNK, stride=PACK_ROWS), :])
                      for r in range(PACK_ROWS)]
            a = jnp.zeros((MOE_CHUNK, D_EXPERT), F32)
            u = jnp.zeros((MOE_CHUNK, D_EXPERT), F32)
            for r in range(PACK_ROWS):
                lo, hi = halves[r]
                xb = jnp.concatenate([lo, hi], axis=1).astype(BF16)
                a = a + _dot(xb, wg_ref[0, r * 2 * LANES:(r + 1) * 2 * LANES, :].astype(BF16))
                u = u + _dot(xb, wu_ref[0, r * 2 * LANES:(r + 1) * 2 * LANES, :].astype(BF16))
            hid = ((a * jax.nn.sigmoid(a)) * u).astype(BF16)
            y = _dot(hid, wd_ref[0].astype(BF16))
            valid = (r0 + lax.broadcasted_iota(jnp.int32, (MOE_CHUNK, 1), 0)) < (base + n)
            for r in range(PACK_ROWS):
                c0 = r * 2 * LANES
                lo, hi = halves[r]
                xy_ref[pl.ds(row0 + r, MOE_CHUNK, stride=PACK_ROWS), :] = _pack_bf16_pair(
                    jnp.where(valid, y[:, c0:c0 + LANES], lo), jnp.where(valid, y[:, c0 + LANES:c0 + 2 * LANES], hi))
            return c

        lax.fori_loop(0, (n + MOE_CHUNK - 1) // MOE_CHUNK, body, 0)

    @pl.when(j >= first_combine_step)
    def _combine():
        t0 = (j - first_combine_step) * MOE_COMBINE_TILE

        def body(g, c):
            for i in range(MOE_ROW_UNROLL):
                t = g * MOE_ROW_UNROLL + i
                for slot in range(2):
                    p = pos_ref[slot * tb + t0 + t]
                    cb_ref[slot, pl.ds(pl.multiple_of(t * PACK_ROWS, PACK_ROWS), PACK_ROWS), :] = (
                        xy_ref[pl.ds(pl.multiple_of(p, PACK_ROWS), PACK_ROWS), :])
            return c

        lax.fori_loop(0, MOE_COMBINE_TILE // MOE_ROW_UNROLL, body, 0)
        w1 = wt_ref[:, 0:1]
        w2 = wt_ref[:, 1:2]
        for r in range(PACK_ROWS):
            lo1, hi1 = _unpack_bf16_pair(cb_ref[0, pl.ds(r, MOE_COMBINE_TILE, stride=PACK_ROWS), :])
            lo2, hi2 = _unpack_bf16_pair(cb_ref[1, pl.ds(r, MOE_COMBINE_TILE, stride=PACK_ROWS), :])
            c0 = r * 2 * LANES
            o_ref[:, c0:c0 + LANES] = x1_ref[:, c0:c0 + LANES] + w1 * lo1 + w2 * lo2
            o_ref[:, c0 + LANES:c0 + 2 * LANES] = x1_ref[:, c0 + LANES:c0 + 2 * LANES] + w1 * hi1 + w2 * hi2


def _full(shape):
    return pl.BlockSpec(shape, lambda *_: (0,) * len(shape))


def _prep_weights(w_in, gk_fwd_w, gk_fwd_b, gk_bwd_w, gk_bwd_b, mla_w_qb, mla_w_kvb, q_norm_gain, k_norm_gain):
    splits = np.cumsum([0, GLA_QK_WIDTH, GLA_QK_WIDTH, GLA_WIDTH, GLA_WIDTH, GLA_GATE_RANK, GLA_GATE_RANK,
                        MLA_Q_RANK, MLA_KV_RANK, MLA_ROPE])
    seg = [w_in[:, splits[i]:splits[i + 1]] for i in range(9)]
    d = w_in.shape[0]
    misc = jnp.zeros((d, LANES), F32)
    misc = misc.at[:, 0:GLA_GATE_RANK].set(seg[4])
    misc = misc.at[:, MISC_LRB:MISC_LRB + GLA_GATE_RANK].set(seg[5])
    misc = misc.at[:, ROPE_LO:ROPE_LO + MLA_ROPE].set(seg[8])
    win = jnp.concatenate([seg[0], seg[1], seg[2], seg[3], seg[6], seg[7], misc], axis=1).astype(BF16)

    wg = jnp.zeros((LANES, 2 * GLA_QK_WIDTH), F32)
    wg = wg.at[0:GLA_GATE_RANK, 0:GLA_QK_WIDTH].set(gk_fwd_w)
    wg = wg.at[MISC_LRB:MISC_LRB + GLA_GATE_RANK, GLA_QK_WIDTH:].set(gk_bwd_w)
    bg = jnp.concatenate([gk_fwd_b, gk_bwd_b])[None, :]

    pad = HEAD_TILE - MLA_QK
    wqb = jnp.pad(mla_w_qb.reshape(MLA_Q_RANK, MLA_HEADS, MLA_QK), ((0, 0), (0, 0), (0, pad)))
    wqb = wqb.reshape(MLA_Q_RANK, MLA_HEADS * HEAD_TILE)
    wkv = mla_w_kvb.reshape(MLA_KV_RANK, MLA_HEADS, MLA_NOPE + MLA_V)
    wkvk = jnp.pad(wkv[:, :, :MLA_NOPE], ((0, 0), (0, 0), (0, HEAD_TILE - MLA_NOPE)))
    wkvk = wkvk.reshape(MLA_KV_RANK, MLA_HEADS * HEAD_TILE).astype(BF16)
    wkvv = wkv[:, :, MLA_NOPE:].reshape(MLA_KV_RANK, MLA_WIDTH).astype(BF16)
    qng = jnp.pad(q_norm_gain, (0, pad))[None, :]
    kng = jnp.pad(k_norm_gain, (0, pad))[None, :]
    def partner(w, sign):
        first, second = w[..., ROPE_LO:ROPE_LO + ROPE_HALF], w[..., ROPE_LO + ROPE_HALF:ROPE_LO + MLA_ROPE]
        return jnp.concatenate([jnp.zeros_like(w[..., :ROPE_LO]), sign * second, first,
                                jnp.zeros_like(w[..., ROPE_LO + MLA_ROPE:])], axis=-1)

    wqb_rot = partner(wqb.reshape(MLA_Q_RANK, MLA_HEADS, HEAD_TILE), -1.0).reshape(MLA_Q_RANK, -1).astype(BF16)
    qng_rot = partner(qng, 1.0)
    kng_rot = partner(kng, 1.0)
    return win, wg.astype(BF16), bg, wqb.astype(BF16), wqb_rot, wkvk, wkvv, qng, qng_rot, kng, kng_rot


def _rope_consts():
    inv = ROPE_BASE ** (-np.arange(0, MLA_ROPE, 2, dtype=np.float32) / MLA_ROPE)
    place = np.zeros((ROPE_HALF, LANES), np.float32)
    place[np.arange(ROPE_HALF), ROPE_LO + np.arange(ROPE_HALF)] = 1.0
    place[np.arange(ROPE_HALF), ROPE_LO + ROPE_HALF + np.arange(ROPE_HALF)] = 1.0
    ones_bd = np.kron(np.eye(2, dtype=np.float32), np.ones((LANES, LANES), np.float32))
    return jnp.asarray(inv[:, None]), jnp.asarray(place, BF16), jnp.asarray(ones_bd, BF16)


def _layer(x2, pos2, batch, seq, norm1_gain, w_in, gk_fwd_w, gk_fwd_b, gk_bwd_w, gk_bwd_b, gla_out_gain,
           mla_q_gain, mla_w_qb, mla_kv_gain, mla_w_kvb, q_norm_gain, k_norm_gain, w_out, norm2_gain,
           w_router_group, b_router_group, w_router_expert, b_router_expert,
           w_expert_gate, w_expert_up, w_expert_down):
    T, D = x2.shape
    win, wg, bg, wqb, wqb_rot, wkvk, wkvv, qng, qng_rot, kng, kng_rot = _prep_weights(
        w_in, gk_fwd_w, gk_fwd_b, gk_bwd_w, gk_bwd_b, mla_w_qb, mla_w_kvb, q_norm_gain, k_norm_gain)

    TM = 512
    spb = seq // TM
    tok = lambda w: pl.BlockSpec((TM, w), lambda i: (i, 0))
    head_spec = pl.BlockSpec((1, MLA_HEADS, TM, HEAD_TILE), lambda i: (i // spb, 0, i % spb, 0))
    gq, gk, gv, gg, lg, mq, mk, mv = pl.pallas_call(
        _proj_kernel,
        grid=(T // TM,),
        in_specs=[tok(D), pl.BlockSpec((1, 1, TM), lambda i: (i, 0, 0)), _full((1, D)), _full((D, PROJ_WIDTH)),
                  _full((LANES, 2 * GLA_QK_WIDTH)), _full((1, 2 * GLA_QK_WIDTH)), _full((1, MLA_Q_RANK)),
                  _full((MLA_Q_RANK, MLA_HEADS * HEAD_TILE)), _full((MLA_Q_RANK, MLA_HEADS * HEAD_TILE)),
                  _full((1, MLA_KV_RANK)), _full((MLA_KV_RANK, MLA_HEADS * HEAD_TILE)),
                  _full((MLA_KV_RANK, MLA_WIDTH)), _full((1, HEAD_TILE)), _full((1, HEAD_TILE)),
                  _full((1, HEAD_TILE)), _full((1, HEAD_TILE)),
                  _full((ROPE_HALF, 1)), _full((ROPE_HALF, LANES)), _full((2 * LANES, 2 * LANES))],
        out_specs=[tok(GLA_QK_WIDTH), tok(GLA_QK_WIDTH), tok(GLA_WIDTH), tok(GLA_WIDTH), tok(2 * GLA_QK_WIDTH),
                   head_spec, head_spec, head_spec],
        out_shape=[jax.ShapeDtypeStruct((T, GLA_QK_WIDTH), BF16), jax.ShapeDtypeStruct((T, GLA_QK_WIDTH), BF16),
                   jax.ShapeDtypeStruct((T, GLA_WIDTH), BF16), jax.ShapeDtypeStruct((T, GLA_WIDTH), BF16),
                   jax.ShapeDtypeStruct((T, 2 * GLA_QK_WIDTH), F32),
                   jax.ShapeDtypeStruct((batch, MLA_HEADS, seq, HEAD_TILE), BF16),
                   jax.ShapeDtypeStruct((batch, MLA_HEADS, seq, HEAD_TILE), BF16),
                   jax.ShapeDtypeStruct((batch, MLA_HEADS, seq, HEAD_TILE), BF16)],
        compiler_params=pltpu.CompilerParams(dimension_semantics=("parallel",), vmem_limit_bytes=VMEM_LIMIT),
        name="proj",
    )(x2, pos2.reshape(T // TM, 1, TM), norm1_gain[None, :], win, wg, bg, mla_q_gain[None, :], wqb, wqb_rot,
      mla_kv_gain[None, :], wkvk, wkvv, qng, qng_rot, kng, kng_rot, *_rope_consts())

    seqspec = lambda w: pl.BlockSpec((seq, w), lambda b: (b, 0))
    gla_out = pl.pallas_call(
        _gla_kernel,
        grid=(batch,),
        in_specs=[seqspec(GLA_QK_WIDTH), seqspec(GLA_QK_WIDTH), seqspec(GLA_WIDTH), seqspec(2 * GLA_QK_WIDTH),
                  seqspec(GLA_WIDTH), _full((1, GLA_DV))],
        out_specs=seqspec(GLA_WIDTH),
        out_shape=jax.ShapeDtypeStruct((T, GLA_WIDTH), BF16),
        scratch_shapes=[pltpu.VMEM((seq, GLA_WIDTH), F32), pltpu.VMEM((seq, GLA_WIDTH), F32),
                        pltpu.VMEM((GLA_DV, GLA_QK_WIDTH), F32), pltpu.VMEM((GLA_DV, GLA_QK_WIDTH), F32)],
        compiler_params=pltpu.CompilerParams(dimension_semantics=("parallel",), vmem_limit_bytes=VMEM_LIMIT),
        name="gla",
    )(gq, gk, gv, lg, gg, gla_out_gain[None, :])

    TQ = 256
    nq = seq // TQ
    mla_out = pl.pallas_call(
        _attn_kernel,
        grid=(batch, nq),
        in_specs=[pl.BlockSpec((1, MLA_HEADS, TQ, HEAD_TILE), lambda b, i: (b, 0, i, 0)),
                  pl.BlockSpec((1, MLA_HEADS, seq, HEAD_TILE), lambda b, i: (b, 0, 0, 0)),
                  pl.BlockSpec((1, MLA_HEADS, seq, HEAD_TILE), lambda b, i: (b, 0, 0, 0))],
        out_specs=pl.BlockSpec((TQ, MLA_WIDTH), lambda b, i: (b * nq + i, 0)),
        out_shape=jax.ShapeDtypeStruct((T, MLA_WIDTH), BF16),
        compiler_params=pltpu.CompilerParams(dimension_semantics=("parallel", "parallel"),
                                             vmem_limit_bytes=VMEM_LIMIT),
        name="attn",
    )(mq, mk, mv)

    wr = jnp.zeros((ROUTER_ROWS, D), F32)
    wr = wr.at[0:N_GROUPS].set(w_router_group.T)
    wr = wr.at[ROUTER_EXPERT_ROW:ROUTER_EXPERT_ROW + N_EXPERTS].set(w_router_expert.T)
    wr_hi, wr_lo = _split_bf16(wr)
    br = jnp.zeros((ROUTER_ROWS, 1), F32)
    br = br.at[0:N_GROUPS, 0].set(b_router_group)
    br = br.at[ROUTER_EXPERT_ROW:ROUTER_EXPERT_ROW + N_EXPERTS, 0].set(b_router_expert)
    tb = min(MOE_SUPER_BLOCK, T)
    nsb = T // tb
    TO = min(OUTPROJ_TILE, tb)
    tiles_per_sb = tb // TO
    tok = lambda w: pl.BlockSpec((TO, w), lambda i: (i, 0))
    tri = jnp.asarray(np.triu(np.ones((TO, TO), np.float32)), BF16)
    x1, h2, route, cnt = pl.pallas_call(
        functools.partial(_outproj_kernel, tiles_per_sb),
        grid=(T // TO,),
        in_specs=[tok(D), tok(GLA_WIDTH), tok(MLA_WIDTH), _full((D, D)), _full((1, D)),
                  _full((ROUTER_ROWS, D)), _full((ROUTER_ROWS, D)), _full((ROUTER_ROWS, 1)), _full((TO, TO))],
        out_specs=[tok(D), tok(D), pl.BlockSpec((8, TO), lambda i: (0, i)),
                   pl.BlockSpec((1, N_EXPERTS, LANES), lambda i: (i // tiles_per_sb, 0, 0))],
        out_shape=[jax.ShapeDtypeStruct((T, D), F32), jax.ShapeDtypeStruct((T, D), BF16),
                   jax.ShapeDtypeStruct((8, T), F32), jax.ShapeDtypeStruct((nsb, N_EXPERTS, LANES), F32)],
        scratch_shapes=[pltpu.VMEM((N_EXPERTS, 1), F32)],
        compiler_params=pltpu.CompilerParams(dimension_semantics=("arbitrary",), vmem_limit_bytes=VMEM_LIMIT),
        name="outproj",
    )(x2, gla_out, mla_out, w_out.astype(BF16), norm2_gain[None, :], wr_hi, wr_lo, br, tri)

    strict = jnp.asarray(np.tril(np.ones((N_EXPERTS, N_EXPERTS), np.float32), -1), BF16)
    pos, offs = pl.pallas_call(
        _slots_kernel,
        grid=(nsb,),
        in_specs=[pl.BlockSpec((8, tb), lambda s: (0, s)), pl.BlockSpec((1, N_EXPERTS, LANES), lambda s: (s, 0, 0)),
                  _full((N_EXPERTS, N_EXPERTS))],
        out_specs=[pl.BlockSpec((1, 2, tb), lambda s: (s, 0, 0)),
                   pl.BlockSpec((1, N_EXPERTS, LANES), lambda s: (s, 0, 0))],
        out_shape=[jax.ShapeDtypeStruct((nsb, 2, tb), jnp.int32),
                   jax.ShapeDtypeStruct((nsb, N_EXPERTS, LANES), jnp.int32)],
        compiler_params=pltpu.CompilerParams(dimension_semantics=("parallel",)),
        name="slots",
    )(route, cnt, strict)
    pos = pos.reshape(-1)
    flat_off = offs[:, :, 0].reshape(-1)
    counts = cnt[:, :, 0].astype(jnp.int32)
    wts = jnp.stack([route[2], route[3]], axis=1)
    wge = w_expert_gate.reshape(N_EXPERTS, D, D_EXPERT)
    wue = w_expert_up.reshape(N_EXPERTS, D, D_EXPERT)
    wde = w_expert_down.reshape(N_EXPERTS, D_EXPERT, D)
    nct = tb // MOE_COMBINE_TILE
    nst = tb // MOE_SCATTER_TILE
    slots =(2 * tb + N_EXPERTS + MOE_CHUNK + 7) // 8 * 8
    hidx = lambda sb, j, off, cn: (sb * nst + jnp.minimum(j, nst - 1), 0)
    eidx = lambda sb, j, off, cn: (jnp.clip(j - nst, 0, N_EXPERTS - 1), 0, 0)
    tidx = lambda sb, j, off, cn: (sb * nct + jnp.clip(j - nst - N_EXPERTS, 0, nct - 1), 0)
    out = pl.pallas_call(
        functools.partial(_moe_kernel, tb),
        grid_spec=pltpu.PrefetchScalarGridSpec(
            num_scalar_prefetch=2,
            grid=(nsb, nst + N_EXPERTS + nct),
            in_specs=[pl.BlockSpec((MOE_SCATTER_TILE, D), hidx),
                      pl.BlockSpec((2 * tb,), lambda sb, j, off, cn: (sb,), memory_space=pltpu.SMEM),
                      pl.BlockSpec((1, D, D_EXPERT), eidx), pl.BlockSpec((1, D, D_EXPERT), eidx),
                      pl.BlockSpec((1, D_EXPERT, D), eidx),
                      pl.BlockSpec((MOE_COMBINE_TILE, D), tidx), pl.BlockSpec((MOE_COMBINE_TILE, 2), tidx)],
            out_specs=pl.BlockSpec((MOE_COMBINE_TILE, D), tidx),
            scratch_shapes=[pltpu.VMEM((slots * PACK_ROWS, LANES), U32),
                            pltpu.VMEM((2, max(MOE_SCATTER_TILE, MOE_COMBINE_TILE) * PACK_ROWS, LANES), U32)]),
        out_shape=jax.ShapeDtypeStruct((T, D), F32),
        compiler_params=pltpu.CompilerParams(dimension_semantics=("arbitrary", "arbitrary"),
                                             vmem_limit_bytes=VMEM_LIMIT),
        name="moe",
    )(flat_off, counts.reshape(-1), h2, pos, wge, wue, wde, x1, wts)
    return out


def kernel(x, positions, norm1_gain, w_in, gla_gk_fwd_w, gla_gk_fwd_b, gla_gk_bwd_w, gla_gk_bwd_b, gla_out_gain, mla_q_gain, mla_w_qb, mla_kv_gain, mla_w_kvb, q_norm_gain, k_norm_gain, w_out, norm2_gain, w_router_group, b_router_group, w_router_expert, b_router_expert, w_expert_gate, w_expert_up, w_expert_down):
    batch, seq, d = x.shape
    x2 = x.reshape(batch * seq, d)
    pos2 = positions.reshape(batch * seq, 1)
    for l in range(norm1_gain.shape[0]):
        x2 = _layer(x2, pos2, batch, seq, norm1_gain[l], w_in[l], gla_gk_fwd_w[l], gla_gk_fwd_b[l],
                    gla_gk_bwd_w[l], gla_gk_bwd_b[l], gla_out_gain[l], mla_q_gain[l], mla_w_qb[l],
                    mla_kv_gain[l], mla_w_kvb[l], q_norm_gain[l], k_norm_gain[l], w_out[l], norm2_gain[l],
                    w_router_group[l], b_router_group[l], w_router_expert[l], b_router_expert[l],
                    w_expert_gate[l], w_expert_up[l], w_expert_down[l])
    return x2.reshape(batch, seq, d)
```

```python
import functools

import numpy as np
import jax
import jax.numpy as jnp
from jax import lax
from jax.experimental import pallas as pl
from jax.experimental.pallas import tpu as pltpu

F32 = jnp.float32
BF16 = jnp.bfloat16

D_MODEL = 1024
GLA_HEADS = 4
GLA_DK = 64
GLA_DV = 128
GLA_GATE_RANK = 16
GLA_GATE_NORMALIZER = 16.0
GLA_CHUNK = 64
GLA_CHUNK_UNROLL = 4
GLA_FINISH_ROWS = 256
MLA_HEADS = 8
MLA_NOPE = 64
MLA_ROPE = 32
MLA_QK = MLA_NOPE + MLA_ROPE
MLA_V = 64
MLA_Q_RANK = 256
MLA_KV_RANK = 128
ROPE_BASE = 10000.0
GLA_QK_WIDTH = GLA_HEADS * GLA_DK
GLA_WIDTH = GLA_HEADS * GLA_DV
MLA_WIDTH = MLA_HEADS * MLA_V
N_GROUPS = 4
EXPERTS_PER_GROUP = 8
N_EXPERTS = N_GROUPS * EXPERTS_PER_GROUP
D_EXPERT = 256
EPS = 1e-6

LANES = 128
HEAD_TILE = LANES
ROPE_LO = MLA_NOPE
ROPE_HALF = MLA_ROPE // 2
LOG2_E = 1.4426950408889634
ONES_ROWS = 16

COL_GQ = 0
COL_GK = COL_GQ + GLA_QK_WIDTH
COL_GV = COL_GK + GLA_QK_WIDTH
COL_GG = COL_GV + GLA_WIDTH
COL_MQ = COL_GG + GLA_WIDTH
COL_MKV = COL_MQ + MLA_Q_RANK
COL_MISC = COL_MKV + MLA_KV_RANK
PROJ_WIDTH = COL_MISC + LANES
MISC_LRB = GLA_GATE_RANK

ROUTER_ROWS = LANES
ROUTER_EXPERT_ROW = 8

U32 = jnp.uint32
HI_HALF_MASK = np.uint32(0xFFFF0000)
PACK_ROWS = D_MODEL // (2 * LANES)
MOE_SUPER_BLOCK = 8192
MOE_CHUNK = 576
MOE_SCATTER_TILE = 512
MOE_COMBINE_TILE = 256
OUTPROJ_TILE = 1024
MOE_ROW_UNROLL = 8

VMEM_LIMIT = 56 * 1024 * 1024


def _dot(a, b):
    return jnp.dot(a, b, preferred_element_type=F32)


def _dot_nt(a, b):
    return lax.dot_general(a, b, (((1,), (1,)), ((), ())), preferred_element_type=F32)


def _dot_tn(a, b):
    return lax.dot_general(a, b, (((0,), (0,)), ((), ())), preferred_element_type=F32)


def _split_bf16(x):
    hi = x.astype(BF16)
    lo = (x - hi.astype(F32)).astype(BF16)
    return hi, lo


def _proj_kernel(x_ref, pos_ref, n1_ref, win_ref, wg_ref, bg_ref, qgain_ref, wqb_ref, wqbr_ref, kvgain_ref,
                 wkvk_ref, wkvv_ref, qng_ref, qngr_ref, kng_ref, kngr_ref, freq_ref, place_ref, ones_ref,
                 gq_ref, gk_ref, gv_ref, gg_ref, lg_ref, mq_ref, mk_ref, mv_ref):
    x = x_ref[...]
    h = x * lax.rsqrt(jnp.mean(x * x, axis=-1, keepdims=True) + EPS) * n1_ref[...]
    proj = _dot(h.astype(BF16), win_ref[...])

    gq_ref[...] = (proj[:, COL_GQ:COL_GK] * GLA_DK ** -0.5).astype(BF16)
    gk_ref[...] = proj[:, COL_GK:COL_GV].astype(BF16)
    gv_ref[...] = proj[:, COL_GV:COL_GG].astype(BF16)
    gg_ref[...] = proj[:, COL_GG:COL_MQ].astype(BF16)

    misc = proj[:, COL_MISC:PROJ_WIDTH]
    z = _dot(misc.astype(BF16), wg_ref[...]) + bg_ref[...]
    log_sig = -(jnp.maximum(-z, 0.0) + jnp.log(1.0 + jnp.exp(-jnp.abs(z))))
    lg_ref[...] = log_sig / GLA_GATE_NORMALIZER

    lane = lax.broadcasted_iota(jnp.int32, (1, LANES), 1)
    ang_t = freq_ref[...] * pos_ref[0].astype(F32)
    place = place_ref[...]

    def to_rows(tab_t):
        hi, lo = _split_bf16(tab_t)
        return _dot_tn(hi, place) + _dot_tn(lo, place)

    in_rope = (lane >= ROPE_LO) & (lane < ROPE_LO + MLA_ROPE)
    c_tab = jnp.where(lane < ROPE_LO, 1.0, to_rows(jnp.cos(ang_t)))
    s_tab = to_rows(jnp.sin(ang_t))
    ones_bd = ones_ref[...]

    def norm_rope_pair(t2, r2, gc, gs):
        ss = _dot((t2 * t2).astype(BF16), ones_bd) * (1.0 / MLA_QK)
        rs = lax.rsqrt(ss + EPS)
        outs = []
        for i in range(2):
            sl = slice(i * LANES, (i + 1) * LANES)
            outs.append(rs[:, sl] * (t2[:, sl] * gc + r2[:, sl] * gs))
        return outs

    qa = proj[:, COL_MQ:COL_MKV]
    qn = qa * lax.rsqrt(jnp.mean(qa * qa, axis=-1, keepdims=True) + EPS) * qgain_ref[...]
    qn = qn.astype(BF16)
    mq = _dot(qn, wqb_ref[...])
    mq_rot = _dot(qn, wqbr_ref[...])
    kva = proj[:, COL_MKV:COL_MISC]
    kvn = (kva * lax.rsqrt(jnp.mean(kva * kva, axis=-1, keepdims=True) + EPS) * kvgain_ref[...]).astype(BF16)
    kn = _dot(kvn, wkvk_ref[...])
    mv_ref[0] = _dot_nt(wkvv_ref[...], kvn).astype(BF16)
    rope_tile = jnp.where(in_rope, misc, 0.0)
    up = (lane >= ROPE_LO + ROPE_HALF) & (lane < ROPE_LO + MLA_ROPE)
    dn = (lane >= ROPE_LO) & (lane < ROPE_LO + ROPE_HALF)
    rope_rot = (jnp.where(up, pltpu.roll(rope_tile, ROPE_HALF, 1), 0.0)
                - jnp.where(dn, pltpu.roll(rope_tile, LANES - ROPE_HALF, 1), 0.0))
    rope2 = jnp.concatenate([rope_tile, rope_tile], axis=1)
    rope_rot2 = jnp.concatenate([rope_rot, rope_rot], axis=1)
    scale = MLA_QK ** -0.5 * LOG2_E
    q_gc, q_gs = (qng_ref[...] * scale) * c_tab, (qngr_ref[...] * scale) * s_tab
    k_gc, k_gs = kng_ref[...] * c_tab, kngr_ref[...] * s_tab
    for hp in range(MLA_HEADS // 2):
        sl = slice(hp * 2 * HEAD_TILE, (hp + 1) * 2 * HEAD_TILE)
        q2 = norm_rope_pair(mq[:, sl], mq_rot[:, sl], q_gc, q_gs)
        k2 = norm_rope_pair(kn[:, sl] + rope2, rope_rot2, k_gc, k_gs)
        for i in range(2):
            mq_ref[0, 2 * hp + i] = q2[i].astype(BF16)
            mk_ref[0, 2 * hp + i] = k2[i].astype(BF16)


def _gla_kernel(q_ref, k_ref, v_ref, lg_ref, gate_ref, gain_ref, o_ref, accf_ref, accb_ref, stf_ref, stb_ref):
    seq = q_ref.shape[0]
    C = GLA_CHUNK
    n_chunks = seq // C
    HK = GLA_QK_WIDTH

    lane_head = lax.broadcasted_iota(jnp.int32, (1, HK), 1) // GLA_DK
    rowi = lax.broadcasted_iota(jnp.int32, (C, 1), 0)
    srow = lax.broadcasted_iota(jnp.int32, (GLA_HEADS * C, C), 0) % C
    scol = lax.broadcasted_iota(jnp.int32, (GLA_HEADS * C, C), 1)
    fwd_cfg = (True, C // 2 - 1, C - 1, scol <= srow, 0)
    bwd_cfg = (False, C // 2, 0, scol > srow, HK)

    def stack_heads(t):
        return jnp.concatenate([jnp.where(lane_head == hd, t, 0.0) for hd in range(GLA_HEADS)], axis=0)

    def chunk_step(c, st, cfg, acc_ref):
        prefix, ref_i, last_i, keep, lg_off = cfg
        rows = pl.ds(pl.multiple_of(c * C, C), C)
        b = lg_ref[rows, lg_off:lg_off + HK]
        shift = 1
        while shift < C:
            if prefix:
                b = b + jnp.where(rowi >= shift, pltpu.roll(b, shift, 0), 0.0)
            else:
                b = b + jnp.where(rowi < C - shift, pltpu.roll(b, C - shift, 0), 0.0)
            shift *= 2
        b_ref = b[ref_i:ref_i + 1, :]
        b_last = b[last_i:last_i + 1, :]
        q = q_ref[rows, :].astype(F32)
        k = k_ref[rows, :].astype(F32)
        v = v_ref[rows, :]
        q_rel = q * jnp.exp(b - b_ref)
        k_rel = (k * jnp.exp(b_ref - b)).astype(BF16)
        k_dec = (k * jnp.exp(b_last - b)).astype(BF16)
        q_dec = q * jnp.exp(b)
        sc = _dot_nt(stack_heads(q_rel).astype(BF16), k_rel)
        sc = jnp.where(keep, sc, 0.0).astype(BF16)
        o_inter = _dot_nt(stack_heads(q_dec).astype(BF16), st.astype(BF16))
        kv_t = _dot_tn(v, k_dec)
        new_st = jnp.exp(b_last) * st
        for hd in range(GLA_HEADS):
            vs = slice(hd * GLA_DV, (hd + 1) * GLA_DV)
            rs = slice(hd * C, (hd + 1) * C)
            acc_ref[rows, vs] = _dot(sc[rs, :], v[:, vs]) + o_inter[rs, :]
            new_st = new_st + jnp.where(lane_head == hd, kv_t[vs, :], 0.0)
        return new_st

    stf_ref[...] = jnp.zeros_like(stf_ref)
    stb_ref[...] = jnp.zeros_like(stb_ref)

    def body(i, carry):
        st_f = stf_ref[...]
        st_b = stb_ref[...]
        for u in range(GLA_CHUNK_UNROLL):
            cf = i * GLA_CHUNK_UNROLL + u
            st_f = chunk_step(cf, st_f, fwd_cfg, accf_ref)
            st_b = chunk_step(n_chunks - 1 - cf, st_b, bwd_cfg, accb_ref)
        stf_ref[...] = st_f
        stb_ref[...] = st_b
        return carry

    lax.fori_loop(0, n_chunks // GLA_CHUNK_UNROLL, body, 0)

    gain = gain_ref[...]

    def finish(r, carry):
        rows = pl.ds(pl.multiple_of(r * GLA_FINISH_ROWS, GLA_FINISH_ROWS), GLA_FINISH_ROWS)
        for hd in range(GLA_HEADS):
            vs = slice(hd * GLA_DV, (hd + 1) * GLA_DV)
            o = accf_ref[rows, vs] + accb_ref[rows, vs]
            on = o * lax.rsqrt(jnp.mean(o * o, axis=-1, keepdims=True) + EPS) * gain
            gt = gate_ref[rows, vs].astype(F32)
            o_ref[rows, vs] = (on * (gt * jax.nn.sigmoid(gt))).astype(BF16)
        return carry

    lax.fori_loop(0, seq // GLA_FINISH_ROWS, finish, 0)


def _attn_kernel(q_ref, k_ref, vt_ref, o_ref):
    seq = k_ref.shape[2]
    ones = jnp.ones((ONES_ROWS, seq), BF16)
    outs = []
    for hd in range(MLA_HEADS):
        s = _dot_nt(q_ref[0, hd], k_ref[0, hd])
        p = jnp.exp2(s - jnp.max(s, axis=-1, keepdims=True)).astype(BF16)
        vt_aug = jnp.concatenate([vt_ref[0, hd * MLA_V:(hd + 1) * MLA_V, :], ones], axis=0)
        ot = _dot_nt(vt_aug, p)
        outs.append(ot[0:MLA_V, :] / ot[MLA_V:MLA_V + 1, :])
    o_ref[0] = jnp.concatenate(outs, axis=0).astype(BF16)


def _unpack_bf16_pair(w):
    lo = pltpu.unpack_elementwise(w, index=0, packed_dtype=BF16, unpacked_dtype=F32)
    hi = pltpu.unpack_elementwise(w, index=1, packed_dtype=BF16, unpacked_dtype=F32)
    return lo, hi


def _pack_bf16_pair(lo, hi):
    return pltpu.pack_elementwise([lo, hi], packed_dtype=BF16)


def _outproj_kernel(tiles_per_sb, x_ref, gla_ref, mla_ref, wo_ref, n2_ref, wr_hi_ref, wr_lo_ref, br_ref, tri_ref,
                    x1_ref, h2_ref, route_ref, cnt_ref, carry_ref):
    mix = _dot(gla_ref[...], wo_ref[0:GLA_WIDTH, :]) + _dot_tn(mla_ref[0], wo_ref[GLA_WIDTH:, :])
    x1 = x_ref[...] + mix
    x1_ref[...] = x1
    h2 = x1 * lax.rsqrt(jnp.mean(x1 * x1, axis=-1, keepdims=True) + EPS) * n2_ref[...]
    h2_ref[...] = h2.astype(BF16)

    h_hi, h_lo = _split_bf16(h2)
    w_hi = wr_hi_ref[...]
    logits = (_dot_nt(w_hi, h_hi) + _dot_nt(w_hi, h_lo) + _dot_nt(wr_lo_ref[...], h_hi)) + br_ref[...]
    tm = logits.shape[1]
    gl = logits[0:N_GROUPS, :]
    ge = jnp.exp(gl - jnp.max(gl, axis=0, keepdims=True))
    pg = ge / jnp.sum(ge, axis=0, keepdims=True)
    p_top = jnp.max(pg, axis=0, keepdims=True)
    gi = lax.broadcasted_iota(jnp.int32, (N_GROUPS, tm), 0)
    g_idx = jnp.min(jnp.where(pg == p_top, gi, N_GROUPS), axis=0, keepdims=True)
    sel = jnp.zeros((EXPERTS_PER_GROUP, tm), F32)
    for g in range(N_GROUPS):
        r0 = ROUTER_EXPERT_ROW + g * EXPERTS_PER_GROUP
        sel = sel + jnp.where(g_idx == g, logits[r0:r0 + EXPERTS_PER_GROUP, :], 0.0)
    se = jnp.exp(sel - jnp.max(sel, axis=0, keepdims=True))
    pe = se / jnp.sum(se, axis=0, keepdims=True)
    ei = lax.broadcasted_iota(jnp.int32, (EXPERTS_PER_GROUP, tm), 0)
    m1 = jnp.max(pe, axis=0, keepdims=True)
    i1 = jnp.min(jnp.where(pe == m1, ei, EXPERTS_PER_GROUP), axis=0, keepdims=True)
    pe2 = jnp.where(ei == i1, -1.0, pe)
    m2 = jnp.max(pe2, axis=0, keepdims=True)
    i2 = jnp.min(jnp.where(pe2 == m2, ei, EXPERTS_PER_GROUP), axis=0, keepdims=True)
    den = m1 + m2
    e1 = g_idx * EXPERTS_PER_GROUP + i1
    e2 = g_idx * EXPERTS_PER_GROUP + i2

    @pl.when(pl.program_id(0) % tiles_per_sb == 0)
    def _():
        carry_ref[...] = jnp.zeros_like(carry_ref)

    eall = lax.broadcasted_iota(jnp.int32, (N_EXPERTS, tm), 0)
    tri = tri_ref[...]
    carry = carry_ref[...]
    ranks = []
    for eid in (e1, e2):
        hot = eall == eid
        prefix = _dot(hot.astype(BF16), tri)
        ranks.append(jnp.sum(jnp.where(hot, carry + prefix - 1.0, 0.0), axis=0, keepdims=True))
        carry = carry + prefix[:, tm - 1:tm]
    carry_ref[...] = carry
    cnt_ref[0] = jnp.broadcast_to(carry, (N_EXPERTS, LANES))
    zeros = jnp.zeros((2, tm), F32)
    route_ref[...] = jnp.concatenate([e1.astype(F32), e2.astype(F32), p_top * (m1 / den), p_top * (m2 / den),
                                      ranks[0], ranks[1], zeros], axis=0)


def _slots_kernel(route_ref, cnt_ref, strict_ref, pos_ref, off_ref):
    counts = cnt_ref[0]
    padded = 2.0 * jnp.floor(counts * 0.5 + 0.5)
    hi = jnp.floor(padded * (1.0 / 256.0))
    lo = padded - 256.0 * hi
    strict = strict_ref[...]
    off = 256.0 * _dot(strict, hi.astype(BF16)) + _dot(strict, lo.astype(BF16))
    off_ref[0] = off.astype(jnp.int32)
    tb = route_ref.shape[1]
    eall = lax.broadcasted_iota(jnp.int32, (N_EXPERTS, tb), 0)
    off_col = off[:, 0:1]
    for s in range(2):
        hot = eall == route_ref[s:s + 1, :].astype(jnp.int32)
        base = jnp.sum(jnp.where(hot, off_col, 0.0), axis=0, keepdims=True)
        pos_ref[0, s:s + 1, :] = ((base + route_ref[4 + s:5 + s, :]) * PACK_ROWS).astype(jnp.int32)


def _moe_kernel(tb, off_ref, cnt_ref, h2_ref, pos_ref, wg_ref, wu_ref, wd_ref, x1_ref, wt_ref, o_ref,
                xy_ref, cb_ref):
    sb = pl.program_id(0)
    j = pl.program_id(1)
    n_scatter = tb // MOE_SCATTER_TILE
    first_expert_step = n_scatter
    first_combine_step = n_scatter + N_EXPERTS

    @pl.when(j == 0)
    def _clear():
        xy_ref[...] = jnp.zeros_like(xy_ref)

    @pl.when(j < n_scatter)
    def _scatter():
        t0 = j * MOE_SCATTER_TILE
        h = h2_ref[...].astype(F32)
        for r in range(PACK_ROWS):
            c0 = r * 2 * LANES
            cb_ref[0, pl.ds(r, MOE_SCATTER_TILE, stride=PACK_ROWS), :] = _pack_bf16_pair(
                h[:, c0:c0 + LANES], h[:, c0 + LANES:c0 + 2 * LANES])

        def body(g, c):
            for i in range(MOE_ROW_UNROLL // 2):
                pair = g * (MOE_ROW_UNROLL // 2) + i
                v = cb_ref[0, pl.ds(pl.multiple_of(pair * 2 * PACK_ROWS, 2 * PACK_ROWS), 2 * PACK_ROWS), :]
                for half in range(2):
                    t = t0 + pair * 2 + half
                    piece = v[half * PACK_ROWS:(half + 1) * PACK_ROWS, :]
                    for slot in range(2):
                        p = pos_ref[slot * tb + t]
                        xy_ref[pl.ds(pl.multiple_of(p, PACK_ROWS), PACK_ROWS), :] = piece
            return c

        lax.fori_loop(0, MOE_SCATTER_TILE // MOE_ROW_UNROLL, body, 0)

    @pl.when((j >= first_expert_step) & (j < first_combine_step))
    def _experts():
        e = sb * N_EXPERTS + j - first_expert_step
        n = cnt_ref[e]
        base = off_ref[e]

        def body(i, c):
            r0 = base + i * MOE_CHUNK
            row0 = pl.multiple_of(r0 * PACK_ROWS, 2 * PACK_ROWS)
            halves = [_unpack_bf16_pair(xy_ref[pl.ds(row0 + r, MOE_CHUNK, stride=PACK_ROWS), :])
                      for r in range(PACK_ROWS)]
            a = jnp.zeros((MOE_CHUNK, D_EXPERT), F32)
            u = jnp.zeros((MOE_CHUNK, D_EXPERT), F32)
            for r in range(PACK_ROWS):
                lo, hi = halves[r]
                xb = jnp.concatenate([lo, hi], axis=1).astype(BF16)
                a = a + _dot(xb, wg_ref[0, r * 2 * LANES:(r + 1) * 2 * LANES, :].astype(BF16))
                u = u + _dot(xb, wu_ref[0, r * 2 * LANES:(r + 1) * 2 * LANES, :].astype(BF16))
            hid = ((a * jax.nn.sigmoid(a)) * u).astype(BF16)
            y = _dot(hid, wd_ref[0].astype(BF16))
            valid = (r0 + lax.broadcasted_iota(jnp.int32, (MOE_CHUNK, 1), 0)) < (base + n)
            for r in range(PACK_ROWS):
                c0 = r * 2 * LANES
                lo, hi = halves[r]
                xy_ref[pl.ds(row0 + r, MOE_CHUNK, stride=PACK_ROWS), :] = _pack_bf16_pair(
                    jnp.where(valid, y[:, c0:c0 + LANES], lo), jnp.where(valid, y[:, c0 + LANES:c0 + 2 * LANES], hi))
            return c

        lax.fori_loop(0, (n + MOE_CHUNK - 1) // MOE_CHUNK, body, 0)

    @pl.when(j >= first_combine_step)
    def _combine():
        t0 = (j - first_combine_step) * MOE_COMBINE_TILE

        def body(g, c):
            for i in range(MOE_ROW_UNROLL):
                t = g * MOE_ROW_UNROLL + i
                for slot in range(2):
                    p = pos_ref[slot * tb + t0 + t]
                    cb_ref[slot, pl.ds(pl.multiple_of(t * PACK_ROWS, PACK_ROWS), PACK_ROWS), :] = (
                        xy_ref[pl.ds(pl.multiple_of(p, PACK_ROWS), PACK_ROWS), :])
            return c

        lax.fori_loop(0, MOE_COMBINE_TILE // MOE_ROW_UNROLL, body, 0)
        w1 = wt_ref[:, 0:1]
        w2 = wt_ref[:, 1:2]
        for r in range(PACK_ROWS):
            lo1, hi1 = _unpack_bf16_pair(cb_ref[0, pl.ds(r, MOE_COMBINE_TILE, stride=PACK_ROWS), :])
            lo2, hi2 = _unpack_bf16_pair(cb_ref[1, pl.ds(r, MOE_COMBINE_TILE, stride=PACK_ROWS), :])
            c0 = r * 2 * LANES
            o_ref[:, c0:c0 + LANES] = x1_ref[:, c0:c0 + LANES] + w1 * lo1 + w2 * lo2
            o_ref[:, c0 + LANES:c0 + 2 * LANES] = x1_ref[:, c0 + LANES:c0 + 2 * LANES] + w1 * hi1 + w2 * hi2


def _full(shape):
    return pl.BlockSpec(shape, lambda *_: (0,) * len(shape))


def _prep_weights(w_in, gk_fwd_w, gk_fwd_b, gk_bwd_w, gk_bwd_b, mla_w_qb, mla_w_kvb, q_norm_gain, k_norm_gain):
    splits = np.cumsum([0, GLA_QK_WIDTH, GLA_QK_WIDTH, GLA_WIDTH, GLA_WIDTH, GLA_GATE_RANK, GLA_GATE_RANK,
                        MLA_Q_RANK, MLA_KV_RANK, MLA_ROPE])
    seg = [w_in[:, splits[i]:splits[i + 1]] for i in range(9)]
    d = w_in.shape[0]
    misc = jnp.zeros((d, LANES), F32)
    misc = misc.at[:, 0:GLA_GATE_RANK].set(seg[4])
    misc = misc.at[:, MISC_LRB:MISC_LRB + GLA_GATE_RANK].set(seg[5])
    misc = misc.at[:, ROPE_LO:ROPE_LO + MLA_ROPE].set(seg[8])
    win = jnp.concatenate([seg[0], seg[1], seg[2], seg[3], seg[6], seg[7], misc], axis=1).astype(BF16)

    wg = jnp.zeros((LANES, 2 * GLA_QK_WIDTH), F32)
    wg = wg.at[0:GLA_GATE_RANK, 0:GLA_QK_WIDTH].set(gk_fwd_w)
    wg = wg.at[MISC_LRB:MISC_LRB + GLA_GATE_RANK, GLA_QK_WIDTH:].set(gk_bwd_w)
    bg = jnp.concatenate([gk_fwd_b, gk_bwd_b])[None, :]

    pad = HEAD_TILE - MLA_QK
    wqb = jnp.pad(mla_w_qb.reshape(MLA_Q_RANK, MLA_HEADS, MLA_QK), ((0, 0), (0, 0), (0, pad)))
    wqb = wqb.reshape(MLA_Q_RANK, MLA_HEADS * HEAD_TILE)
    wkv = mla_w_kvb.reshape(MLA_KV_RANK, MLA_HEADS, MLA_NOPE + MLA_V)
    wkvk = jnp.pad(wkv[:, :, :MLA_NOPE], ((0, 0), (0, 0), (0, HEAD_TILE - MLA_NOPE)))
    wkvk = wkvk.reshape(MLA_KV_RANK, MLA_HEADS * HEAD_TILE).astype(BF16)
    wkvv = wkv[:, :, MLA_NOPE:].reshape(MLA_KV_RANK, MLA_WIDTH).T.astype(BF16)
    qng = jnp.pad(q_norm_gain, (0, pad))[None, :]
    kng = jnp.pad(k_norm_gain, (0, pad))[None, :]
    def partner(w, sign):
        first, second = w[..., ROPE_LO:ROPE_LO + ROPE_HALF], w[..., ROPE_LO + ROPE_HALF:ROPE_LO + MLA_ROPE]
        return jnp.concatenate([jnp.zeros_like(w[..., :ROPE_LO]), sign * second, first,
                                jnp.zeros_like(w[..., ROPE_LO + MLA_ROPE:])], axis=-1)

    wqb_rot = partner(wqb.reshape(MLA_Q_RANK, MLA_HEADS, HEAD_TILE), -1.0).reshape(MLA_Q_RANK, -1).astype(BF16)
    qng_rot = partner(qng, 1.0)
    kng_rot = partner(kng, 1.0)
    return win, wg.astype(BF16), bg, wqb.astype(BF16), wqb_rot, wkvk, wkvv, qng, qng_rot, kng, kng_rot


def _rope_consts():
    inv = ROPE_BASE ** (-np.arange(0, MLA_ROPE, 2, dtype=np.float32) / MLA_ROPE)
    place = np.zeros((ROPE_HALF, LANES), np.float32)
    place[np.arange(ROPE_HALF), ROPE_LO + np.arange(ROPE_HALF)] = 1.0
    place[np.arange(ROPE_HALF), ROPE_LO + ROPE_HALF + np.arange(ROPE_HALF)] = 1.0
    ones_bd = np.kron(np.eye(2, dtype=np.float32), np.ones((LANES, LANES), np.float32))
    return jnp.asarray(inv[:, None]), jnp.asarray(place, BF16), jnp.asarray(ones_bd, BF16)


def _layer(x2, pos2, batch, seq, norm1_gain, w_in, gk_fwd_w, gk_fwd_b, gk_bwd_w, gk_bwd_b, gla_out_gain,
           mla_q_gain, mla_w_qb, mla_kv_gain, mla_w_kvb, q_norm_gain, k_norm_gain, w_out, norm2_gain,
           w_router_group, b_router_group, w_router_expert, b_router_expert,
           w_expert_gate, w_expert_up, w_expert_down):
    T, D = x2.shape
    win, wg, bg, wqb, wqb_rot, wkvk, wkvv, qng, qng_rot, kng, kng_rot = _prep_weights(
        w_in, gk_fwd_w, gk_fwd_b, gk_bwd_w, gk_bwd_b, mla_w_qb, mla_w_kvb, q_norm_gain, k_norm_gain)

    TM = 512
    spb = seq // TM
    tok = lambda w: pl.BlockSpec((TM, w), lambda i: (i, 0))
    head_spec = pl.BlockSpec((1, MLA_HEADS, TM, HEAD_TILE), lambda i: (i // spb, 0, i % spb, 0))
    gq, gk, gv, gg, lg, mq, mk, mv = pl.pallas_call(
        _proj_kernel,
        grid=(T // TM,),
        in_specs=[tok(D), pl.BlockSpec((1, 1, TM), lambda i: (i, 0, 0)), _full((1, D)), _full((D, PROJ_WIDTH)),
                  _full((LANES, 2 * GLA_QK_WIDTH)), _full((1, 2 * GLA_QK_WIDTH)), _full((1, MLA_Q_RANK)),
                  _full((MLA_Q_RANK, MLA_HEADS * HEAD_TILE)), _full((MLA_Q_RANK, MLA_HEADS * HEAD_TILE)),
                  _full((1, MLA_KV_RANK)), _full((MLA_KV_RANK, MLA_HEADS * HEAD_TILE)),
                  _full((MLA_WIDTH, MLA_KV_RANK)), _full((1, HEAD_TILE)), _full((1, HEAD_TILE)),
                  _full((1, HEAD_TILE)), _full((1, HEAD_TILE)),
                  _full((ROPE_HALF, 1)), _full((ROPE_HALF, LANES)), _full((2 * LANES, 2 * LANES))],
        out_specs=[tok(GLA_QK_WIDTH), tok(GLA_QK_WIDTH), tok(GLA_WIDTH), tok(GLA_WIDTH), tok(2 * GLA_QK_WIDTH),
                   head_spec, head_spec, pl.BlockSpec((1, MLA_WIDTH, TM), lambda i: (i // spb, 0, i % spb))],
        out_shape=[jax.ShapeDtypeStruct((T, GLA_QK_WIDTH), BF16), jax.ShapeDtypeStruct((T, GLA_QK_WIDTH), BF16),
                   jax.ShapeDtypeStruct((T, GLA_WIDTH), BF16), jax.ShapeDtypeStruct((T, GLA_WIDTH), BF16),
                   jax.ShapeDtypeStruct((T, 2 * GLA_QK_WIDTH), F32),
                   jax.ShapeDtypeStruct((batch, MLA_HEADS, seq, HEAD_TILE), BF16),
                   jax.ShapeDtypeStruct((batch, MLA_HEADS, seq, HEAD_TILE), BF16),
                   jax.ShapeDtypeStruct((batch, MLA_WIDTH, seq), BF16)],
        compiler_params=pltpu.CompilerParams(dimension_semantics=("parallel",), vmem_limit_bytes=VMEM_LIMIT),
        name="proj",
    )(x2, pos2.reshape(T // TM, 1, TM), norm1_gain[None, :], win, wg, bg, mla_q_gain[None, :], wqb, wqb_rot,
      mla_kv_gain[None, :], wkvk, wkvv, qng, qng_rot, kng, kng_rot, *_rope_consts())

    seqspec = lambda w: pl.BlockSpec((seq, w), lambda b: (b, 0))
    gla_out = pl.pallas_call(
        _gla_kernel,
        grid=(batch,),
        in_specs=[seqspec(GLA_QK_WIDTH), seqspec(GLA_QK_WIDTH), seqspec(GLA_WIDTH), seqspec(2 * GLA_QK_WIDTH),
                  seqspec(GLA_WIDTH), _full((1, GLA_DV))],
        out_specs=seqspec(GLA_WIDTH),
        out_shape=jax.ShapeDtypeStruct((T, GLA_WIDTH), BF16),
        scratch_shapes=[pltpu.VMEM((seq, GLA_WIDTH), F32), pltpu.VMEM((seq, GLA_WIDTH), F32),
                        pltpu.VMEM((GLA_DV, GLA_QK_WIDTH), F32), pltpu.VMEM((GLA_DV, GLA_QK_WIDTH), F32)],
        compiler_params=pltpu.CompilerParams(dimension_semantics=("parallel",), vmem_limit_bytes=VMEM_LIMIT),
        name="gla",
    )(gq, gk, gv, lg, gg, gla_out_gain[None, :])

    TQ = 256
    nq = seq // TQ
    mla_out = pl.pallas_call(
        _attn_kernel,
        grid=(batch, nq),
        in_specs=[pl.BlockSpec((1, MLA_HEADS, TQ, HEAD_TILE), lambda b, i: (b, 0, i, 0)),
                  pl.BlockSpec((1, MLA_HEADS, seq, HEAD_TILE), lambda b, i: (b, 0, 0, 0)),
                  pl.BlockSpec((1, MLA_WIDTH, seq), lambda b, i: (b, 0, 0))],
        out_specs=pl.BlockSpec((1, MLA_WIDTH, TQ), lambda b, i: (b, 0, i)),
        out_shape=jax.ShapeDtypeStruct((batch, MLA_WIDTH, seq), BF16),
        compiler_params=pltpu.CompilerParams(dimension_semantics=("parallel", "parallel"),
                                             vmem_limit_bytes=VMEM_LIMIT),
        name="attn",
    )(mq, mk, mv)

    wr = jnp.zeros((ROUTER_ROWS, D), F32)
    wr = wr.at[0:N_GROUPS].set(w_router_group.T)
    wr = wr.at[ROUTER_EXPERT_ROW:ROUTER_EXPERT_ROW + N_EXPERTS].set(w_router_expert.T)
    wr_hi, wr_lo = _split_bf16(wr)
    br = jnp.zeros((ROUTER_ROWS, 1), F32)
    br = br.at[0:N_GROUPS, 0].set(b_router_group)
    br = br.at[ROUTER_EXPERT_ROW:ROUTER_EXPERT_ROW + N_EXPERTS, 0].set(b_router_expert)
    tb = min(MOE_SUPER_BLOCK, T)
    nsb = T // tb
    TO = min(OUTPROJ_TILE, seq)
    tiles_per_sb = tb // TO
    tok = lambda w: pl.BlockSpec((TO, w), lambda i: (i, 0))
    tri = jnp.asarray(np.triu(np.ones((TO, TO), np.float32)), BF16)
    x1, h2, route, cnt = pl.pallas_call(
        functools.partial(_outproj_kernel, tiles_per_sb),
        grid=(T // TO,),
        in_specs=[tok(D), tok(GLA_WIDTH),
                  pl.BlockSpec((1, MLA_WIDTH, TO), lambda i: (i // (seq // TO), 0, i % (seq // TO))),
                  _full((D, D)), _full((1, D)),
                  _full((ROUTER_ROWS, D)), _full((ROUTER_ROWS, D)), _full((ROUTER_ROWS, 1)), _full((TO, TO))],
        out_specs=[tok(D), tok(D), pl.BlockSpec((8, TO), lambda i: (0, i)),
                   pl.BlockSpec((1, N_EXPERTS, LANES), lambda i: (i // tiles_per_sb, 0, 0))],
        out_shape=[jax.ShapeDtypeStruct((T, D), F32), jax.ShapeDtypeStruct((T, D), BF16),
                   jax.ShapeDtypeStruct((8, T), F32), jax.ShapeDtypeStruct((nsb, N_EXPERTS, LANES), F32)],
        scratch_shapes=[pltpu.VMEM((N_EXPERTS, 1), F32)],
        compiler_params=pltpu.CompilerParams(dimension_semantics=("arbitrary",), vmem_limit_bytes=VMEM_LIMIT),
        name="outproj",
    )(x2, gla_out, mla_out, w_out.astype(BF16), norm2_gain[None, :], wr_hi, wr_lo, br, tri)

    strict = jnp.asarray(np.tril(np.ones((N_EXPERTS, N_EXPERTS), np.float32), -1), BF16)
    pos, offs = pl.pallas_call(
        _slots_kernel,
        grid=(nsb,),
        in_specs=[pl.BlockSpec((8, tb), lambda s: (0, s)), pl.BlockSpec((1, N_EXPERTS, LANES), lambda s: (s, 0, 0)),
                  _full((N_EXPERTS, N_EXPERTS))],
        out_specs=[pl.BlockSpec((1, 2, tb), lambda s: (s, 0, 0)),
                   pl.BlockSpec((1, N_EXPERTS, LANES), lambda s: (s, 0, 0))],
        out_shape=[jax.ShapeDtypeStruct((nsb, 2, tb), jnp.int32),
                   jax.ShapeDtypeStruct((nsb, N_EXPERTS, LANES), jnp.int32)],
        compiler_params=pltpu.CompilerParams(dimension_semantics=("parallel",)),
        name="slots",
    )(route, cnt, strict)
    pos = pos.reshape(-1)
    flat_off = offs[:, :, 0].reshape(-1)
    counts = cnt[:, :, 0].astype(jnp.int32)
    wts = jnp.stack([route[2], route[3]], axis=1)
    wge = w_expert_gate.reshape(N_EXPERTS, D, D_EXPERT)
    wue = w_expert_up.reshape(N_EXPERTS, D, D_EXPERT)
    wde = w_expert_down.reshape(N_EXPERTS, D_EXPERT, D)
    nct = tb // MOE_COMBINE_TILE
    nst = tb // MOE_SCATTER_TILE
    slots =(2 * tb + N_EXPERTS + MOE_CHUNK + 7) // 8 * 8
    hidx = lambda sb, j, off, cn: (sb * nst + jnp.minimum(j, nst - 1), 0)
    eidx = lambda sb, j, off, cn: (jnp.clip(j - nst, 0, N_EXPERTS - 1), 0, 0)
    tidx = lambda sb, j, off, cn: (sb * nct + jnp.clip(j - nst - N_EXPERTS, 0, nct - 1), 0)
    out = pl.pallas_call(
        functools.partial(_moe_kernel, tb),
        grid_spec=pltpu.PrefetchScalarGridSpec(
            num_scalar_prefetch=2,
            grid=(nsb, nst + N_EXPERTS + nct),
            in_specs=[pl.BlockSpec((MOE_SCATTER_TILE, D), hidx),
                      pl.BlockSpec((2 * tb,), lambda sb, j, off, cn: (sb,), memory_space=pltpu.SMEM),
                      pl.BlockSpec((1, D, D_EXPERT), eidx), pl.BlockSpec((1, D, D_EXPERT), eidx),
                      pl.BlockSpec((1, D_EXPERT, D), eidx),
                      pl.BlockSpec((MOE_COMBINE_TILE, D), tidx), pl.BlockSpec((MOE_COMBINE_TILE, 2), tidx)],
            out_specs=pl.BlockSpec((MOE_COMBINE_TILE, D), tidx),
            scratch_shapes=[pltpu.VMEM((slots * PACK_ROWS, LANES), U32),
                            pltpu.VMEM((2, max(MOE_SCATTER_TILE, MOE_COMBINE_TILE) * PACK_ROWS, LANES), U32)]),
        out_shape=jax.ShapeDtypeStruct((T, D), F32),
        compiler_params=pltpu.CompilerParams(dimension_semantics=("arbitrary", "arbitrary"),
                                             vmem_limit_bytes=VMEM_LIMIT),
        name="moe",
    )(flat_off, counts.reshape(-1), h2, pos, wge, wue, wde, x1, wts)
    return out


def kernel(x, positions, norm1_gain, w_in, gla_gk_fwd_w, gla_gk_fwd_b, gla_gk_bwd_w, gla_gk_bwd_b, gla_out_gain, mla_q_gain, mla_w_qb, mla_kv_gain, mla_w_kvb, q_norm_gain, k_norm_gain, w_out, norm2_gain, w_router_group, b_router_group, w_router_expert, b_router_expert, w_expert_gate, w_expert_up, w_expert_down):
    batch, seq, d = x.shape
    x2 = x.reshape(batch * seq, d)
    pos2 = positions.reshape(batch * seq, 1)
    for l in range(norm1_gain.shape[0]):
        x2 = _layer(x2, pos2, batch, seq, norm1_gain[l], w_in[l], gla_gk_fwd_w[l], gla_gk_fwd_b[l],
                    gla_gk_bwd_w[l], gla_gk_bwd_b[l], gla_out_gain[l], mla_q_gain[l], mla_w_qb[l],
                    mla_kv_gain[l], mla_w_kvb[l], q_norm_gain[l], k_norm_gain[l], w_out[l], norm2_gain[l],
                    w_router_group[l], b_router_group[l], w_router_expert[l], b_router_expert[l],
                    w_expert_gate[l], w_expert_up[l], w_expert_down[l])
    return x2.reshape(batch, seq, d)
```

```python
import functools

import numpy as np
import jax
import jax.numpy as jnp
from jax import lax
from jax.experimental import pallas as pl
from jax.experimental.pallas import tpu as pltpu

F32 = jnp.float32
BF16 = jnp.bfloat16

D_MODEL = 1024
GLA_HEADS = 4
GLA_DK = 64
GLA_DV = 128
GLA_GATE_RANK = 16
GLA_GATE_NORMALIZER = 16.0
GLA_CHUNK = 64
GLA_CHUNK_UNROLL = 4
GLA_FINISH_ROWS = 256
MLA_HEADS = 8
MLA_NOPE = 64
MLA_ROPE = 32
MLA_QK = MLA_NOPE + MLA_ROPE
MLA_V = 64
MLA_Q_RANK = 256
MLA_KV_RANK = 128
ROPE_BASE = 10000.0
GLA_QK_WIDTH = GLA_HEADS * GLA_DK
GLA_WIDTH = GLA_HEADS * GLA_DV
MLA_WIDTH = MLA_HEADS * MLA_V
N_GROUPS = 4
EXPERTS_PER_GROUP = 8
N_EXPERTS = N_GROUPS * EXPERTS_PER_GROUP
D_EXPERT = 256
EPS = 1e-6

LANES = 128
HEAD_TILE = LANES
ROPE_LO = MLA_NOPE
ROPE_HALF = MLA_ROPE // 2
LOG2_E = 1.4426950408889634
ATTN_SCORES_AHEAD = 3
ONES_ROWS = 16

COL_GQ = 0
COL_GK = COL_GQ + GLA_QK_WIDTH
COL_GV = COL_GK + GLA_QK_WIDTH
COL_GG = COL_GV + GLA_WIDTH
COL_MQ = COL_GG + GLA_WIDTH
COL_MKV = COL_MQ + MLA_Q_RANK
COL_MISC = COL_MKV + MLA_KV_RANK
PROJ_WIDTH = COL_MISC + LANES
MISC_LRB = GLA_GATE_RANK

ROUTER_ROWS = LANES
ROUTER_EXPERT_ROW = 8

U32 = jnp.uint32
HI_HALF_MASK = np.uint32(0xFFFF0000)
PACK_ROWS = D_MODEL // (2 * LANES)
MOE_SUPER_BLOCK = 8192
MOE_CHUNK = 576
MOE_SCATTER_TILE = 512
MOE_COMBINE_TILE = 256
OUTPROJ_TILE = 1024
MOE_ROW_UNROLL = 8

VMEM_LIMIT = 56 * 1024 * 1024


def _dot(a, b):
    return jnp.dot(a, b, preferred_element_type=F32)


def _dot_nt(a, b):
    return lax.dot_general(a, b, (((1,), (1,)), ((), ())), preferred_element_type=F32)


def _dot_tn(a, b):
    return lax.dot_general(a, b, (((0,), (0,)), ((), ())), preferred_element_type=F32)


def _split_bf16(x):
    hi = x.astype(BF16)
    lo = (x - hi.astype(F32)).astype(BF16)
    return hi, lo


def _proj_kernel(x_ref, pos_ref, n1_ref, win_ref, wg_ref, bg_ref, qgain_ref, wqb_ref, wqbr_ref, kvgain_ref,
                 wkvk_ref, wkvv_ref, qng_ref, qngr_ref, kng_ref, kngr_ref, freq_ref, place_ref, ones_ref,
                 gq_ref, gk_ref, gv_ref, gg_ref, lg_ref, mq_ref, mk_ref, mv_ref):
    x = x_ref[...]
    h = x * lax.rsqrt(jnp.mean(x * x, axis=-1, keepdims=True) + EPS) * n1_ref[...]
    proj = _dot(h.astype(BF16), win_ref[...])

    gq_ref[...] = (proj[:, COL_GQ:COL_GK] * GLA_DK ** -0.5).astype(BF16)
    gk_ref[...] = proj[:, COL_GK:COL_GV].astype(BF16)
    gv_ref[...] = proj[:, COL_GV:COL_GG].astype(BF16)
    gg_ref[...] = proj[:, COL_GG:COL_MQ].astype(BF16)

    misc = proj[:, COL_MISC:PROJ_WIDTH]
    z = _dot(misc.astype(BF16), wg_ref[...]) + bg_ref[...]
    log_sig = -(jnp.maximum(-z, 0.0) + jnp.log(1.0 + jnp.exp(-jnp.abs(z))))
    lg_ref[...] = log_sig / GLA_GATE_NORMALIZER

    lane = lax.broadcasted_iota(jnp.int32, (1, LANES), 1)
    ang_t = freq_ref[...] * pos_ref[0].astype(F32)
    place = place_ref[...]

    def to_rows(tab_t):
        hi, lo = _split_bf16(tab_t)
        return _dot_tn(hi, place) + _dot_tn(lo, place)

    in_rope = (lane >= ROPE_LO) & (lane < ROPE_LO + MLA_ROPE)
    c_tab = jnp.where(lane < ROPE_LO, 1.0, to_rows(jnp.cos(ang_t)))
    s_tab = to_rows(jnp.sin(ang_t))
    ones_bd = ones_ref[...]

    def norm_rope_pair(t2, r2, gc, gs):
        ss = _dot((t2 * t2).astype(BF16), ones_bd) * (1.0 / MLA_QK)
        rs = lax.rsqrt(ss + EPS)
        outs = []
        for i in range(2):
            sl = slice(i * LANES, (i + 1) * LANES)
            outs.append(rs[:, sl] * (t2[:, sl] * gc + r2[:, sl] * gs))
        return outs

    qa = proj[:, COL_MQ:COL_MKV]
    qn = qa * lax.rsqrt(jnp.mean(qa * qa, axis=-1, keepdims=True) + EPS) * qgain_ref[...]
    qn = qn.astype(BF16)
    mq = _dot(qn, wqb_ref[...])
    mq_rot = _dot(qn, wqbr_ref[...])
    kva = proj[:, COL_MKV:COL_MISC]
    kvn = (kva * lax.rsqrt(jnp.mean(kva * kva, axis=-1, keepdims=True) + EPS) * kvgain_ref[...]).astype(BF16)
    kn = _dot(kvn, wkvk_ref[...])
    mv_ref[0] = _dot_nt(wkvv_ref[...], kvn).astype(BF16)
    rope_tile = jnp.where(in_rope, misc, 0.0)
    up = (lane >= ROPE_LO + ROPE_HALF) & (lane < ROPE_LO + MLA_ROPE)
    dn = (lane >= ROPE_LO) & (lane < ROPE_LO + ROPE_HALF)
    rope_rot = (jnp.where(up, pltpu.roll(rope_tile, ROPE_HALF, 1), 0.0)
                - jnp.where(dn, pltpu.roll(rope_tile, LANES - ROPE_HALF, 1), 0.0))
    rope2 = jnp.concatenate([rope_tile, rope_tile], axis=1)
    rope_rot2 = jnp.concatenate([rope_rot, rope_rot], axis=1)
    scale = MLA_QK ** -0.5 * LOG2_E
    q_gc, q_gs = (qng_ref[...] * scale) * c_tab, (qngr_ref[...] * scale) * s_tab
    k_gc, k_gs = kng_ref[...] * c_tab, kngr_ref[...] * s_tab
    for hp in range(MLA_HEADS // 2):
        sl = slice(hp * 2 * HEAD_TILE, (hp + 1) * 2 * HEAD_TILE)
        q2 = norm_rope_pair(mq[:, sl], mq_rot[:, sl], q_gc, q_gs)
        k2 = norm_rope_pair(kn[:, sl] + rope2, rope_rot2, k_gc, k_gs)
        for i in range(2):
            mq_ref[0, 2 * hp + i] = q2[i].astype(BF16)
            mk_ref[0, 2 * hp + i] = k2[i].astype(BF16)


def _gla_kernel(q_ref, k_ref, v_ref, lg_ref, gate_ref, gain_ref, o_ref, accf_ref, accb_ref, stf_ref, stb_ref):
    seq = q_ref.shape[0]
    C = GLA_CHUNK
    n_chunks = seq // C
    HK = GLA_QK_WIDTH

    lane_head = lax.broadcasted_iota(jnp.int32, (1, HK), 1) // GLA_DK
    rowi = lax.broadcasted_iota(jnp.int32, (C, 1), 0)
    srow = lax.broadcasted_iota(jnp.int32, (GLA_HEADS * C, C), 0) % C
    scol = lax.broadcasted_iota(jnp.int32, (GLA_HEADS * C, C), 1)
    fwd_cfg = (True, C // 2 - 1, C - 1, scol <= srow, 0)
    bwd_cfg = (False, C // 2, 0, scol > srow, HK)

    def stack_heads(t):
        return jnp.concatenate([jnp.where(lane_head == hd, t, 0.0) for hd in range(GLA_HEADS)], axis=0)

    def chunk_step(c, st, cfg, acc_ref):
        prefix, ref_i, last_i, keep, lg_off = cfg
        rows = pl.ds(pl.multiple_of(c * C, C), C)
        b = lg_ref[rows, lg_off:lg_off + HK]
        shift = 1
        while shift < C:
            if prefix:
                b = b + jnp.where(rowi >= shift, pltpu.roll(b, shift, 0), 0.0)
            else:
                b = b + jnp.where(rowi < C - shift, pltpu.roll(b, C - shift, 0), 0.0)
            shift *= 2
        b_ref = b[ref_i:ref_i + 1, :]
        b_last = b[last_i:last_i + 1, :]
        q = q_ref[rows, :].astype(F32)
        k = k_ref[rows, :].astype(F32)
        v = v_ref[rows, :]
        q_rel = q * jnp.exp(b - b_ref)
        k_rel = (k * jnp.exp(b_ref - b)).astype(BF16)
        k_dec = (k * jnp.exp(b_last - b)).astype(BF16)
        q_dec = q * jnp.exp(b)
        sc = _dot_nt(stack_heads(q_rel).astype(BF16), k_rel)
        sc = jnp.where(keep, sc, 0.0).astype(BF16)
        o_inter = _dot_nt(stack_heads(q_dec).astype(BF16), st.astype(BF16))
        kv_t = _dot_tn(v, k_dec)
        new_st = jnp.exp(b_last) * st
        for hd in range(GLA_HEADS):
            vs = slice(hd * GLA_DV, (hd + 1) * GLA_DV)
            rs = slice(hd * C, (hd + 1) * C)
            acc_ref[rows, vs] = _dot(sc[rs, :], v[:, vs]) + o_inter[rs, :]
            new_st = new_st + jnp.where(lane_head == hd, kv_t[vs, :], 0.0)
        return new_st

    stf_ref[...] = jnp.zeros_like(stf_ref)
    stb_ref[...] = jnp.zeros_like(stb_ref)

    def body(i, carry):
        st_f = stf_ref[...]
        st_b = stb_ref[...]
        for u in range(GLA_CHUNK_UNROLL):
            cf = i * GLA_CHUNK_UNROLL + u
            st_f = chunk_step(cf, st_f, fwd_cfg, accf_ref)
            st_b = chunk_step(n_chunks - 1 - cf, st_b, bwd_cfg, accb_ref)
        stf_ref[...] = st_f
        stb_ref[...] = st_b
        return carry

    lax.fori_loop(0, n_chunks // GLA_CHUNK_UNROLL, body, 0)

    gain = gain_ref[...]

    def finish(r, carry):
        rows = pl.ds(pl.multiple_of(r * GLA_FINISH_ROWS, GLA_FINISH_ROWS), GLA_FINISH_ROWS)
        for hd in range(GLA_HEADS):
            vs = slice(hd * GLA_DV, (hd + 1) * GLA_DV)
            o = accf_ref[rows, vs] + accb_ref[rows, vs]
            on = o * lax.rsqrt(jnp.mean(o * o, axis=-1, keepdims=True) + EPS) * gain
            gt = gate_ref[rows, vs].astype(F32)
            o_ref[rows, vs] = (on * (gt * jax.nn.sigmoid(gt))).astype(BF16)
        return carry

    lax.fori_loop(0, seq // GLA_FINISH_ROWS, finish, 0)


def _attn_kernel(q_ref, k_ref, vt_ref, o_ref):
    seq = k_ref.shape[2]
    ones = jnp.ones((ONES_ROWS, seq), BF16)
    outs = []
    scores = [_dot_nt(q_ref[0, h], k_ref[0, h]) for h in range(ATTN_SCORES_AHEAD)]
    for hd in range(MLA_HEADS):
        if hd + ATTN_SCORES_AHEAD < MLA_HEADS:
            scores.append(_dot_nt(q_ref[0, hd + ATTN_SCORES_AHEAD], k_ref[0, hd + ATTN_SCORES_AHEAD]))
        s = scores[hd]
        p = jnp.exp2(s - jnp.max(s, axis=-1, keepdims=True)).astype(BF16)
        vt_aug = jnp.concatenate([vt_ref[0, hd * MLA_V:(hd + 1) * MLA_V, :], ones], axis=0)
        ot = _dot_nt(vt_aug, p)
        outs.append(ot[0:MLA_V, :] / ot[MLA_V:MLA_V + 1, :])
    o_ref[0] = jnp.concatenate(outs, axis=0).astype(BF16)


def _unpack_bf16_pair(w):
    lo = pltpu.unpack_elementwise(w, index=0, packed_dtype=BF16, unpacked_dtype=F32)
    hi = pltpu.unpack_elementwise(w, index=1, packed_dtype=BF16, unpacked_dtype=F32)
    return lo, hi


def _pack_bf16_pair(lo, hi):
    return pltpu.pack_elementwise([lo, hi], packed_dtype=BF16)


def _outproj_kernel(tiles_per_sb, x_ref, gla_ref, mla_ref, wo_ref, n2_ref, wr_ref, br_ref, tri_ref,
                    x1_ref, h2_ref, route_ref, cnt_ref, carry_ref):
    mix = _dot(gla_ref[...], wo_ref[0:GLA_WIDTH, :]) + _dot_tn(mla_ref[0], wo_ref[GLA_WIDTH:, :])
    x1 = x_ref[...] + mix
    x1_ref[...] = x1
    h2 = x1 * lax.rsqrt(jnp.mean(x1 * x1, axis=-1, keepdims=True) + EPS) * n2_ref[...]
    h2b = h2.astype(BF16)
    h2_ref[...] = h2b

    logits = _dot_nt(wr_ref[...], h2b) + br_ref[...]
    tm = logits.shape[1]
    gl = logits[0:N_GROUPS, :]
    ge = jnp.exp(gl - jnp.max(gl, axis=0, keepdims=True))
    pg = ge / jnp.sum(ge, axis=0, keepdims=True)
    p_top = jnp.max(pg, axis=0, keepdims=True)
    gi = lax.broadcasted_iota(jnp.int32, (N_GROUPS, tm), 0)
    g_idx = jnp.min(jnp.where(pg == p_top, gi, N_GROUPS), axis=0, keepdims=True)
    sel = jnp.zeros((EXPERTS_PER_GROUP, tm), F32)
    for g in range(N_GROUPS):
        r0 = ROUTER_EXPERT_ROW + g * EXPERTS_PER_GROUP
        sel = sel + jnp.where(g_idx == g, logits[r0:r0 + EXPERTS_PER_GROUP, :], 0.0)
    se = jnp.exp(sel - jnp.max(sel, axis=0, keepdims=True))
    pe = se / jnp.sum(se, axis=0, keepdims=True)
    ei = lax.broadcasted_iota(jnp.int32, (EXPERTS_PER_GROUP, tm), 0)
    m1 = jnp.max(pe, axis=0, keepdims=True)
    i1 = jnp.min(jnp.where(pe == m1, ei, EXPERTS_PER_GROUP), axis=0, keepdims=True)
    pe2 = jnp.where(ei == i1, -1.0, pe)
    m2 = jnp.max(pe2, axis=0, keepdims=True)
    i2 = jnp.min(jnp.where(pe2 == m2, ei, EXPERTS_PER_GROUP), axis=0, keepdims=True)
    den = m1 + m2
    e1 = g_idx * EXPERTS_PER_GROUP + i1
    e2 = g_idx * EXPERTS_PER_GROUP + i2

    @pl.when(pl.program_id(0) % tiles_per_sb == 0)
    def _():
        carry_ref[...] = jnp.zeros_like(carry_ref)

    eall = lax.broadcasted_iota(jnp.int32, (N_EXPERTS, tm), 0)
    tri = tri_ref[...]
    carry = carry_ref[...]
    ranks = []
    for eid in (e1, e2):
        hot = eall == eid
        prefix = _dot(hot.astype(BF16), tri)
        ranks.append(jnp.sum(jnp.where(hot, carry + prefix - 1.0, 0.0), axis=0, keepdims=True))
        carry = carry + prefix[:, tm - 1:tm]
    carry_ref[...] = carry
    cnt_ref[0] = jnp.broadcast_to(carry, (N_EXPERTS, LANES))
    zeros = jnp.zeros((2, tm), F32)
    route_ref[...] = jnp.concatenate([e1.astype(F32), e2.astype(F32), p_top * (m1 / den), p_top * (m2 / den),
                                      ranks[0], ranks[1], zeros], axis=0)


def _slots_kernel(route_ref, cnt_ref, strict_ref, pos_ref, off_ref):
    counts = cnt_ref[0]
    padded = 2.0 * jnp.floor(counts * 0.5 + 0.5)
    hi = jnp.floor(padded * (1.0 / 256.0))
    lo = padded - 256.0 * hi
    strict = strict_ref[...]
    off = 256.0 * _dot(strict, hi.astype(BF16)) + _dot(strict, lo.astype(BF16))
    off_ref[0] = off.astype(jnp.int32)
    tb = route_ref.shape[1]
    eall = lax.broadcasted_iota(jnp.int32, (N_EXPERTS, tb), 0)
    off_col = off[:, 0:1]
    for s in range(2):
        hot = eall == route_ref[s:s + 1, :].astype(jnp.int32)
        base = jnp.sum(jnp.where(hot, off_col, 0.0), axis=0, keepdims=True)
        pos_ref[0, s:s + 1, :] = ((base + route_ref[4 + s:5 + s, :]) * PACK_ROWS).astype(jnp.int32)


def _moe_kernel(tb, off_ref, cnt_ref, h2_ref, pos_ref, wg_ref, wu_ref, wd_ref, x1_ref, wt_ref, o_ref,
                xy_ref, cb_ref):
    sb = pl.program_id(0)
    j = pl.program_id(1)
    n_scatter = tb // MOE_SCATTER_TILE
    first_expert_step = n_scatter
    first_combine_step = n_scatter + N_EXPERTS

    @pl.when(j == 0)
    def _clear():
        xy_ref[...] = jnp.zeros_like(xy_ref)

    @pl.when(j < n_scatter)
    def _scatter():
        t0 = j * MOE_SCATTER_TILE
        h = h2_ref[...].astype(F32)
        for r in range(PACK_ROWS):
            c0 = r * 2 * LANES
            cb_ref[0, pl.ds(r, MOE_SCATTER_TILE, stride=PACK_ROWS), :] = _pack_bf16_pair(
                h[:, c0:c0 + LANES], h[:, c0 + LANES:c0 + 2 * LANES])

        def body(g, c):
            for i in range(MOE_ROW_UNROLL // 2):
                pair = g * (MOE_ROW_UNROLL // 2) + i
                v = cb_ref[0, pl.ds(pl.multiple_of(pair * 2 * PACK_ROWS, 2 * PACK_ROWS), 2 * PACK_ROWS), :]
                for half in range(2):
                    t = t0 + pair * 2 + half
                    piece = v[half * PACK_ROWS:(half + 1) * PACK_ROWS, :]
                    for slot in range(2):
                        p = pos_ref[slot * tb + t]
                        xy_ref[pl.ds(pl.multiple_of(p, PACK_ROWS), PACK_ROWS), :] = piece
            return c

        lax.fori_loop(0, MOE_SCATTER_TILE // MOE_ROW_UNROLL, body, 0)

    @pl.when((j >= first_expert_step) & (j < first_combine_step))
    def _experts():
        e = sb * N_EXPERTS + j - first_expert_step
        n = cnt_ref[e]
        base = off_ref[e]

        def body(i, c):
            r0 = base + i * MOE_CHUNK
            row0 = pl.multiple_of(r0 * PACK_ROWS, 2 * PACK_ROWS)
            halves = [_unpack_bf16_pair(xy_ref[pl.ds(row0 + r, MOE_CHUNK, stride=PACK_ROWS), :])
                      for r in range(PACK_ROWS)]
            a = jnp.zeros((MOE_CHUNK, D_EXPERT), F32)
            u = jnp.zeros((MOE_CHUNK, D_EXPERT), F32)
            for r in range(PACK_ROWS):
                lo, hi = halves[r]
                xb = jnp.concatenate([lo, hi], axis=1).astype(BF16)
                a = a + _dot(xb, wg_ref[0, r * 2 * LANES:(r + 1) * 2 * LANES, :].astype(BF16))
                u = u + _dot(xb, wu_ref[0, r * 2 * LANES:(r + 1) * 2 * LANES, :].astype(BF16))
            hid = ((a * jax.nn.sigmoid(a)) * u).astype(BF16)
            y = _dot(hid, wd_ref[0].astype(BF16))
            valid = (r0 + lax.broadcasted_iota(jnp.int32, (MOE_CHUNK, 1), 0)) < (base + n)
            for r in range(PACK_ROWS):
                c0 = r * 2 * LANES
                lo, hi = halves[r]
                xy_ref[pl.ds(row0 + r, MOE_CHUNK, stride=PACK_ROWS), :] = _pack_bf16_pair(
                    jnp.where(valid, y[:, c0:c0 + LANES], lo), jnp.where(valid, y[:, c0 + LANES:c0 + 2 * LANES], hi))
            return c

        lax.fori_loop(0, (n + MOE_CHUNK - 1) // MOE_CHUNK, body, 0)

    @pl.when(j >= first_combine_step)
    def _combine():
        t0 = (j - first_combine_step) * MOE_COMBINE_TILE

        def body(g, c):
            for i in range(MOE_ROW_UNROLL):
                t = g * MOE_ROW_UNROLL + i
                for slot in range(2):
                    p = pos_ref[slot * tb + t0 + t]
                    cb_ref[slot, pl.ds(pl.multiple_of(t * PACK_ROWS, PACK_ROWS), PACK_ROWS), :] = (
                        xy_ref[pl.ds(pl.multiple_of(p, PACK_ROWS), PACK_ROWS), :])
            return c

        lax.fori_loop(0, MOE_COMBINE_TILE // MOE_ROW_UNROLL, body, 0)
        w1 = wt_ref[:, 0:1]
        w2 = wt_ref[:, 1:2]
        for r in range(PACK_ROWS):
            lo1, hi1 = _unpack_bf16_pair(cb_ref[0, pl.ds(r, MOE_COMBINE_TILE, stride=PACK_ROWS), :])
            lo2, hi2 = _unpack_bf16_pair(cb_ref[1, pl.ds(r, MOE_COMBINE_TILE, stride=PACK_ROWS), :])
            c0 = r * 2 * LANES
            o_ref[:, c0:c0 + LANES] = x1_ref[:, c0:c0 + LANES] + w1 * lo1 + w2 * lo2
            o_ref[:, c0 + LANES:c0 + 2 * LANES] = x1_ref[:, c0 + LANES:c0 + 2 * LANES] + w1 * hi1 + w2 * hi2


def _full(shape):
    return pl.BlockSpec(shape, lambda *_: (0,) * len(shape))


def _prep_weights(w_in, gk_fwd_w, gk_fwd_b, gk_bwd_w, gk_bwd_b, mla_w_qb, mla_w_kvb, q_norm_gain, k_norm_gain):
    splits = np.cumsum([0, GLA_QK_WIDTH, GLA_QK_WIDTH, GLA_WIDTH, GLA_WIDTH, GLA_GATE_RANK, GLA_GATE_RANK,
                        MLA_Q_RANK, MLA_KV_RANK, MLA_ROPE])
    seg = [w_in[:, splits[i]:splits[i + 1]] for i in range(9)]
    d = w_in.shape[0]
    misc = jnp.zeros((d, LANES), F32)
    misc = misc.at[:, 0:GLA_GATE_RANK].set(seg[4])
    misc = misc.at[:, MISC_LRB:MISC_LRB + GLA_GATE_RANK].set(seg[5])
    misc = misc.at[:, ROPE_LO:ROPE_LO + MLA_ROPE].set(seg[8])
    win = jnp.concatenate([seg[0], seg[1], seg[2], seg[3], seg[6], seg[7], misc], axis=1).astype(BF16)

    wg = jnp.zeros((LANES, 2 * GLA_QK_WIDTH), F32)
    wg = wg.at[0:GLA_GATE_RANK, 0:GLA_QK_WIDTH].set(gk_fwd_w)
    wg = wg.at[MISC_LRB:MISC_LRB + GLA_GATE_RANK, GLA_QK_WIDTH:].set(gk_bwd_w)
    bg = jnp.concatenate([gk_fwd_b, gk_bwd_b])[None, :]

    pad = HEAD_TILE - MLA_QK
    wqb = jnp.pad(mla_w_qb.reshape(MLA_Q_RANK, MLA_HEADS, MLA_QK), ((0, 0), (0, 0), (0, pad)))
    wqb = wqb.reshape(MLA_Q_RANK, MLA_HEADS * HEAD_TILE)
    wkv = mla_w_kvb.reshape(MLA_KV_RANK, MLA_HEADS, MLA_NOPE + MLA_V)
    wkvk = jnp.pad(wkv[:, :, :MLA_NOPE], ((0, 0), (0, 0), (0, HEAD_TILE - MLA_NOPE)))
    wkvk = wkvk.reshape(MLA_KV_RANK, MLA_HEADS * HEAD_TILE).astype(BF16)
    wkvv = wkv[:, :, MLA_NOPE:].reshape(MLA_KV_RANK, MLA_WIDTH).T.astype(BF16)
    qng = jnp.pad(q_norm_gain, (0, pad))[None, :]
    kng = jnp.pad(k_norm_gain, (0, pad))[None, :]
    def partner(w, sign):
        first, second = w[..., ROPE_LO:ROPE_LO + ROPE_HALF], w[..., ROPE_LO + ROPE_HALF:ROPE_LO + MLA_ROPE]
        return jnp.concatenate([jnp.zeros_like(w[..., :ROPE_LO]), sign * second, first,
                                jnp.zeros_like(w[..., ROPE_LO + MLA_ROPE:])], axis=-1)

    wqb_rot = partner(wqb.reshape(MLA_Q_RANK, MLA_HEADS, HEAD_TILE), -1.0).reshape(MLA_Q_RANK, -1).astype(BF16)
    qng_rot = partner(qng, 1.0)
    kng_rot = partner(kng, 1.0)
    return win, wg.astype(BF16), bg, wqb.astype(BF16), wqb_rot, wkvk, wkvv, qng, qng_rot, kng, kng_rot


def _rope_consts():
    inv = ROPE_BASE ** (-np.arange(0, MLA_ROPE, 2, dtype=np.float32) / MLA_ROPE)
    place = np.zeros((ROPE_HALF, LANES), np.float32)
    place[np.arange(ROPE_HALF), ROPE_LO + np.arange(ROPE_HALF)] = 1.0
    place[np.arange(ROPE_HALF), ROPE_LO + ROPE_HALF + np.arange(ROPE_HALF)] = 1.0
    ones_bd = np.kron(np.eye(2, dtype=np.float32), np.ones((LANES, LANES), np.float32))
    return jnp.asarray(inv[:, None]), jnp.asarray(place, BF16), jnp.asarray(ones_bd, BF16)


def _layer(x2, pos2, batch, seq, norm1_gain, w_in, gk_fwd_w, gk_fwd_b, gk_bwd_w, gk_bwd_b, gla_out_gain,
           mla_q_gain, mla_w_qb, mla_kv_gain, mla_w_kvb, q_norm_gain, k_norm_gain, w_out, norm2_gain,
           w_router_group, b_router_group, w_router_expert, b_router_expert,
           w_expert_gate, w_expert_up, w_expert_down):
    T, D = x2.shape
    win, wg, bg, wqb, wqb_rot, wkvk, wkvv, qng, qng_rot, kng, kng_rot = _prep_weights(
        w_in, gk_fwd_w, gk_fwd_b, gk_bwd_w, gk_bwd_b, mla_w_qb, mla_w_kvb, q_norm_gain, k_norm_gain)

    TM = 512
    spb = seq // TM
    tok = lambda w: pl.BlockSpec((TM, w), lambda i: (i, 0))
    head_spec = pl.BlockSpec((1, MLA_HEADS, TM, HEAD_TILE), lambda i: (i // spb, 0, i % spb, 0))
    gq, gk, gv, gg, lg, mq, mk, mv = pl.pallas_call(
        _proj_kernel,
        grid=(T // TM,),
        in_specs=[tok(D), pl.BlockSpec((1, 1, TM), lambda i: (i, 0, 0)), _full((1, D)), _full((D, PROJ_WIDTH)),
                  _full((LANES, 2 * GLA_QK_WIDTH)), _full((1, 2 * GLA_QK_WIDTH)), _full((1, MLA_Q_RANK)),
                  _full((MLA_Q_RANK, MLA_HEADS * HEAD_TILE)), _full((MLA_Q_RANK, MLA_HEADS * HEAD_TILE)),
                  _full((1, MLA_KV_RANK)), _full((MLA_KV_RANK, MLA_HEADS * HEAD_TILE)),
                  _full((MLA_WIDTH, MLA_KV_RANK)), _full((1, HEAD_TILE)), _full((1, HEAD_TILE)),
                  _full((1, HEAD_TILE)), _full((1, HEAD_TILE)),
                  _full((ROPE_HALF, 1)), _full((ROPE_HALF, LANES)), _full((2 * LANES, 2 * LANES))],
        out_specs=[tok(GLA_QK_WIDTH), tok(GLA_QK_WIDTH), tok(GLA_WIDTH), tok(GLA_WIDTH), tok(2 * GLA_QK_WIDTH),
                   head_spec, head_spec, pl.BlockSpec((1, MLA_WIDTH, TM), lambda i: (i // spb, 0, i % spb))],
        out_shape=[jax.ShapeDtypeStruct((T, GLA_QK_WIDTH), BF16), jax.ShapeDtypeStruct((T, GLA_QK_WIDTH), BF16),
                   jax.ShapeDtypeStruct((T, GLA_WIDTH), BF16), jax.ShapeDtypeStruct((T, GLA_WIDTH), BF16),
                   jax.ShapeDtypeStruct((T, 2 * GLA_QK_WIDTH), F32),
                   jax.ShapeDtypeStruct((batch, MLA_HEADS, seq, HEAD_TILE), BF16),
                   jax.ShapeDtypeStruct((batch, MLA_HEADS, seq, HEAD_TILE), BF16),
                   jax.ShapeDtypeStruct((batch, MLA_WIDTH, seq), BF16)],
        compiler_params=pltpu.CompilerParams(dimension_semantics=("parallel",), vmem_limit_bytes=VMEM_LIMIT),
        name="proj",
    )(x2, pos2.reshape(T // TM, 1, TM), norm1_gain[None, :], win, wg, bg, mla_q_gain[None, :], wqb, wqb_rot,
      mla_kv_gain[None, :], wkvk, wkvv, qng, qng_rot, kng, kng_rot, *_rope_consts())

    seqspec = lambda w: pl.BlockSpec((seq, w), lambda b: (b, 0))
    gla_out = pl.pallas_call(
        _gla_kernel,
        grid=(batch,),
        in_specs=[seqspec(GLA_QK_WIDTH), seqspec(GLA_QK_WIDTH), seqspec(GLA_WIDTH), seqspec(2 * GLA_QK_WIDTH),
                  seqspec(GLA_WIDTH), _full((1, GLA_DV))],
        out_specs=seqspec(GLA_WIDTH),
        out_shape=jax.ShapeDtypeStruct((T, GLA_WIDTH), BF16),
        scratch_shapes=[pltpu.VMEM((seq, GLA_WIDTH), F32), pltpu.VMEM((seq, GLA_WIDTH), F32),
                        pltpu.VMEM((GLA_DV, GLA_QK_WIDTH), F32), pltpu.VMEM((GLA_DV, GLA_QK_WIDTH), F32)],
        compiler_params=pltpu.CompilerParams(dimension_semantics=("parallel",), vmem_limit_bytes=VMEM_LIMIT),
        name="gla",
    )(gq, gk, gv, lg, gg, gla_out_gain[None, :])

    TQ = 256
    nq = seq // TQ
    mla_out = pl.pallas_call(
        _attn_kernel,
        grid=(batch, nq),
        in_specs=[pl.BlockSpec((1, MLA_HEADS, TQ, HEAD_TILE), lambda b, i: (b, 0, i, 0)),
                  pl.BlockSpec((1, MLA_HEADS, seq, HEAD_TILE), lambda b, i: (b, 0, 0, 0)),
                  pl.BlockSpec((1, MLA_WIDTH, seq), lambda b, i: (b, 0, 0))],
        out_specs=pl.BlockSpec((1, MLA_WIDTH, TQ), lambda b, i: (b, 0, i)),
        out_shape=jax.ShapeDtypeStruct((batch, MLA_WIDTH, seq), BF16),
        compiler_params=pltpu.CompilerParams(dimension_semantics=("parallel", "parallel"),
                                             vmem_limit_bytes=VMEM_LIMIT),
        name="attn",
    )(mq, mk, mv)

    wr = jnp.zeros((ROUTER_ROWS, D), F32)
    wr = wr.at[0:N_GROUPS].set(w_router_group.T)
    wr = wr.at[ROUTER_EXPERT_ROW:ROUTER_EXPERT_ROW + N_EXPERTS].set(w_router_expert.T)
    br = jnp.zeros((ROUTER_ROWS, 1), F32)
    br = br.at[0:N_GROUPS, 0].set(b_router_group)
    br = br.at[ROUTER_EXPERT_ROW:ROUTER_EXPERT_ROW + N_EXPERTS, 0].set(b_router_expert)
    tb = min(MOE_SUPER_BLOCK, T)
    nsb = T // tb
    TO = min(OUTPROJ_TILE, seq)
    tiles_per_sb = tb // TO
    tok = lambda w: pl.BlockSpec((TO, w), lambda i: (i, 0))
    tri = jnp.asarray(np.triu(np.ones((TO, TO), np.float32)), BF16)
    x1, h2, route, cnt = pl.pallas_call(
        functools.partial(_outproj_kernel, tiles_per_sb),
        grid=(T // TO,),
        in_specs=[tok(D), tok(GLA_WIDTH),
                  pl.BlockSpec((1, MLA_WIDTH, TO), lambda i: (i // (seq // TO), 0, i % (seq // TO))),
                  _full((D, D)), _full((1, D)),
                  _full((ROUTER_ROWS, D)), _full((ROUTER_ROWS, 1)), _full((TO, TO))],
        out_specs=[tok(D), tok(D), pl.BlockSpec((8, TO), lambda i: (0, i)),
                   pl.BlockSpec((1, N_EXPERTS, LANES), lambda i: (i // tiles_per_sb, 0, 0))],
        out_shape=[jax.ShapeDtypeStruct((T, D), F32), jax.ShapeDtypeStruct((T, D), BF16),
                   jax.ShapeDtypeStruct((8, T), F32), jax.ShapeDtypeStruct((nsb, N_EXPERTS, LANES), F32)],
        scratch_shapes=[pltpu.VMEM((N_EXPERTS, 1), F32)],
        compiler_params=pltpu.CompilerParams(dimension_semantics=("arbitrary",), vmem_limit_bytes=VMEM_LIMIT),
        name="outproj",
    )(x2, gla_out, mla_out, w_out.astype(BF16), norm2_gain[None, :], wr.astype(BF16), br, tri)

    strict = jnp.asarray(np.tril(np.ones((N_EXPERTS, N_EXPERTS), np.float32), -1), BF16)
    pos, offs = pl.pallas_call(
        _slots_kernel,
        grid=(nsb,),
        in_specs=[pl.BlockSpec((8, tb), lambda s: (0, s)), pl.BlockSpec((1, N_EXPERTS, LANES), lambda s: (s, 0, 0)),
                  _full((N_EXPERTS, N_EXPERTS))],
        out_specs=[pl.BlockSpec((1, 2, tb), lambda s: (s, 0, 0)),
                   pl.BlockSpec((1, N_EXPERTS, LANES), lambda s: (s, 0, 0))],
        out_shape=[jax.ShapeDtypeStruct((nsb, 2, tb), jnp.int32),
                   jax.ShapeDtypeStruct((nsb, N_EXPERTS, LANES), jnp.int32)],
        compiler_params=pltpu.CompilerParams(dimension_semantics=("parallel",)),
        name="slots",
    )(route, cnt, strict)
    pos = pos.reshape(-1)
    flat_off = offs[:, :, 0].reshape(-1)
    counts = cnt[:, :, 0].astype(jnp.int32)
    wts = jnp.stack([route[2], route[3]], axis=1)
    wge = w_expert_gate.reshape(N_EXPERTS, D, D_EXPERT)
    wue = w_expert_up.reshape(N_EXPERTS, D, D_EXPERT)
    wde = w_expert_down.reshape(N_EXPERTS, D_EXPERT, D)
    nct = tb // MOE_COMBINE_TILE
    nst = tb // MOE_SCATTER_TILE
    slots =(2 * tb + N_EXPERTS + MOE_CHUNK + 7) // 8 * 8
    hidx = lambda sb, j, off, cn: (sb * nst + jnp.minimum(j, nst - 1), 0)
    eidx = lambda sb, j, off, cn: (jnp.clip(j - nst, 0, N_EXPERTS - 1), 0, 0)
    tidx = lambda sb, j, off, cn: (sb * nct + jnp.clip(j - nst - N_EXPERTS, 0, nct - 1), 0)
    out = pl.pallas_call(
        functools.partial(_moe_kernel, tb),
        grid_spec=pltpu.PrefetchScalarGridSpec(
            num_scalar_prefetch=2,
            grid=(nsb, nst + N_EXPERTS + nct),
            in_specs=[pl.BlockSpec((MOE_SCATTER_TILE, D), hidx),
                      pl.BlockSpec((2 * tb,), lambda sb, j, off, cn: (sb,), memory_space=pltpu.SMEM),
                      pl.BlockSpec((1, D, D_EXPERT), eidx), pl.BlockSpec((1, D, D_EXPERT), eidx),
                      pl.BlockSpec((1, D_EXPERT, D), eidx),
                      pl.BlockSpec((MOE_COMBINE_TILE, D), tidx), pl.BlockSpec((MOE_COMBINE_TILE, 2), tidx)],
            out_specs=pl.BlockSpec((MOE_COMBINE_TILE, D), tidx),
            scratch_shapes=[pltpu.VMEM((slots * PACK_ROWS, LANES), U32),
                            pltpu.VMEM((2, max(MOE_SCATTER_TILE, MOE_COMBINE_TILE) * PACK_ROWS, LANES), U32)]),
        out_shape=jax.ShapeDtypeStruct((T, D), F32),
        compiler_params=pltpu.CompilerParams(dimension_semantics=("arbitrary", "arbitrary"),
                                             vmem_limit_bytes=VMEM_LIMIT),
        name="moe",
    )(flat_off, counts.reshape(-1), h2, pos, wge, wue, wde, x1, wts)
    return out


def kernel(x, positions, norm1_gain, w_in, gla_gk_fwd_w, gla_gk_fwd_b, gla_gk_bwd_w, gla_gk_bwd_b, gla_out_gain, mla_q_gain, mla_w_qb, mla_kv_gain, mla_w_kvb, q_norm_gain, k_norm_gain, w_out, norm2_gain, w_router_group, b_router_group, w_router_expert, b_router_expert, w_expert_gate, w_expert_up, w_expert_down):
    batch, seq, d = x.shape
    x2 = x.reshape(batch * seq, d)
    pos2 = positions.reshape(batch * seq, 1)
    for l in range(norm1_gain.shape[0]):
        x2 = _layer(x2, pos2, batch, seq, norm1_gain[l], w_in[l], gla_gk_fwd_w[l], gla_gk_fwd_b[l],
                    gla_gk_bwd_w[l], gla_gk_bwd_b[l], gla_out_gain[l], mla_q_gain[l], mla_w_qb[l],
                    mla_kv_gain[l], mla_w_kvb[l], q_norm_gain[l], k_norm_gain[l], w_out[l], norm2_gain[l],
                    w_router_group[l], b_router_group[l], w_router_expert[l], b_router_expert[l],
                    w_expert_gate[l], w_expert_up[l], w_expert_down[l])
    return x2.reshape(batch, seq, d)
```

```python
import functools

import numpy as np
import jax
import jax.numpy as jnp
from jax import lax
from jax.experimental import pallas as pl
from jax.experimental.pallas import tpu as pltpu

F32 = jnp.float32
BF16 = jnp.bfloat16

D_MODEL = 1024
GLA_HEADS = 4
GLA_DK = 64
GLA_DV = 128
GLA_GATE_RANK = 16
GLA_GATE_NORMALIZER = 16.0
GLA_CHUNK = 64
GLA_CHUNK_UNROLL = 4
GLA_FINISH_ROWS = 256
MLA_HEADS = 8
MLA_NOPE = 64
MLA_ROPE = 32
MLA_QK = MLA_NOPE + MLA_ROPE
MLA_V = 64
MLA_Q_RANK = 256
MLA_KV_RANK = 128
ROPE_BASE = 10000.0
GLA_QK_WIDTH = GLA_HEADS * GLA_DK
GLA_WIDTH = GLA_HEADS * GLA_DV
MLA_WIDTH = MLA_HEADS * MLA_V
N_GROUPS = 4
EXPERTS_PER_GROUP = 8
N_EXPERTS = N_GROUPS * EXPERTS_PER_GROUP
D_EXPERT = 256
EPS = 1e-6

LANES = 128
HEAD_TILE = LANES
ROPE_LO = MLA_NOPE
ROPE_HALF = MLA_ROPE // 2
LOG2_E = 1.4426950408889634
ATTN_SCORES_AHEAD = 3
ONES_ROWS = 16

COL_GQ = 0
COL_GK = COL_GQ + GLA_QK_WIDTH
COL_GV = COL_GK + GLA_QK_WIDTH
COL_GG = COL_GV + GLA_WIDTH
COL_MQ = COL_GG + GLA_WIDTH
COL_MKV = COL_MQ + MLA_Q_RANK
COL_MISC = COL_MKV + MLA_KV_RANK
PROJ_WIDTH = COL_MISC + LANES
MISC_LRB = GLA_GATE_RANK

ROUTER_ROWS = LANES
ROUTER_EXPERT_ROW = 8

U32 = jnp.uint32
HI_HALF_MASK = np.uint32(0xFFFF0000)
PACK_ROWS = D_MODEL // (2 * LANES)
MOE_SUPER_BLOCK = 8192
MOE_CHUNK = 576
MOE_SCATTER_TILE = 512
MOE_COMBINE_TILE = 256
OUTPROJ_TILE = 1024
assert 2 * MOE_COMBINE_TILE == MOE_SCATTER_TILE

VMEM_LIMIT = 56 * 1024 * 1024


def _dot(a, b):
    return jnp.dot(a, b, preferred_element_type=F32)


def _dot_nt(a, b):
    return lax.dot_general(a, b, (((1,), (1,)), ((), ())), preferred_element_type=F32)


def _dot_tn(a, b):
    return lax.dot_general(a, b, (((0,), (0,)), ((), ())), preferred_element_type=F32)


def _split_bf16(x):
    hi = x.astype(BF16)
    lo = (x - hi.astype(F32)).astype(BF16)
    return hi, lo


def _proj_kernel(x_ref, pos_ref, n1_ref, win_ref, wg_ref, bg_ref, qgain_ref, wqb_ref, wqbr_ref, kvgain_ref,
                 wkvk_ref, wkvv_ref, qng_ref, qngr_ref, kng_ref, kngr_ref, freq_ref, place_ref, ones_ref,
                 gq_ref, gk_ref, gv_ref, gg_ref, lg_ref, mq_ref, mk_ref, mv_ref):
    x = x_ref[...]
    h = x * lax.rsqrt(jnp.mean(x * x, axis=-1, keepdims=True) + EPS) * n1_ref[...]
    proj = _dot(h.astype(BF16), win_ref[...])

    gq_ref[...] = (proj[:, COL_GQ:COL_GK] * GLA_DK ** -0.5).astype(BF16)
    gk_ref[...] = proj[:, COL_GK:COL_GV].astype(BF16)
    gv_ref[...] = proj[:, COL_GV:COL_GG].astype(BF16)
    gg_ref[...] = proj[:, COL_GG:COL_MQ].astype(BF16)

    misc = proj[:, COL_MISC:PROJ_WIDTH]
    z = _dot(misc.astype(BF16), wg_ref[...]) + bg_ref[...]
    log_sig = -(jnp.maximum(-z, 0.0) + jnp.log(1.0 + jnp.exp(-jnp.abs(z))))
    lg_ref[...] = log_sig / GLA_GATE_NORMALIZER

    lane = lax.broadcasted_iota(jnp.int32, (1, LANES), 1)
    ang_t = freq_ref[...] * pos_ref[0].astype(F32)
    place = place_ref[...]

    def to_rows(tab_t):
        hi, lo = _split_bf16(tab_t)
        return _dot_tn(hi, place) + _dot_tn(lo, place)

    in_rope = (lane >= ROPE_LO) & (lane < ROPE_LO + MLA_ROPE)
    c_tab = jnp.where(lane < ROPE_LO, 1.0, to_rows(jnp.cos(ang_t)))
    s_tab = to_rows(jnp.sin(ang_t))
    ones_bd = ones_ref[...]

    def norm_rope_pair(t2, r2, gc, gs):
        ss = _dot((t2 * t2).astype(BF16), ones_bd) * (1.0 / MLA_QK)
        rs = lax.rsqrt(ss + EPS)
        outs = []
        for i in range(2):
            sl = slice(i * LANES, (i + 1) * LANES)
            outs.append(rs[:, sl] * (t2[:, sl] * gc + r2[:, sl] * gs))
        return outs

    qa = proj[:, COL_MQ:COL_MKV]
    qn = qa * lax.rsqrt(jnp.mean(qa * qa, axis=-1, keepdims=True) + EPS) * qgain_ref[...]
    qn = qn.astype(BF16)
    mq = _dot(qn, wqb_ref[...])
    mq_rot = _dot(qn, wqbr_ref[...])
    kva = proj[:, COL_MKV:COL_MISC]
    kvn = (kva * lax.rsqrt(jnp.mean(kva * kva, axis=-1, keepdims=True) + EPS) * kvgain_ref[...]).astype(BF16)
    kn = _dot(kvn, wkvk_ref[...])
    mv_ref[0] = _dot_nt(wkvv_ref[...], kvn).astype(BF16)
    rope_tile = jnp.where(in_rope, misc, 0.0)
    up = (lane >= ROPE_LO + ROPE_HALF) & (lane < ROPE_LO + MLA_ROPE)
    dn = (lane >= ROPE_LO) & (lane < ROPE_LO + ROPE_HALF)
    rope_rot = (jnp.where(up, pltpu.roll(rope_tile, ROPE_HALF, 1), 0.0)
                - jnp.where(dn, pltpu.roll(rope_tile, LANES - ROPE_HALF, 1), 0.0))
    rope2 = jnp.concatenate([rope_tile, rope_tile], axis=1)
    rope_rot2 = jnp.concatenate([rope_rot, rope_rot], axis=1)
    scale = MLA_QK ** -0.5 * LOG2_E
    q_gc, q_gs = (qng_ref[...] * scale) * c_tab, (qngr_ref[...] * scale) * s_tab
    k_gc, k_gs = kng_ref[...] * c_tab, kngr_ref[...] * s_tab
    for hp in range(MLA_HEADS // 2):
        sl = slice(hp * 2 * HEAD_TILE, (hp + 1) * 2 * HEAD_TILE)
        q2 = norm_rope_pair(mq[:, sl], mq_rot[:, sl], q_gc, q_gs)
        k2 = norm_rope_pair(kn[:, sl] + rope2, rope_rot2, k_gc, k_gs)
        for i in range(2):
            mq_ref[0, 2 * hp + i] = q2[i].astype(BF16)
            mk_ref[0, 2 * hp + i] = k2[i].astype(BF16)


def _gla_kernel(q_ref, k_ref, v_ref, lg_ref, gate_ref, gain_ref, o_ref, accf_ref, accb_ref, stf_ref, stb_ref):
    seq = q_ref.shape[0]
    C = GLA_CHUNK
    n_chunks = seq // C
    HK = GLA_QK_WIDTH

    lane_head = lax.broadcasted_iota(jnp.int32, (1, HK), 1) // GLA_DK
    rowi = lax.broadcasted_iota(jnp.int32, (C, 1), 0)
    srow = lax.broadcasted_iota(jnp.int32, (GLA_HEADS * C, C), 0) % C
    scol = lax.broadcasted_iota(jnp.int32, (GLA_HEADS * C, C), 1)
    fwd_cfg = (True, C // 2 - 1, C - 1, scol <= srow, 0)
    bwd_cfg = (False, C // 2, 0, scol > srow, HK)

    def stack_heads(t):
        return jnp.concatenate([jnp.where(lane_head == hd, t, 0.0) for hd in range(GLA_HEADS)], axis=0)

    def chunk_step(c, st, cfg, acc_ref):
        prefix, ref_i, last_i, keep, lg_off = cfg
        rows = pl.ds(pl.multiple_of(c * C, C), C)
        b = lg_ref[rows, lg_off:lg_off + HK]
        shift = 1
        while shift < C:
            if prefix:
                b = b + jnp.where(rowi >= shift, pltpu.roll(b, shift, 0), 0.0)
            else:
                b = b + jnp.where(rowi < C - shift, pltpu.roll(b, C - shift, 0), 0.0)
            shift *= 2
        b_ref = b[ref_i:ref_i + 1, :]
        b_last = b[last_i:last_i + 1, :]
        q = q_ref[rows, :].astype(F32)
        k = k_ref[rows, :].astype(F32)
        v = v_ref[rows, :]
        q_rel = q * jnp.exp(b - b_ref)
        k_rel = (k * jnp.exp(b_ref - b)).astype(BF16)
        k_dec = (k * jnp.exp(b_last - b)).astype(BF16)
        q_dec = q * jnp.exp(b)
        sc = _dot_nt(stack_heads(q_rel).astype(BF16), k_rel)
        sc = jnp.where(keep, sc, 0.0).astype(BF16)
        o_inter = _dot_nt(stack_heads(q_dec).astype(BF16), st.astype(BF16))
        kv_t = _dot_tn(v, k_dec)
        new_st = jnp.exp(b_last) * st
        for hd in range(GLA_HEADS):
            vs = slice(hd * GLA_DV, (hd + 1) * GLA_DV)
            rs = slice(hd * C, (hd + 1) * C)
            acc_ref[rows, vs] = _dot(sc[rs, :], v[:, vs]) + o_inter[rs, :]
            new_st = new_st + jnp.where(lane_head == hd, kv_t[vs, :], 0.0)
        return new_st

    stf_ref[...] = jnp.zeros_like(stf_ref)
    stb_ref[...] = jnp.zeros_like(stb_ref)

    def body(i, carry):
        st_f = stf_ref[...]
        st_b = stb_ref[...]
        for u in range(GLA_CHUNK_UNROLL):
            cf = i * GLA_CHUNK_UNROLL + u
            st_f = chunk_step(cf, st_f, fwd_cfg, accf_ref)
            st_b = chunk_step(n_chunks - 1 - cf, st_b, bwd_cfg, accb_ref)
        stf_ref[...] = st_f
        stb_ref[...] = st_b
        return carry

    lax.fori_loop(0, n_chunks // GLA_CHUNK_UNROLL, body, 0)

    gain = gain_ref[...]

    def finish(r, carry):
        rows = pl.ds(pl.multiple_of(r * GLA_FINISH_ROWS, GLA_FINISH_ROWS), GLA_FINISH_ROWS)
        for hd in range(GLA_HEADS):
            vs = slice(hd * GLA_DV, (hd + 1) * GLA_DV)
            o = accf_ref[rows, vs] + accb_ref[rows, vs]
            on = o * lax.rsqrt(jnp.mean(o * o, axis=-1, keepdims=True) + EPS) * gain
            gt = gate_ref[rows, vs].astype(F32)
            o_ref[rows, vs] = (on * (gt * jax.nn.sigmoid(gt))).astype(BF16)
        return carry

    lax.fori_loop(0, seq // GLA_FINISH_ROWS, finish, 0)


def _attn_kernel(q_ref, k_ref, vt_ref, o_ref):
    seq = k_ref.shape[2]
    ones = jnp.ones((ONES_ROWS, seq), BF16)
    outs = []
    scores = [_dot_nt(q_ref[0, h], k_ref[0, h]) for h in range(ATTN_SCORES_AHEAD)]
    for hd in range(MLA_HEADS):
        if hd + ATTN_SCORES_AHEAD < MLA_HEADS:
            scores.append(_dot_nt(q_ref[0, hd + ATTN_SCORES_AHEAD], k_ref[0, hd + ATTN_SCORES_AHEAD]))
        s = scores[hd]
        p = jnp.exp2(s - jnp.max(s, axis=-1, keepdims=True)).astype(BF16)
        vt_aug = jnp.concatenate([vt_ref[0, hd * MLA_V:(hd + 1) * MLA_V, :], ones], axis=0)
        ot = _dot_nt(vt_aug, p)
        outs.append(ot[0:MLA_V, :] / ot[MLA_V:MLA_V + 1, :])
    o_ref[0] = jnp.concatenate(outs, axis=0).astype(BF16)


def _unpack_bf16_pair(w):
    lo = pltpu.unpack_elementwise(w, index=0, packed_dtype=BF16, unpacked_dtype=F32)
    hi = pltpu.unpack_elementwise(w, index=1, packed_dtype=BF16, unpacked_dtype=F32)
    return lo, hi


def _pack_bf16_pair(lo, hi):
    return pltpu.pack_elementwise([lo, hi], packed_dtype=BF16)


def _outproj_kernel(tiles_per_sb, x_ref, gla_ref, mla_ref, wo_ref, n2_ref, wr_ref, br_ref, tri_ref,
                    x1_ref, h2_ref, route_ref, cnt_ref, carry_ref):
    mix = _dot(gla_ref[...], wo_ref[0:GLA_WIDTH, :]) + _dot_tn(mla_ref[0], wo_ref[GLA_WIDTH:, :])
    x1 = x_ref[...] + mix
    x1_ref[...] = x1
    h2 = x1 * lax.rsqrt(jnp.mean(x1 * x1, axis=-1, keepdims=True) + EPS) * n2_ref[...]
    h2b = h2.astype(BF16)
    h2_ref[...] = h2b

    logits = _dot_nt(wr_ref[...], h2b) + br_ref[...]
    tm = logits.shape[1]
    gl = logits[0:N_GROUPS, :]
    ge = jnp.exp(gl - jnp.max(gl, axis=0, keepdims=True))
    pg = ge / jnp.sum(ge, axis=0, keepdims=True)
    p_top = jnp.max(pg, axis=0, keepdims=True)
    gi = lax.broadcasted_iota(jnp.int32, (N_GROUPS, tm), 0)
    g_idx = jnp.min(jnp.where(pg == p_top, gi, N_GROUPS), axis=0, keepdims=True)
    sel = jnp.zeros((EXPERTS_PER_GROUP, tm), F32)
    for g in range(N_GROUPS):
        r0 = ROUTER_EXPERT_ROW + g * EXPERTS_PER_GROUP
        sel = sel + jnp.where(g_idx == g, logits[r0:r0 + EXPERTS_PER_GROUP, :], 0.0)
    se = jnp.exp(sel - jnp.max(sel, axis=0, keepdims=True))
    pe = se / jnp.sum(se, axis=0, keepdims=True)
    ei = lax.broadcasted_iota(jnp.int32, (EXPERTS_PER_GROUP, tm), 0)
    m1 = jnp.max(pe, axis=0, keepdims=True)
    i1 = jnp.min(jnp.where(pe == m1, ei, EXPERTS_PER_GROUP), axis=0, keepdims=True)
    pe2 = jnp.where(ei == i1, -1.0, pe)
    m2 = jnp.max(pe2, axis=0, keepdims=True)
    i2 = jnp.min(jnp.where(pe2 == m2, ei, EXPERTS_PER_GROUP), axis=0, keepdims=True)
    den = m1 + m2
    e1 = g_idx * EXPERTS_PER_GROUP + i1
    e2 = g_idx * EXPERTS_PER_GROUP + i2

    @pl.when(pl.program_id(0) % tiles_per_sb == 0)
    def _():
        carry_ref[...] = jnp.zeros_like(carry_ref)

    eall = lax.broadcasted_iota(jnp.int32, (N_EXPERTS, tm), 0)
    tri = tri_ref[...]
    carry = carry_ref[...]
    ranks = []
    for eid in (e1, e2):
        hot = eall == eid
        prefix = _dot(hot.astype(BF16), tri)
        ranks.append(jnp.sum(jnp.where(hot, carry + prefix - 1.0, 0.0), axis=0, keepdims=True))
        carry = carry + prefix[:, tm - 1:tm]
    carry_ref[...] = carry
    cnt_ref[0] = jnp.broadcast_to(carry, (N_EXPERTS, LANES))
    zeros = jnp.zeros((2, tm), F32)
    route_ref[...] = jnp.concatenate([e1.astype(F32), e2.astype(F32), p_top * (m1 / den), p_top * (m2 / den),
                                      ranks[0], ranks[1], zeros], axis=0)


def _slots_kernel(route_ref, cnt_ref, strict_ref, pos_ref, off_ref):
    counts = cnt_ref[0]
    padded = 2.0 * jnp.floor(counts * 0.5 + 0.5)
    hi = jnp.floor(padded * (1.0 / 256.0))
    lo = padded - 256.0 * hi
    strict = strict_ref[...]
    off = 256.0 * _dot(strict, hi.astype(BF16)) + _dot(strict, lo.astype(BF16))
    off_ref[0] = off.astype(jnp.int32)
    tb = route_ref.shape[1]
    eall = lax.broadcasted_iota(jnp.int32, (N_EXPERTS, tb), 0)
    off_col = off[:, 0:1]
    for s in range(2):
        hot = eall == route_ref[s:s + 1, :].astype(jnp.int32)
        base = jnp.sum(jnp.where(hot, off_col, 0.0), axis=0, keepdims=True)
        pos_ref[0, s:s + 1, :] = ((base + route_ref[4 + s:5 + s, :]) * PACK_ROWS).astype(jnp.int32)


def _moe_kernel(tb, off_ref, cnt_ref, h2_ref, pos_ref, wg_ref, wu_ref, wd_ref, x1_ref, wt_ref, o_ref,
                xy_ref, cba_ref, cbb_ref):
    sb = pl.program_id(0)
    j = pl.program_id(1)
    n_scatter = tb // MOE_SCATTER_TILE
    n_combine = tb // MOE_COMBINE_TILE
    first_expert_step = n_scatter + 1
    first_combine_step = first_expert_step + N_EXPERTS

    def pack(dst_ref):
        h = h2_ref[...].astype(F32)
        for r in range(PACK_ROWS):
            c0 = r * 2 * LANES
            dst_ref[pl.ds(r, MOE_SCATTER_TILE, stride=PACK_ROWS), :] = _pack_bf16_pair(
                h[:, c0:c0 + LANES], h[:, c0 + LANES:c0 + 2 * LANES])

    def scatter(src_ref, tile):
        t0 = tile * MOE_SCATTER_TILE
        for pair in range(MOE_SCATTER_TILE // 2):
            v = src_ref[pair * 2 * PACK_ROWS:(pair + 1) * 2 * PACK_ROWS, :]
            for half in range(2):
                piece = v[half * PACK_ROWS:(half + 1) * PACK_ROWS, :]
                for slot in range(2):
                    p = pos_ref[slot * tb + t0 + pair * 2 + half]
                    xy_ref[pl.ds(pl.multiple_of(p, PACK_ROWS), PACK_ROWS), :] = piece

    def gather(dst_ref, tile):
        t0 = tile * MOE_COMBINE_TILE
        for t in range(MOE_COMBINE_TILE):
            for slot in range(2):
                p = pos_ref[slot * tb + t0 + t]
                r0 = (slot * MOE_COMBINE_TILE + t) * PACK_ROWS
                dst_ref[r0:r0 + PACK_ROWS, :] = xy_ref[pl.ds(pl.multiple_of(p, PACK_ROWS), PACK_ROWS), :]

    def weighted_sum(src_ref):
        w1 = wt_ref[:, 0:1]
        w2 = wt_ref[:, 1:2]
        second = MOE_COMBINE_TILE * PACK_ROWS
        for r in range(PACK_ROWS):
            lo1, hi1 = _unpack_bf16_pair(src_ref[pl.ds(r, MOE_COMBINE_TILE, stride=PACK_ROWS), :])
            lo2, hi2 = _unpack_bf16_pair(src_ref[pl.ds(second + r, MOE_COMBINE_TILE, stride=PACK_ROWS), :])
            c0 = r * 2 * LANES
            o_ref[:, c0:c0 + LANES] = x1_ref[:, c0:c0 + LANES] + w1 * lo1 + w2 * lo2
            o_ref[:, c0 + LANES:c0 + 2 * LANES] = x1_ref[:, c0 + LANES:c0 + 2 * LANES] + w1 * hi1 + w2 * hi2

    bufs = (cba_ref, cbb_ref)

    @pl.when(j == 0)
    def _first_pack():
        xy_ref[...] = jnp.zeros_like(xy_ref)
        pack(bufs[0])

    for parity in range(2):
        @pl.when((j >= 1) & (j < n_scatter) & (j % 2 == parity))
        def _scatter_and_pack():
            pack(bufs[parity])
            scatter(bufs[1 - parity], j - 1)

    @pl.when(j == n_scatter)
    def _last_scatter():
        scatter(bufs[(n_scatter - 1) % 2], n_scatter - 1)

    @pl.when((j >= first_expert_step) & (j < first_combine_step))
    def _experts():
        e = sb * N_EXPERTS + j - first_expert_step
        n = cnt_ref[e]
        base = off_ref[e]

        def body(i, c):
            r0 = base + i * MOE_CHUNK
            row0 = pl.multiple_of(r0 * PACK_ROWS, 2 * PACK_ROWS)
            halves = [_unpack_bf16_pair(xy_ref[pl.ds(row0 + r, MOE_CHUNK, stride=PACK_ROWS), :])
                      for r in range(PACK_ROWS)]
            a = jnp.zeros((MOE_CHUNK, D_EXPERT), F32)
            u = jnp.zeros((MOE_CHUNK, D_EXPERT), F32)
            for r in range(PACK_ROWS):
                lo, hi = halves[r]
                xb = jnp.concatenate([lo, hi], axis=1).astype(BF16)
                a = a + _dot(xb, wg_ref[0, r * 2 * LANES:(r + 1) * 2 * LANES, :].astype(BF16))
                u = u + _dot(xb, wu_ref[0, r * 2 * LANES:(r + 1) * 2 * LANES, :].astype(BF16))
            hid = ((a * jax.nn.sigmoid(a)) * u).astype(BF16)
            y = _dot(hid, wd_ref[0].astype(BF16))
            valid = (r0 + lax.broadcasted_iota(jnp.int32, (MOE_CHUNK, 1), 0)) < (base + n)
            for r in range(PACK_ROWS):
                c0 = r * 2 * LANES
                lo, hi = halves[r]
                xy_ref[pl.ds(row0 + r, MOE_CHUNK, stride=PACK_ROWS), :] = _pack_bf16_pair(
                    jnp.where(valid, y[:, c0:c0 + LANES], lo), jnp.where(valid, y[:, c0 + LANES:c0 + 2 * LANES], hi))
            return c

        lax.fori_loop(0, (n + MOE_CHUNK - 1) // MOE_CHUNK, body, 0)

    cstep = j - first_combine_step

    @pl.when(cstep == 0)
    def _first_gather():
        gather(bufs[0], 0)

    for parity in range(2):
        @pl.when((cstep >= 1) & (cstep < n_combine) & (cstep % 2 == parity))
        def _gather_and_sum():
            gather(bufs[parity], cstep)
            weighted_sum(bufs[1 - parity])

    @pl.when(cstep == n_combine)
    def _last_sum():
        weighted_sum(bufs[(n_combine - 1) % 2])


def _full(shape):
    return pl.BlockSpec(shape, lambda *_: (0,) * len(shape))


def _prep_weights(w_in, gk_fwd_w, gk_fwd_b, gk_bwd_w, gk_bwd_b, mla_w_qb, mla_w_kvb, q_norm_gain, k_norm_gain):
    splits = np.cumsum([0, GLA_QK_WIDTH, GLA_QK_WIDTH, GLA_WIDTH, GLA_WIDTH, GLA_GATE_RANK, GLA_GATE_RANK,
                        MLA_Q_RANK, MLA_KV_RANK, MLA_ROPE])
    seg = [w_in[:, splits[i]:splits[i + 1]] for i in range(9)]
    d = w_in.shape[0]
    misc = jnp.zeros((d, LANES), F32)
    misc = misc.at[:, 0:GLA_GATE_RANK].set(seg[4])
    misc = misc.at[:, MISC_LRB:MISC_LRB + GLA_GATE_RANK].set(seg[5])
    misc = misc.at[:, ROPE_LO:ROPE_LO + MLA_ROPE].set(seg[8])
    win = jnp.concatenate([seg[0], seg[1], seg[2], seg[3], seg[6], seg[7], misc], axis=1).astype(BF16)

    wg = jnp.zeros((LANES, 2 * GLA_QK_WIDTH), F32)
    wg = wg.at[0:GLA_GATE_RANK, 0:GLA_QK_WIDTH].set(gk_fwd_w)
    wg = wg.at[MISC_LRB:MISC_LRB + GLA_GATE_RANK, GLA_QK_WIDTH:].set(gk_bwd_w)
    bg = jnp.concatenate([gk_fwd_b, gk_bwd_b])[None, :]

    pad = HEAD_TILE - MLA_QK
    wqb = jnp.pad(mla_w_qb.reshape(MLA_Q_RANK, MLA_HEADS, MLA_QK), ((0, 0), (0, 0), (0, pad)))
    wqb = wqb.reshape(MLA_Q_RANK, MLA_HEADS * HEAD_TILE)
    wkv = mla_w_kvb.reshape(MLA_KV_RANK, MLA_HEADS, MLA_NOPE + MLA_V)
    wkvk = jnp.pad(wkv[:, :, :MLA_NOPE], ((0, 0), (0, 0), (0, HEAD_TILE - MLA_NOPE)))
    wkvk = wkvk.reshape(MLA_KV_RANK, MLA_HEADS * HEAD_TILE).astype(BF16)
    wkvv = wkv[:, :, MLA_NOPE:].reshape(MLA_KV_RANK, MLA_WIDTH).T.astype(BF16)
    qng = jnp.pad(q_norm_gain, (0, pad))[None, :]
    kng = jnp.pad(k_norm_gain, (0, pad))[None, :]
    def partner(w, sign):
        first, second = w[..., ROPE_LO:ROPE_LO + ROPE_HALF], w[..., ROPE_LO + ROPE_HALF:ROPE_LO + MLA_ROPE]
        return jnp.concatenate([jnp.zeros_like(w[..., :ROPE_LO]), sign * second, first,
                                jnp.zeros_like(w[..., ROPE_LO + MLA_ROPE:])], axis=-1)

    wqb_rot = partner(wqb.reshape(MLA_Q_RANK, MLA_HEADS, HEAD_TILE), -1.0).reshape(MLA_Q_RANK, -1).astype(BF16)
    qng_rot = partner(qng, 1.0)
    kng_rot = partner(kng, 1.0)
    return win, wg.astype(BF16), bg, wqb.astype(BF16), wqb_rot, wkvk, wkvv, qng, qng_rot, kng, kng_rot


def _rope_consts():
    inv = ROPE_BASE ** (-np.arange(0, MLA_ROPE, 2, dtype=np.float32) / MLA_ROPE)
    place = np.zeros((ROPE_HALF, LANES), np.float32)
    place[np.arange(ROPE_HALF), ROPE_LO + np.arange(ROPE_HALF)] = 1.0
    place[np.arange(ROPE_HALF), ROPE_LO + ROPE_HALF + np.arange(ROPE_HALF)] = 1.0
    ones_bd = np.kron(np.eye(2, dtype=np.float32), np.ones((LANES, LANES), np.float32))
    return jnp.asarray(inv[:, None]), jnp.asarray(place, BF16), jnp.asarray(ones_bd, BF16)


def _layer(x2, pos2, batch, seq, norm1_gain, w_in, gk_fwd_w, gk_fwd_b, gk_bwd_w, gk_bwd_b, gla_out_gain,
           mla_q_gain, mla_w_qb, mla_kv_gain, mla_w_kvb, q_norm_gain, k_norm_gain, w_out, norm2_gain,
           w_router_group, b_router_group, w_router_expert, b_router_expert,
           w_expert_gate, w_expert_up, w_expert_down):
    T, D = x2.shape
    win, wg, bg, wqb, wqb_rot, wkvk, wkvv, qng, qng_rot, kng, kng_rot = _prep_weights(
        w_in, gk_fwd_w, gk_fwd_b, gk_bwd_w, gk_bwd_b, mla_w_qb, mla_w_kvb, q_norm_gain, k_norm_gain)

    TM = 512
    spb = seq // TM
    tok = lambda w: pl.BlockSpec((TM, w), lambda i: (i, 0))
    head_spec = pl.BlockSpec((1, MLA_HEADS, TM, HEAD_TILE), lambda i: (i // spb, 0, i % spb, 0))
    gq, gk, gv, gg, lg, mq, mk, mv = pl.pallas_call(
        _proj_kernel,
        grid=(T // TM,),
        in_specs=[tok(D), pl.BlockSpec((1, 1, TM), lambda i: (i, 0, 0)), _full((1, D)), _full((D, PROJ_WIDTH)),
                  _full((LANES, 2 * GLA_QK_WIDTH)), _full((1, 2 * GLA_QK_WIDTH)), _full((1, MLA_Q_RANK)),
                  _full((MLA_Q_RANK, MLA_HEADS * HEAD_TILE)), _full((MLA_Q_RANK, MLA_HEADS * HEAD_TILE)),
                  _full((1, MLA_KV_RANK)), _full((MLA_KV_RANK, MLA_HEADS * HEAD_TILE)),
                  _full((MLA_WIDTH, MLA_KV_RANK)), _full((1, HEAD_TILE)), _full((1, HEAD_TILE)),
                  _full((1, HEAD_TILE)), _full((1, HEAD_TILE)),
                  _full((ROPE_HALF, 1)), _full((ROPE_HALF, LANES)), _full((2 * LANES, 2 * LANES))],
        out_specs=[tok(GLA_QK_WIDTH), tok(GLA_QK_WIDTH), tok(GLA_WIDTH), tok(GLA_WIDTH), tok(2 * GLA_QK_WIDTH),
                   head_spec, head_spec, pl.BlockSpec((1, MLA_WIDTH, TM), lambda i: (i // spb, 0, i % spb))],
        out_shape=[jax.ShapeDtypeStruct((T, GLA_QK_WIDTH), BF16), jax.ShapeDtypeStruct((T, GLA_QK_WIDTH), BF16),
                   jax.ShapeDtypeStruct((T, GLA_WIDTH), BF16), jax.ShapeDtypeStruct((T, GLA_WIDTH), BF16),
                   jax.ShapeDtypeStruct((T, 2 * GLA_QK_WIDTH), F32),
                   jax.ShapeDtypeStruct((batch, MLA_HEADS, seq, HEAD_TILE), BF16),
                   jax.ShapeDtypeStruct((batch, MLA_HEADS, seq, HEAD_TILE), BF16),
                   jax.ShapeDtypeStruct((batch, MLA_WIDTH, seq), BF16)],
        compiler_params=pltpu.CompilerParams(dimension_semantics=("parallel",), vmem_limit_bytes=VMEM_LIMIT),
        name="proj",
    )(x2, pos2.reshape(T // TM, 1, TM), norm1_gain[None, :], win, wg, bg, mla_q_gain[None, :], wqb, wqb_rot,
      mla_kv_gain[None, :], wkvk, wkvv, qng, qng_rot, kng, kng_rot, *_rope_consts())

    seqspec = lambda w: pl.BlockSpec((seq, w), lambda b: (b, 0))
    gla_out = pl.pallas_call(
        _gla_kernel,
        grid=(batch,),
        in_specs=[seqspec(GLA_QK_WIDTH), seqspec(GLA_QK_WIDTH), seqspec(GLA_WIDTH), seqspec(2 * GLA_QK_WIDTH),
                  seqspec(GLA_WIDTH), _full((1, GLA_DV))],
        out_specs=seqspec(GLA_WIDTH),
        out_shape=jax.ShapeDtypeStruct((T, GLA_WIDTH), BF16),
        scratch_shapes=[pltpu.VMEM((seq, GLA_WIDTH), F32), pltpu.VMEM((seq, GLA_WIDTH), F32),
                        pltpu.VMEM((GLA_DV, GLA_QK_WIDTH), F32), pltpu.VMEM((GLA_DV, GLA_QK_WIDTH), F32)],
        compiler_params=pltpu.CompilerParams(dimension_semantics=("parallel",), vmem_limit_bytes=VMEM_LIMIT),
        name="gla",
    )(gq, gk, gv, lg, gg, gla_out_gain[None, :])

    TQ = 256
    nq = seq // TQ
    mla_out = pl.pallas_call(
        _attn_kernel,
        grid=(batch, nq),
        in_specs=[pl.BlockSpec((1, MLA_HEADS, TQ, HEAD_TILE), lambda b, i: (b, 0, i, 0)),
                  pl.BlockSpec((1, MLA_HEADS, seq, HEAD_TILE), lambda b, i: (b, 0, 0, 0)),
                  pl.BlockSpec((1, MLA_WIDTH, seq), lambda b, i: (b, 0, 0))],
        out_specs=pl.BlockSpec((1, MLA_WIDTH, TQ), lambda b, i: (b, 0, i)),
        out_shape=jax.ShapeDtypeStruct((batch, MLA_WIDTH, seq), BF16),
        compiler_params=pltpu.CompilerParams(dimension_semantics=("parallel", "parallel"),
                                             vmem_limit_bytes=VMEM_LIMIT),
        name="attn",
    )(mq, mk, mv)

    wr = jnp.zeros((ROUTER_ROWS, D), F32)
    wr = wr.at[0:N_GROUPS].set(w_router_group.T)
    wr = wr.at[ROUTER_EXPERT_ROW:ROUTER_EXPERT_ROW + N_EXPERTS].set(w_router_expert.T)
    br = jnp.zeros((ROUTER_ROWS, 1), F32)
    br = br.at[0:N_GROUPS, 0].set(b_router_group)
    br = br.at[ROUTER_EXPERT_ROW:ROUTER_EXPERT_ROW + N_EXPERTS, 0].set(b_router_expert)
    tb = min(MOE_SUPER_BLOCK, T)
    nsb = T // tb
    TO = min(OUTPROJ_TILE, seq)
    tiles_per_sb = tb // TO
    tok = lambda w: pl.BlockSpec((TO, w), lambda i: (i, 0))
    tri = jnp.asarray(np.triu(np.ones((TO, TO), np.float32)), BF16)
    x1, h2, route, cnt = pl.pallas_call(
        functools.partial(_outproj_kernel, tiles_per_sb),
        grid=(T // TO,),
        in_specs=[tok(D), tok(GLA_WIDTH),
                  pl.BlockSpec((1, MLA_WIDTH, TO), lambda i: (i // (seq // TO), 0, i % (seq // TO))),
                  _full((D, D)), _full((1, D)),
                  _full((ROUTER_ROWS, D)), _full((ROUTER_ROWS, 1)), _full((TO, TO))],
        out_specs=[tok(D), tok(D), pl.BlockSpec((8, TO), lambda i: (0, i)),
                   pl.BlockSpec((1, N_EXPERTS, LANES), lambda i: (i // tiles_per_sb, 0, 0))],
        out_shape=[jax.ShapeDtypeStruct((T, D), F32), jax.ShapeDtypeStruct((T, D), BF16),
                   jax.ShapeDtypeStruct((8, T), F32), jax.ShapeDtypeStruct((nsb, N_EXPERTS, LANES), F32)],
        scratch_shapes=[pltpu.VMEM((N_EXPERTS, 1), F32)],
        compiler_params=pltpu.CompilerParams(dimension_semantics=("arbitrary",), vmem_limit_bytes=VMEM_LIMIT),
        name="outproj",
    )(x2, gla_out, mla_out, w_out.astype(BF16), norm2_gain[None, :], wr.astype(BF16), br, tri)

    strict = jnp.asarray(np.tril(np.ones((N_EXPERTS, N_EXPERTS), np.float32), -1), BF16)
    pos, offs = pl.pallas_call(
        _slots_kernel,
        grid=(nsb,),
        in_specs=[pl.BlockSpec((8, tb), lambda s: (0, s)), pl.BlockSpec((1, N_EXPERTS, LANES), lambda s: (s, 0, 0)),
                  _full((N_EXPERTS, N_EXPERTS))],
        out_specs=[pl.BlockSpec((1, 2, tb), lambda s: (s, 0, 0)),
                   pl.BlockSpec((1, N_EXPERTS, LANES), lambda s: (s, 0, 0))],
        out_shape=[jax.ShapeDtypeStruct((nsb, 2, tb), jnp.int32),
                   jax.ShapeDtypeStruct((nsb, N_EXPERTS, LANES), jnp.int32)],
        compiler_params=pltpu.CompilerParams(dimension_semantics=("parallel",)),
        name="slots",
    )(route, cnt, strict)
    pos = pos.reshape(-1)
    flat_off = offs[:, :, 0].reshape(-1)
    counts = cnt[:, :, 0].astype(jnp.int32)
    wts = jnp.stack([route[2], route[3]], axis=1)
    wge = w_expert_gate.reshape(N_EXPERTS, D, D_EXPERT)
    wue = w_expert_up.reshape(N_EXPERTS, D, D_EXPERT)
    wde = w_expert_down.reshape(N_EXPERTS, D_EXPERT, D)
    nct = tb // MOE_COMBINE_TILE
    nst = tb // MOE_SCATTER_TILE
    slots =(2 * tb + N_EXPERTS + MOE_CHUNK + 7) // 8 * 8
    hidx = lambda sb, j, off, cn: (sb * nst + jnp.minimum(j, nst - 1), 0)
    eidx = lambda sb, j, off, cn: (jnp.clip(j - (nst + 1), 0, N_EXPERTS - 1), 0, 0)
    tidx = lambda sb, j, off, cn: (sb * nct + jnp.clip(j - (nst + 1 + N_EXPERTS) - 1, 0, nct - 1), 0)
    out = pl.pallas_call(
        functools.partial(_moe_kernel, tb),
        grid_spec=pltpu.PrefetchScalarGridSpec(
            num_scalar_prefetch=2,
            grid=(nsb, (nst + 1) + N_EXPERTS + (nct + 1)),
            in_specs=[pl.BlockSpec((MOE_SCATTER_TILE, D), hidx),
                      pl.BlockSpec((2 * tb,), lambda sb, j, off, cn: (sb,), memory_space=pltpu.SMEM),
                      pl.BlockSpec((1, D, D_EXPERT), eidx), pl.BlockSpec((1, D, D_EXPERT), eidx),
                      pl.BlockSpec((1, D_EXPERT, D), eidx),
                      pl.BlockSpec((MOE_COMBINE_TILE, D), tidx), pl.BlockSpec((MOE_COMBINE_TILE, 2), tidx)],
            out_specs=pl.BlockSpec((MOE_COMBINE_TILE, D), tidx),
            scratch_shapes=[pltpu.VMEM((slots * PACK_ROWS, LANES), U32),
                            pltpu.VMEM((MOE_SCATTER_TILE * PACK_ROWS, LANES), U32),
                            pltpu.VMEM((MOE_SCATTER_TILE * PACK_ROWS, LANES), U32)]),
        out_shape=jax.ShapeDtypeStruct((T, D), F32),
        compiler_params=pltpu.CompilerParams(dimension_semantics=("arbitrary", "arbitrary"),
                                             vmem_limit_bytes=VMEM_LIMIT),
        name="moe",
    )(flat_off, counts.reshape(-1), h2, pos, wge, wue, wde, x1, wts)
    return out


def kernel(x, positions, norm1_gain, w_in, gla_gk_fwd_w, gla_gk_fwd_b, gla_gk_bwd_w, gla_gk_bwd_b, gla_out_gain, mla_q_gain, mla_w_qb, mla_kv_gain, mla_w_kvb, q_norm_gain, k_norm_gain, w_out, norm2_gain, w_router_group, b_router_group, w_router_expert, b_router_expert, w_expert_gate, w_expert_up, w_expert_down):
    batch, seq, d = x.shape
    x2 = x.reshape(batch * seq, d)
    pos2 = positions.reshape(batch * seq, 1)
    for l in range(norm1_gain.shape[0]):
        x2 = _layer(x2, pos2, batch, seq, norm1_gain[l], w_in[l], gla_gk_fwd_w[l], gla_gk_fwd_b[l],
                    gla_gk_bwd_w[l], gla_gk_bwd_b[l], gla_out_gain[l], mla_q_gain[l], mla_w_qb[l],
                    mla_kv_gain[l], mla_w_kvb[l], q_norm_gain[l], k_norm_gain[l], w_out[l], norm2_gain[l],
                    w_router_group[l], b_router_group[l], w_router_expert[l], b_router_expert[l],
                    w_expert_gate[l], w_expert_up[l], w_expert_down[l])
    return x2.reshape(batch, seq, d)
```

```python
import functools

import numpy as np
import jax
import jax.numpy as jnp
from jax import lax
from jax.experimental import pallas as pl
from jax.experimental.pallas import tpu as pltpu

F32 = jnp.float32
BF16 = jnp.bfloat16

D_MODEL = 1024
GLA_HEADS = 4
GLA_DK = 64
GLA_DV = 128
GLA_GATE_RANK = 16
GLA_GATE_NORMALIZER = 16.0
GLA_CHUNK = 64
GLA_CHUNK_UNROLL = 8
GLA_FINISH_ROWS = 256
MLA_HEADS = 8
MLA_NOPE = 64
MLA_ROPE = 32
MLA_QK = MLA_NOPE + MLA_ROPE
MLA_V = 64
MLA_Q_RANK = 256
MLA_KV_RANK = 128
ROPE_BASE = 10000.0
GLA_QK_WIDTH = GLA_HEADS * GLA_DK
GLA_WIDTH = GLA_HEADS * GLA_DV
MLA_WIDTH = MLA_HEADS * MLA_V
N_GROUPS = 4
EXPERTS_PER_GROUP = 8
N_EXPERTS = N_GROUPS * EXPERTS_PER_GROUP
D_EXPERT = 256
EPS = 1e-6

LANES = 128
HEAD_TILE = LANES
ROPE_LO = MLA_NOPE
ROPE_HALF = MLA_ROPE // 2
LOG2_E = 1.4426950408889634
ATTN_SCORES_AHEAD = 3
ONES_ROWS = 16

COL_GQ = 0
COL_GK = COL_GQ + GLA_QK_WIDTH
COL_GV = COL_GK + GLA_QK_WIDTH
COL_GG = COL_GV + GLA_WIDTH
COL_MQ = COL_GG + GLA_WIDTH
COL_MKV = COL_MQ + MLA_Q_RANK
COL_MISC = COL_MKV + MLA_KV_RANK
PROJ_WIDTH = COL_MISC + LANES
MISC_LRB = GLA_GATE_RANK

ROUTER_ROWS = LANES
ROUTER_EXPERT_ROW = 8

U32 = jnp.uint32
HI_HALF_MASK = np.uint32(0xFFFF0000)
PACK_ROWS = D_MODEL // (2 * LANES)
MOE_SUPER_BLOCK = 8192
MOE_CHUNK = 576
MOE_SCATTER_TILE = 512
MOE_COMBINE_TILE = 256
PROJ_TILE = 1024
OUTPROJ_TILE = 1024
assert 2 * MOE_COMBINE_TILE == MOE_SCATTER_TILE

VMEM_LIMIT = 56 * 1024 * 1024


def _dot(a, b):
    return jnp.dot(a, b, preferred_element_type=F32)


def _dot_nt(a, b):
    return lax.dot_general(a, b, (((1,), (1,)), ((), ())), preferred_element_type=F32)


def _dot_tn(a, b):
    return lax.dot_general(a, b, (((0,), (0,)), ((), ())), preferred_element_type=F32)


def _split_bf16(x):
    hi = x.astype(BF16)
    lo = (x - hi.astype(F32)).astype(BF16)
    return hi, lo


def _proj_kernel(x_ref, pos_ref, n1_ref, win_ref, wg_ref, bg_ref, qgain_ref, wqb_ref, wqbr_ref, kvgain_ref,
                 wkvk_ref, wkvv_ref, qng_ref, qngr_ref, kng_ref, kngr_ref, freq_ref, place_ref, ones_ref,
                 gq_ref, gk_ref, gv_ref, gg_ref, lg_ref, mq_ref, mk_ref, mv_ref):
    x = x_ref[...]
    h = x * lax.rsqrt(jnp.mean(x * x, axis=-1, keepdims=True) + EPS) * n1_ref[...]
    proj = _dot(h.astype(BF16), win_ref[...])

    gq_ref[...] = (proj[:, COL_GQ:COL_GK] * GLA_DK ** -0.5).astype(BF16)
    gk_ref[...] = proj[:, COL_GK:COL_GV].astype(BF16)
    gv_ref[...] = proj[:, COL_GV:COL_GG].astype(BF16)
    gg_ref[...] = proj[:, COL_GG:COL_MQ].astype(BF16)

    misc = proj[:, COL_MISC:PROJ_WIDTH]
    z = _dot(misc.astype(BF16), wg_ref[...]) + bg_ref[...]
    log_sig = -(jnp.maximum(-z, 0.0) + jnp.log(1.0 + jnp.exp(-jnp.abs(z))))
    lg_ref[...] = log_sig / GLA_GATE_NORMALIZER

    lane = lax.broadcasted_iota(jnp.int32, (1, LANES), 1)
    ang_t = freq_ref[...] * pos_ref[0].astype(F32)
    cos_hi, cos_lo = _split_bf16(jnp.cos(ang_t))
    sin_hi, sin_lo = _split_bf16(jnp.sin(ang_t))
    tabs = _dot_tn(jnp.concatenate([cos_hi, cos_lo, sin_hi, sin_lo], axis=0), place_ref[...])
    in_rope = (lane >= ROPE_LO) & (lane < ROPE_LO + MLA_ROPE)
    c_tab = jnp.where(lane < ROPE_LO, 1.0, tabs[:, 0:LANES])
    s_tab = tabs[:, LANES:2 * LANES]
    ones_bd = ones_ref[...]

    def norm_rope_pair(t2, r2, gc, gs):
        ss = _dot((t2 * t2).astype(BF16), ones_bd) * (1.0 / MLA_QK)
        rs = lax.rsqrt(ss + EPS)
        outs = []
        for i in range(2):
            sl = slice(i * LANES, (i + 1) * LANES)
            outs.append(rs[:, sl] * (t2[:, sl] * gc + r2[:, sl] * gs))
        return outs

    qa = proj[:, COL_MQ:COL_MKV]
    qn = qa * lax.rsqrt(jnp.mean(qa * qa, axis=-1, keepdims=True) + EPS) * qgain_ref[...]
    qn = qn.astype(BF16)
    mq = _dot(qn, wqb_ref[...])
    mq_rot = _dot(qn, wqbr_ref[...])
    kva = proj[:, COL_MKV:COL_MISC]
    kvn = (kva * lax.rsqrt(jnp.mean(kva * kva, axis=-1, keepdims=True) + EPS) * kvgain_ref[...]).astype(BF16)
    kn = _dot(kvn, wkvk_ref[...])
    mv_ref[0] = _dot_nt(wkvv_ref[...], kvn).astype(BF16)
    rope_tile = jnp.where(in_rope, misc, 0.0)
    up = (lane >= ROPE_LO + ROPE_HALF) & (lane < ROPE_LO + MLA_ROPE)
    dn = (lane >= ROPE_LO) & (lane < ROPE_LO + ROPE_HALF)
    rope_rot = (jnp.where(up, pltpu.roll(rope_tile, ROPE_HALF, 1), 0.0)
                - jnp.where(dn, pltpu.roll(rope_tile, LANES - ROPE_HALF, 1), 0.0))
    rope2 = jnp.concatenate([rope_tile, rope_tile], axis=1)
    rope_rot2 = jnp.concatenate([rope_rot, rope_rot], axis=1)
    scale = MLA_QK ** -0.5 * LOG2_E
    q_gc, q_gs = (qng_ref[...] * scale) * c_tab, (qngr_ref[...] * scale) * s_tab
    k_gc, k_gs = kng_ref[...] * c_tab, kngr_ref[...] * s_tab
    for hp in range(MLA_HEADS // 2):
        sl = slice(hp * 2 * HEAD_TILE, (hp + 1) * 2 * HEAD_TILE)
        q2 = norm_rope_pair(mq[:, sl], mq_rot[:, sl], q_gc, q_gs)
        k2 = norm_rope_pair(kn[:, sl] + rope2, rope_rot2, k_gc, k_gs)
        for i in range(2):
            mq_ref[0, 2 * hp + i] = q2[i].astype(BF16)
            mk_ref[0, 2 * hp + i] = k2[i].astype(BF16)


def _gla_kernel(q_ref, k_ref, v_ref, lg_ref, gate_ref, gain_ref, o_ref, accf_ref, accb_ref, stf_ref, stb_ref):
    seq = q_ref.shape[0]
    C = GLA_CHUNK
    n_chunks = seq // C
    HK = GLA_QK_WIDTH

    lane_head = lax.broadcasted_iota(jnp.int32, (1, HK), 1) // GLA_DK
    rowi = lax.broadcasted_iota(jnp.int32, (C, 1), 0)
    srow = lax.broadcasted_iota(jnp.int32, (GLA_HEADS * C, C), 0) % C
    scol = lax.broadcasted_iota(jnp.int32, (GLA_HEADS * C, C), 1)
    fwd_cfg = (True, C // 2 - 1, C - 1, scol <= srow, 0)
    bwd_cfg = (False, C // 2, 0, scol > srow, HK)

    def stack_heads(t):
        return jnp.concatenate([jnp.where(lane_head == hd, t, 0.0) for hd in range(GLA_HEADS)], axis=0)

    def chunk_step(c, st, cfg, acc_ref):
        prefix, ref_i, last_i, keep, lg_off = cfg
        rows = pl.ds(pl.multiple_of(c * C, C), C)
        b = lg_ref[rows, lg_off:lg_off + HK]
        shift = 1
        while shift < C:
            if prefix:
                b = b + jnp.where(rowi >= shift, pltpu.roll(b, shift, 0), 0.0)
            else:
                b = b + jnp.where(rowi < C - shift, pltpu.roll(b, C - shift, 0), 0.0)
            shift *= 2
        b_ref = b[ref_i:ref_i + 1, :]
        b_last = b[last_i:last_i + 1, :]
        q = q_ref[rows, :].astype(F32)
        k = k_ref[rows, :].astype(F32)
        v = v_ref[rows, :]
        q_rel = q * jnp.exp(b - b_ref)
        k_rel = (k * jnp.exp(b_ref - b)).astype(BF16)
        k_dec = (k * jnp.exp(b_last - b)).astype(BF16)
        q_dec = q * jnp.exp(b)
        sc = _dot_nt(stack_heads(q_rel).astype(BF16), k_rel)
        sc = jnp.where(keep, sc, 0.0).astype(BF16)
        o_inter = _dot_nt(stack_heads(q_dec).astype(BF16), st.astype(BF16))
        kv_t = _dot_tn(v, k_dec)
        new_st = jnp.exp(b_last) * st
        for hd in range(GLA_HEADS):
            vs = slice(hd * GLA_DV, (hd + 1) * GLA_DV)
            rs = slice(hd * C, (hd + 1) * C)
            acc_ref[rows, vs] = _dot(sc[rs, :], v[:, vs]) + o_inter[rs, :]
            new_st = new_st + jnp.where(lane_head == hd, kv_t[vs, :], 0.0)
        return new_st

    stf_ref[...] = jnp.zeros_like(stf_ref)
    stb_ref[...] = jnp.zeros_like(stb_ref)

    def body(i, carry):
        st_f = stf_ref[...]
        st_b = stb_ref[...]
        for u in range(GLA_CHUNK_UNROLL):
            cf = i * GLA_CHUNK_UNROLL + u
            st_f = chunk_step(cf, st_f, fwd_cfg, accf_ref)
            st_b = chunk_step(n_chunks - 1 - cf, st_b, bwd_cfg, accb_ref)
        stf_ref[...] = st_f
        stb_ref[...] = st_b
        return carry

    lax.fori_loop(0, n_chunks // GLA_CHUNK_UNROLL, body, 0)

    gain = gain_ref[...]

    def finish(r, carry):
        rows = pl.ds(pl.multiple_of(r * GLA_FINISH_ROWS, GLA_FINISH_ROWS), GLA_FINISH_ROWS)
        for hd in range(GLA_HEADS):
            vs = slice(hd * GLA_DV, (hd + 1) * GLA_DV)
            o = accf_ref[rows, vs] + accb_ref[rows, vs]
            on = o * lax.rsqrt(jnp.mean(o * o, axis=-1, keepdims=True) + EPS) * gain
            gt = gate_ref[rows, vs].astype(F32)
            o_ref[rows, vs] = (on * (gt * jax.nn.sigmoid(gt))).astype(BF16)
        return carry

    lax.fori_loop(0, seq // GLA_FINISH_ROWS, finish, 0)


def _attn_kernel(q_ref, k_ref, vt_ref, o_ref):
    seq = k_ref.shape[2]
    ones = jnp.ones((ONES_ROWS, seq), BF16)
    outs = []
    scores = [_dot_nt(q_ref[0, h], k_ref[0, h]) for h in range(ATTN_SCORES_AHEAD)]
    for hd in range(MLA_HEADS):
        if hd + ATTN_SCORES_AHEAD < MLA_HEADS:
            scores.append(_dot_nt(q_ref[0, hd + ATTN_SCORES_AHEAD], k_ref[0, hd + ATTN_SCORES_AHEAD]))
        s = scores[hd]
        p = jnp.exp2(s - jnp.max(s, axis=-1, keepdims=True)).astype(BF16)
        vt_aug = jnp.concatenate([vt_ref[0, hd * MLA_V:(hd + 1) * MLA_V, :], ones], axis=0)
        ot = _dot_nt(vt_aug, p)
        outs.append(ot[0:MLA_V, :] / ot[MLA_V:MLA_V + 1, :])
    o_ref[0] = jnp.concatenate(outs, axis=0).astype(BF16)


def _unpack_bf16_pair(w):
    lo = pltpu.unpack_elementwise(w, index=0, packed_dtype=BF16, unpacked_dtype=F32)
    hi = pltpu.unpack_elementwise(w, index=1, packed_dtype=BF16, unpacked_dtype=F32)
    return lo, hi


def _pack_bf16_pair(lo, hi):
    return pltpu.pack_elementwise([lo, hi], packed_dtype=BF16)


def _outproj_kernel(tiles_per_sb, x_ref, gla_ref, mla_ref, wo_ref, n2_ref, wr_ref, br_ref, tri_ref,
                    x1_ref, h2_ref, route_ref, cnt_ref, carry_ref):
    mix = _dot(gla_ref[...], wo_ref[0:GLA_WIDTH, :]) + _dot_tn(mla_ref[0], wo_ref[GLA_WIDTH:, :])
    x1 = x_ref[...] + mix
    x1_ref[...] = x1
    h2 = x1 * lax.rsqrt(jnp.mean(x1 * x1, axis=-1, keepdims=True) + EPS) * n2_ref[...]
    h2b = h2.astype(BF16)
    h2_ref[...] = h2b

    logits = _dot_nt(wr_ref[...], h2b) + br_ref[...]
    tm = logits.shape[1]
    gl = logits[0:N_GROUPS, :]
    ge = jnp.exp(gl - jnp.max(gl, axis=0, keepdims=True))
    pg = ge / jnp.sum(ge, axis=0, keepdims=True)
    p_top = jnp.max(pg, axis=0, keepdims=True)
    gi = lax.broadcasted_iota(jnp.int32, (N_GROUPS, tm), 0)
    g_idx = jnp.min(jnp.where(pg == p_top, gi, N_GROUPS), axis=0, keepdims=True)
    sel = jnp.zeros((EXPERTS_PER_GROUP, tm), F32)
    for g in range(N_GROUPS):
        r0 = ROUTER_EXPERT_ROW + g * EXPERTS_PER_GROUP
        sel = sel + jnp.where(g_idx == g, logits[r0:r0 + EXPERTS_PER_GROUP, :], 0.0)
    se = jnp.exp(sel - jnp.max(sel, axis=0, keepdims=True))
    pe = se / jnp.sum(se, axis=0, keepdims=True)
    ei = lax.broadcasted_iota(jnp.int32, (EXPERTS_PER_GROUP, tm), 0)
    m1 = jnp.max(pe, axis=0, keepdims=True)
    i1 = jnp.min(jnp.where(pe == m1, ei, EXPERTS_PER_GROUP), axis=0, keepdims=True)
    pe2 = jnp.where(ei == i1, -1.0, pe)
    m2 = jnp.max(pe2, axis=0, keepdims=True)
    i2 = jnp.min(jnp.where(pe2 == m2, ei, EXPERTS_PER_GROUP), axis=0, keepdims=True)
    den = m1 + m2
    e1 = g_idx * EXPERTS_PER_GROUP + i1
    e2 = g_idx * EXPERTS_PER_GROUP + i2

    @pl.when(pl.program_id(0) % tiles_per_sb == 0)
    def _():
        carry_ref[...] = jnp.zeros_like(carry_ref)

    eall = lax.broadcasted_iota(jnp.int32, (N_EXPERTS, tm), 0)
    tri = tri_ref[...]
    carry = carry_ref[...]
    ranks = []
    for eid in (e1, e2):
        hot = eall == eid
        prefix = _dot(hot.astype(BF16), tri)
        ranks.append(jnp.sum(jnp.where(hot, carry + prefix - 1.0, 0.0), axis=0, keepdims=True))
        carry = carry + prefix[:, tm - 1:tm]
    carry_ref[...] = carry
    cnt_ref[0] = jnp.broadcast_to(carry, (N_EXPERTS, LANES))
    zeros = jnp.zeros((2, tm), F32)
    route_ref[...] = jnp.concatenate([e1.astype(F32), e2.astype(F32), p_top * (m1 / den), p_top * (m2 / den),
                                      ranks[0], ranks[1], zeros], axis=0)


def _slots_kernel(route_ref, cnt_ref, strict_ref, pos_ref, off_ref):
    counts = cnt_ref[0]
    padded = 2.0 * jnp.floor(counts * 0.5 + 0.5)
    hi = jnp.floor(padded * (1.0 / 256.0))
    lo = padded - 256.0 * hi
    strict = strict_ref[...]
    off = 256.0 * _dot(strict, hi.astype(BF16)) + _dot(strict, lo.astype(BF16))
    off_ref[0] = off.astype(jnp.int32)
    tb = route_ref.shape[1]
    eall = lax.broadcasted_iota(jnp.int32, (N_EXPERTS, tb), 0)
    off_col = off[:, 0:1]
    for s in range(2):
        hot = eall == route_ref[s:s + 1, :].astype(jnp.int32)
        base = jnp.sum(jnp.where(hot, off_col, 0.0), axis=0, keepdims=True)
        pos_ref[0, s:s + 1, :] = ((base + route_ref[4 + s:5 + s, :]) * PACK_ROWS).astype(jnp.int32)


def _moe_kernel(tb, off_ref, cnt_ref, h2_ref, pos_ref, wg_ref, wu_ref, wd_ref, x1_ref, wt_ref, o_ref,
                xy_ref, cba_ref, cbb_ref):
    sb = pl.program_id(0)
    j = pl.program_id(1)
    n_scatter = tb // MOE_SCATTER_TILE
    n_combine = tb // MOE_COMBINE_TILE
    first_expert_step = n_scatter + 1
    first_combine_step = first_expert_step + N_EXPERTS

    def pack(dst_ref):
        h = h2_ref[...].astype(F32)
        for r in range(PACK_ROWS):
            c0 = r * 2 * LANES
            dst_ref[pl.ds(r, MOE_SCATTER_TILE, stride=PACK_ROWS), :] = _pack_bf16_pair(
                h[:, c0:c0 + LANES], h[:, c0 + LANES:c0 + 2 * LANES])

    def scatter(src_ref, tile):
        t0 = tile * MOE_SCATTER_TILE
        for pair in range(MOE_SCATTER_TILE // 2):
            v = src_ref[pair * 2 * PACK_ROWS:(pair + 1) * 2 * PACK_ROWS, :]
            for half in range(2):
                piece = v[half * PACK_ROWS:(half + 1) * PACK_ROWS, :]
                for slot in range(2):
                    p = pos_ref[slot * tb + t0 + pair * 2 + half]
                    xy_ref[pl.ds(pl.multiple_of(p, PACK_ROWS), PACK_ROWS), :] = piece

    def gather(dst_ref, tile):
        t0 = tile * MOE_COMBINE_TILE
        for t in range(MOE_COMBINE_TILE):
            for slot in range(2):
                p = pos_ref[slot * tb + t0 + t]
                r0 = (slot * MOE_COMBINE_TILE + t) * PACK_ROWS
                dst_ref[r0:r0 + PACK_ROWS, :] = xy_ref[pl.ds(pl.multiple_of(p, PACK_ROWS), PACK_ROWS), :]

    def weighted_sum(src_ref):
        w1 = wt_ref[:, 0:1]
        w2 = wt_ref[:, 1:2]
        second = MOE_COMBINE_TILE * PACK_ROWS
        for r in range(PACK_ROWS):
            lo1, hi1 = _unpack_bf16_pair(src_ref[pl.ds(r, MOE_COMBINE_TILE, stride=PACK_ROWS), :])
            lo2, hi2 = _unpack_bf16_pair(src_ref[pl.ds(second + r, MOE_COMBINE_TILE, stride=PACK_ROWS), :])
            c0 = r * 2 * LANES
            o_ref[:, c0:c0 + LANES] = x1_ref[:, c0:c0 + LANES] + w1 * lo1 + w2 * lo2
            o_ref[:, c0 + LANES:c0 + 2 * LANES] = x1_ref[:, c0 + LANES:c0 + 2 * LANES] + w1 * hi1 + w2 * hi2

    bufs = (cba_ref, cbb_ref)

    @pl.when(j == 0)
    def _first_pack():
        xy_ref[...] = jnp.zeros_like(xy_ref)
        pack(bufs[0])

    for parity in range(2):
        @pl.when((j >= 1) & (j < n_scatter) & (j % 2 == parity))
        def _scatter_and_pack():
            pack(bufs[parity])
            scatter(bufs[1 - parity], j - 1)

    @pl.when(j == n_scatter)
    def _last_scatter():
        scatter(bufs[(n_scatter - 1) % 2], n_scatter - 1)

    @pl.when((j >= first_expert_step) & (j < first_combine_step))
    def _experts():
        e = sb * N_EXPERTS + j - first_expert_step
        n = cnt_ref[e]
        base = off_ref[e]

        def body(i, c):
            r0 = base + i * MOE_CHUNK
            row0 = pl.multiple_of(r0 * PACK_ROWS, 2 * PACK_ROWS)
            halves = [_unpack_bf16_pair(xy_ref[pl.ds(row0 + r, MOE_CHUNK, stride=PACK_ROWS), :])
                      for r in range(PACK_ROWS)]
            a = jnp.zeros((MOE_CHUNK, D_EXPERT), F32)
            u = jnp.zeros((MOE_CHUNK, D_EXPERT), F32)
            for r in range(PACK_ROWS):
                lo, hi = halves[r]
                xb = jnp.concatenate([lo, hi], axis=1).astype(BF16)
                a = a + _dot(xb, wg_ref[0, r * 2 * LANES:(r + 1) * 2 * LANES, :].astype(BF16))
                u = u + _dot(xb, wu_ref[0, r * 2 * LANES:(r + 1) * 2 * LANES, :].astype(BF16))
            hid = ((a * jax.nn.sigmoid(a)) * u).astype(BF16)
            y = _dot(hid, wd_ref[0].astype(BF16))
            valid = (r0 + lax.broadcasted_iota(jnp.int32, (MOE_CHUNK, 1), 0)) < (base + n)
            for r in range(PACK_ROWS):
                c0 = r * 2 * LANES
                lo, hi = halves[r]
                xy_ref[pl.ds(row0 + r, MOE_CHUNK, stride=PACK_ROWS), :] = _pack_bf16_pair(
                    jnp.where(valid, y[:, c0:c0 + LANES], lo), jnp.where(valid, y[:, c0 + LANES:c0 + 2 * LANES], hi))
            return c

        lax.fori_loop(0, (n + MOE_CHUNK - 1) // MOE_CHUNK, body, 0)

    cstep = j - first_combine_step

    @pl.when(cstep == 0)
    def _first_gather():
        gather(bufs[0], 0)

    for parity in range(2):
        @pl.when((cstep >= 1) & (cstep < n_combine) & (cstep % 2 == parity))
        def _gather_and_sum():
            gather(bufs[parity], cstep)
            weighted_sum(bufs[1 - parity])

    @pl.when(cstep == n_combine)
    def _last_sum():
        weighted_sum(bufs[(n_combine - 1) % 2])


def _full(shape):
    return pl.BlockSpec(shape, lambda *_: (0,) * len(shape))


def _prep_weights(w_in, gk_fwd_w, gk_fwd_b, gk_bwd_w, gk_bwd_b, mla_w_qb, mla_w_kvb, q_norm_gain, k_norm_gain):
    splits = np.cumsum([0, GLA_QK_WIDTH, GLA_QK_WIDTH, GLA_WIDTH, GLA_WIDTH, GLA_GATE_RANK, GLA_GATE_RANK,
                        MLA_Q_RANK, MLA_KV_RANK, MLA_ROPE])
    seg = [w_in[:, splits[i]:splits[i + 1]] for i in range(9)]
    d = w_in.shape[0]
    misc = jnp.zeros((d, LANES), F32)
    misc = misc.at[:, 0:GLA_GATE_RANK].set(seg[4])
    misc = misc.at[:, MISC_LRB:MISC_LRB + GLA_GATE_RANK].set(seg[5])
    misc = misc.at[:, ROPE_LO:ROPE_LO + MLA_ROPE].set(seg[8])
    win = jnp.concatenate([seg[0], seg[1], seg[2], seg[3], seg[6], seg[7], misc], axis=1).astype(BF16)

    wg = jnp.zeros((LANES, 2 * GLA_QK_WIDTH), F32)
    wg = wg.at[0:GLA_GATE_RANK, 0:GLA_QK_WIDTH].set(gk_fwd_w)
    wg = wg.at[MISC_LRB:MISC_LRB + GLA_GATE_RANK, GLA_QK_WIDTH:].set(gk_bwd_w)
    bg = jnp.concatenate([gk_fwd_b, gk_bwd_b])[None, :]

    pad = HEAD_TILE - MLA_QK
    wqb = jnp.pad(mla_w_qb.reshape(MLA_Q_RANK, MLA_HEADS, MLA_QK), ((0, 0), (0, 0), (0, pad)))
    wqb = wqb.reshape(MLA_Q_RANK, MLA_HEADS * HEAD_TILE)
    wkv = mla_w_kvb.reshape(MLA_KV_RANK, MLA_HEADS, MLA_NOPE + MLA_V)
    wkvk = jnp.pad(wkv[:, :, :MLA_NOPE], ((0, 0), (0, 0), (0, HEAD_TILE - MLA_NOPE)))
    wkvk = wkvk.reshape(MLA_KV_RANK, MLA_HEADS * HEAD_TILE).astype(BF16)
    wkvv = wkv[:, :, MLA_NOPE:].reshape(MLA_KV_RANK, MLA_WIDTH).T.astype(BF16)
    qng = jnp.pad(q_norm_gain, (0, pad))[None, :]
    kng = jnp.pad(k_norm_gain, (0, pad))[None, :]
    def partner(w, sign):
        first, second = w[..., ROPE_LO:ROPE_LO + ROPE_HALF], w[..., ROPE_LO + ROPE_HALF:ROPE_LO + MLA_ROPE]
        return jnp.concatenate([jnp.zeros_like(w[..., :ROPE_LO]), sign * second, first,
                                jnp.zeros_like(w[..., ROPE_LO + MLA_ROPE:])], axis=-1)

    wqb_rot = partner(wqb.reshape(MLA_Q_RANK, MLA_HEADS, HEAD_TILE), -1.0).reshape(MLA_Q_RANK, -1).astype(BF16)
    qng_rot = partner(qng, 1.0)
    kng_rot = partner(kng, 1.0)
    return win, wg.astype(BF16), bg, wqb.astype(BF16), wqb_rot, wkvk, wkvv, qng, qng_rot, kng, kng_rot


def _rope_consts():
    inv = ROPE_BASE ** (-np.arange(0, MLA_ROPE, 2, dtype=np.float32) / MLA_ROPE)
    one = np.zeros((ROPE_HALF, LANES), np.float32)
    one[np.arange(ROPE_HALF), ROPE_LO + np.arange(ROPE_HALF)] = 1.0
    one[np.arange(ROPE_HALF), ROPE_LO + ROPE_HALF + np.arange(ROPE_HALF)] = 1.0
    zero = np.zeros_like(one)
    place = np.block([[one, zero], [one, zero], [zero, one], [zero, one]])
    ones_bd = np.kron(np.eye(2, dtype=np.float32), np.ones((LANES, LANES), np.float32))
    return jnp.asarray(inv[:, None]), jnp.asarray(place, BF16), jnp.asarray(ones_bd, BF16)


def _layer(x2, pos2, batch, seq, norm1_gain, w_in, gk_fwd_w, gk_fwd_b, gk_bwd_w, gk_bwd_b, gla_out_gain,
           mla_q_gain, mla_w_qb, mla_kv_gain, mla_w_kvb, q_norm_gain, k_norm_gain, w_out, norm2_gain,
           w_router_group, b_router_group, w_router_expert, b_router_expert,
           w_expert_gate, w_expert_up, w_expert_down):
    T, D = x2.shape
    win, wg, bg, wqb, wqb_rot, wkvk, wkvv, qng, qng_rot, kng, kng_rot = _prep_weights(
        w_in, gk_fwd_w, gk_fwd_b, gk_bwd_w, gk_bwd_b, mla_w_qb, mla_w_kvb, q_norm_gain, k_norm_gain)

    TM = min(PROJ_TILE, seq)
    spb = seq // TM
    tok = lambda w: pl.BlockSpec((TM, w), lambda i: (i, 0))
    head_spec = pl.BlockSpec((1, MLA_HEADS, TM, HEAD_TILE), lambda i: (i // spb, 0, i % spb, 0))
    gq, gk, gv, gg, lg, mq, mk, mv = pl.pallas_call(
        _proj_kernel,
        grid=(T // TM,),
        in_specs=[tok(D), pl.BlockSpec((1, 1, TM), lambda i: (i, 0, 0)), _full((1, D)), _full((D, PROJ_WIDTH)),
                  _full((LANES, 2 * GLA_QK_WIDTH)), _full((1, 2 * GLA_QK_WIDTH)), _full((1, MLA_Q_RANK)),
                  _full((MLA_Q_RANK, MLA_HEADS * HEAD_TILE)), _full((MLA_Q_RANK, MLA_HEADS * HEAD_TILE)),
                  _full((1, MLA_KV_RANK)), _full((MLA_KV_RANK, MLA_HEADS * HEAD_TILE)),
                  _full((MLA_WIDTH, MLA_KV_RANK)), _full((1, HEAD_TILE)), _full((1, HEAD_TILE)),
                  _full((1, HEAD_TILE)), _full((1, HEAD_TILE)),
                  _full((ROPE_HALF, 1)), _full((4 * ROPE_HALF, 2 * LANES)), _full((2 * LANES, 2 * LANES))],
        out_specs=[tok(GLA_QK_WIDTH), tok(GLA_QK_WIDTH), tok(GLA_WIDTH), tok(GLA_WIDTH), tok(2 * GLA_QK_WIDTH),
                   head_spec, head_spec, pl.BlockSpec((1, MLA_WIDTH, TM), lambda i: (i // spb, 0, i % spb))],
        out_shape=[jax.ShapeDtypeStruct((T, GLA_QK_WIDTH), BF16), jax.ShapeDtypeStruct((T, GLA_QK_WIDTH), BF16),
                   jax.ShapeDtypeStruct((T, GLA_WIDTH), BF16), jax.ShapeDtypeStruct((T, GLA_WIDTH), BF16),
                   jax.ShapeDtypeStruct((T, 2 * GLA_QK_WIDTH), F32),
                   jax.ShapeDtypeStruct((batch, MLA_HEADS, seq, HEAD_TILE), BF16),
                   jax.ShapeDtypeStruct((batch, MLA_HEADS, seq, HEAD_TILE), BF16),
                   jax.ShapeDtypeStruct((batch, MLA_WIDTH, seq), BF16)],
        compiler_params=pltpu.CompilerParams(dimension_semantics=("parallel",), vmem_limit_bytes=VMEM_LIMIT),
        name="proj",
    )(x2, pos2.reshape(T // TM, 1, TM), norm1_gain[None, :], win, wg, bg, mla_q_gain[None, :], wqb, wqb_rot,
      mla_kv_gain[None, :], wkvk, wkvv, qng, qng_rot, kng, kng_rot, *_rope_consts())

    seqspec = lambda w: pl.BlockSpec((seq, w), lambda b: (b, 0))
    gla_out = pl.pallas_call(
        _gla_kernel,
        grid=(batch,),
        in_specs=[seqspec(GLA_QK_WIDTH), seqspec(GLA_QK_WIDTH), seqspec(GLA_WIDTH), seqspec(2 * GLA_QK_WIDTH),
                  seqspec(GLA_WIDTH), _full((1, GLA_DV))],
        out_specs=seqspec(GLA_WIDTH),
        out_shape=jax.ShapeDtypeStruct((T, GLA_WIDTH), BF16),
        scratch_shapes=[pltpu.VMEM((seq, GLA_WIDTH), F32), pltpu.VMEM((seq, GLA_WIDTH), F32),
                        pltpu.VMEM((GLA_DV, GLA_QK_WIDTH), F32), pltpu.VMEM((GLA_DV, GLA_QK_WIDTH), F32)],
        compiler_params=pltpu.CompilerParams(dimension_semantics=("parallel",), vmem_limit_bytes=VMEM_LIMIT),
        name="gla",
    )(gq, gk, gv, lg, gg, gla_out_gain[None, :])

    TQ = 256
    nq = seq // TQ
    mla_out = pl.pallas_call(
        _attn_kernel,
        grid=(batch, nq),
        in_specs=[pl.BlockSpec((1, MLA_HEADS, TQ, HEAD_TILE), lambda b, i: (b, 0, i, 0)),
                  pl.BlockSpec((1, MLA_HEADS, seq, HEAD_TILE), lambda b, i: (b, 0, 0, 0)),
                  pl.BlockSpec((1, MLA_WIDTH, seq), lambda b, i: (b, 0, 0))],
        out_specs=pl.BlockSpec((1, MLA_WIDTH, TQ), lambda b, i: (b, 0, i)),
        out_shape=jax.ShapeDtypeStruct((batch, MLA_WIDTH, seq), BF16),
        compiler_params=pltpu.CompilerParams(dimension_semantics=("parallel", "parallel"),
                                             vmem_limit_bytes=VMEM_LIMIT),
        name="attn",
    )(mq, mk, mv)

    wr = jnp.zeros((ROUTER_ROWS, D), F32)
    wr = wr.at[0:N_GROUPS].set(w_router_group.T)
    wr = wr.at[ROUTER_EXPERT_ROW:ROUTER_EXPERT_ROW + N_EXPERTS].set(w_router_expert.T)
    br = jnp.zeros((ROUTER_ROWS, 1), F32)
    br = br.at[0:N_GROUPS, 0].set(b_router_group)
    br = br.at[ROUTER_EXPERT_ROW:ROUTER_EXPERT_ROW + N_EXPERTS, 0].set(b_router_expert)
    tb = min(MOE_SUPER_BLOCK, T)
    nsb = T // tb
    TO = min(OUTPROJ_TILE, seq)
    tiles_per_sb = tb // TO
    tok = lambda w: pl.BlockSpec((TO, w), lambda i: (i, 0))
    tri = jnp.asarray(np.triu(np.ones((TO, TO), np.float32)), BF16)
    x1, h2, route, cnt = pl.pallas_call(
        functools.partial(_outproj_kernel, tiles_per_sb),
        grid=(T // TO,),
        in_specs=[tok(D), tok(GLA_WIDTH),
                  pl.BlockSpec((1, MLA_WIDTH, TO), lambda i: (i // (seq // TO), 0, i % (seq // TO))),
                  _full((D, D)), _full((1, D)),
                  _full((ROUTER_ROWS, D)), _full((ROUTER_ROWS, 1)), _full((TO, TO))],
        out_specs=[tok(D), tok(D), pl.BlockSpec((8, TO), lambda i: (0, i)),
                   pl.BlockSpec((1, N_EXPERTS, LANES), lambda i: (i // tiles_per_sb, 0, 0))],
        out_shape=[jax.ShapeDtypeStruct((T, D), F32), jax.ShapeDtypeStruct((T, D), BF16),
                   jax.ShapeDtypeStruct((8, T), F32), jax.ShapeDtypeStruct((nsb, N_EXPERTS, LANES), F32)],
        scratch_shapes=[pltpu.VMEM((N_EXPERTS, 1), F32)],
        compiler_params=pltpu.CompilerParams(dimension_semantics=("arbitrary",), vmem_limit_bytes=VMEM_LIMIT),
        name="outproj",
    )(x2, gla_out, mla_out, w_out.astype(BF16), norm2_gain[None, :], wr.astype(BF16), br, tri)

    strict = jnp.asarray(np.tril(np.ones((N_EXPERTS, N_EXPERTS), np.float32), -1), BF16)
    pos, offs = pl.pallas_call(
        _slots_kernel,
        grid=(nsb,),
        in_specs=[pl.BlockSpec((8, tb), lambda s: (0, s)), pl.BlockSpec((1, N_EXPERTS, LANES), lambda s: (s, 0, 0)),
                  _full((N_EXPERTS, N_EXPERTS))],
        out_specs=[pl.BlockSpec((1, 2, tb), lambda s: (s, 0, 0)),
                   pl.BlockSpec((1, N_EXPERTS, LANES), lambda s: (s, 0, 0))],
        out_shape=[jax.ShapeDtypeStruct((nsb, 2, tb), jnp.int32),
                   jax.ShapeDtypeStruct((nsb, N_EXPERTS, LANES), jnp.int32)],
        compiler_params=pltpu.CompilerParams(dimension_semantics=("parallel",)),
        name="slots",
    )(route, cnt, strict)
    pos = pos.reshape(-1)
    flat_off = offs[:, :, 0].reshape(-1)
    counts = cnt[:, :, 0].astype(jnp.int32)
    wts = jnp.stack([route[2], route[3]], axis=1)
    wge = w_expert_gate.reshape(N_EXPERTS, D, D_EXPERT)
    wue = w_expert_up.reshape(N_EXPERTS, D, D_EXPERT)
    wde = w_expert_down.reshape(N_EXPERTS, D_EXPERT, D)
    nct = tb // MOE_COMBINE_TILE
    nst = tb // MOE_SCATTER_TILE
    slots =(2 * tb + N_EXPERTS + MOE_CHUNK + 7) // 8 * 8
    hidx = lambda sb, j, off, cn: (sb * nst + jnp.minimum(j, nst - 1), 0)
    eidx = lambda sb, j, off, cn: (jnp.clip(j - (nst + 1), 0, N_EXPERTS - 1), 0, 0)
    tidx = lambda sb, j, off, cn: (sb * nct + jnp.clip(j - (nst + 1 + N_EXPERTS) - 1, 0, nct - 1), 0)
    out = pl.pallas_call(
        functools.partial(_moe_kernel, tb),
        grid_spec=pltpu.PrefetchScalarGridSpec(
            num_scalar_prefetch=2,
            grid=(nsb, (nst + 1) + N_EXPERTS + (nct + 1)),
            in_specs=[pl.BlockSpec((MOE_SCATTER_TILE, D), hidx),
                      pl.BlockSpec((2 * tb,), lambda sb, j, off, cn: (sb,), memory_space=pltpu.SMEM),
                      pl.BlockSpec((1, D, D_EXPERT), eidx), pl.BlockSpec((1, D, D_EXPERT), eidx),
                      pl.BlockSpec((1, D_EXPERT, D), eidx),
                      pl.BlockSpec((MOE_COMBINE_TILE, D), tidx), pl.BlockSpec((MOE_COMBINE_TILE, 2), tidx)],
            out_specs=pl.BlockSpec((MOE_COMBINE_TILE, D), tidx),
            scratch_shapes=[pltpu.VMEM((slots * PACK_ROWS, LANES), U32),
                            pltpu.VMEM((MOE_SCATTER_TILE * PACK_ROWS, LANES), U32),
                            pltpu.VMEM((MOE_SCATTER_TILE * PACK_ROWS, LANES), U32)]),
        out_shape=jax.ShapeDtypeStruct((T, D), F32),
        compiler_params=pltpu.CompilerParams(dimension_semantics=("arbitrary", "arbitrary"),
                                             vmem_limit_bytes=VMEM_LIMIT),
        name="moe",
    )(flat_off, counts.reshape(-1), h2, pos, wge, wue, wde, x1, wts)
    return out


def kernel(x, positions, norm1_gain, w_in, gla_gk_fwd_w, gla_gk_fwd_b, gla_gk_bwd_w, gla_gk_bwd_b, gla_out_gain, mla_q_gain, mla_w_qb, mla_kv_gain, mla_w_kvb, q_norm_gain, k_norm_gain, w_out, norm2_gain, w_router_group, b_router_group, w_router_expert, b_router_expert, w_expert_gate, w_expert_up, w_expert_down):
    batch, seq, d = x.shape
    x2 = x.reshape(batch * seq, d)
    pos2 = positions.reshape(batch * seq, 1)
    for l in range(norm1_gain.shape[0]):
        x2 = _layer(x2, pos2, batch, seq, norm1_gain[l], w_in[l], gla_gk_fwd_w[l], gla_gk_fwd_b[l],
                    gla_gk_bwd_w[l], gla_gk_bwd_b[l], gla_out_gain[l], mla_q_gain[l], mla_w_qb[l],
                    mla_kv_gain[l], mla_w_kvb[l], q_norm_gain[l], k_norm_gain[l], w_out[l], norm2_gain[l],
                    w_router_group[l], b_router_group[l], w_router_expert[l], b_router_expert[l],
                    w_expert_gate[l], w_expert_up[l], w_expert_down[l])
    return x2.reshape(batch, seq, d)
```

```python
import functools

import numpy as np
import jax
import jax.numpy as jnp
from jax import lax
from jax.experimental import pallas as pl
from jax.experimental.pallas import tpu as pltpu

F32 = jnp.float32
BF16 = jnp.bfloat16

D_MODEL = 1024
GLA_HEADS = 4
GLA_DK = 64
GLA_DV = 128
GLA_GATE_RANK = 16
GLA_GATE_NORMALIZER = 16.0
GLA_CHUNK = 64
GLA_CHUNK_UNROLL = 8
GLA_FINISH_ROWS = 256
MLA_HEADS = 8
MLA_NOPE = 64
MLA_ROPE = 32
MLA_QK = MLA_NOPE + MLA_ROPE
MLA_V = 64
MLA_Q_RANK = 256
MLA_KV_RANK = 128
ROPE_BASE = 10000.0
GLA_QK_WIDTH = GLA_HEADS * GLA_DK
GLA_WIDTH = GLA_HEADS * GLA_DV
MLA_WIDTH = MLA_HEADS * MLA_V
N_GROUPS = 4
EXPERTS_PER_GROUP = 8
N_EXPERTS = N_GROUPS * EXPERTS_PER_GROUP
D_EXPERT = 256
EPS = 1e-6

LANES = 128
HEAD_TILE = LANES
ROPE_LO = MLA_NOPE
ROPE_HALF = MLA_ROPE // 2
LOG2_E = 1.4426950408889634
ATTN_SCORES_AHEAD = 3
ONES_ROWS = 16

COL_GQ = 0
COL_GK = COL_GQ + GLA_QK_WIDTH
COL_GV = COL_GK + GLA_QK_WIDTH
COL_GG = COL_GV + GLA_WIDTH
COL_MQ = COL_GG + GLA_WIDTH
COL_MKV = COL_MQ + MLA_Q_RANK
COL_MISC = COL_MKV + MLA_KV_RANK
PROJ_WIDTH = COL_MISC + LANES

ROUTER_ROWS = LANES
ROUTER_EXPERT_ROW = 8

U32 = jnp.uint32
HI_HALF_MASK = np.uint32(0xFFFF0000)
PACK_ROWS = D_MODEL // (2 * LANES)
MOE_SUPER_BLOCK = 8192
MOE_CHUNK = 576
MOE_SCATTER_TILE = 1024
MOE_COMBINE_TILE = 512
PROJ_TILE = 1024
OUTPROJ_TILE = 1024
assert 2 * MOE_COMBINE_TILE == MOE_SCATTER_TILE

VMEM_LIMIT = 56 * 1024 * 1024
MOE_VMEM_LIMIT = 59900 * 1024


def _dot(a, b):
    return jnp.dot(a, b, preferred_element_type=F32)


def _dot_nt(a, b):
    return lax.dot_general(a, b, (((1,), (1,)), ((), ())), preferred_element_type=F32)


def _dot_tn(a, b):
    return lax.dot_general(a, b, (((0,), (0,)), ((), ())), preferred_element_type=F32)


def _split_bf16(x):
    hi = x.astype(BF16)
    lo = (x - hi.astype(F32)).astype(BF16)
    return hi, lo


def _proj_kernel(x_ref, pos_ref, n1_ref, win_ref, wg_ref, bg_ref, qgain_ref, wqb_ref, wqbr_ref, kvgain_ref,
                 wkvk_ref, wkvv_ref, qng_ref, qngr_ref, kng_ref, kngr_ref, freq_ref, place_ref, ones_ref,
                 gq_ref, gk_ref, gv_ref, gg_ref, lg_ref, mq_ref, mk_ref, mv_ref):
    x = x_ref[...]
    h = x * lax.rsqrt(jnp.mean(x * x, axis=-1, keepdims=True) + EPS) * n1_ref[...]
    proj = _dot(h.astype(BF16), win_ref[...])

    gq_ref[...] = (proj[:, COL_GQ:COL_GK] * GLA_DK ** -0.5).astype(BF16)
    gk_ref[...] = proj[:, COL_GK:COL_GV].astype(BF16)
    gv_ref[...] = proj[:, COL_GV:COL_GG].astype(BF16)
    gg_ref[...] = proj[:, COL_GG:COL_MQ].astype(BF16)

    misc = proj[:, COL_MISC:PROJ_WIDTH]
    z = _dot(misc.astype(BF16), wg_ref[...]) + bg_ref[...]
    log_sig = -(jnp.maximum(-z, 0.0) + jnp.log(1.0 + jnp.exp(-jnp.abs(z))))
    lg_ref[...] = log_sig / GLA_GATE_NORMALIZER

    lane = lax.broadcasted_iota(jnp.int32, (1, LANES), 1)
    ang_t = freq_ref[...] * pos_ref[0].astype(F32)
    cos_hi, cos_lo = _split_bf16(jnp.cos(ang_t))
    sin_hi, sin_lo = _split_bf16(jnp.sin(ang_t))
    tabs = _dot_tn(jnp.concatenate([cos_hi, cos_lo, sin_hi, sin_lo], axis=0), place_ref[...])
    in_rope = (lane >= ROPE_LO) & (lane < ROPE_LO + MLA_ROPE)
    c_tab = jnp.where(lane < ROPE_LO, 1.0, tabs[:, 0:LANES])
    s_tab = tabs[:, LANES:2 * LANES]
    ones_bd = ones_ref[...]

    def norm_rope_pair(t2, r2, gc, gs):
        ss = _dot((t2 * t2).astype(BF16), ones_bd) * (1.0 / MLA_QK)
        rs = lax.rsqrt(ss + EPS)
        outs = []
        for i in range(2):
            sl = slice(i * LANES, (i + 1) * LANES)
            outs.append(rs[:, sl] * (t2[:, sl] * gc + r2[:, sl] * gs))
        return outs

    qa = proj[:, COL_MQ:COL_MKV]
    qn = qa * lax.rsqrt(jnp.mean(qa * qa, axis=-1, keepdims=True) + EPS) * qgain_ref[...]
    qn = qn.astype(BF16)
    mq = _dot(qn, wqb_ref[...])
    mq_rot = _dot(qn, wqbr_ref[...])
    kva = proj[:, COL_MKV:COL_MISC]
    kvn = (kva * lax.rsqrt(jnp.mean(kva * kva, axis=-1, keepdims=True) + EPS) * kvgain_ref[...]).astype(BF16)
    kn = _dot(kvn, wkvk_ref[...])
    mv_ref[0] = _dot_nt(wkvv_ref[...], kvn).astype(BF16)
    rope_tile = jnp.where(in_rope, misc, 0.0)
    up = (lane >= ROPE_LO + ROPE_HALF) & (lane < ROPE_LO + MLA_ROPE)
    dn = (lane >= ROPE_LO) & (lane < ROPE_LO + ROPE_HALF)
    rope_rot = (jnp.where(up, pltpu.roll(rope_tile, ROPE_HALF, 1), 0.0)
                - jnp.where(dn, pltpu.roll(rope_tile, LANES - ROPE_HALF, 1), 0.0))
    rope2 = jnp.concatenate([rope_tile, rope_tile], axis=1)
    rope_rot2 = jnp.concatenate([rope_rot, rope_rot], axis=1)
    scale = MLA_QK ** -0.5 * LOG2_E
    q_gc, q_gs = (qng_ref[...] * scale) * c_tab, (qngr_ref[...] * scale) * s_tab
    k_gc, k_gs = kng_ref[...] * c_tab, kngr_ref[...] * s_tab
    for hp in range(MLA_HEADS // 2):
        sl = slice(hp * 2 * HEAD_TILE, (hp + 1) * 2 * HEAD_TILE)
        q2 = norm_rope_pair(mq[:, sl], mq_rot[:, sl], q_gc, q_gs)
        k2 = norm_rope_pair(kn[:, sl] + rope2, rope_rot2, k_gc, k_gs)
        for i in range(2):
            mq_ref[0, 2 * hp + i] = q2[i].astype(BF16)
            mk_ref[0, 2 * hp + i] = k2[i].astype(BF16)


def _gla_kernel(q_ref, k_ref, v_ref, lg_ref, gate_ref, gain_ref, o_ref, accf_ref, accb_ref, stf_ref, stb_ref):
    seq = q_ref.shape[0]
    C = GLA_CHUNK
    n_chunks = seq // C
    HK = GLA_QK_WIDTH

    lane_head = lax.broadcasted_iota(jnp.int32, (1, HK), 1) // GLA_DK
    rowi = lax.broadcasted_iota(jnp.int32, (C, 1), 0)
    srow = lax.broadcasted_iota(jnp.int32, (GLA_HEADS * C, C), 0) % C
    scol = lax.broadcasted_iota(jnp.int32, (GLA_HEADS * C, C), 1)
    fwd_cfg = (True, C // 2 - 1, C - 1, scol <= srow, 0)
    bwd_cfg = (False, C // 2, 0, scol > srow, HK)

    def stack_heads(t):
        return jnp.concatenate([jnp.where(lane_head == hd, t, 0.0) for hd in range(GLA_HEADS)], axis=0)

    def chunk_step(c, st, cfg, acc_ref):
        prefix, ref_i, last_i, keep, lg_off = cfg
        rows = pl.ds(pl.multiple_of(c * C, C), C)
        b = lg_ref[rows, lg_off:lg_off + HK]
        shift = 1
        while shift < C:
            if prefix:
                b = b + jnp.where(rowi >= shift, pltpu.roll(b, shift, 0), 0.0)
            else:
                b = b + jnp.where(rowi < C - shift, pltpu.roll(b, C - shift, 0), 0.0)
            shift *= 2
        b_ref = b[ref_i:ref_i + 1, :]
        b_last = b[last_i:last_i + 1, :]
        q = q_ref[rows, :].astype(F32)
        k = k_ref[rows, :].astype(F32)
        v = v_ref[rows, :]
        q_rel = q * jnp.exp(b - b_ref)
        k_rel = (k * jnp.exp(b_ref - b)).astype(BF16)
        k_dec = (k * jnp.exp(b_last - b)).astype(BF16)
        q_dec = q * jnp.exp(b)
        sc = _dot_nt(stack_heads(q_rel).astype(BF16), k_rel)
        sc = jnp.where(keep, sc, 0.0).astype(BF16)
        o_inter = _dot_nt(stack_heads(q_dec).astype(BF16), st.astype(BF16))
        kv_t = _dot_tn(v, k_dec)
        new_st = jnp.exp(b_last) * st
        for hd in range(GLA_HEADS):
            vs = slice(hd * GLA_DV, (hd + 1) * GLA_DV)
            rs = slice(hd * C, (hd + 1) * C)
            acc_ref[rows, vs] = _dot(sc[rs, :], v[:, vs]) + o_inter[rs, :]
            new_st = new_st + jnp.where(lane_head == hd, kv_t[vs, :], 0.0)
        return new_st

    stf_ref[...] = jnp.zeros_like(stf_ref)
    stb_ref[...] = jnp.zeros_like(stb_ref)

    def body(i, carry):
        st_f = stf_ref[...]
        st_b = stb_ref[...]
        for u in range(GLA_CHUNK_UNROLL):
            cf = i * GLA_CHUNK_UNROLL + u
            st_f = chunk_step(cf, st_f, fwd_cfg, accf_ref)
            st_b = chunk_step(n_chunks - 1 - cf, st_b, bwd_cfg, accb_ref)
        stf_ref[...] = st_f
        stb_ref[...] = st_b
        return carry

    lax.fori_loop(0, n_chunks // GLA_CHUNK_UNROLL, body, 0)

    gain = gain_ref[...]

    def finish(r, carry):
        rows = pl.ds(pl.multiple_of(r * GLA_FINISH_ROWS, GLA_FINISH_ROWS), GLA_FINISH_ROWS)
        for hd in range(GLA_HEADS):
            vs = slice(hd * GLA_DV, (hd + 1) * GLA_DV)
            o = accf_ref[rows, vs] + accb_ref[rows, vs]
            on = o * lax.rsqrt(jnp.mean(o * o, axis=-1, keepdims=True) + EPS) * gain
            gt = gate_ref[rows, vs].astype(F32)
            o_ref[rows, vs] = (on * (gt * jax.nn.sigmoid(gt))).astype(BF16)
        return carry

    lax.fori_loop(0, seq // GLA_FINISH_ROWS, finish, 0)


def _attn_kernel(q_ref, k_ref, vt_ref, o_ref):
    seq = k_ref.shape[2]
    ones = jnp.ones((ONES_ROWS, seq), BF16)
    outs = []
    scores = [_dot_nt(q_ref[0, h], k_ref[0, h]) for h in range(ATTN_SCORES_AHEAD)]
    for hd in range(MLA_HEADS):
        if hd + ATTN_SCORES_AHEAD < MLA_HEADS:
            scores.append(_dot_nt(q_ref[0, hd + ATTN_SCORES_AHEAD], k_ref[0, hd + ATTN_SCORES_AHEAD]))
        s = scores[hd]
        p = jnp.exp2(s - jnp.max(s, axis=-1, keepdims=True)).astype(BF16)
        vt_aug = jnp.concatenate([vt_ref[0, hd * MLA_V:(hd + 1) * MLA_V, :], ones], axis=0)
        ot = _dot_nt(vt_aug, p)
        outs.append(ot[0:MLA_V, :] / ot[MLA_V:MLA_V + 1, :])
    o_ref[0] = jnp.concatenate(outs, axis=0).astype(BF16)


def _unpack_bf16_pair(w):
    lo = pltpu.unpack_elementwise(w, index=0, packed_dtype=BF16, unpacked_dtype=F32)
    hi = pltpu.unpack_elementwise(w, index=1, packed_dtype=BF16, unpacked_dtype=F32)
    return lo, hi


def _pack_bf16_pair(lo, hi):
    return pltpu.pack_elementwise([lo, hi], packed_dtype=BF16)


def _outproj_kernel(tiles_per_sb, x_ref, gla_ref, mla_ref, wo_ref, n2_ref, wr_ref, br_ref, tri_ref,
                    x1_ref, h2_ref, route_ref, cnt_ref, carry_ref):
    mix = _dot(gla_ref[...], wo_ref[0:GLA_WIDTH, :]) + _dot_tn(mla_ref[0], wo_ref[GLA_WIDTH:, :])
    x1 = x_ref[...] + mix
    x1_ref[...] = x1
    h2 = x1 * lax.rsqrt(jnp.mean(x1 * x1, axis=-1, keepdims=True) + EPS) * n2_ref[...]
    h2b = h2.astype(BF16)
    h2_ref[...] = h2b

    logits = _dot_nt(wr_ref[...], h2b) + br_ref[...]
    tm = logits.shape[1]
    gl = logits[0:N_GROUPS, :]
    ge = jnp.exp(gl - jnp.max(gl, axis=0, keepdims=True))
    pg = ge / jnp.sum(ge, axis=0, keepdims=True)
    p_top = jnp.max(pg, axis=0, keepdims=True)
    gi = lax.broadcasted_iota(jnp.int32, (N_GROUPS, tm), 0)
    g_idx = jnp.min(jnp.where(pg == p_top, gi, N_GROUPS), axis=0, keepdims=True)
    sel = jnp.zeros((EXPERTS_PER_GROUP, tm), F32)
    for g in range(N_GROUPS):
        r0 = ROUTER_EXPERT_ROW + g * EXPERTS_PER_GROUP
        sel = sel + jnp.where(g_idx == g, logits[r0:r0 + EXPERTS_PER_GROUP, :], 0.0)
    se = jnp.exp(sel - jnp.max(sel, axis=0, keepdims=True))
    pe = se / jnp.sum(se, axis=0, keepdims=True)
    ei = lax.broadcasted_iota(jnp.int32, (EXPERTS_PER_GROUP, tm), 0)
    m1 = jnp.max(pe, axis=0, keepdims=True)
    i1 = jnp.min(jnp.where(pe == m1, ei, EXPERTS_PER_GROUP), axis=0, keepdims=True)
    pe2 = jnp.where(ei == i1, -1.0, pe)
    m2 = jnp.max(pe2, axis=0, keepdims=True)
    i2 = jnp.min(jnp.where(pe2 == m2, ei, EXPERTS_PER_GROUP), axis=0, keepdims=True)
    den = m1 + m2
    e1 = g_idx * EXPERTS_PER_GROUP + i1
    e2 = g_idx * EXPERTS_PER_GROUP + i2

    @pl.when(pl.program_id(0) % tiles_per_sb == 0)
    def _():
        carry_ref[...] = jnp.zeros_like(carry_ref)

    eall = lax.broadcasted_iota(jnp.int32, (N_EXPERTS, tm), 0)
    tri = tri_ref[...]
    carry = carry_ref[...]
    ranks = []
    for eid in (e1, e2):
        hot = eall == eid
        prefix = _dot(hot.astype(BF16), tri)
        ranks.append(jnp.sum(jnp.where(hot, carry + prefix - 1.0, 0.0), axis=0, keepdims=True))
        carry = carry + prefix[:, tm - 1:tm]
    carry_ref[...] = carry
    cnt_ref[0] = jnp.broadcast_to(carry, (N_EXPERTS, LANES))
    zeros = jnp.zeros((2, tm), F32)
    route_ref[...] = jnp.concatenate([e1.astype(F32), e2.astype(F32), p_top * (m1 / den), p_top * (m2 / den),
                                      ranks[0], ranks[1], zeros], axis=0)


def _slots_kernel(route_ref, cnt_ref, strict_ref, pos_ref, off_ref):
    counts = cnt_ref[0]
    padded = 2.0 * jnp.floor(counts * 0.5 + 0.5)
    hi = jnp.floor(padded * (1.0 / 256.0))
    lo = padded - 256.0 * hi
    strict = strict_ref[...]
    off = 256.0 * _dot(strict, hi.astype(BF16)) + _dot(strict, lo.astype(BF16))
    off_ref[0] = off.astype(jnp.int32)
    tb = route_ref.shape[1]
    eall = lax.broadcasted_iota(jnp.int32, (N_EXPERTS, tb), 0)
    off_col = off[:, 0:1]
    for s in range(2):
        hot = eall == route_ref[s:s + 1, :].astype(jnp.int32)
        base = jnp.sum(jnp.where(hot, off_col, 0.0), axis=0, keepdims=True)
        pos_ref[0, s:s + 1, :] = ((base + route_ref[4 + s:5 + s, :]) * PACK_ROWS).astype(jnp.int32)


def _moe_kernel(tb, off_ref, cnt_ref, h2_ref, pos_ref, wg_ref, wu_ref, wd_ref, x1_ref, wt_ref, o_ref,
                xy_ref, cba_ref, cbb_ref):
    sb = pl.program_id(0)
    j = pl.program_id(1)
    n_scatter = tb // MOE_SCATTER_TILE
    n_combine = tb // MOE_COMBINE_TILE
    first_expert_step = n_scatter + 1
    first_combine_step = first_expert_step + N_EXPERTS

    def pack(dst_ref):
        h = h2_ref[...].astype(F32)
        for r in range(PACK_ROWS):
            c0 = r * 2 * LANES
            dst_ref[pl.ds(r, MOE_SCATTER_TILE, stride=PACK_ROWS), :] = _pack_bf16_pair(
                h[:, c0:c0 + LANES], h[:, c0 + LANES:c0 + 2 * LANES])

    def scatter(src_ref, tile):
        t0 = tile * MOE_SCATTER_TILE
        for pair in range(MOE_SCATTER_TILE // 2):
            v = src_ref[pair * 2 * PACK_ROWS:(pair + 1) * 2 * PACK_ROWS, :]
            for half in range(2):
                piece = v[half * PACK_ROWS:(half + 1) * PACK_ROWS, :]
                for slot in range(2):
                    p = pos_ref[slot * tb + t0 + pair * 2 + half]
                    xy_ref[pl.ds(pl.multiple_of(p, PACK_ROWS), PACK_ROWS), :] = piece

    def gather(dst_ref, tile):
        t0 = tile * MOE_COMBINE_TILE
        for t in range(MOE_COMBINE_TILE):
            for slot in range(2):
                p = pos_ref[slot * tb + t0 + t]
                r0 = (slot * MOE_COMBINE_TILE + t) * PACK_ROWS
                dst_ref[r0:r0 + PACK_ROWS, :] = xy_ref[pl.ds(pl.multiple_of(p, PACK_ROWS), PACK_ROWS), :]

    def weighted_sum(src_ref):
        w1 = wt_ref[:, 0:1]
        w2 = wt_ref[:, 1:2]
        second = MOE_COMBINE_TILE * PACK_ROWS
        for r in range(PACK_ROWS):
            lo1, hi1 = _unpack_bf16_pair(src_ref[pl.ds(r, MOE_COMBINE_TILE, stride=PACK_ROWS), :])
            lo2, hi2 = _unpack_bf16_pair(src_ref[pl.ds(second + r, MOE_COMBINE_TILE, stride=PACK_ROWS), :])
            c0 = r * 2 * LANES
            o_ref[:, c0:c0 + LANES] = x1_ref[:, c0:c0 + LANES] + w1 * lo1 + w2 * lo2
            o_ref[:, c0 + LANES:c0 + 2 * LANES] = x1_ref[:, c0 + LANES:c0 + 2 * LANES] + w1 * hi1 + w2 * hi2

    bufs = (cba_ref, cbb_ref)

    @pl.when(j == 0)
    def _first_pack():
        xy_ref[...] = jnp.zeros_like(xy_ref)
        pack(bufs[0])

    for parity in range(2):
        @pl.when((j >= 1) & (j < n_scatter) & (j % 2 == parity))
        def _scatter_and_pack():
            pack(bufs[parity])
            scatter(bufs[1 - parity], j - 1)

    @pl.when(j == n_scatter)
    def _last_scatter():
        scatter(bufs[(n_scatter - 1) % 2], n_scatter - 1)

    @pl.when((j >= first_expert_step) & (j < first_combine_step))
    def _experts():
        e = sb * N_EXPERTS + j - first_expert_step
        n = cnt_ref[e]
        base = off_ref[e]

        def body(i, c):
            r0 = base + i * MOE_CHUNK
            row0 = pl.multiple_of(r0 * PACK_ROWS, 2 * PACK_ROWS)
            halves = [_unpack_bf16_pair(xy_ref[pl.ds(row0 + r, MOE_CHUNK, stride=PACK_ROWS), :])
                      for r in range(PACK_ROWS)]
            a = jnp.zeros((MOE_CHUNK, D_EXPERT), F32)
            u = jnp.zeros((MOE_CHUNK, D_EXPERT), F32)
            for r in range(PACK_ROWS):
                lo, hi = halves[r]
                xb = jnp.concatenate([lo, hi], axis=1).astype(BF16)
                a = a + _dot(xb, wg_ref[0, r * 2 * LANES:(r + 1) * 2 * LANES, :].astype(BF16))
                u = u + _dot(xb, wu_ref[0, r * 2 * LANES:(r + 1) * 2 * LANES, :].astype(BF16))
            hid = ((a * jax.nn.sigmoid(a)) * u).astype(BF16)
            y = _dot(hid, wd_ref[0].astype(BF16))
            valid = (r0 + lax.broadcasted_iota(jnp.int32, (MOE_CHUNK, 1), 0)) < (base + n)
            for r in range(PACK_ROWS):
                c0 = r * 2 * LANES
                lo, hi = halves[r]
                xy_ref[pl.ds(row0 + r, MOE_CHUNK, stride=PACK_ROWS), :] = _pack_bf16_pair(
                    jnp.where(valid, y[:, c0:c0 + LANES], lo), jnp.where(valid, y[:, c0 + LANES:c0 + 2 * LANES], hi))
            return c

        lax.fori_loop(0, (n + MOE_CHUNK - 1) // MOE_CHUNK, body, 0)

    cstep = j - first_combine_step

    @pl.when(cstep == 0)
    def _first_gather():
        gather(bufs[0], 0)

    for parity in range(2):
        @pl.when((cstep >= 1) & (cstep < n_combine) & (cstep % 2 == parity))
        def _gather_and_sum():
            gather(bufs[parity], cstep)
            weighted_sum(bufs[1 - parity])

    @pl.when(cstep == n_combine)
    def _last_sum():
        weighted_sum(bufs[(n_combine - 1) % 2])


def _full(shape):
    return pl.BlockSpec(shape, lambda *_: (0,) * len(shape))


def _prep_weights(w_in, gk_fwd_w, gk_fwd_b, gk_bwd_w, gk_bwd_b, mla_w_qb, mla_w_kvb, q_norm_gain, k_norm_gain):
    splits = np.cumsum([0, GLA_QK_WIDTH, GLA_QK_WIDTH, GLA_WIDTH, GLA_WIDTH, GLA_GATE_RANK, GLA_GATE_RANK,
                        MLA_Q_RANK, MLA_KV_RANK, MLA_ROPE])
    wb = w_in.astype(BF16)
    d = w_in.shape[0]
    gap = jnp.zeros((d, ROPE_LO - 2 * GLA_GATE_RANK), BF16)
    tail = jnp.zeros((d, LANES - ROPE_LO - MLA_ROPE), BF16)
    win = jnp.concatenate([wb[:, :splits[4]], wb[:, splits[6]:splits[8]], wb[:, splits[4]:splits[6]], gap,
                           wb[:, splits[8]:splits[9]], tail], axis=1)

    zq = jnp.zeros((GLA_GATE_RANK, GLA_QK_WIDTH), F32)
    wg = jnp.concatenate([jnp.concatenate([gk_fwd_w, zq], axis=1), jnp.concatenate([zq, gk_bwd_w], axis=1),
                          jnp.zeros((LANES - 2 * GLA_GATE_RANK, 2 * GLA_QK_WIDTH), F32)], axis=0)
    bg = jnp.concatenate([gk_fwd_b, gk_bwd_b])[None, :]

    pad = HEAD_TILE - MLA_QK
    wqb = jnp.pad(mla_w_qb.reshape(MLA_Q_RANK, MLA_HEADS, MLA_QK), ((0, 0), (0, 0), (0, pad)))
    wqb = wqb.reshape(MLA_Q_RANK, MLA_HEADS * HEAD_TILE)
    wkv = mla_w_kvb.reshape(MLA_KV_RANK, MLA_HEADS, MLA_NOPE + MLA_V)
    wkvk = jnp.pad(wkv[:, :, :MLA_NOPE], ((0, 0), (0, 0), (0, HEAD_TILE - MLA_NOPE)))
    wkvk = wkvk.reshape(MLA_KV_RANK, MLA_HEADS * HEAD_TILE).astype(BF16)
    wkvv = wkv[:, :, MLA_NOPE:].reshape(MLA_KV_RANK, MLA_WIDTH).T.astype(BF16)
    qng = jnp.pad(q_norm_gain, (0, pad))[None, :]
    kng = jnp.pad(k_norm_gain, (0, pad))[None, :]
    def partner(w, sign):
        first, second = w[..., ROPE_LO:ROPE_LO + ROPE_HALF], w[..., ROPE_LO + ROPE_HALF:ROPE_LO + MLA_ROPE]
        return jnp.concatenate([jnp.zeros_like(w[..., :ROPE_LO]), sign * second, first,
                                jnp.zeros_like(w[..., ROPE_LO + MLA_ROPE:])], axis=-1)

    wqb_rot = partner(wqb.reshape(MLA_Q_RANK, MLA_HEADS, HEAD_TILE), -1.0).reshape(MLA_Q_RANK, -1).astype(BF16)
    qng_rot = partner(qng, 1.0)
    kng_rot = partner(kng, 1.0)
    return win, wg.astype(BF16), bg, wqb.astype(BF16), wqb_rot, wkvk, wkvv, qng, qng_rot, kng, kng_rot


def _rope_consts():
    inv = ROPE_BASE ** (-np.arange(0, MLA_ROPE, 2, dtype=np.float32) / MLA_ROPE)
    one = np.zeros((ROPE_HALF, LANES), np.float32)
    one[np.arange(ROPE_HALF), ROPE_LO + np.arange(ROPE_HALF)] = 1.0
    one[np.arange(ROPE_HALF), ROPE_LO + ROPE_HALF + np.arange(ROPE_HALF)] = 1.0
    zero = np.zeros_like(one)
    place = np.block([[one, zero], [one, zero], [zero, one], [zero, one]])
    ones_bd = np.kron(np.eye(2, dtype=np.float32), np.ones((LANES, LANES), np.float32))
    return jnp.asarray(inv[:, None]), jnp.asarray(place, BF16), jnp.asarray(ones_bd, BF16)


def _layer(x2, pos2, batch, seq, norm1_gain, w_in, gk_fwd_w, gk_fwd_b, gk_bwd_w, gk_bwd_b, gla_out_gain,
           mla_q_gain, mla_w_qb, mla_kv_gain, mla_w_kvb, q_norm_gain, k_norm_gain, w_out, norm2_gain,
           w_router_group, b_router_group, w_router_expert, b_router_expert,
           w_expert_gate, w_expert_up, w_expert_down):
    T, D = x2.shape
    win, wg, bg, wqb, wqb_rot, wkvk, wkvv, qng, qng_rot, kng, kng_rot = _prep_weights(
        w_in, gk_fwd_w, gk_fwd_b, gk_bwd_w, gk_bwd_b, mla_w_qb, mla_w_kvb, q_norm_gain, k_norm_gain)

    TM = min(PROJ_TILE, seq)
    spb = seq // TM
    tok = lambda w: pl.BlockSpec((TM, w), lambda i: (i, 0))
    head_spec = pl.BlockSpec((1, MLA_HEADS, TM, HEAD_TILE), lambda i: (i // spb, 0, i % spb, 0))
    gq, gk, gv, gg, lg, mq, mk, mv = pl.pallas_call(
        _proj_kernel,
        grid=(T // TM,),
        in_specs=[tok(D), pl.BlockSpec((1, 1, TM), lambda i: (i, 0, 0)), _full((1, D)), _full((D, PROJ_WIDTH)),
                  _full((LANES, 2 * GLA_QK_WIDTH)), _full((1, 2 * GLA_QK_WIDTH)), _full((1, MLA_Q_RANK)),
                  _full((MLA_Q_RANK, MLA_HEADS * HEAD_TILE)), _full((MLA_Q_RANK, MLA_HEADS * HEAD_TILE)),
                  _full((1, MLA_KV_RANK)), _full((MLA_KV_RANK, MLA_HEADS * HEAD_TILE)),
                  _full((MLA_WIDTH, MLA_KV_RANK)), _full((1, HEAD_TILE)), _full((1, HEAD_TILE)),
                  _full((1, HEAD_TILE)), _full((1, HEAD_TILE)),
                  _full((ROPE_HALF, 1)), _full((4 * ROPE_HALF, 2 * LANES)), _full((2 * LANES, 2 * LANES))],
        out_specs=[tok(GLA_QK_WIDTH), tok(GLA_QK_WIDTH), tok(GLA_WIDTH), tok(GLA_WIDTH), tok(2 * GLA_QK_WIDTH),
                   head_spec, head_spec, pl.BlockSpec((1, MLA_WIDTH, TM), lambda i: (i // spb, 0, i % spb))],
        out_shape=[jax.ShapeDtypeStruct((T, GLA_QK_WIDTH), BF16), jax.ShapeDtypeStruct((T, GLA_QK_WIDTH), BF16),
                   jax.ShapeDtypeStruct((T, GLA_WIDTH), BF16), jax.ShapeDtypeStruct((T, GLA_WIDTH), BF16),
                   jax.ShapeDtypeStruct((T, 2 * GLA_QK_WIDTH), F32),
                   jax.ShapeDtypeStruct((batch, MLA_HEADS, seq, HEAD_TILE), BF16),
                   jax.ShapeDtypeStruct((batch, MLA_HEADS, seq, HEAD_TILE), BF16),
                   jax.ShapeDtypeStruct((batch, MLA_WIDTH, seq), BF16)],
        compiler_params=pltpu.CompilerParams(dimension_semantics=("parallel",), vmem_limit_bytes=VMEM_LIMIT),
        name="proj",
    )(x2, pos2.reshape(T // TM, 1, TM), norm1_gain[None, :], win, wg, bg, mla_q_gain[None, :], wqb, wqb_rot,
      mla_kv_gain[None, :], wkvk, wkvv, qng, qng_rot, kng, kng_rot, *_rope_consts())

    seqspec = lambda w: pl.BlockSpec((seq, w), lambda b: (b, 0))
    gla_out = pl.pallas_call(
        _gla_kernel,
        grid=(batch,),
        in_specs=[seqspec(GLA_QK_WIDTH), seqspec(GLA_QK_WIDTH), seqspec(GLA_WIDTH), seqspec(2 * GLA_QK_WIDTH),
                  seqspec(GLA_WIDTH), _full((1, GLA_DV))],
        out_specs=seqspec(GLA_WIDTH),
        out_shape=jax.ShapeDtypeStruct((T, GLA_WIDTH), BF16),
        scratch_shapes=[pltpu.VMEM((seq, GLA_WIDTH), F32), pltpu.VMEM((seq, GLA_WIDTH), F32),
                        pltpu.VMEM((GLA_DV, GLA_QK_WIDTH), F32), pltpu.VMEM((GLA_DV, GLA_QK_WIDTH), F32)],
        compiler_params=pltpu.CompilerParams(dimension_semantics=("parallel",), vmem_limit_bytes=VMEM_LIMIT),
        name="gla",
    )(gq, gk, gv, lg, gg, gla_out_gain[None, :])

    TQ = 256
    nq = seq // TQ
    mla_out = pl.pallas_call(
        _attn_kernel,
        grid=(batch, nq),
        in_specs=[pl.BlockSpec((1, MLA_HEADS, TQ, HEAD_TILE), lambda b, i: (b, 0, i, 0)),
                  pl.BlockSpec((1, MLA_HEADS, seq, HEAD_TILE), lambda b, i: (b, 0, 0, 0)),
                  pl.BlockSpec((1, MLA_WIDTH, seq), lambda b, i: (b, 0, 0))],
        out_specs=pl.BlockSpec((1, MLA_WIDTH, TQ), lambda b, i: (b, 0, i)),
        out_shape=jax.ShapeDtypeStruct((batch, MLA_WIDTH, seq), BF16),
        compiler_params=pltpu.CompilerParams(dimension_semantics=("parallel", "parallel"),
                                             vmem_limit_bytes=VMEM_LIMIT),
        name="attn",
    )(mq, mk, mv)

    row_gap = ROUTER_EXPERT_ROW - N_GROUPS
    row_tail = ROUTER_ROWS - ROUTER_EXPERT_ROW - N_EXPERTS
    wr = jnp.concatenate([w_router_group, jnp.zeros((D, row_gap), F32), w_router_expert,
                          jnp.zeros((D, row_tail), F32)], axis=1).T
    br = jnp.concatenate([b_router_group, jnp.zeros((row_gap,), F32), b_router_expert,
                          jnp.zeros((row_tail,), F32)])[:, None]
    tb = min(MOE_SUPER_BLOCK, T)
    nsb = T // tb
    TO = min(OUTPROJ_TILE, seq)
    tiles_per_sb = tb // TO
    tok = lambda w: pl.BlockSpec((TO, w), lambda i: (i, 0))
    tri = jnp.asarray(np.triu(np.ones((TO, TO), np.float32)), BF16)
    x1, h2, route, cnt = pl.pallas_call(
        functools.partial(_outproj_kernel, tiles_per_sb),
        grid=(T // TO,),
        in_specs=[tok(D), tok(GLA_WIDTH),
                  pl.BlockSpec((1, MLA_WIDTH, TO), lambda i: (i // (seq // TO), 0, i % (seq // TO))),
                  _full((D, D)), _full((1, D)),
                  _full((ROUTER_ROWS, D)), _full((ROUTER_ROWS, 1)), _full((TO, TO))],
        out_specs=[tok(D), tok(D), pl.BlockSpec((8, TO), lambda i: (0, i)),
                   pl.BlockSpec((1, N_EXPERTS, LANES), lambda i: (i // tiles_per_sb, 0, 0))],
        out_shape=[jax.ShapeDtypeStruct((T, D), F32), jax.ShapeDtypeStruct((T, D), BF16),
                   jax.ShapeDtypeStruct((8, T), F32), jax.ShapeDtypeStruct((nsb, N_EXPERTS, LANES), F32)],
        scratch_shapes=[pltpu.VMEM((N_EXPERTS, 1), F32)],
        compiler_params=pltpu.CompilerParams(dimension_semantics=("arbitrary",), vmem_limit_bytes=VMEM_LIMIT),
        name="outproj",
    )(x2, gla_out, mla_out, w_out.astype(BF16), norm2_gain[None, :], wr.astype(BF16), br, tri)

    strict = jnp.asarray(np.tril(np.ones((N_EXPERTS, N_EXPERTS), np.float32), -1), BF16)
    pos, offs = pl.pallas_call(
        _slots_kernel,
        grid=(nsb,),
        in_specs=[pl.BlockSpec((8, tb), lambda s: (0, s)), pl.BlockSpec((1, N_EXPERTS, LANES), lambda s: (s, 0, 0)),
                  _full((N_EXPERTS, N_EXPERTS))],
        out_specs=[pl.BlockSpec((1, 2, tb), lambda s: (s, 0, 0)),
                   pl.BlockSpec((1, N_EXPERTS, LANES), lambda s: (s, 0, 0))],
        out_shape=[jax.ShapeDtypeStruct((nsb, 2, tb), jnp.int32),
                   jax.ShapeDtypeStruct((nsb, N_EXPERTS, LANES), jnp.int32)],
        compiler_params=pltpu.CompilerParams(dimension_semantics=("parallel",)),
        name="slots",
    )(route, cnt, strict)
    pos = pos.reshape(-1)
    flat_off = offs[:, :, 0].reshape(-1)
    counts = cnt[:, :, 0].astype(jnp.int32)
    wts = jnp.stack([route[2], route[3]], axis=1)
    wge = w_expert_gate.reshape(N_EXPERTS, D, D_EXPERT)
    wue = w_expert_up.reshape(N_EXPERTS, D, D_EXPERT)
    wde = w_expert_down.reshape(N_EXPERTS, D_EXPERT, D)
    nct = tb // MOE_COMBINE_TILE
    nst = tb // MOE_SCATTER_TILE
    slots =(2 * tb + N_EXPERTS + MOE_CHUNK + 7) // 8 * 8
    hidx = lambda sb, j, off, cn: (sb * nst + jnp.minimum(j, nst - 1), 0)
    eidx = lambda sb, j, off, cn: (jnp.clip(j - (nst + 1), 0, N_EXPERTS - 1), 0, 0)
    tidx = lambda sb, j, off, cn: (sb * nct + jnp.clip(j - (nst + 1 + N_EXPERTS) - 1, 0, nct - 1), 0)
    out = pl.pallas_call(
        functools.partial(_moe_kernel, tb),
        grid_spec=pltpu.PrefetchScalarGridSpec(
            num_scalar_prefetch=2,
            grid=(nsb, (nst + 1) + N_EXPERTS + (nct + 1)),
            in_specs=[pl.BlockSpec((MOE_SCATTER_TILE, D), hidx),
                      pl.BlockSpec((2 * tb,), lambda sb, j, off, cn: (sb,), memory_space=pltpu.SMEM),
                      pl.BlockSpec((1, D, D_EXPERT), eidx), pl.BlockSpec((1, D, D_EXPERT), eidx),
                      pl.BlockSpec((1, D_EXPERT, D), eidx),
                      pl.BlockSpec((MOE_COMBINE_TILE, D), tidx), pl.BlockSpec((MOE_COMBINE_TILE, 2), tidx)],
            out_specs=pl.BlockSpec((MOE_COMBINE_TILE, D), tidx),
            scratch_shapes=[pltpu.VMEM((slots * PACK_ROWS, LANES), U32),
                            pltpu.VMEM((MOE_SCATTER_TILE * PACK_ROWS, LANES), U32),
                            pltpu.VMEM((MOE_SCATTER_TILE * PACK_ROWS, LANES), U32)]),
        out_shape=jax.ShapeDtypeStruct((T, D), F32),
        compiler_params=pltpu.CompilerParams(dimension_semantics=("arbitrary", "arbitrary"),
                                             vmem_limit_bytes=MOE_VMEM_LIMIT),
        name="moe",
    )(flat_off, counts.reshape(-1), h2, pos, wge, wue, wde, x1, wts)
    return out


def kernel(x, positions, norm1_gain, w_in, gla_gk_fwd_w, gla_gk_fwd_b, gla_gk_bwd_w, gla_gk_bwd_b, gla_out_gain, mla_q_gain, mla_w_qb, mla_kv_gain, mla_w_kvb, q_norm_gain, k_norm_gain, w_out, norm2_gain, w_router_group, b_router_group, w_router_expert, b_router_expert, w_expert_gate, w_expert_up, w_expert_down):
    batch, seq, d = x.shape
    x2 = x.reshape(batch * seq, d)
    pos2 = positions.reshape(batch * seq, 1)
    for l in range(norm1_gain.shape[0]):
        x2 = _layer(x2, pos2, batch, seq, norm1_gain[l], w_in[l], gla_gk_fwd_w[l], gla_gk_fwd_b[l],
                    gla_gk_bwd_w[l], gla_gk_bwd_b[l], gla_out_gain[l], mla_q_gain[l], mla_w_qb[l],
                    mla_kv_gain[l], mla_w_kvb[l], q_norm_gain[l], k_norm_gain[l], w_out[l], norm2_gain[l],
                    w_router_group[l], b_router_group[l], w_router_expert[l], b_router_expert[l],
                    w_expert_gate[l], w_expert_up[l], w_expert_down[l])
    return x2.reshape(batch, seq, d)
```

```python
import functools

import numpy as np
import jax
import jax.numpy as jnp
from jax import lax
from jax.experimental import pallas as pl
from jax.experimental.pallas import tpu as pltpu

F32 = jnp.float32
BF16 = jnp.bfloat16

D_MODEL = 1024
GLA_HEADS = 4
GLA_DK = 64
GLA_DV = 128
GLA_GATE_RANK = 16
GLA_GATE_NORMALIZER = 16.0
GLA_CHUNK = 64
GLA_CHUNK_UNROLL = 8
MLA_HEADS = 8
MLA_NOPE = 64
MLA_ROPE = 32
MLA_QK = MLA_NOPE + MLA_ROPE
MLA_V = 64
MLA_Q_RANK = 256
MLA_KV_RANK = 128
ROPE_BASE = 10000.0
GLA_QK_WIDTH = GLA_HEADS * GLA_DK
GLA_WIDTH = GLA_HEADS * GLA_DV
MLA_WIDTH = MLA_HEADS * MLA_V
N_GROUPS = 4
EXPERTS_PER_GROUP = 8
N_EXPERTS = N_GROUPS * EXPERTS_PER_GROUP
D_EXPERT = 256
EPS = 1e-6

LANES = 128
HEAD_TILE = LANES
ROPE_LO = MLA_NOPE
ROPE_HALF = MLA_ROPE // 2
LOG2_E = 1.4426950408889634
ATTN_SCORES_AHEAD = 3
ONES_ROWS = 16

COL_GQ = 0
COL_GK = COL_GQ + GLA_QK_WIDTH
COL_GV = COL_GK + GLA_QK_WIDTH
COL_GG = COL_GV + GLA_WIDTH
COL_MQ = COL_GG + GLA_WIDTH
COL_MKV = COL_MQ + MLA_Q_RANK
COL_MISC = COL_MKV + MLA_KV_RANK
PROJ_WIDTH = COL_MISC + LANES

ROUTER_ROWS = LANES
ROUTER_EXPERT_ROW = 8

U32 = jnp.uint32
HI_HALF_MASK = np.uint32(0xFFFF0000)
PACK_ROWS = D_MODEL // (2 * LANES)
MOE_SUPER_BLOCK = 8192
MOE_CHUNK = 576
MOE_SCATTER_TILE = 1024
MOE_COMBINE_TILE = 512
PROJ_TILE = 1024
OUTPROJ_TILE = 1024
assert 2 * MOE_COMBINE_TILE == MOE_SCATTER_TILE

VMEM_LIMIT = 56 * 1024 * 1024
MOE_VMEM_LIMIT = 59900 * 1024


def _dot(a, b):
    return jnp.dot(a, b, preferred_element_type=F32)


def _dot_nt(a, b):
    return lax.dot_general(a, b, (((1,), (1,)), ((), ())), preferred_element_type=F32)


def _dot_tn(a, b):
    return lax.dot_general(a, b, (((0,), (0,)), ((), ())), preferred_element_type=F32)


def _split_bf16(x):
    hi = x.astype(BF16)
    lo = (x - hi.astype(F32)).astype(BF16)
    return hi, lo


def _proj_kernel(x_ref, pos_ref, n1_ref, win_ref, wg_ref, bg_ref, qgain_ref, wqb_ref, wqbr_ref, kvgain_ref,
                 wkvk_ref, wkvv_ref, qng_ref, qngr_ref, kng_ref, kngr_ref, freq_ref, place_ref, ones_ref,
                 gq_ref, gk_ref, gv_ref, gg_ref, lg_ref, mq_ref, mk_ref, mv_ref):
    x = x_ref[...]
    h = x * lax.rsqrt(jnp.mean(x * x, axis=-1, keepdims=True) + EPS) * n1_ref[...]
    proj = _dot(h.astype(BF16), win_ref[...])

    gq_ref[...] = (proj[:, COL_GQ:COL_GK] * GLA_DK ** -0.5).astype(BF16)
    gk_ref[...] = proj[:, COL_GK:COL_GV].astype(BF16)
    gv_ref[...] = proj[:, COL_GV:COL_GG].astype(BF16)
    gg_ref[...] = proj[:, COL_GG:COL_MQ].astype(BF16)

    misc = proj[:, COL_MISC:PROJ_WIDTH]
    z = _dot(misc.astype(BF16), wg_ref[...]) + bg_ref[...]
    log_sig = -(jnp.maximum(-z, 0.0) + jnp.log(1.0 + jnp.exp(-jnp.abs(z))))
    lg_ref[...] = log_sig / GLA_GATE_NORMALIZER

    lane = lax.broadcasted_iota(jnp.int32, (1, LANES), 1)
    ang_t = freq_ref[...] * pos_ref[0].astype(F32)
    cos_hi, cos_lo = _split_bf16(jnp.cos(ang_t))
    sin_hi, sin_lo = _split_bf16(jnp.sin(ang_t))
    tabs = _dot_tn(jnp.concatenate([cos_hi, cos_lo, sin_hi, sin_lo], axis=0), place_ref[...])
    in_rope = (lane >= ROPE_LO) & (lane < ROPE_LO + MLA_ROPE)
    c_tab = jnp.where(lane < ROPE_LO, 1.0, tabs[:, 0:LANES])
    s_tab = tabs[:, LANES:2 * LANES]
    ones_bd = ones_ref[...]

    def norm_rope_pair(t2, r2, gc, gs):
        ss = _dot((t2 * t2).astype(BF16), ones_bd) * (1.0 / MLA_QK)
        rs = lax.rsqrt(ss + EPS)
        outs = []
        for i in range(2):
            sl = slice(i * LANES, (i + 1) * LANES)
            outs.append(rs[:, sl] * (t2[:, sl] * gc + r2[:, sl] * gs))
        return outs

    qa = proj[:, COL_MQ:COL_MKV]
    qn = qa * lax.rsqrt(jnp.mean(qa * qa, axis=-1, keepdims=True) + EPS) * qgain_ref[...]
    qn = qn.astype(BF16)
    mq = _dot(qn, wqb_ref[...])
    mq_rot = _dot(qn, wqbr_ref[...])
    kva = proj[:, COL_MKV:COL_MISC]
    kvn = (kva * lax.rsqrt(jnp.mean(kva * kva, axis=-1, keepdims=True) + EPS) * kvgain_ref[...]).astype(BF16)
    kn = _dot(kvn, wkvk_ref[...])
    mv_ref[0] = _dot_nt(wkvv_ref[...], kvn).astype(BF16)
    rope_tile = jnp.where(in_rope, misc, 0.0)
    up = (lane >= ROPE_LO + ROPE_HALF) & (lane < ROPE_LO + MLA_ROPE)
    dn = (lane >= ROPE_LO) & (lane < ROPE_LO + ROPE_HALF)
    rope_rot = (jnp.where(up, pltpu.roll(rope_tile, ROPE_HALF, 1), 0.0)
                - jnp.where(dn, pltpu.roll(rope_tile, LANES - ROPE_HALF, 1), 0.0))
    rope2 = jnp.concatenate([rope_tile, rope_tile], axis=1)
    rope_rot2 = jnp.concatenate([rope_rot, rope_rot], axis=1)
    scale = MLA_QK ** -0.5 * LOG2_E
    q_gc, q_gs = (qng_ref[...] * scale) * c_tab, (qngr_ref[...] * scale) * s_tab
    k_gc, k_gs = kng_ref[...] * c_tab, kngr_ref[...] * s_tab
    for hp in range(MLA_HEADS // 2):
        sl = slice(hp * 2 * HEAD_TILE, (hp + 1) * 2 * HEAD_TILE)
        q2 = norm_rope_pair(mq[:, sl], mq_rot[:, sl], q_gc, q_gs)
        k2 = norm_rope_pair(kn[:, sl] + rope2, rope_rot2, k_gc, k_gs)
        for i in range(2):
            mq_ref[0, 2 * hp + i] = q2[i].astype(BF16)
            mk_ref[0, 2 * hp + i] = k2[i].astype(BF16)


def _gla_kernel(q_ref, k_ref, v_ref, lg_ref, gate_ref, gain_ref, o_ref, accf_ref, accb_ref, stf_ref, stb_ref):
    seq = q_ref.shape[0]
    C = GLA_CHUNK
    n_chunks = seq // C
    HK = GLA_QK_WIDTH

    lane_head = lax.broadcasted_iota(jnp.int32, (1, HK), 1) // GLA_DK
    rowi = lax.broadcasted_iota(jnp.int32, (C, 1), 0)
    srow = lax.broadcasted_iota(jnp.int32, (GLA_HEADS * C, C), 0) % C
    scol = lax.broadcasted_iota(jnp.int32, (GLA_HEADS * C, C), 1)
    gain = gain_ref[...]
    fwd_cfg = (True, C // 2 - 1, C - 1, scol <= srow, 0)
    bwd_cfg = (False, C // 2, 0, scol > srow, HK)

    def stack_heads(t):
        return jnp.concatenate([jnp.where(lane_head == hd, t, 0.0) for hd in range(GLA_HEADS)], axis=0)

    def chunk_step(c, st, cfg, acc_ref, other_acc_ref=None):
        prefix, ref_i, last_i, keep, lg_off = cfg
        rows = pl.ds(pl.multiple_of(c * C, C), C)
        b = lg_ref[rows, lg_off:lg_off + HK]
        shift = 1
        while shift < C:
            if prefix:
                b = b + jnp.where(rowi >= shift, pltpu.roll(b, shift, 0), 0.0)
            else:
                b = b + jnp.where(rowi < C - shift, pltpu.roll(b, C - shift, 0), 0.0)
            shift *= 2
        b_ref = b[ref_i:ref_i + 1, :]
        b_last = b[last_i:last_i + 1, :]
        q = q_ref[rows, :].astype(F32)
        k = k_ref[rows, :].astype(F32)
        v = v_ref[rows, :]
        q_rel = q * jnp.exp(b - b_ref)
        k_rel = (k * jnp.exp(b_ref - b)).astype(BF16)
        k_dec = (k * jnp.exp(b_last - b)).astype(BF16)
        q_dec = q * jnp.exp(b)
        sc = _dot_nt(stack_heads(q_rel).astype(BF16), k_rel)
        sc = jnp.where(keep, sc, 0.0).astype(BF16)
        o_inter = _dot_nt(stack_heads(q_dec).astype(BF16), st.astype(BF16))
        kv_t = _dot_tn(v, k_dec)
        new_st = jnp.exp(b_last) * st
        for hd in range(GLA_HEADS):
            vs = slice(hd * GLA_DV, (hd + 1) * GLA_DV)
            rs = slice(hd * C, (hd + 1) * C)
            o_h = _dot(sc[rs, :], v[:, vs]) + o_inter[rs, :]
            if other_acc_ref is None:
                acc_ref[rows, vs] = o_h
            else:
                o = o_h + other_acc_ref[rows, vs]
                on = o * lax.rsqrt(jnp.mean(o * o, axis=-1, keepdims=True) + EPS) * gain
                gt = gate_ref[rows, vs].astype(F32)
                o_ref[rows, vs] = (on * (gt * jax.nn.sigmoid(gt))).astype(BF16)
            new_st = new_st + jnp.where(lane_head == hd, kv_t[vs, :], 0.0)
        return new_st

    stf_ref[...] = jnp.zeros_like(stf_ref)
    stb_ref[...] = jnp.zeros_like(stb_ref)

    unroll = min(GLA_CHUNK_UNROLL, n_chunks // 2)
    n_iter = n_chunks // unroll

    def make_body(finalize):
        def body(i, carry):
            st_f = stf_ref[...]
            st_b = stb_ref[...]
            for u in range(unroll):
                cf = i * unroll + u
                st_f = chunk_step(cf, st_f, fwd_cfg, accf_ref, accb_ref if finalize else None)
                st_b = chunk_step(n_chunks - 1 - cf, st_b, bwd_cfg, accb_ref, accf_ref if finalize else None)
            stf_ref[...] = st_f
            stb_ref[...] = st_b
            return carry
        return body

    lax.fori_loop(0, n_iter // 2, make_body(False), 0)
    lax.fori_loop(n_iter // 2, n_iter, make_body(True), 0)


def _attn_kernel(q_ref, k_ref, vt_ref, o_ref):
    seq = k_ref.shape[2]
    ones = jnp.ones((ONES_ROWS, seq), BF16)
    outs = []
    scores = [_dot_nt(q_ref[0, h], k_ref[0, h]) for h in range(ATTN_SCORES_AHEAD)]
    for hd in range(MLA_HEADS):
        if hd + ATTN_SCORES_AHEAD < MLA_HEADS:
            scores.append(_dot_nt(q_ref[0, hd + ATTN_SCORES_AHEAD], k_ref[0, hd + ATTN_SCORES_AHEAD]))
        s = scores[hd]
        p = jnp.exp2(s - jnp.max(s, axis=-1, keepdims=True)).astype(BF16)
        vt_aug = jnp.concatenate([vt_ref[0, hd * MLA_V:(hd + 1) * MLA_V, :], ones], axis=0)
        ot = _dot_nt(vt_aug, p)
        outs.append(ot[0:MLA_V, :] / ot[MLA_V:MLA_V + 1, :])
    o_ref[0] = jnp.concatenate(outs, axis=0).astype(BF16)


def _unpack_bf16_pair(w):
    lo = pltpu.unpack_elementwise(w, index=0, packed_dtype=BF16, unpacked_dtype=F32)
    hi = pltpu.unpack_elementwise(w, index=1, packed_dtype=BF16, unpacked_dtype=F32)
    return lo, hi


def _pack_bf16_pair(lo, hi):
    return pltpu.pack_elementwise([lo, hi], packed_dtype=BF16)


def _outproj_kernel(tiles_per_sb, x_ref, gla_ref, mla_ref, wo_ref, n2_ref, wr_ref, br_ref, tri_ref,
                    x1_ref, h2_ref, route_ref, cnt_ref, carry_ref, wob_ref):
    @pl.when(pl.program_id(0) == 0)
    def _():
        wob_ref[...] = wo_ref[...].astype(BF16)

    mix = _dot(gla_ref[...], wob_ref[0:GLA_WIDTH, :]) + _dot_tn(mla_ref[0], wob_ref[GLA_WIDTH:, :])
    x1 = x_ref[...] + mix
    x1_ref[...] = x1
    h2 = x1 * lax.rsqrt(jnp.mean(x1 * x1, axis=-1, keepdims=True) + EPS) * n2_ref[...]
    h2b = h2.astype(BF16)
    h2_ref[...] = h2b

    logits = _dot_nt(wr_ref[...], h2b) + br_ref[...]
    tm = logits.shape[1]
    gl = logits[0:N_GROUPS, :]
    ge = jnp.exp(gl - jnp.max(gl, axis=0, keepdims=True))
    pg = ge / jnp.sum(ge, axis=0, keepdims=True)
    p_top = jnp.max(pg, axis=0, keepdims=True)
    gi = lax.broadcasted_iota(jnp.int32, (N_GROUPS, tm), 0)
    g_idx = jnp.min(jnp.where(pg == p_top, gi, N_GROUPS), axis=0, keepdims=True)
    sel = jnp.zeros((EXPERTS_PER_GROUP, tm), F32)
    for g in range(N_GROUPS):
        r0 = ROUTER_EXPERT_ROW + g * EXPERTS_PER_GROUP
        sel = sel + jnp.where(g_idx == g, logits[r0:r0 + EXPERTS_PER_GROUP, :], 0.0)
    se = jnp.exp(sel - jnp.max(sel, axis=0, keepdims=True))
    pe = se / jnp.sum(se, axis=0, keepdims=True)
    ei = lax.broadcasted_iota(jnp.int32, (EXPERTS_PER_GROUP, tm), 0)
    m1 = jnp.max(pe, axis=0, keepdims=True)
    i1 = jnp.min(jnp.where(pe == m1, ei, EXPERTS_PER_GROUP), axis=0, keepdims=True)
    pe2 = jnp.where(ei == i1, -1.0, pe)
    m2 = jnp.max(pe2, axis=0, keepdims=True)
    i2 = jnp.min(jnp.where(pe2 == m2, ei, EXPERTS_PER_GROUP), axis=0, keepdims=True)
    den = m1 + m2
    e1 = g_idx * EXPERTS_PER_GROUP + i1
    e2 = g_idx * EXPERTS_PER_GROUP + i2

    @pl.when(pl.program_id(0) % tiles_per_sb == 0)
    def _():
        carry_ref[...] = jnp.zeros_like(carry_ref)

    eall = lax.broadcasted_iota(jnp.int32, (N_EXPERTS, tm), 0)
    tri = tri_ref[...]
    carry = carry_ref[...]
    ranks = []
    for eid in (e1, e2):
        hot = eall == eid
        prefix = _dot(hot.astype(BF16), tri)
        ranks.append(jnp.sum(jnp.where(hot, carry + prefix - 1.0, 0.0), axis=0, keepdims=True))
        carry = carry + prefix[:, tm - 1:tm]
    carry_ref[...] = carry
    cnt_ref[0] = jnp.broadcast_to(carry, (N_EXPERTS, LANES))
    zeros = jnp.zeros((2, tm), F32)
    route_ref[...] = jnp.concatenate([e1.astype(F32), e2.astype(F32), p_top * (m1 / den), p_top * (m2 / den),
                                      ranks[0], ranks[1], zeros], axis=0)


def _slots_kernel(route_ref, cnt_ref, strict_ref, pos_ref, off_ref):
    counts = cnt_ref[0]
    padded = 2.0 * jnp.floor(counts * 0.5 + 0.5)
    hi = jnp.floor(padded * (1.0 / 256.0))
    lo = padded - 256.0 * hi
    strict = strict_ref[...]
    off = 256.0 * _dot(strict, hi.astype(BF16)) + _dot(strict, lo.astype(BF16))
    off_ref[0] = off.astype(jnp.int32)
    tb = route_ref.shape[1]
    eall = lax.broadcasted_iota(jnp.int32, (N_EXPERTS, tb), 0)
    off_col = off[:, 0:1]
    for s in range(2):
        hot = eall == route_ref[s:s + 1, :].astype(jnp.int32)
        base = jnp.sum(jnp.where(hot, off_col, 0.0), axis=0, keepdims=True)
        pos_ref[0, s:s + 1, :] = ((base + route_ref[4 + s:5 + s, :]) * PACK_ROWS).astype(jnp.int32)


def _moe_kernel(tb, off_ref, cnt_ref, h2_ref, pos_ref, wg_ref, wu_ref, wd_ref, x1_ref, wt_ref, o_ref,
                xy_ref, cba_ref, cbb_ref):
    sb = pl.program_id(0)
    j = pl.program_id(1)
    n_scatter = tb // MOE_SCATTER_TILE
    n_combine = tb // MOE_COMBINE_TILE
    first_expert_step = n_scatter + 1
    first_combine_step = first_expert_step + N_EXPERTS

    def pack(dst_ref):
        h = h2_ref[...].astype(F32)
        for r in range(PACK_ROWS):
            c0 = r * 2 * LANES
            dst_ref[pl.ds(r, MOE_SCATTER_TILE, stride=PACK_ROWS), :] = _pack_bf16_pair(
                h[:, c0:c0 + LANES], h[:, c0 + LANES:c0 + 2 * LANES])

    def scatter(src_ref, tile):
        t0 = tile * MOE_SCATTER_TILE
        for pair in range(MOE_SCATTER_TILE // 2):
            v = src_ref[pair * 2 * PACK_ROWS:(pair + 1) * 2 * PACK_ROWS, :]
            for half in range(2):
                piece = v[half * PACK_ROWS:(half + 1) * PACK_ROWS, :]
                for slot in range(2):
                    p = pos_ref[slot * tb + t0 + pair * 2 + half]
                    xy_ref[pl.ds(pl.multiple_of(p, PACK_ROWS), PACK_ROWS), :] = piece

    def gather(dst_ref, tile):
        t0 = tile * MOE_COMBINE_TILE
        for t in range(MOE_COMBINE_TILE):
            for slot in range(2):
                p = pos_ref[slot * tb + t0 + t]
                r0 = (slot * MOE_COMBINE_TILE + t) * PACK_ROWS
                dst_ref[r0:r0 + PACK_ROWS, :] = xy_ref[pl.ds(pl.multiple_of(p, PACK_ROWS), PACK_ROWS), :]

    def weighted_sum(src_ref):
        w1 = wt_ref[:, 0:1]
        w2 = wt_ref[:, 1:2]
        second = MOE_COMBINE_TILE * PACK_ROWS
        for r in range(PACK_ROWS):
            lo1, hi1 = _unpack_bf16_pair(src_ref[pl.ds(r, MOE_COMBINE_TILE, stride=PACK_ROWS), :])
            lo2, hi2 = _unpack_bf16_pair(src_ref[pl.ds(second + r, MOE_COMBINE_TILE, stride=PACK_ROWS), :])
            c0 = r * 2 * LANES
            o_ref[:, c0:c0 + LANES] = x1_ref[:, c0:c0 + LANES] + w1 * lo1 + w2 * lo2
            o_ref[:, c0 + LANES:c0 + 2 * LANES] = x1_ref[:, c0 + LANES:c0 + 2 * LANES] + w1 * hi1 + w2 * hi2

    bufs = (cba_ref, cbb_ref)

    @pl.when(j == 0)
    def _first_pack():
        xy_ref[...] = jnp.zeros_like(xy_ref)
        pack(bufs[0])

    for parity in range(2):
        @pl.when((j >= 1) & (j < n_scatter) & (j % 2 == parity))
        def _scatter_and_pack():
            pack(bufs[parity])
            scatter(bufs[1 - parity], j - 1)

    @pl.when(j == n_scatter)
    def _last_scatter():
        scatter(bufs[(n_scatter - 1) % 2], n_scatter - 1)

    @pl.when((j >= first_expert_step) & (j < first_combine_step))
    def _experts():
        e = sb * N_EXPERTS + j - first_expert_step
        n = cnt_ref[e]
        base = off_ref[e]

        def body(i, c):
            r0 = base + i * MOE_CHUNK
            row0 = pl.multiple_of(r0 * PACK_ROWS, 2 * PACK_ROWS)
            halves = [_unpack_bf16_pair(xy_ref[pl.ds(row0 + r, MOE_CHUNK, stride=PACK_ROWS), :])
                      for r in range(PACK_ROWS)]
            a = jnp.zeros((MOE_CHUNK, D_EXPERT), F32)
            u = jnp.zeros((MOE_CHUNK, D_EXPERT), F32)
            for r in range(PACK_ROWS):
                lo, hi = halves[r]
                xb = jnp.concatenate([lo, hi], axis=1).astype(BF16)
                a = a + _dot(xb, wg_ref[0, r * 2 * LANES:(r + 1) * 2 * LANES, :].astype(BF16))
                u = u + _dot(xb, wu_ref[0, r * 2 * LANES:(r + 1) * 2 * LANES, :].astype(BF16))
            hid = ((a * jax.nn.sigmoid(a)) * u).astype(BF16)
            y = _dot(hid, wd_ref[0].astype(BF16))
            valid = (r0 + lax.broadcasted_iota(jnp.int32, (MOE_CHUNK, 1), 0)) < (base + n)
            for r in range(PACK_ROWS):
                c0 = r * 2 * LANES
                lo, hi = halves[r]
                xy_ref[pl.ds(row0 + r, MOE_CHUNK, stride=PACK_ROWS), :] = _pack_bf16_pair(
                    jnp.where(valid, y[:, c0:c0 + LANES], lo), jnp.where(valid, y[:, c0 + LANES:c0 + 2 * LANES], hi))
            return c

        lax.fori_loop(0, (n + MOE_CHUNK - 1) // MOE_CHUNK, body, 0)

    cstep = j - first_combine_step

    @pl.when(cstep == 0)
    def _first_gather():
        gather(bufs[0], 0)

    for parity in range(2):
        @pl.when((cstep >= 1) & (cstep < n_combine) & (cstep % 2 == parity))
        def _gather_and_sum():
            gather(bufs[parity], cstep)
            weighted_sum(bufs[1 - parity])

    @pl.when(cstep == n_combine)
    def _last_sum():
        weighted_sum(bufs[(n_combine - 1) % 2])


def _full(shape):
    return pl.BlockSpec(shape, lambda *_: (0,) * len(shape))


def _prep_weights(w_in, gk_fwd_w, gk_fwd_b, gk_bwd_w, gk_bwd_b, mla_w_qb, mla_w_kvb, q_norm_gain, k_norm_gain):
    splits = np.cumsum([0, GLA_QK_WIDTH, GLA_QK_WIDTH, GLA_WIDTH, GLA_WIDTH, GLA_GATE_RANK, GLA_GATE_RANK,
                        MLA_Q_RANK, MLA_KV_RANK, MLA_ROPE])
    wb = w_in.astype(BF16)
    d = w_in.shape[0]
    gap = jnp.zeros((d, ROPE_LO - 2 * GLA_GATE_RANK), BF16)
    tail = jnp.zeros((d, LANES - ROPE_LO - MLA_ROPE), BF16)
    win = jnp.concatenate([wb[:, :splits[4]], wb[:, splits[6]:splits[8]], wb[:, splits[4]:splits[6]], gap,
                           wb[:, splits[8]:splits[9]], tail], axis=1)

    zq = jnp.zeros((GLA_GATE_RANK, GLA_QK_WIDTH), F32)
    wg = jnp.concatenate([jnp.concatenate([gk_fwd_w, zq], axis=1), jnp.concatenate([zq, gk_bwd_w], axis=1),
                          jnp.zeros((LANES - 2 * GLA_GATE_RANK, 2 * GLA_QK_WIDTH), F32)], axis=0)
    bg = jnp.concatenate([gk_fwd_b, gk_bwd_b])[None, :]

    pad = HEAD_TILE - MLA_QK
    wqb = jnp.pad(mla_w_qb.reshape(MLA_Q_RANK, MLA_HEADS, MLA_QK), ((0, 0), (0, 0), (0, pad)))
    wqb = wqb.reshape(MLA_Q_RANK, MLA_HEADS * HEAD_TILE)
    wkv = mla_w_kvb.reshape(MLA_KV_RANK, MLA_HEADS, MLA_NOPE + MLA_V)
    wkvk = jnp.pad(wkv[:, :, :MLA_NOPE], ((0, 0), (0, 0), (0, HEAD_TILE - MLA_NOPE)))
    wkvk = wkvk.reshape(MLA_KV_RANK, MLA_HEADS * HEAD_TILE).astype(BF16)
    wkvv = wkv[:, :, MLA_NOPE:].reshape(MLA_KV_RANK, MLA_WIDTH).T.astype(BF16)
    qng = jnp.pad(q_norm_gain, (0, pad))[None, :]
    kng = jnp.pad(k_norm_gain, (0, pad))[None, :]
    def partner(w, sign):
        first, second = w[..., ROPE_LO:ROPE_LO + ROPE_HALF], w[..., ROPE_LO + ROPE_HALF:ROPE_LO + MLA_ROPE]
        return jnp.concatenate([jnp.zeros_like(w[..., :ROPE_LO]), sign * second, first,
                                jnp.zeros_like(w[..., ROPE_LO + MLA_ROPE:])], axis=-1)

    wqb_rot = partner(wqb.reshape(MLA_Q_RANK, MLA_HEADS, HEAD_TILE), -1.0).reshape(MLA_Q_RANK, -1).astype(BF16)
    qng_rot = partner(qng, 1.0)
    kng_rot = partner(kng, 1.0)
    return win, wg.astype(BF16), bg, wqb.astype(BF16), wqb_rot, wkvk, wkvv, qng, qng_rot, kng, kng_rot


def _rope_consts():
    inv = ROPE_BASE ** (-np.arange(0, MLA_ROPE, 2, dtype=np.float32) / MLA_ROPE)
    one = np.zeros((ROPE_HALF, LANES), np.float32)
    one[np.arange(ROPE_HALF), ROPE_LO + np.arange(ROPE_HALF)] = 1.0
    one[np.arange(ROPE_HALF), ROPE_LO + ROPE_HALF + np.arange(ROPE_HALF)] = 1.0
    zero = np.zeros_like(one)
    place = np.block([[one, zero], [one, zero], [zero, one], [zero, one]])
    ones_bd = np.kron(np.eye(2, dtype=np.float32), np.ones((LANES, LANES), np.float32))
    return jnp.asarray(inv[:, None]), jnp.asarray(place, BF16), jnp.asarray(ones_bd, BF16)


def _layer(x2, pos2, batch, seq, norm1_gain, w_in, gk_fwd_w, gk_fwd_b, gk_bwd_w, gk_bwd_b, gla_out_gain,
           mla_q_gain, mla_w_qb, mla_kv_gain, mla_w_kvb, q_norm_gain, k_norm_gain, w_out, norm2_gain,
           w_router_group, b_router_group, w_router_expert, b_router_expert,
           w_expert_gate, w_expert_up, w_expert_down):
    T, D = x2.shape
    win, wg, bg, wqb, wqb_rot, wkvk, wkvv, qng, qng_rot, kng, kng_rot = _prep_weights(
        w_in, gk_fwd_w, gk_fwd_b, gk_bwd_w, gk_bwd_b, mla_w_qb, mla_w_kvb, q_norm_gain, k_norm_gain)

    TM = min(PROJ_TILE, seq)
    spb = seq // TM
    tok = lambda w: pl.BlockSpec((TM, w), lambda i: (i, 0))
    head_spec = pl.BlockSpec((1, MLA_HEADS, TM, HEAD_TILE), lambda i: (i // spb, 0, i % spb, 0))
    gq, gk, gv, gg, lg, mq, mk, mv = pl.pallas_call(
        _proj_kernel,
        grid=(T // TM,),
        in_specs=[tok(D), pl.BlockSpec((1, 1, TM), lambda i: (i, 0, 0)), _full((1, D)), _full((D, PROJ_WIDTH)),
                  _full((LANES, 2 * GLA_QK_WIDTH)), _full((1, 2 * GLA_QK_WIDTH)), _full((1, MLA_Q_RANK)),
                  _full((MLA_Q_RANK, MLA_HEADS * HEAD_TILE)), _full((MLA_Q_RANK, MLA_HEADS * HEAD_TILE)),
                  _full((1, MLA_KV_RANK)), _full((MLA_KV_RANK, MLA_HEADS * HEAD_TILE)),
                  _full((MLA_WIDTH, MLA_KV_RANK)), _full((1, HEAD_TILE)), _full((1, HEAD_TILE)),
                  _full((1, HEAD_TILE)), _full((1, HEAD_TILE)),
                  _full((ROPE_HALF, 1)), _full((4 * ROPE_HALF, 2 * LANES)), _full((2 * LANES, 2 * LANES))],
        out_specs=[tok(GLA_QK_WIDTH), tok(GLA_QK_WIDTH), tok(GLA_WIDTH), tok(GLA_WIDTH), tok(2 * GLA_QK_WIDTH),
                   head_spec, head_spec, pl.BlockSpec((1, MLA_WIDTH, TM), lambda i: (i // spb, 0, i % spb))],
        out_shape=[jax.ShapeDtypeStruct((T, GLA_QK_WIDTH), BF16), jax.ShapeDtypeStruct((T, GLA_QK_WIDTH), BF16),
                   jax.ShapeDtypeStruct((T, GLA_WIDTH), BF16), jax.ShapeDtypeStruct((T, GLA_WIDTH), BF16),
                   jax.ShapeDtypeStruct((T, 2 * GLA_QK_WIDTH), F32),
                   jax.ShapeDtypeStruct((batch, MLA_HEADS, seq, HEAD_TILE), BF16),
                   jax.ShapeDtypeStruct((batch, MLA_HEADS, seq, HEAD_TILE), BF16),
                   jax.ShapeDtypeStruct((batch, MLA_WIDTH, seq), BF16)],
        compiler_params=pltpu.CompilerParams(dimension_semantics=("parallel",), vmem_limit_bytes=VMEM_LIMIT),
        name="proj",
    )(x2, pos2.reshape(T // TM, 1, TM), norm1_gain[None, :], win, wg, bg, mla_q_gain[None, :], wqb, wqb_rot,
      mla_kv_gain[None, :], wkvk, wkvv, qng, qng_rot, kng, kng_rot, *_rope_consts())

    seqspec = lambda w: pl.BlockSpec((seq, w), lambda b: (b, 0))
    gla_out = pl.pallas_call(
        _gla_kernel,
        grid=(batch,),
        in_specs=[seqspec(GLA_QK_WIDTH), seqspec(GLA_QK_WIDTH), seqspec(GLA_WIDTH), seqspec(2 * GLA_QK_WIDTH),
                  seqspec(GLA_WIDTH), _full((1, GLA_DV))],
        out_specs=seqspec(GLA_WIDTH),
        out_shape=jax.ShapeDtypeStruct((T, GLA_WIDTH), BF16),
        scratch_shapes=[pltpu.VMEM((seq, GLA_WIDTH), F32), pltpu.VMEM((seq, GLA_WIDTH), F32),
                        pltpu.VMEM((GLA_DV, GLA_QK_WIDTH), F32), pltpu.VMEM((GLA_DV, GLA_QK_WIDTH), F32)],
        compiler_params=pltpu.CompilerParams(dimension_semantics=("parallel",), vmem_limit_bytes=VMEM_LIMIT),
        name="gla",
    )(gq, gk, gv, lg, gg, gla_out_gain[None, :])

    TQ = 256
    nq = seq // TQ
    mla_out = pl.pallas_call(
        _attn_kernel,
        grid=(batch, nq),
        in_specs=[pl.BlockSpec((1, MLA_HEADS, TQ, HEAD_TILE), lambda b, i: (b, 0, i, 0)),
                  pl.BlockSpec((1, MLA_HEADS, seq, HEAD_TILE), lambda b, i: (b, 0, 0, 0)),
                  pl.BlockSpec((1, MLA_WIDTH, seq), lambda b, i: (b, 0, 0))],
        out_specs=pl.BlockSpec((1, MLA_WIDTH, TQ), lambda b, i: (b, 0, i)),
        out_shape=jax.ShapeDtypeStruct((batch, MLA_WIDTH, seq), BF16),
        compiler_params=pltpu.CompilerParams(dimension_semantics=("parallel", "parallel"),
                                             vmem_limit_bytes=VMEM_LIMIT),
        name="attn",
    )(mq, mk, mv)

    row_gap = ROUTER_EXPERT_ROW - N_GROUPS
    row_tail = ROUTER_ROWS - ROUTER_EXPERT_ROW - N_EXPERTS
    wr = jnp.concatenate([w_router_group, jnp.zeros((D, row_gap), F32), w_router_expert,
                          jnp.zeros((D, row_tail), F32)], axis=1).T
    br = jnp.concatenate([b_router_group, jnp.zeros((row_gap,), F32), b_router_expert,
                          jnp.zeros((row_tail,), F32)])[:, None]
    tb = min(MOE_SUPER_BLOCK, T)
    nsb = T // tb
    TO = min(OUTPROJ_TILE, seq)
    tiles_per_sb = tb // TO
    tok = lambda w: pl.BlockSpec((TO, w), lambda i: (i, 0))
    tri = jnp.asarray(np.triu(np.ones((TO, TO), np.float32)), BF16)
    x1, h2, route, cnt = pl.pallas_call(
        functools.partial(_outproj_kernel, tiles_per_sb),
        grid=(T // TO,),
        in_specs=[tok(D), tok(GLA_WIDTH),
                  pl.BlockSpec((1, MLA_WIDTH, TO), lambda i: (i // (seq // TO), 0, i % (seq // TO))),
                  _full((D, D)), _full((1, D)),
                  _full((ROUTER_ROWS, D)), _full((ROUTER_ROWS, 1)), _full((TO, TO))],
        out_specs=[tok(D), tok(D), pl.BlockSpec((8, TO), lambda i: (0, i)),
                   pl.BlockSpec((1, N_EXPERTS, LANES), lambda i: (i // tiles_per_sb, 0, 0))],
        out_shape=[jax.ShapeDtypeStruct((T, D), F32), jax.ShapeDtypeStruct((T, D), BF16),
                   jax.ShapeDtypeStruct((8, T), F32), jax.ShapeDtypeStruct((nsb, N_EXPERTS, LANES), F32)],
        scratch_shapes=[pltpu.VMEM((N_EXPERTS, 1), F32), pltpu.VMEM((D, D), BF16)],
        compiler_params=pltpu.CompilerParams(dimension_semantics=("arbitrary",), vmem_limit_bytes=VMEM_LIMIT),
        name="outproj",
    )(x2, gla_out, mla_out, w_out, norm2_gain[None, :], wr.astype(BF16), br, tri)

    strict = jnp.asarray(np.tril(np.ones((N_EXPERTS, N_EXPERTS), np.float32), -1), BF16)
    pos, offs = pl.pallas_call(
        _slots_kernel,
        grid=(nsb,),
        in_specs=[pl.BlockSpec((8, tb), lambda s: (0, s)), pl.BlockSpec((1, N_EXPERTS, LANES), lambda s: (s, 0, 0)),
                  _full((N_EXPERTS, N_EXPERTS))],
        out_specs=[pl.BlockSpec((1, 2, tb), lambda s: (s, 0, 0)),
                   pl.BlockSpec((1, N_EXPERTS, LANES), lambda s: (s, 0, 0))],
        out_shape=[jax.ShapeDtypeStruct((nsb, 2, tb), jnp.int32),
                   jax.ShapeDtypeStruct((nsb, N_EXPERTS, LANES), jnp.int32)],
        compiler_params=pltpu.CompilerParams(dimension_semantics=("parallel",)),
        name="slots",
    )(route, cnt, strict)
    pos = pos.reshape(-1)
    flat_off = offs[:, :, 0].reshape(-1)
    counts = cnt[:, :, 0].astype(jnp.int32)
    wts = jnp.stack([route[2], route[3]], axis=1)
    wge = w_expert_gate.reshape(N_EXPERTS, D, D_EXPERT)
    wue = w_expert_up.reshape(N_EXPERTS, D, D_EXPERT)
    wde = w_expert_down.reshape(N_EXPERTS, D_EXPERT, D)
    nct = tb // MOE_COMBINE_TILE
    nst = tb // MOE_SCATTER_TILE
    slots =(2 * tb + N_EXPERTS + MOE_CHUNK + 7) // 8 * 8
    hidx = lambda sb, j, off, cn: (sb * nst + jnp.minimum(j, nst - 1), 0)
    eidx = lambda sb, j, off, cn: (jnp.clip(j - (nst + 1), 0, N_EXPERTS - 1), 0, 0)
    tidx = lambda sb, j, off, cn: (sb * nct + jnp.clip(j - (nst + 1 + N_EXPERTS) - 1, 0, nct - 1), 0)
    out = pl.pallas_call(
        functools.partial(_moe_kernel, tb),
        grid_spec=pltpu.PrefetchScalarGridSpec(
            num_scalar_prefetch=2,
            grid=(nsb, (nst + 1) + N_EXPERTS + (nct + 1)),
            in_specs=[pl.BlockSpec((MOE_SCATTER_TILE, D), hidx),
                      pl.BlockSpec((2 * tb,), lambda sb, j, off, cn: (sb,), memory_space=pltpu.SMEM),
                      pl.BlockSpec((1, D, D_EXPERT), eidx), pl.BlockSpec((1, D, D_EXPERT), eidx),
                      pl.BlockSpec((1, D_EXPERT, D), eidx),
                      pl.BlockSpec((MOE_COMBINE_TILE, D), tidx), pl.BlockSpec((MOE_COMBINE_TILE, 2), tidx)],
            out_specs=pl.BlockSpec((MOE_COMBINE_TILE, D), tidx),
            scratch_shapes=[pltpu.VMEM((slots * PACK_ROWS, LANES), U32),
                            pltpu.VMEM((MOE_SCATTER_TILE * PACK_ROWS, LANES), U32),
                            pltpu.VMEM((MOE_SCATTER_TILE * PACK_ROWS, LANES), U32)]),
        out_shape=jax.ShapeDtypeStruct((T, D), F32),
        compiler_params=pltpu.CompilerParams(dimension_semantics=("arbitrary", "arbitrary"),
                                             vmem_limit_bytes=MOE_VMEM_LIMIT),
        name="moe",
    )(flat_off, counts.reshape(-1), h2, pos, wge, wue, wde, x1, wts)
    return out


def kernel(x, positions, norm1_gain, w_in, gla_gk_fwd_w, gla_gk_fwd_b, gla_gk_bwd_w, gla_gk_bwd_b, gla_out_gain, mla_q_gain, mla_w_qb, mla_kv_gain, mla_w_kvb, q_norm_gain, k_norm_gain, w_out, norm2_gain, w_router_group, b_router_group, w_router_expert, b_router_expert, w_expert_gate, w_expert_up, w_expert_down):
    batch, seq, d = x.shape
    x2 = x.reshape(batch * seq, d)
    pos2 = positions.reshape(batch * seq, 1)
    for l in range(norm1_gain.shape[0]):
        x2 = _layer(x2, pos2, batch, seq, norm1_gain[l], w_in[l], gla_gk_fwd_w[l], gla_gk_fwd_b[l],
                    gla_gk_bwd_w[l], gla_gk_bwd_b[l], gla_out_gain[l], mla_q_gain[l], mla_w_qb[l],
                    mla_kv_gain[l], mla_w_kvb[l], q_norm_gain[l], k_norm_gain[l], w_out[l], norm2_gain[l],
                    w_router_group[l], b_router_group[l], w_router_expert[l], b_router_expert[l],
                    w_expert_gate[l], w_expert_up[l], w_expert_down[l])
    return x2.reshape(batch, seq, d)
```

```python
import functools

import numpy as np
import jax
import jax.numpy as jnp
from jax import lax
from jax.experimental import pallas as pl
from jax.experimental.pallas import tpu as pltpu

F32 = jnp.float32
BF16 = jnp.bfloat16

D_MODEL = 1024
GLA_HEADS = 4
GLA_DK = 64
GLA_DV = 128
GLA_GATE_RANK = 16
GLA_GATE_NORMALIZER = 16.0
GLA_CHUNK = 64
GLA_CHUNK_UNROLL = 8
MLA_HEADS = 8
MLA_NOPE = 64
MLA_ROPE = 32
MLA_QK = MLA_NOPE + MLA_ROPE
MLA_V = 64
MLA_Q_RANK = 256
MLA_KV_RANK = 128
ROPE_BASE = 10000.0
GLA_QK_WIDTH = GLA_HEADS * GLA_DK
GLA_WIDTH = GLA_HEADS * GLA_DV
MLA_WIDTH = MLA_HEADS * MLA_V
N_GROUPS = 4
EXPERTS_PER_GROUP = 8
N_EXPERTS = N_GROUPS * EXPERTS_PER_GROUP
D_EXPERT = 256
EPS = 1e-6

LANES = 128
HEAD_TILE = LANES
ROPE_LO = MLA_NOPE
ROPE_HALF = MLA_ROPE // 2
LOG2_E = 1.4426950408889634
ATTN_SCORES_AHEAD = 3
ONES_ROWS = 16

COL_GQ = 0
COL_GK = COL_GQ + GLA_QK_WIDTH
COL_GV = COL_GK + GLA_QK_WIDTH
COL_GG = COL_GV + GLA_WIDTH
COL_MQ = COL_GG + GLA_WIDTH
COL_MKV = COL_MQ + MLA_Q_RANK
COL_MISC = COL_MKV + MLA_KV_RANK
PROJ_WIDTH = COL_MISC + LANES

ROUTER_ROWS = LANES
ROUTER_EXPERT_ROW = 8

U32 = jnp.uint32
HI_HALF_MASK = np.uint32(0xFFFF0000)
PACK_ROWS = D_MODEL // (2 * LANES)
MOE_SUPER_BLOCK = 8192
MOE_CHUNK = 576
MOE_SCATTER_TILE = 1024
MOE_COMBINE_TILE = 512
PROJ_TILE = 1024
OUTPROJ_TILE = 1024
assert 2 * MOE_COMBINE_TILE == MOE_SCATTER_TILE

VMEM_LIMIT = 56 * 1024 * 1024
MOE_VMEM_LIMIT = 59900 * 1024


def _dot(a, b):
    return jnp.dot(a, b, preferred_element_type=F32)


def _dot_nt(a, b):
    return lax.dot_general(a, b, (((1,), (1,)), ((), ())), preferred_element_type=F32)


def _dot_tn(a, b):
    return lax.dot_general(a, b, (((0,), (0,)), ((), ())), preferred_element_type=F32)


def _split_bf16(x):
    hi = x.astype(BF16)
    lo = (x - hi.astype(F32)).astype(BF16)
    return hi, lo


def _proj_kernel(x_ref, pos_ref, n1_ref, win_ref, wg_ref, bg_ref, qgain_ref, wqb_ref, wqbr_ref, kvgain_ref,
                 wkvk_ref, wkvv_ref, qng_ref, qngr_ref, kng_ref, kngr_ref, freq_ref, place_ref, ones_ref,
                 gq_ref, gk_ref, gv_ref, gg_ref, lg_ref, mq_ref, mk_ref, mv_ref):
    x = x_ref[...]
    h = x * lax.rsqrt(jnp.mean(x * x, axis=-1, keepdims=True) + EPS) * n1_ref[...]
    proj = _dot(h.astype(BF16), win_ref[...])

    gq_ref[...] = (proj[:, COL_GQ:COL_GK] * GLA_DK ** -0.5).astype(BF16)
    gk_ref[...] = proj[:, COL_GK:COL_GV].astype(BF16)
    gv_ref[...] = proj[:, COL_GV:COL_GG].astype(BF16)
    gg_ref[...] = proj[:, COL_GG:COL_MQ].astype(BF16)

    misc = proj[:, COL_MISC:PROJ_WIDTH]
    z = _dot(misc.astype(BF16), wg_ref[...]) + bg_ref[...]
    log_sig = -(jnp.maximum(-z, 0.0) + jnp.log(1.0 + jnp.exp(-jnp.abs(z))))
    lg_ref[...] = log_sig / GLA_GATE_NORMALIZER

    lane = lax.broadcasted_iota(jnp.int32, (1, LANES), 1)
    ang_t = freq_ref[...] * pos_ref[0].astype(F32)
    cos_hi, cos_lo = _split_bf16(jnp.cos(ang_t))
    sin_hi, sin_lo = _split_bf16(jnp.sin(ang_t))
    tabs = _dot_tn(jnp.concatenate([cos_hi, cos_lo, sin_hi, sin_lo], axis=0), place_ref[...])
    in_rope = (lane >= ROPE_LO) & (lane < ROPE_LO + MLA_ROPE)
    c_tab = jnp.where(lane < ROPE_LO, 1.0, tabs[:, 0:LANES])
    s_tab = tabs[:, LANES:2 * LANES]
    ones_bd = ones_ref[...]

    def norm_rope_pair(t2, r2, gc, gs):
        ss = _dot((t2 * t2).astype(BF16), ones_bd) * (1.0 / MLA_QK)
        rs = lax.rsqrt(ss + EPS)
        outs = []
        for i in range(2):
            sl = slice(i * LANES, (i + 1) * LANES)
            outs.append(rs[:, sl] * (t2[:, sl] * gc + r2[:, sl] * gs))
        return outs

    qa = proj[:, COL_MQ:COL_MKV]
    qn = qa * lax.rsqrt(jnp.mean(qa * qa, axis=-1, keepdims=True) + EPS) * qgain_ref[...]
    qn = qn.astype(BF16)
    mq = _dot(qn, wqb_ref[...])
    mq_rot = _dot(qn, wqbr_ref[...])
    kva = proj[:, COL_MKV:COL_MISC]
    kvn = (kva * lax.rsqrt(jnp.mean(kva * kva, axis=-1, keepdims=True) + EPS) * kvgain_ref[...]).astype(BF16)
    kn = _dot(kvn, wkvk_ref[...])
    mv_ref[0] = _dot_nt(wkvv_ref[...], kvn).astype(BF16)
    rope_tile = jnp.where(in_rope, misc, 0.0)
    up = (lane >= ROPE_LO + ROPE_HALF) & (lane < ROPE_LO + MLA_ROPE)
    dn = (lane >= ROPE_LO) & (lane < ROPE_LO + ROPE_HALF)
    rope_rot = (jnp.where(up, pltpu.roll(rope_tile, ROPE_HALF, 1), 0.0)
                - jnp.where(dn, pltpu.roll(rope_tile, LANES - ROPE_HALF, 1), 0.0))
    rope2 = jnp.concatenate([rope_tile, rope_tile], axis=1)
    rope_rot2 = jnp.concatenate([rope_rot, rope_rot], axis=1)
    scale = MLA_QK ** -0.5 * LOG2_E
    q_gc, q_gs = (qng_ref[...] * scale) * c_tab, (qngr_ref[...] * scale) * s_tab
    k_gc, k_gs = kng_ref[...] * c_tab, kngr_ref[...] * s_tab
    for hp in range(MLA_HEADS // 2):
        sl = slice(hp * 2 * HEAD_TILE, (hp + 1) * 2 * HEAD_TILE)
        q2 = norm_rope_pair(mq[:, sl], mq_rot[:, sl], q_gc, q_gs)
        k2 = norm_rope_pair(kn[:, sl] + rope2, rope_rot2, k_gc, k_gs)
        for i in range(2):
            mq_ref[0, 2 * hp + i] = q2[i].astype(BF16)
            mk_ref[0, 2 * hp + i] = k2[i].astype(BF16)


def _gla_kernel(q_ref, k_ref, v_ref, lg_ref, gate_ref, gain_ref, o_ref, accf_ref, accb_ref, stf_ref, stb_ref):
    seq = q_ref.shape[0]
    C = GLA_CHUNK
    n_chunks = seq // C
    HK = GLA_QK_WIDTH

    lane_head = lax.broadcasted_iota(jnp.int32, (1, HK), 1) // GLA_DK
    rowi = lax.broadcasted_iota(jnp.int32, (C, 1), 0)
    srow = lax.broadcasted_iota(jnp.int32, (GLA_HEADS * C, C), 0) % C
    scol = lax.broadcasted_iota(jnp.int32, (GLA_HEADS * C, C), 1)
    gain = gain_ref[...]
    fwd_cfg = (True, C // 2 - 1, C - 1, scol <= srow, 0)
    bwd_cfg = (False, C // 2, 0, scol > srow, HK)

    def stack_heads(t):
        return jnp.concatenate([jnp.where(lane_head == hd, t, 0.0) for hd in range(GLA_HEADS)], axis=0)

    def chunk_step(c, st, cfg, acc_ref, other_acc_ref=None):
        prefix, ref_i, last_i, keep, lg_off = cfg
        rows = pl.ds(pl.multiple_of(c * C, C), C)
        b = lg_ref[rows, lg_off:lg_off + HK]
        shift = 1
        while shift < C:
            if prefix:
                b = b + jnp.where(rowi >= shift, pltpu.roll(b, shift, 0), 0.0)
            else:
                b = b + jnp.where(rowi < C - shift, pltpu.roll(b, C - shift, 0), 0.0)
            shift *= 2
        b_ref = b[ref_i:ref_i + 1, :]
        b_last = b[last_i:last_i + 1, :]
        q = q_ref[rows, :].astype(F32)
        k = k_ref[rows, :].astype(F32)
        v = v_ref[rows, :]
        q_rel = q * jnp.exp(b - b_ref)
        k_rel = (k * jnp.exp(b_ref - b)).astype(BF16)
        k_dec = (k * jnp.exp(b_last - b)).astype(BF16)
        q_dec = q * jnp.exp(b)
        sc = _dot_nt(stack_heads(q_rel).astype(BF16), k_rel)
        sc = jnp.where(keep, sc, 0.0).astype(BF16)
        o_inter = _dot_nt(stack_heads(q_dec).astype(BF16), st.astype(BF16))
        kv_t = _dot_tn(v, k_dec)
        new_st = jnp.exp(b_last) * st
        for hd in range(GLA_HEADS):
            vs = slice(hd * GLA_DV, (hd + 1) * GLA_DV)
            rs = slice(hd * C, (hd + 1) * C)
            o_h = _dot(sc[rs, :], v[:, vs]) + o_inter[rs, :]
            if other_acc_ref is None:
                acc_ref[rows, vs] = o_h
            else:
                o = o_h + other_acc_ref[rows, vs]
                on = o * lax.rsqrt(jnp.mean(o * o, axis=-1, keepdims=True) + EPS) * gain
                gt = gate_ref[rows, vs].astype(F32)
                o_ref[rows, vs] = (on * (gt * jax.nn.sigmoid(gt))).astype(BF16)
            new_st = new_st + jnp.where(lane_head == hd, kv_t[vs, :], 0.0)
        return new_st

    stf_ref[...] = jnp.zeros_like(stf_ref)
    stb_ref[...] = jnp.zeros_like(stb_ref)

    unroll = min(GLA_CHUNK_UNROLL, n_chunks // 2)
    n_iter = n_chunks // unroll

    def make_body(finalize):
        def body(i, carry):
            st_f = stf_ref[...]
            st_b = stb_ref[...]
            for u in range(unroll):
                cf = i * unroll + u
                st_f = chunk_step(cf, st_f, fwd_cfg, accf_ref, accb_ref if finalize else None)
                st_b = chunk_step(n_chunks - 1 - cf, st_b, bwd_cfg, accb_ref, accf_ref if finalize else None)
            stf_ref[...] = st_f
            stb_ref[...] = st_b
            return carry
        return body

    lax.fori_loop(0, n_iter // 2, make_body(False), 0)
    lax.fori_loop(n_iter // 2, n_iter, make_body(True), 0)


def _attn_kernel(q_ref, k_ref, vt_ref, o_ref):
    seq = k_ref.shape[2]
    ones = jnp.ones((ONES_ROWS, seq), BF16)
    outs = []
    scores = [_dot_nt(q_ref[0, h], k_ref[0, h]) for h in range(ATTN_SCORES_AHEAD)]
    for hd in range(MLA_HEADS):
        if hd + ATTN_SCORES_AHEAD < MLA_HEADS:
            scores.append(_dot_nt(q_ref[0, hd + ATTN_SCORES_AHEAD], k_ref[0, hd + ATTN_SCORES_AHEAD]))
        s = scores[hd]
        p = jnp.exp2(s - jnp.max(s, axis=-1, keepdims=True)).astype(BF16)
        vt_aug = jnp.concatenate([vt_ref[0, hd * MLA_V:(hd + 1) * MLA_V, :], ones], axis=0)
        ot = _dot_nt(vt_aug, p)
        outs.append(ot[0:MLA_V, :] / ot[MLA_V:MLA_V + 1, :])
    o_ref[0] = jnp.concatenate(outs, axis=0).astype(BF16)


def _unpack_bf16_pair(w):
    lo = pltpu.unpack_elementwise(w, index=0, packed_dtype=BF16, unpacked_dtype=F32)
    hi = pltpu.unpack_elementwise(w, index=1, packed_dtype=BF16, unpacked_dtype=F32)
    return lo, hi


def _pack_bf16_pair(lo, hi):
    return pltpu.pack_elementwise([lo, hi], packed_dtype=BF16)


def _outproj_kernel(tiles_per_sb, x_ref, gla_ref, mla_ref, wo_ref, n2_ref, wr_ref, br_ref, tri_ref,
                    x1_ref, h2_ref, route_ref, cnt_ref, carry_ref, wob_ref):
    @pl.when(pl.program_id(0) == 0)
    def _():
        wob_ref[...] = wo_ref[...].astype(BF16)

    mix = _dot(gla_ref[...], wob_ref[0:GLA_WIDTH, :]) + _dot_tn(mla_ref[0], wob_ref[GLA_WIDTH:, :])
    x1 = x_ref[...] + mix
    x1_ref[...] = x1
    h2 = x1 * lax.rsqrt(jnp.mean(x1 * x1, axis=-1, keepdims=True) + EPS) * n2_ref[...]
    h2b = h2.astype(BF16)
    h2_ref[...] = h2b

    logits = _dot_nt(wr_ref[...], h2b) + br_ref[...]
    tm = logits.shape[1]
    gl = logits[0:N_GROUPS, :]
    ge = jnp.exp(gl - jnp.max(gl, axis=0, keepdims=True))
    pg = ge / jnp.sum(ge, axis=0, keepdims=True)
    p_top = jnp.max(pg, axis=0, keepdims=True)
    gi = lax.broadcasted_iota(jnp.int32, (N_GROUPS, tm), 0)
    g_idx = jnp.min(jnp.where(pg == p_top, gi, N_GROUPS), axis=0, keepdims=True)
    sel = jnp.zeros((EXPERTS_PER_GROUP, tm), F32)
    for g in range(N_GROUPS):
        r0 = ROUTER_EXPERT_ROW + g * EXPERTS_PER_GROUP
        sel = sel + jnp.where(g_idx == g, logits[r0:r0 + EXPERTS_PER_GROUP, :], 0.0)
    se = jnp.exp(sel - jnp.max(sel, axis=0, keepdims=True))
    pe = se / jnp.sum(se, axis=0, keepdims=True)
    ei = lax.broadcasted_iota(jnp.int32, (EXPERTS_PER_GROUP, tm), 0)
    m1 = jnp.max(pe, axis=0, keepdims=True)
    i1 = jnp.min(jnp.where(pe == m1, ei, EXPERTS_PER_GROUP), axis=0, keepdims=True)
    pe2 = jnp.where(ei == i1, -1.0, pe)
    m2 = jnp.max(pe2, axis=0, keepdims=True)
    i2 = jnp.min(jnp.where(pe2 == m2, ei, EXPERTS_PER_GROUP), axis=0, keepdims=True)
    den = m1 + m2
    e1 = g_idx * EXPERTS_PER_GROUP + i1
    e2 = g_idx * EXPERTS_PER_GROUP + i2

    @pl.when(pl.program_id(0) % tiles_per_sb == 0)
    def _():
        carry_ref[...] = jnp.zeros_like(carry_ref)

    eall = lax.broadcasted_iota(jnp.int32, (N_EXPERTS, tm), 0)
    tri = tri_ref[...]
    carry = carry_ref[...]
    ranks = []
    for eid in (e1, e2):
        hot = eall == eid
        prefix = _dot(hot.astype(BF16), tri)
        ranks.append(jnp.sum(jnp.where(hot, carry + prefix - 1.0, 0.0), axis=0, keepdims=True))
        carry = carry + prefix[:, tm - 1:tm]
    carry_ref[...] = carry
    cnt_ref[0] = jnp.broadcast_to(carry, (N_EXPERTS, LANES))
    zeros = jnp.zeros((2, tm), F32)
    route_ref[...] = jnp.concatenate([e1.astype(F32), e2.astype(F32), p_top * (m1 / den), p_top * (m2 / den),
                                      ranks[0], ranks[1], zeros], axis=0)


def _slots_kernel(route_ref, cnt_ref, strict_ref, pos_ref, off_ref):
    counts = cnt_ref[0]
    padded = 2.0 * jnp.floor(counts * 0.5 + 0.5)
    hi = jnp.floor(padded * (1.0 / 256.0))
    lo = padded - 256.0 * hi
    strict = strict_ref[...]
    off = 256.0 * _dot(strict, hi.astype(BF16)) + _dot(strict, lo.astype(BF16))
    off_ref[0] = off.astype(jnp.int32)
    tb = route_ref.shape[1]
    eall = lax.broadcasted_iota(jnp.int32, (N_EXPERTS, tb), 0)
    off_col = off[:, 0:1]
    for s in range(2):
        hot = eall == route_ref[s:s + 1, :].astype(jnp.int32)
        base = jnp.sum(jnp.where(hot, off_col, 0.0), axis=0, keepdims=True)
        pos_ref[0, s:s + 1, :] = ((base + route_ref[4 + s:5 + s, :]) * PACK_ROWS).astype(jnp.int32)


def _moe_kernel(tb, off_ref, cnt_ref, h2_ref, pos_ref, wg_ref, wu_ref, wd_ref, x1_ref, wt_ref, o_ref,
                xy_ref, cba_ref, cbb_ref):
    sb = pl.program_id(0)
    j = pl.program_id(1)
    n_scatter = tb // MOE_SCATTER_TILE
    n_combine = tb // MOE_COMBINE_TILE
    first_expert_step = n_scatter + 1
    first_combine_step = first_expert_step + N_EXPERTS

    def pack(dst_ref):
        h = h2_ref[...].astype(F32)
        for r in range(PACK_ROWS):
            c0 = r * 2 * LANES
            dst_ref[pl.ds(r, MOE_SCATTER_TILE, stride=PACK_ROWS), :] = _pack_bf16_pair(
                h[:, c0:c0 + LANES], h[:, c0 + LANES:c0 + 2 * LANES])

    def scatter(src_ref, tile):
        t0 = tile * MOE_SCATTER_TILE
        for pair in range(MOE_SCATTER_TILE // 2):
            v = src_ref[pair * 2 * PACK_ROWS:(pair + 1) * 2 * PACK_ROWS, :]
            for half in range(2):
                piece = v[half * PACK_ROWS:(half + 1) * PACK_ROWS, :]
                for slot in range(2):
                    p = pos_ref[slot * tb + t0 + pair * 2 + half]
                    xy_ref[pl.ds(pl.multiple_of(p, PACK_ROWS), PACK_ROWS), :] = piece

    def gather(dst_ref, tile):
        t0 = tile * MOE_COMBINE_TILE
        for t in range(MOE_COMBINE_TILE):
            for slot in range(2):
                p = pos_ref[slot * tb + t0 + t]
                r0 = (slot * MOE_COMBINE_TILE + t) * PACK_ROWS
                dst_ref[r0:r0 + PACK_ROWS, :] = xy_ref[pl.ds(pl.multiple_of(p, PACK_ROWS), PACK_ROWS), :]

    def weighted_sum(src_ref):
        wt = wt_ref[...].T
        w1 = wt[:, 2:3]
        w2 = wt[:, 3:4]
        second = MOE_COMBINE_TILE * PACK_ROWS
        for r in range(PACK_ROWS):
            lo1, hi1 = _unpack_bf16_pair(src_ref[pl.ds(r, MOE_COMBINE_TILE, stride=PACK_ROWS), :])
            lo2, hi2 = _unpack_bf16_pair(src_ref[pl.ds(second + r, MOE_COMBINE_TILE, stride=PACK_ROWS), :])
            c0 = r * 2 * LANES
            o_ref[:, c0:c0 + LANES] = x1_ref[:, c0:c0 + LANES] + w1 * lo1 + w2 * lo2
            o_ref[:, c0 + LANES:c0 + 2 * LANES] = x1_ref[:, c0 + LANES:c0 + 2 * LANES] + w1 * hi1 + w2 * hi2

    bufs = (cba_ref, cbb_ref)

    @pl.when(j == 0)
    def _first_pack():
        zero_slot = jnp.zeros((PACK_ROWS, LANES), U32)
        for e in range(N_EXPERTS):
            p = (off_ref[sb * N_EXPERTS + e] + cnt_ref[sb * N_EXPERTS + e]) * PACK_ROWS
            xy_ref[pl.ds(pl.multiple_of(p, PACK_ROWS), PACK_ROWS), :] = zero_slot
        tail0 = 2 * tb * PACK_ROWS
        xy_ref[tail0:, :] = jnp.zeros((xy_ref.shape[0] - tail0, LANES), U32)
        pack(bufs[0])

    for parity in range(2):
        @pl.when((j >= 1) & (j < n_scatter) & (j % 2 == parity))
        def _scatter_and_pack():
            pack(bufs[parity])
            scatter(bufs[1 - parity], j - 1)

    @pl.when(j == n_scatter)
    def _last_scatter():
        scatter(bufs[(n_scatter - 1) % 2], n_scatter - 1)

    @pl.when((j >= first_expert_step) & (j < first_combine_step))
    def _experts():
        e = sb * N_EXPERTS + j - first_expert_step
        n = cnt_ref[e]
        base = off_ref[e]

        def body(i, c):
            r0 = base + i * MOE_CHUNK
            row0 = pl.multiple_of(r0 * PACK_ROWS, 2 * PACK_ROWS)
            halves = [_unpack_bf16_pair(xy_ref[pl.ds(row0 + r, MOE_CHUNK, stride=PACK_ROWS), :])
                      for r in range(PACK_ROWS)]
            a = jnp.zeros((MOE_CHUNK, D_EXPERT), F32)
            u = jnp.zeros((MOE_CHUNK, D_EXPERT), F32)
            for r in range(PACK_ROWS):
                lo, hi = halves[r]
                xb = jnp.concatenate([lo, hi], axis=1).astype(BF16)
                a = a + _dot(xb, wg_ref[0, r * 2 * LANES:(r + 1) * 2 * LANES, :].astype(BF16))
                u = u + _dot(xb, wu_ref[0, r * 2 * LANES:(r + 1) * 2 * LANES, :].astype(BF16))
            hid = ((a * jax.nn.sigmoid(a)) * u).astype(BF16)
            y = _dot(hid, wd_ref[0].astype(BF16))
            valid = (r0 + lax.broadcasted_iota(jnp.int32, (MOE_CHUNK, 1), 0)) < (base + n)
            for r in range(PACK_ROWS):
                c0 = r * 2 * LANES
                lo, hi = halves[r]
                xy_ref[pl.ds(row0 + r, MOE_CHUNK, stride=PACK_ROWS), :] = _pack_bf16_pair(
                    jnp.where(valid, y[:, c0:c0 + LANES], lo), jnp.where(valid, y[:, c0 + LANES:c0 + 2 * LANES], hi))
            return c

        lax.fori_loop(0, (n + MOE_CHUNK - 1) // MOE_CHUNK, body, 0)

    cstep = j - first_combine_step

    @pl.when(cstep == 0)
    def _first_gather():
        gather(bufs[0], 0)

    for parity in range(2):
        @pl.when((cstep >= 1) & (cstep < n_combine) & (cstep % 2 == parity))
        def _gather_and_sum():
            gather(bufs[parity], cstep)
            weighted_sum(bufs[1 - parity])

    @pl.when(cstep == n_combine)
    def _last_sum():
        weighted_sum(bufs[(n_combine - 1) % 2])


def _full(shape):
    return pl.BlockSpec(shape, lambda *_: (0,) * len(shape))


def _prep_weights(w_in, gk_fwd_w, gk_fwd_b, gk_bwd_w, gk_bwd_b, mla_w_qb, mla_w_kvb, q_norm_gain, k_norm_gain):
    splits = np.cumsum([0, GLA_QK_WIDTH, GLA_QK_WIDTH, GLA_WIDTH, GLA_WIDTH, GLA_GATE_RANK, GLA_GATE_RANK,
                        MLA_Q_RANK, MLA_KV_RANK, MLA_ROPE])
    wb = w_in.astype(BF16)
    d = w_in.shape[0]
    gap = jnp.zeros((d, ROPE_LO - 2 * GLA_GATE_RANK), BF16)
    tail = jnp.zeros((d, LANES - ROPE_LO - MLA_ROPE), BF16)
    win = jnp.concatenate([wb[:, :splits[4]], wb[:, splits[6]:splits[8]], wb[:, splits[4]:splits[6]], gap,
                           wb[:, splits[8]:splits[9]], tail], axis=1)

    zq = jnp.zeros((GLA_GATE_RANK, GLA_QK_WIDTH), F32)
    wg = jnp.concatenate([jnp.concatenate([gk_fwd_w, zq], axis=1), jnp.concatenate([zq, gk_bwd_w], axis=1),
                          jnp.zeros((LANES - 2 * GLA_GATE_RANK, 2 * GLA_QK_WIDTH), F32)], axis=0)
    bg = jnp.concatenate([gk_fwd_b, gk_bwd_b])[None, :]

    pad = HEAD_TILE - MLA_QK
    wqb = jnp.pad(mla_w_qb.reshape(MLA_Q_RANK, MLA_HEADS, MLA_QK), ((0, 0), (0, 0), (0, pad)))
    wqb = wqb.reshape(MLA_Q_RANK, MLA_HEADS * HEAD_TILE)
    wkv = mla_w_kvb.reshape(MLA_KV_RANK, MLA_HEADS, MLA_NOPE + MLA_V)
    wkvk = jnp.pad(wkv[:, :, :MLA_NOPE], ((0, 0), (0, 0), (0, HEAD_TILE - MLA_NOPE)))
    wkvk = wkvk.reshape(MLA_KV_RANK, MLA_HEADS * HEAD_TILE).astype(BF16)
    wkvv = wkv[:, :, MLA_NOPE:].reshape(MLA_KV_RANK, MLA_WIDTH).T.astype(BF16)
    qng = jnp.pad(q_norm_gain, (0, pad))[None, :]
    kng = jnp.pad(k_norm_gain, (0, pad))[None, :]
    def partner(w, sign):
        first, second = w[..., ROPE_LO:ROPE_LO + ROPE_HALF], w[..., ROPE_LO + ROPE_HALF:ROPE_LO + MLA_ROPE]
        return jnp.concatenate([jnp.zeros_like(w[..., :ROPE_LO]), sign * second, first,
                                jnp.zeros_like(w[..., ROPE_LO + MLA_ROPE:])], axis=-1)

    wqb_rot = partner(wqb.reshape(MLA_Q_RANK, MLA_HEADS, HEAD_TILE), -1.0).reshape(MLA_Q_RANK, -1).astype(BF16)
    qng_rot = partner(qng, 1.0)
    kng_rot = partner(kng, 1.0)
    return win, wg.astype(BF16), bg, wqb.astype(BF16), wqb_rot, wkvk, wkvv, qng, qng_rot, kng, kng_rot


def _rope_consts():
    inv = ROPE_BASE ** (-np.arange(0, MLA_ROPE, 2, dtype=np.float32) / MLA_ROPE)
    one = np.zeros((ROPE_HALF, LANES), np.float32)
    one[np.arange(ROPE_HALF), ROPE_LO + np.arange(ROPE_HALF)] = 1.0
    one[np.arange(ROPE_HALF), ROPE_LO + ROPE_HALF + np.arange(ROPE_HALF)] = 1.0
    zero = np.zeros_like(one)
    place = np.block([[one, zero], [one, zero], [zero, one], [zero, one]])
    ones_bd = np.kron(np.eye(2, dtype=np.float32), np.ones((LANES, LANES), np.float32))
    return jnp.asarray(inv[:, None]), jnp.asarray(place, BF16), jnp.asarray(ones_bd, BF16)


def _layer(x2, pos2, batch, seq, norm1_gain, w_in, gk_fwd_w, gk_fwd_b, gk_bwd_w, gk_bwd_b, gla_out_gain,
           mla_q_gain, mla_w_qb, mla_kv_gain, mla_w_kvb, q_norm_gain, k_norm_gain, w_out, norm2_gain,
           w_router_group, b_router_group, w_router_expert, b_router_expert,
           w_expert_gate, w_expert_up, w_expert_down):
    T, D = x2.shape
    win, wg, bg, wqb, wqb_rot, wkvk, wkvv, qng, qng_rot, kng, kng_rot = _prep_weights(
        w_in, gk_fwd_w, gk_fwd_b, gk_bwd_w, gk_bwd_b, mla_w_qb, mla_w_kvb, q_norm_gain, k_norm_gain)

    TM = min(PROJ_TILE, seq)
    spb = seq // TM
    tok = lambda w: pl.BlockSpec((TM, w), lambda i: (i, 0))
    head_spec = pl.BlockSpec((1, MLA_HEADS, TM, HEAD_TILE), lambda i: (i // spb, 0, i % spb, 0))
    gq, gk, gv, gg, lg, mq, mk, mv = pl.pallas_call(
        _proj_kernel,
        grid=(T // TM,),
        in_specs=[tok(D), pl.BlockSpec((1, 1, TM), lambda i: (i, 0, 0)), _full((1, D)), _full((D, PROJ_WIDTH)),
                  _full((LANES, 2 * GLA_QK_WIDTH)), _full((1, 2 * GLA_QK_WIDTH)), _full((1, MLA_Q_RANK)),
                  _full((MLA_Q_RANK, MLA_HEADS * HEAD_TILE)), _full((MLA_Q_RANK, MLA_HEADS * HEAD_TILE)),
                  _full((1, MLA_KV_RANK)), _full((MLA_KV_RANK, MLA_HEADS * HEAD_TILE)),
                  _full((MLA_WIDTH, MLA_KV_RANK)), _full((1, HEAD_TILE)), _full((1, HEAD_TILE)),
                  _full((1, HEAD_TILE)), _full((1, HEAD_TILE)),
                  _full((ROPE_HALF, 1)), _full((4 * ROPE_HALF, 2 * LANES)), _full((2 * LANES, 2 * LANES))],
        out_specs=[tok(GLA_QK_WIDTH), tok(GLA_QK_WIDTH), tok(GLA_WIDTH), tok(GLA_WIDTH), tok(2 * GLA_QK_WIDTH),
                   head_spec, head_spec, pl.BlockSpec((1, MLA_WIDTH, TM), lambda i: (i // spb, 0, i % spb))],
        out_shape=[jax.ShapeDtypeStruct((T, GLA_QK_WIDTH), BF16), jax.ShapeDtypeStruct((T, GLA_QK_WIDTH), BF16),
                   jax.ShapeDtypeStruct((T, GLA_WIDTH), BF16), jax.ShapeDtypeStruct((T, GLA_WIDTH), BF16),
                   jax.ShapeDtypeStruct((T, 2 * GLA_QK_WIDTH), F32),
                   jax.ShapeDtypeStruct((batch, MLA_HEADS, seq, HEAD_TILE), BF16),
                   jax.ShapeDtypeStruct((batch, MLA_HEADS, seq, HEAD_TILE), BF16),
                   jax.ShapeDtypeStruct((batch, MLA_WIDTH, seq), BF16)],
        compiler_params=pltpu.CompilerParams(dimension_semantics=("parallel",), vmem_limit_bytes=VMEM_LIMIT),
        name="proj",
    )(x2, pos2.reshape(T // TM, 1, TM), norm1_gain[None, :], win, wg, bg, mla_q_gain[None, :], wqb, wqb_rot,
      mla_kv_gain[None, :], wkvk, wkvv, qng, qng_rot, kng, kng_rot, *_rope_consts())

    seqspec = lambda w: pl.BlockSpec((seq, w), lambda b: (b, 0))
    gla_out = pl.pallas_call(
        _gla_kernel,
        grid=(batch,),
        in_specs=[seqspec(GLA_QK_WIDTH), seqspec(GLA_QK_WIDTH), seqspec(GLA_WIDTH), seqspec(2 * GLA_QK_WIDTH),
                  seqspec(GLA_WIDTH), _full((1, GLA_DV))],
        out_specs=seqspec(GLA_WIDTH),
        out_shape=jax.ShapeDtypeStruct((T, GLA_WIDTH), BF16),
        scratch_shapes=[pltpu.VMEM((seq, GLA_WIDTH), F32), pltpu.VMEM((seq, GLA_WIDTH), F32),
                        pltpu.VMEM((GLA_DV, GLA_QK_WIDTH), F32), pltpu.VMEM((GLA_DV, GLA_QK_WIDTH), F32)],
        compiler_params=pltpu.CompilerParams(dimension_semantics=("parallel",), vmem_limit_bytes=VMEM_LIMIT),
        name="gla",
    )(gq, gk, gv, lg, gg, gla_out_gain[None, :])

    TQ = 256
    nq = seq // TQ
    mla_out = pl.pallas_call(
        _attn_kernel,
        grid=(batch, nq),
        in_specs=[pl.BlockSpec((1, MLA_HEADS, TQ, HEAD_TILE), lambda b, i: (b, 0, i, 0)),
                  pl.BlockSpec((1, MLA_HEADS, seq, HEAD_TILE), lambda b, i: (b, 0, 0, 0)),
                  pl.BlockSpec((1, MLA_WIDTH, seq), lambda b, i: (b, 0, 0))],
        out_specs=pl.BlockSpec((1, MLA_WIDTH, TQ), lambda b, i: (b, 0, i)),
        out_shape=jax.ShapeDtypeStruct((batch, MLA_WIDTH, seq), BF16),
        compiler_params=pltpu.CompilerParams(dimension_semantics=("parallel", "parallel"),
                                             vmem_limit_bytes=VMEM_LIMIT),
        name="attn",
    )(mq, mk, mv)

    row_gap = ROUTER_EXPERT_ROW - N_GROUPS
    row_tail = ROUTER_ROWS - ROUTER_EXPERT_ROW - N_EXPERTS
    wr = jnp.concatenate([w_router_group, jnp.zeros((D, row_gap), F32), w_router_expert,
                          jnp.zeros((D, row_tail), F32)], axis=1).T
    br = jnp.concatenate([b_router_group, jnp.zeros((row_gap,), F32), b_router_expert,
                          jnp.zeros((row_tail,), F32)])[:, None]
    tb = min(MOE_SUPER_BLOCK, T)
    nsb = T // tb
    TO = min(OUTPROJ_TILE, seq)
    tiles_per_sb = tb // TO
    tok = lambda w: pl.BlockSpec((TO, w), lambda i: (i, 0))
    tri = jnp.asarray(np.triu(np.ones((TO, TO), np.float32)), BF16)
    x1, h2, route, cnt = pl.pallas_call(
        functools.partial(_outproj_kernel, tiles_per_sb),
        grid=(T // TO,),
        in_specs=[tok(D), tok(GLA_WIDTH),
                  pl.BlockSpec((1, MLA_WIDTH, TO), lambda i: (i // (seq // TO), 0, i % (seq // TO))),
                  _full((D, D)), _full((1, D)),
                  _full((ROUTER_ROWS, D)), _full((ROUTER_ROWS, 1)), _full((TO, TO))],
        out_specs=[tok(D), tok(D), pl.BlockSpec((8, TO), lambda i: (0, i)),
                   pl.BlockSpec((1, N_EXPERTS, LANES), lambda i: (i // tiles_per_sb, 0, 0))],
        out_shape=[jax.ShapeDtypeStruct((T, D), F32), jax.ShapeDtypeStruct((T, D), BF16),
                   jax.ShapeDtypeStruct((8, T), F32), jax.ShapeDtypeStruct((nsb, N_EXPERTS, LANES), F32)],
        scratch_shapes=[pltpu.VMEM((N_EXPERTS, 1), F32), pltpu.VMEM((D, D), BF16)],
        compiler_params=pltpu.CompilerParams(dimension_semantics=("arbitrary",), vmem_limit_bytes=VMEM_LIMIT),
        name="outproj",
    )(x2, gla_out, mla_out, w_out, norm2_gain[None, :], wr.astype(BF16), br, tri)

    strict = jnp.asarray(np.tril(np.ones((N_EXPERTS, N_EXPERTS), np.float32), -1), BF16)
    pos, offs = pl.pallas_call(
        _slots_kernel,
        grid=(nsb,),
        in_specs=[pl.BlockSpec((8, tb), lambda s: (0, s)), pl.BlockSpec((1, N_EXPERTS, LANES), lambda s: (s, 0, 0)),
                  _full((N_EXPERTS, N_EXPERTS))],
        out_specs=[pl.BlockSpec((1, 2, tb), lambda s: (s, 0, 0)),
                   pl.BlockSpec((1, N_EXPERTS, LANES), lambda s: (s, 0, 0))],
        out_shape=[jax.ShapeDtypeStruct((nsb, 2, tb), jnp.int32),
                   jax.ShapeDtypeStruct((nsb, N_EXPERTS, LANES), jnp.int32)],
        compiler_params=pltpu.CompilerParams(dimension_semantics=("parallel",)),
        name="slots",
    )(route, cnt, strict)
    pos = pos.reshape(-1)
    flat_off = offs[:, :, 0].reshape(-1)
    counts = cnt[:, :, 0].astype(jnp.int32)
    wge = w_expert_gate.reshape(N_EXPERTS, D, D_EXPERT)
    wue = w_expert_up.reshape(N_EXPERTS, D, D_EXPERT)
    wde = w_expert_down.reshape(N_EXPERTS, D_EXPERT, D)
    nct = tb // MOE_COMBINE_TILE
    nst = tb // MOE_SCATTER_TILE
    slots =(2 * tb + N_EXPERTS + MOE_CHUNK + 7) // 8 * 8
    hidx = lambda sb, j, off, cn: (sb * nst + jnp.minimum(j, nst - 1), 0)
    eidx = lambda sb, j, off, cn: (jnp.clip(j - (nst + 1), 0, N_EXPERTS - 1), 0, 0)
    tidx = lambda sb, j, off, cn: (sb * nct + jnp.clip(j - (nst + 1 + N_EXPERTS) - 1, 0, nct - 1), 0)
    out = pl.pallas_call(
        functools.partial(_moe_kernel, tb),
        grid_spec=pltpu.PrefetchScalarGridSpec(
            num_scalar_prefetch=2,
            grid=(nsb, (nst + 1) + N_EXPERTS + (nct + 1)),
            in_specs=[pl.BlockSpec((MOE_SCATTER_TILE, D), hidx),
                      pl.BlockSpec((2 * tb,), lambda sb, j, off, cn: (sb,), memory_space=pltpu.SMEM),
                      pl.BlockSpec((1, D, D_EXPERT), eidx), pl.BlockSpec((1, D, D_EXPERT), eidx),
                      pl.BlockSpec((1, D_EXPERT, D), eidx),
                      pl.BlockSpec((MOE_COMBINE_TILE, D), tidx),
                      pl.BlockSpec((8, MOE_COMBINE_TILE), lambda sb, j, off, cn: (0, tidx(sb, j, off, cn)[0]))],
            out_specs=pl.BlockSpec((MOE_COMBINE_TILE, D), tidx),
            scratch_shapes=[pltpu.VMEM((slots * PACK_ROWS, LANES), U32),
                            pltpu.VMEM((MOE_SCATTER_TILE * PACK_ROWS, LANES), U32),
                            pltpu.VMEM((MOE_SCATTER_TILE * PACK_ROWS, LANES), U32)]),
        out_shape=jax.ShapeDtypeStruct((T, D), F32),
        compiler_params=pltpu.CompilerParams(dimension_semantics=("arbitrary", "arbitrary"),
                                             vmem_limit_bytes=MOE_VMEM_LIMIT),
        name="moe",
    )(flat_off, counts.reshape(-1), h2, pos, wge, wue, wde, x1, route)
    return out


def kernel(x, positions, norm1_gain, w_in, gla_gk_fwd_w, gla_gk_fwd_b, gla_gk_bwd_w, gla_gk_bwd_b, gla_out_gain, mla_q_gain, mla_w_qb, mla_kv_gain, mla_w_kvb, q_norm_gain, k_norm_gain, w_out, norm2_gain, w_router_group, b_router_group, w_router_expert, b_router_expert, w_expert_gate, w_expert_up, w_expert_down):
    batch, seq, d = x.shape
    x2 = x.reshape(batch * seq, d)
    pos2 = positions.reshape(batch * seq, 1)
    for l in range(norm1_gain.shape[0]):
        x2 = _layer(x2, pos2, batch, seq, norm1_gain[l], w_in[l], gla_gk_fwd_w[l], gla_gk_fwd_b[l],
                    gla_gk_bwd_w[l], gla_gk_bwd_b[l], gla_out_gain[l], mla_q_gain[l], mla_w_qb[l],
                    mla_kv_gain[l], mla_w_kvb[l], q_norm_gain[l], k_norm_gain[l], w_out[l], norm2_gain[l],
                    w_router_group[l], b_router_group[l], w_router_expert[l], b_router_expert[l],
                    w_expert_gate[l], w_expert_up[l], w_expert_down[l])
    return x2.reshape(batch, seq, d)
```

```python
import functools

import numpy as np
import jax
import jax.numpy as jnp
from jax import lax
from jax.experimental import pallas as pl
from jax.experimental.pallas import tpu as pltpu

F32 = jnp.float32
BF16 = jnp.bfloat16

D_MODEL = 1024
GLA_HEADS = 4
GLA_DK = 64
GLA_DV = 128
GLA_GATE_RANK = 16
GLA_GATE_NORMALIZER = 16.0
GLA_CHUNK = 64
GLA_CHUNK_UNROLL = 8
MLA_HEADS = 8
MLA_NOPE = 64
MLA_ROPE = 32
MLA_QK = MLA_NOPE + MLA_ROPE
MLA_V = 64
MLA_Q_RANK = 256
MLA_KV_RANK = 128
ROPE_BASE = 10000.0
GLA_QK_WIDTH = GLA_HEADS * GLA_DK
GLA_WIDTH = GLA_HEADS * GLA_DV
MLA_WIDTH = MLA_HEADS * MLA_V
N_GROUPS = 4
EXPERTS_PER_GROUP = 8
N_EXPERTS = N_GROUPS * EXPERTS_PER_GROUP
D_EXPERT = 256
EPS = 1e-6

LANES = 128
HEAD_TILE = LANES
ROPE_LO = MLA_NOPE
ROPE_HALF = MLA_ROPE // 2
LOG2_E = 1.4426950408889634
ATTN_SCORES_AHEAD = 3
ONES_ROWS = 16

COL_GQ = 0
COL_GK = COL_GQ + GLA_QK_WIDTH
COL_GV = COL_GK + GLA_QK_WIDTH
COL_GG = COL_GV + GLA_WIDTH
COL_GLA_END = COL_GG + GLA_WIDTH
TAIL_MQ = 0
TAIL_MKV = TAIL_MQ + MLA_Q_RANK
TAIL_MISC = TAIL_MKV + MLA_KV_RANK
TAIL_WIDTH = TAIL_MISC + LANES

ROUTER_ROWS = LANES
ROUTER_EXPERT_ROW = 8

U32 = jnp.uint32
HI_HALF_MASK = np.uint32(0xFFFF0000)
PACK_ROWS = D_MODEL // (2 * LANES)
MOE_SUPER_BLOCK = 8192
MOE_CHUNK = 576
MOE_SCATTER_TILE = 1024
MOE_COMBINE_TILE = 512
PROJ_TILE = 1024
OUTPROJ_TILE = 1024
assert 2 * MOE_COMBINE_TILE == MOE_SCATTER_TILE

VMEM_LIMIT = 56 * 1024 * 1024
MOE_VMEM_LIMIT = 59900 * 1024


def _dot(a, b):
    return jnp.dot(a, b, preferred_element_type=F32)


def _dot_nt(a, b):
    return lax.dot_general(a, b, (((1,), (1,)), ((), ())), preferred_element_type=F32)


def _dot_tn(a, b):
    return lax.dot_general(a, b, (((0,), (0,)), ((), ())), preferred_element_type=F32)


def _split_bf16(x):
    hi = x.astype(BF16)
    lo = (x - hi.astype(F32)).astype(BF16)
    return hi, lo


def _proj_kernel(x_ref, pos_ref, n1_ref, win_ref, wtail_ref, wg_ref, bg_ref, qgain_ref, wqb_ref, wqbr_ref, kvgain_ref,
                 wkvk_ref, wkvv_ref, qng_ref, qngr_ref, kng_ref, kngr_ref, freq_ref, place_ref, ones_ref,
                 gq_ref, gk_ref, gv_ref, gg_ref, lg_ref, mq_ref, mk_ref, mv_ref):
    x = x_ref[...]
    h = x * lax.rsqrt(jnp.mean(x * x, axis=-1, keepdims=True) + EPS) * n1_ref[...]
    hb = h.astype(BF16)
    proj = _dot(hb, win_ref[...])
    tail = _dot(hb, wtail_ref[...])

    gq_ref[...] = (proj[:, COL_GQ:COL_GK] * GLA_DK ** -0.5).astype(BF16)
    gk_ref[...] = proj[:, COL_GK:COL_GV].astype(BF16)
    gv_ref[...] = proj[:, COL_GV:COL_GG].astype(BF16)
    gg_ref[...] = proj[:, COL_GG:COL_GLA_END].astype(BF16)

    misc = tail[:, TAIL_MISC:TAIL_WIDTH]
    z = _dot(misc.astype(BF16), wg_ref[...]) + bg_ref[...]
    log_sig = -(jnp.maximum(-z, 0.0) + jnp.log(1.0 + jnp.exp(-jnp.abs(z))))
    lg_ref[...] = log_sig / GLA_GATE_NORMALIZER

    lane = lax.broadcasted_iota(jnp.int32, (1, LANES), 1)
    ang_t = freq_ref[...] * pos_ref[0].astype(F32)
    cos_hi, cos_lo = _split_bf16(jnp.cos(ang_t))
    sin_hi, sin_lo = _split_bf16(jnp.sin(ang_t))
    tabs = _dot_tn(jnp.concatenate([cos_hi, cos_lo, sin_hi, sin_lo], axis=0), place_ref[...])
    in_rope = (lane >= ROPE_LO) & (lane < ROPE_LO + MLA_ROPE)
    c_tab = jnp.where(lane < ROPE_LO, 1.0, tabs[:, 0:LANES])
    s_tab = tabs[:, LANES:2 * LANES]
    ones_bd = ones_ref[...]

    def norm_rope_pair(t2, r2, gc, gs):
        ss = _dot((t2 * t2).astype(BF16), ones_bd) * (1.0 / MLA_QK)
        rs = lax.rsqrt(ss + EPS)
        outs = []
        for i in range(2):
            sl = slice(i * LANES, (i + 1) * LANES)
            outs.append(rs[:, sl] * (t2[:, sl] * gc + r2[:, sl] * gs))
        return outs

    qa = tail[:, TAIL_MQ:TAIL_MKV]
    qn = qa * lax.rsqrt(jnp.mean(qa * qa, axis=-1, keepdims=True) + EPS) * qgain_ref[...]
    qn = qn.astype(BF16)
    mq = _dot(qn, wqb_ref[...])
    mq_rot = _dot(qn, wqbr_ref[...])
    kva = tail[:, TAIL_MKV:TAIL_MISC]
    kvn = (kva * lax.rsqrt(jnp.mean(kva * kva, axis=-1, keepdims=True) + EPS) * kvgain_ref[...]).astype(BF16)
    kn = _dot(kvn, wkvk_ref[...])
    mv_ref[0] = _dot_nt(wkvv_ref[...], kvn).astype(BF16)
    rope_tile = jnp.where(in_rope, misc, 0.0)
    up = (lane >= ROPE_LO + ROPE_HALF) & (lane < ROPE_LO + MLA_ROPE)
    dn = (lane >= ROPE_LO) & (lane < ROPE_LO + ROPE_HALF)
    rope_rot = (jnp.where(up, pltpu.roll(rope_tile, ROPE_HALF, 1), 0.0)
                - jnp.where(dn, pltpu.roll(rope_tile, LANES - ROPE_HALF, 1), 0.0))
    rope2 = jnp.concatenate([rope_tile, rope_tile], axis=1)
    rope_rot2 = jnp.concatenate([rope_rot, rope_rot], axis=1)
    scale = MLA_QK ** -0.5 * LOG2_E
    q_gc, q_gs = (qng_ref[...] * scale) * c_tab, (qngr_ref[...] * scale) * s_tab
    k_gc, k_gs = kng_ref[...] * c_tab, kngr_ref[...] * s_tab
    for hp in range(MLA_HEADS // 2):
        sl = slice(hp * 2 * HEAD_TILE, (hp + 1) * 2 * HEAD_TILE)
        q2 = norm_rope_pair(mq[:, sl], mq_rot[:, sl], q_gc, q_gs)
        k2 = norm_rope_pair(kn[:, sl] + rope2, rope_rot2, k_gc, k_gs)
        for i in range(2):
            mq_ref[0, 2 * hp + i] = q2[i].astype(BF16)
            mk_ref[0, 2 * hp + i] = k2[i].astype(BF16)


def _gla_kernel(q_ref, k_ref, v_ref, lg_ref, gate_ref, gain_ref, o_ref, accf_ref, accb_ref, stf_ref, stb_ref):
    seq = q_ref.shape[0]
    C = GLA_CHUNK
    n_chunks = seq // C
    HK = GLA_QK_WIDTH

    lane_head = lax.broadcasted_iota(jnp.int32, (1, HK), 1) // GLA_DK
    rowi = lax.broadcasted_iota(jnp.int32, (C, 1), 0)
    srow = lax.broadcasted_iota(jnp.int32, (GLA_HEADS * C, C), 0) % C
    scol = lax.broadcasted_iota(jnp.int32, (GLA_HEADS * C, C), 1)
    gain = gain_ref[...]
    fwd_cfg = (True, C // 2 - 1, C - 1, scol <= srow, 0)
    bwd_cfg = (False, C // 2, 0, scol > srow, HK)

    def stack_heads(t):
        return jnp.concatenate([jnp.where(lane_head == hd, t, 0.0) for hd in range(GLA_HEADS)], axis=0)

    def chunk_step(c, st, cfg, acc_ref, other_acc_ref=None):
        prefix, ref_i, last_i, keep, lg_off = cfg
        rows = pl.ds(pl.multiple_of(c * C, C), C)
        b = lg_ref[rows, lg_off:lg_off + HK]
        shift = 1
        while shift < C:
            if prefix:
                b = b + jnp.where(rowi >= shift, pltpu.roll(b, shift, 0), 0.0)
            else:
                b = b + jnp.where(rowi < C - shift, pltpu.roll(b, C - shift, 0), 0.0)
            shift *= 2
        b_ref = b[ref_i:ref_i + 1, :]
        b_last = b[last_i:last_i + 1, :]
        q = q_ref[rows, :].astype(F32)
        k = k_ref[rows, :].astype(F32)
        v = v_ref[rows, :]
        q_rel = q * jnp.exp(b - b_ref)
        k_rel = (k * jnp.exp(b_ref - b)).astype(BF16)
        k_dec = (k * jnp.exp(b_last - b)).astype(BF16)
        q_dec = q * jnp.exp(b)
        sc = _dot_nt(stack_heads(q_rel).astype(BF16), k_rel)
        sc = jnp.where(keep, sc, 0.0).astype(BF16)
        o_inter = _dot_nt(stack_heads(q_dec).astype(BF16), st.astype(BF16))
        kv_t = _dot_tn(v, k_dec)
        new_st = jnp.exp(b_last) * st
        for hd in range(GLA_HEADS):
            vs = slice(hd * GLA_DV, (hd + 1) * GLA_DV)
            rs = slice(hd * C, (hd + 1) * C)
            o_h = _dot(sc[rs, :], v[:, vs]) + o_inter[rs, :]
            if other_acc_ref is None:
                acc_ref[rows, vs] = o_h
            else:
                o = o_h + other_acc_ref[rows, vs]
                on = o * lax.rsqrt(jnp.mean(o * o, axis=-1, keepdims=True) + EPS) * gain
                gt = gate_ref[rows, vs].astype(F32)
                o_ref[rows, vs] = (on * (gt * jax.nn.sigmoid(gt))).astype(BF16)
            new_st = new_st + jnp.where(lane_head == hd, kv_t[vs, :], 0.0)
        return new_st

    stf_ref[...] = jnp.zeros_like(stf_ref)
    stb_ref[...] = jnp.zeros_like(stb_ref)

    unroll = min(GLA_CHUNK_UNROLL, n_chunks // 2)
    n_iter = n_chunks // unroll

    def make_body(finalize):
        def body(i, carry):
            st_f = stf_ref[...]
            st_b = stb_ref[...]
            for u in range(unroll):
                cf = i * unroll + u
                st_f = chunk_step(cf, st_f, fwd_cfg, accf_ref, accb_ref if finalize else None)
                st_b = chunk_step(n_chunks - 1 - cf, st_b, bwd_cfg, accb_ref, accf_ref if finalize else None)
            stf_ref[...] = st_f
            stb_ref[...] = st_b
            return carry
        return body

    lax.fori_loop(0, n_iter // 2, make_body(False), 0)
    lax.fori_loop(n_iter // 2, n_iter, make_body(True), 0)


def _attn_kernel(q_ref, k_ref, vt_ref, o_ref):
    seq = k_ref.shape[2]
    ones = jnp.ones((ONES_ROWS, seq), BF16)
    outs = []
    scores = [_dot_nt(q_ref[0, h], k_ref[0, h]) for h in range(ATTN_SCORES_AHEAD)]
    for hd in range(MLA_HEADS):
        if hd + ATTN_SCORES_AHEAD < MLA_HEADS:
            scores.append(_dot_nt(q_ref[0, hd + ATTN_SCORES_AHEAD], k_ref[0, hd + ATTN_SCORES_AHEAD]))
        s = scores[hd]
        p = jnp.exp2(s - jnp.max(s, axis=-1, keepdims=True)).astype(BF16)
        vt_aug = jnp.concatenate([vt_ref[0, hd * MLA_V:(hd + 1) * MLA_V, :], ones], axis=0)
        ot = _dot_nt(vt_aug, p)
        outs.append(ot[0:MLA_V, :] / ot[MLA_V:MLA_V + 1, :])
    o_ref[0] = jnp.concatenate(outs, axis=0).astype(BF16)


def _unpack_bf16_pair(w):
    lo = pltpu.unpack_elementwise(w, index=0, packed_dtype=BF16, unpacked_dtype=F32)
    hi = pltpu.unpack_elementwise(w, index=1, packed_dtype=BF16, unpacked_dtype=F32)
    return lo, hi


def _pack_bf16_pair(lo, hi):
    return pltpu.pack_elementwise([lo, hi], packed_dtype=BF16)


def _outproj_kernel(tiles_per_sb, x_ref, gla_ref, mla_ref, wo_ref, n2_ref, wr_ref, br_ref, tri_ref,
                    x1_ref, h2_ref, route_ref, cnt_ref, carry_ref, wob_ref):
    @pl.when(pl.program_id(0) == 0)
    def _():
        wob_ref[...] = wo_ref[...].astype(BF16)

    mix = _dot(gla_ref[...], wob_ref[0:GLA_WIDTH, :]) + _dot_tn(mla_ref[0], wob_ref[GLA_WIDTH:, :])
    x1 = x_ref[...] + mix
    x1_ref[...] = x1
    h2 = x1 * lax.rsqrt(jnp.mean(x1 * x1, axis=-1, keepdims=True) + EPS) * n2_ref[...]
    h2b = h2.astype(BF16)
    h2_ref[...] = h2b

    logits = _dot_nt(wr_ref[...], h2b) + br_ref[...]
    tm = logits.shape[1]
    gl = logits[0:N_GROUPS, :]
    ge = jnp.exp(gl - jnp.max(gl, axis=0, keepdims=True))
    pg = ge / jnp.sum(ge, axis=0, keepdims=True)
    p_top = jnp.max(pg, axis=0, keepdims=True)
    gi = lax.broadcasted_iota(jnp.int32, (N_GROUPS, tm), 0)
    g_idx = jnp.min(jnp.where(pg == p_top, gi, N_GROUPS), axis=0, keepdims=True)
    sel = jnp.zeros((EXPERTS_PER_GROUP, tm), F32)
    for g in range(N_GROUPS):
        r0 = ROUTER_EXPERT_ROW + g * EXPERTS_PER_GROUP
        sel = sel + jnp.where(g_idx == g, logits[r0:r0 + EXPERTS_PER_GROUP, :], 0.0)
    se = jnp.exp(sel - jnp.max(sel, axis=0, keepdims=True))
    pe = se / jnp.sum(se, axis=0, keepdims=True)
    ei = lax.broadcasted_iota(jnp.int32, (EXPERTS_PER_GROUP, tm), 0)
    m1 = jnp.max(pe, axis=0, keepdims=True)
    i1 = jnp.min(jnp.where(pe == m1, ei, EXPERTS_PER_GROUP), axis=0, keepdims=True)
    pe2 = jnp.where(ei == i1, -1.0, pe)
    m2 = jnp.max(pe2, axis=0, keepdims=True)
    i2 = jnp.min(jnp.where(pe2 == m2, ei, EXPERTS_PER_GROUP), axis=0, keepdims=True)
    den = m1 + m2
    e1 = g_idx * EXPERTS_PER_GROUP + i1
    e2 = g_idx * EXPERTS_PER_GROUP + i2

    @pl.when(pl.program_id(0) % tiles_per_sb == 0)
    def _():
        carry_ref[...] = jnp.zeros_like(carry_ref)

    eall = lax.broadcasted_iota(jnp.int32, (N_EXPERTS, tm), 0)
    tri = tri_ref[...]
    carry = carry_ref[...]
    ranks = []
    for eid in (e1, e2):
        hot = eall == eid
        prefix = _dot(hot.astype(BF16), tri)
        ranks.append(jnp.sum(jnp.where(hot, carry + prefix - 1.0, 0.0), axis=0, keepdims=True))
        carry = carry + prefix[:, tm - 1:tm]
    carry_ref[...] = carry
    cnt_ref[0] = jnp.broadcast_to(carry, (N_EXPERTS, LANES))
    zeros = jnp.zeros((2, tm), F32)
    route_ref[...] = jnp.concatenate([e1.astype(F32), e2.astype(F32), p_top * (m1 / den), p_top * (m2 / den),
                                      ranks[0], ranks[1], zeros], axis=0)


def _slots_kernel(route_ref, cnt_ref, strict_ref, pos_ref, off_ref):
    counts = cnt_ref[0]
    padded = 2.0 * jnp.floor(counts * 0.5 + 0.5)
    hi = jnp.floor(padded * (1.0 / 256.0))
    lo = padded - 256.0 * hi
    strict = strict_ref[...]
    off = 256.0 * _dot(strict, hi.astype(BF16)) + _dot(strict, lo.astype(BF16))
    off_ref[0] = off.astype(jnp.int32)
    tb = route_ref.shape[1]
    eall = lax.broadcasted_iota(jnp.int32, (N_EXPERTS, tb), 0)
    off_col = off[:, 0:1]
    for s in range(2):
        hot = eall == route_ref[s:s + 1, :].astype(jnp.int32)
        base = jnp.sum(jnp.where(hot, off_col, 0.0), axis=0, keepdims=True)
        pos_ref[0, s:s + 1, :] = ((base + route_ref[4 + s:5 + s, :]) * PACK_ROWS).astype(jnp.int32)


def _moe_kernel(tb, off_ref, cnt_ref, h2_ref, pos_ref, wg_ref, wu_ref, wd_ref, x1_ref, wt_ref, o_ref,
                xy_ref, cba_ref, cbb_ref):
    sb = pl.program_id(0)
    j = pl.program_id(1)
    n_scatter = tb // MOE_SCATTER_TILE
    n_combine = tb // MOE_COMBINE_TILE
    first_expert_step = n_scatter + 1
    first_combine_step = first_expert_step + N_EXPERTS

    def pack(dst_ref):
        h = h2_ref[...].astype(F32)
        for r in range(PACK_ROWS):
            c0 = r * 2 * LANES
            dst_ref[pl.ds(r, MOE_SCATTER_TILE, stride=PACK_ROWS), :] = _pack_bf16_pair(
                h[:, c0:c0 + LANES], h[:, c0 + LANES:c0 + 2 * LANES])

    def scatter(src_ref, tile):
        t0 = tile * MOE_SCATTER_TILE
        for pair in range(MOE_SCATTER_TILE // 2):
            v = src_ref[pair * 2 * PACK_ROWS:(pair + 1) * 2 * PACK_ROWS, :]
            for half in range(2):
                piece = v[half * PACK_ROWS:(half + 1) * PACK_ROWS, :]
                for slot in range(2):
                    p = pos_ref[slot * tb + t0 + pair * 2 + half]
                    xy_ref[pl.ds(pl.multiple_of(p, PACK_ROWS), PACK_ROWS), :] = piece

    def gather(dst_ref, tile):
        t0 = tile * MOE_COMBINE_TILE
        for t in range(MOE_COMBINE_TILE):
            for slot in range(2):
                p = pos_ref[slot * tb + t0 + t]
                r0 = (slot * MOE_COMBINE_TILE + t) * PACK_ROWS
                dst_ref[r0:r0 + PACK_ROWS, :] = xy_ref[pl.ds(pl.multiple_of(p, PACK_ROWS), PACK_ROWS), :]

    def weighted_sum(src_ref):
        wt = wt_ref[...].T
        w1 = wt[:, 2:3]
        w2 = wt[:, 3:4]
        second = MOE_COMBINE_TILE * PACK_ROWS
        for r in range(PACK_ROWS):
            lo1, hi1 = _unpack_bf16_pair(src_ref[pl.ds(r, MOE_COMBINE_TILE, stride=PACK_ROWS), :])
            lo2, hi2 = _unpack_bf16_pair(src_ref[pl.ds(second + r, MOE_COMBINE_TILE, stride=PACK_ROWS), :])
            c0 = r * 2 * LANES
            o_ref[:, c0:c0 + LANES] = x1_ref[:, c0:c0 + LANES] + w1 * lo1 + w2 * lo2
            o_ref[:, c0 + LANES:c0 + 2 * LANES] = x1_ref[:, c0 + LANES:c0 + 2 * LANES] + w1 * hi1 + w2 * hi2

    bufs = (cba_ref, cbb_ref)

    @pl.when(j == 0)
    def _first_pack():
        zero_slot = jnp.zeros((PACK_ROWS, LANES), U32)
        for e in range(N_EXPERTS):
            p = (off_ref[sb * N_EXPERTS + e] + cnt_ref[sb * N_EXPERTS + e]) * PACK_ROWS
            xy_ref[pl.ds(pl.multiple_of(p, PACK_ROWS), PACK_ROWS), :] = zero_slot
        tail0 = 2 * tb * PACK_ROWS
        xy_ref[tail0:, :] = jnp.zeros((xy_ref.shape[0] - tail0, LANES), U32)
        pack(bufs[0])

    for parity in range(2):
        @pl.when((j >= 1) & (j < n_scatter) & (j % 2 == parity))
        def _scatter_and_pack():
            pack(bufs[parity])
            scatter(bufs[1 - parity], j - 1)

    @pl.when(j == n_scatter)
    def _last_scatter():
        scatter(bufs[(n_scatter - 1) % 2], n_scatter - 1)

    @pl.when((j >= first_expert_step) & (j < first_combine_step))
    def _experts():
        e = sb * N_EXPERTS + j - first_expert_step
        n = cnt_ref[e]
        base = off_ref[e]

        def body(i, c):
            r0 = base + i * MOE_CHUNK
            row0 = pl.multiple_of(r0 * PACK_ROWS, 2 * PACK_ROWS)
            halves = [_unpack_bf16_pair(xy_ref[pl.ds(row0 + r, MOE_CHUNK, stride=PACK_ROWS), :])
                      for r in range(PACK_ROWS)]
            a = jnp.zeros((MOE_CHUNK, D_EXPERT), F32)
            u = jnp.zeros((MOE_CHUNK, D_EXPERT), F32)
            for r in range(PACK_ROWS):
                lo, hi = halves[r]
                xb = jnp.concatenate([lo, hi], axis=1).astype(BF16)
                a = a + _dot(xb, wg_ref[0, r * 2 * LANES:(r + 1) * 2 * LANES, :].astype(BF16))
                u = u + _dot(xb, wu_ref[0, r * 2 * LANES:(r + 1) * 2 * LANES, :].astype(BF16))
            hid = ((a * jax.nn.sigmoid(a)) * u).astype(BF16)
            y = _dot(hid, wd_ref[0].astype(BF16))
            valid = (r0 + lax.broadcasted_iota(jnp.int32, (MOE_CHUNK, 1), 0)) < (base + n)
            for r in range(PACK_ROWS):
                c0 = r * 2 * LANES
                lo, hi = halves[r]
                xy_ref[pl.ds(row0 + r, MOE_CHUNK, stride=PACK_ROWS), :] = _pack_bf16_pair(
                    jnp.where(valid, y[:, c0:c0 + LANES], lo), jnp.where(valid, y[:, c0 + LANES:c0 + 2 * LANES], hi))
            return c

        lax.fori_loop(0, (n + MOE_CHUNK - 1) // MOE_CHUNK, body, 0)

    cstep = j - first_combine_step

    @pl.when(cstep == 0)
    def _first_gather():
        gather(bufs[0], 0)

    for parity in range(2):
        @pl.when((cstep >= 1) & (cstep < n_combine) & (cstep % 2 == parity))
        def _gather_and_sum():
            gather(bufs[parity], cstep)
            weighted_sum(bufs[1 - parity])

    @pl.when(cstep == n_combine)
    def _last_sum():
        weighted_sum(bufs[(n_combine - 1) % 2])


def _full(shape):
    return pl.BlockSpec(shape, lambda *_: (0,) * len(shape))


def _prep_weights(w_in, gk_fwd_w, gk_fwd_b, gk_bwd_w, gk_bwd_b, mla_w_qb, mla_w_kvb, q_norm_gain, k_norm_gain):
    splits = np.cumsum([0, GLA_QK_WIDTH, GLA_QK_WIDTH, GLA_WIDTH, GLA_WIDTH, GLA_GATE_RANK, GLA_GATE_RANK,
                        MLA_Q_RANK, MLA_KV_RANK, MLA_ROPE])
    d = w_in.shape[0]
    win = w_in[:, :splits[4]].astype(BF16)
    wt = w_in[:, splits[4]:]
    rel = splits - splits[4]
    gap = jnp.zeros((d, ROPE_LO - 2 * GLA_GATE_RANK), F32)
    end = jnp.zeros((d, LANES - ROPE_LO - MLA_ROPE), F32)
    wtail = jnp.concatenate([wt[:, rel[6]:rel[8]], wt[:, rel[4]:rel[6]], gap, wt[:, rel[8]:rel[9]], end],
                            axis=1).astype(BF16)

    zq = jnp.zeros((GLA_GATE_RANK, GLA_QK_WIDTH), F32)
    wg = jnp.concatenate([jnp.concatenate([gk_fwd_w, zq], axis=1), jnp.concatenate([zq, gk_bwd_w], axis=1),
                          jnp.zeros((LANES - 2 * GLA_GATE_RANK, 2 * GLA_QK_WIDTH), F32)], axis=0)
    bg = jnp.concatenate([gk_fwd_b, gk_bwd_b])[None, :]

    pad = HEAD_TILE - MLA_QK
    wqb = jnp.pad(mla_w_qb.reshape(MLA_Q_RANK, MLA_HEADS, MLA_QK), ((0, 0), (0, 0), (0, pad)))
    wqb = wqb.reshape(MLA_Q_RANK, MLA_HEADS * HEAD_TILE)
    wkv = mla_w_kvb.reshape(MLA_KV_RANK, MLA_HEADS, MLA_NOPE + MLA_V)
    wkvk = jnp.pad(wkv[:, :, :MLA_NOPE], ((0, 0), (0, 0), (0, HEAD_TILE - MLA_NOPE)))
    wkvk = wkvk.reshape(MLA_KV_RANK, MLA_HEADS * HEAD_TILE).astype(BF16)
    wkvv = wkv[:, :, MLA_NOPE:].reshape(MLA_KV_RANK, MLA_WIDTH).T.astype(BF16)
    qng = jnp.pad(q_norm_gain, (0, pad))[None, :]
    kng = jnp.pad(k_norm_gain, (0, pad))[None, :]
    def partner(w, sign):
        first, second = w[..., ROPE_LO:ROPE_LO + ROPE_HALF], w[..., ROPE_LO + ROPE_HALF:ROPE_LO + MLA_ROPE]
        return jnp.concatenate([jnp.zeros_like(w[..., :ROPE_LO]), sign * second, first,
                                jnp.zeros_like(w[..., ROPE_LO + MLA_ROPE:])], axis=-1)

    wqb_rot = partner(wqb.reshape(MLA_Q_RANK, MLA_HEADS, HEAD_TILE), -1.0).reshape(MLA_Q_RANK, -1).astype(BF16)
    qng_rot = partner(qng, 1.0)
    kng_rot = partner(kng, 1.0)
    return win, wtail, wg.astype(BF16), bg, wqb.astype(BF16), wqb_rot, wkvk, wkvv, qng, qng_rot, kng, kng_rot


def _rope_consts():
    inv = ROPE_BASE ** (-np.arange(0, MLA_ROPE, 2, dtype=np.float32) / MLA_ROPE)
    one = np.zeros((ROPE_HALF, LANES), np.float32)
    one[np.arange(ROPE_HALF), ROPE_LO + np.arange(ROPE_HALF)] = 1.0
    one[np.arange(ROPE_HALF), ROPE_LO + ROPE_HALF + np.arange(ROPE_HALF)] = 1.0
    zero = np.zeros_like(one)
    place = np.block([[one, zero], [one, zero], [zero, one], [zero, one]])
    ones_bd = np.kron(np.eye(2, dtype=np.float32), np.ones((LANES, LANES), np.float32))
    return jnp.asarray(inv[:, None]), jnp.asarray(place, BF16), jnp.asarray(ones_bd, BF16)


def _layer(x2, pos2, batch, seq, norm1_gain, w_in, gk_fwd_w, gk_fwd_b, gk_bwd_w, gk_bwd_b, gla_out_gain,
           mla_q_gain, mla_w_qb, mla_kv_gain, mla_w_kvb, q_norm_gain, k_norm_gain, w_out, norm2_gain,
           w_router_group, b_router_group, w_router_expert, b_router_expert,
           w_expert_gate, w_expert_up, w_expert_down):
    T, D = x2.shape
    win, wtail, wg, bg, wqb, wqb_rot, wkvk, wkvv, qng, qng_rot, kng, kng_rot = _prep_weights(
        w_in, gk_fwd_w, gk_fwd_b, gk_bwd_w, gk_bwd_b, mla_w_qb, mla_w_kvb, q_norm_gain, k_norm_gain)

    TM = min(PROJ_TILE, seq)
    spb = seq // TM
    tok = lambda w: pl.BlockSpec((TM, w), lambda i: (i, 0))
    head_spec = pl.BlockSpec((1, MLA_HEADS, TM, HEAD_TILE), lambda i: (i // spb, 0, i % spb, 0))
    gq, gk, gv, gg, lg, mq, mk, mv = pl.pallas_call(
        _proj_kernel,
        grid=(T // TM,),
        in_specs=[tok(D), pl.BlockSpec((1, 1, TM), lambda i: (i, 0, 0)), _full((1, D)), _full((D, COL_GLA_END)),
                  _full((D, TAIL_WIDTH)),
                  _full((LANES, 2 * GLA_QK_WIDTH)), _full((1, 2 * GLA_QK_WIDTH)), _full((1, MLA_Q_RANK)),
                  _full((MLA_Q_RANK, MLA_HEADS * HEAD_TILE)), _full((MLA_Q_RANK, MLA_HEADS * HEAD_TILE)),
                  _full((1, MLA_KV_RANK)), _full((MLA_KV_RANK, MLA_HEADS * HEAD_TILE)),
                  _full((MLA_WIDTH, MLA_KV_RANK)), _full((1, HEAD_TILE)), _full((1, HEAD_TILE)),
                  _full((1, HEAD_TILE)), _full((1, HEAD_TILE)),
                  _full((ROPE_HALF, 1)), _full((4 * ROPE_HALF, 2 * LANES)), _full((2 * LANES, 2 * LANES))],
        out_specs=[tok(GLA_QK_WIDTH), tok(GLA_QK_WIDTH), tok(GLA_WIDTH), tok(GLA_WIDTH), tok(2 * GLA_QK_WIDTH),
                   head_spec, head_spec, pl.BlockSpec((1, MLA_WIDTH, TM), lambda i: (i // spb, 0, i % spb))],
        out_shape=[jax.ShapeDtypeStruct((T, GLA_QK_WIDTH), BF16), jax.ShapeDtypeStruct((T, GLA_QK_WIDTH), BF16),
                   jax.ShapeDtypeStruct((T, GLA_WIDTH), BF16), jax.ShapeDtypeStruct((T, GLA_WIDTH), BF16),
                   jax.ShapeDtypeStruct((T, 2 * GLA_QK_WIDTH), F32),
                   jax.ShapeDtypeStruct((batch, MLA_HEADS, seq, HEAD_TILE), BF16),
                   jax.ShapeDtypeStruct((batch, MLA_HEADS, seq, HEAD_TILE), BF16),
                   jax.ShapeDtypeStruct((batch, MLA_WIDTH, seq), BF16)],
        compiler_params=pltpu.CompilerParams(dimension_semantics=("parallel",), vmem_limit_bytes=VMEM_LIMIT),
        name="proj",
    )(x2, pos2.reshape(T // TM, 1, TM), norm1_gain[None, :], win, wtail, wg, bg, mla_q_gain[None, :], wqb, wqb_rot,
      mla_kv_gain[None, :], wkvk, wkvv, qng, qng_rot, kng, kng_rot, *_rope_consts())

    seqspec = lambda w: pl.BlockSpec((seq, w), lambda b: (b, 0))
    gla_out = pl.pallas_call(
        _gla_kernel,
        grid=(batch,),
        in_specs=[seqspec(GLA_QK_WIDTH), seqspec(GLA_QK_WIDTH), seqspec(GLA_WIDTH), seqspec(2 * GLA_QK_WIDTH),
                  seqspec(GLA_WIDTH), _full((1, GLA_DV))],
        out_specs=seqspec(GLA_WIDTH),
        out_shape=jax.ShapeDtypeStruct((T, GLA_WIDTH), BF16),
        scratch_shapes=[pltpu.VMEM((seq, GLA_WIDTH), F32), pltpu.VMEM((seq, GLA_WIDTH), F32),
                        pltpu.VMEM((GLA_DV, GLA_QK_WIDTH), F32), pltpu.VMEM((GLA_DV, GLA_QK_WIDTH), F32)],
        compiler_params=pltpu.CompilerParams(dimension_semantics=("parallel",), vmem_limit_bytes=VMEM_LIMIT),
        name="gla",
    )(gq, gk, gv, lg, gg, gla_out_gain[None, :])

    TQ = 256
    nq = seq // TQ
    mla_out = pl.pallas_call(
        _attn_kernel,
        grid=(batch, nq),
        in_specs=[pl.BlockSpec((1, MLA_HEADS, TQ, HEAD_TILE), lambda b, i: (b, 0, i, 0)),
                  pl.BlockSpec((1, MLA_HEADS, seq, HEAD_TILE), lambda b, i: (b, 0, 0, 0)),
                  pl.BlockSpec((1, MLA_WIDTH, seq), lambda b, i: (b, 0, 0))],
        out_specs=pl.BlockSpec((1, MLA_WIDTH, TQ), lambda b, i: (b, 0, i)),
        out_shape=jax.ShapeDtypeStruct((batch, MLA_WIDTH, seq), BF16),
        compiler_params=pltpu.CompilerParams(dimension_semantics=("parallel", "parallel"),
                                             vmem_limit_bytes=VMEM_LIMIT),
        name="attn",
    )(mq, mk, mv)

    row_gap = ROUTER_EXPERT_ROW - N_GROUPS
    row_tail = ROUTER_ROWS - ROUTER_EXPERT_ROW - N_EXPERTS
    wr = jnp.concatenate([w_router_group, jnp.zeros((D, row_gap), F32), w_router_expert,
                          jnp.zeros((D, row_tail), F32)], axis=1).T
    br = jnp.concatenate([b_router_group, jnp.zeros((row_gap,), F32), b_router_expert,
                          jnp.zeros((row_tail,), F32)])[:, None]
    tb = min(MOE_SUPER_BLOCK, T)
    nsb = T // tb
    TO = min(OUTPROJ_TILE, seq)
    tiles_per_sb = tb // TO
    tok = lambda w: pl.BlockSpec((TO, w), lambda i: (i, 0))
    tri = jnp.asarray(np.triu(np.ones((TO, TO), np.float32)), BF16)
    x1, h2, route, cnt = pl.pallas_call(
        functools.partial(_outproj_kernel, tiles_per_sb),
        grid=(T // TO,),
        in_specs=[tok(D), tok(GLA_WIDTH),
                  pl.BlockSpec((1, MLA_WIDTH, TO), lambda i: (i // (seq // TO), 0, i % (seq // TO))),
                  _full((D, D)), _full((1, D)),
                  _full((ROUTER_ROWS, D)), _full((ROUTER_ROWS, 1)), _full((TO, TO))],
        out_specs=[tok(D), tok(D), pl.BlockSpec((8, TO), lambda i: (0, i)),
                   pl.BlockSpec((1, N_EXPERTS, LANES), lambda i: (i // tiles_per_sb, 0, 0))],
        out_shape=[jax.ShapeDtypeStruct((T, D), F32), jax.ShapeDtypeStruct((T, D), BF16),
                   jax.ShapeDtypeStruct((8, T), F32), jax.ShapeDtypeStruct((nsb, N_EXPERTS, LANES), F32)],
        scratch_shapes=[pltpu.VMEM((N_EXPERTS, 1), F32), pltpu.VMEM((D, D), BF16)],
        compiler_params=pltpu.CompilerParams(dimension_semantics=("arbitrary",), vmem_limit_bytes=VMEM_LIMIT),
        name="outproj",
    )(x2, gla_out, mla_out, w_out, norm2_gain[None, :], wr.astype(BF16), br, tri)

    strict = jnp.asarray(np.tril(np.ones((N_EXPERTS, N_EXPERTS), np.float32), -1), BF16)
    pos, offs = pl.pallas_call(
        _slots_kernel,
        grid=(nsb,),
        in_specs=[pl.BlockSpec((8, tb), lambda s: (0, s)), pl.BlockSpec((1, N_EXPERTS, LANES), lambda s: (s, 0, 0)),
                  _full((N_EXPERTS, N_EXPERTS))],
        out_specs=[pl.BlockSpec((1, 2, tb), lambda s: (s, 0, 0)),
                   pl.BlockSpec((1, N_EXPERTS, LANES), lambda s: (s, 0, 0))],
        out_shape=[jax.ShapeDtypeStruct((nsb, 2, tb), jnp.int32),
                   jax.ShapeDtypeStruct((nsb, N_EXPERTS, LANES), jnp.int32)],
        compiler_params=pltpu.CompilerParams(dimension_semantics=("parallel",)),
        name="slots",
    )(route, cnt, strict)
    pos = pos.reshape(-1)
    flat_off = offs[:, :, 0].reshape(-1)
    counts = cnt[:, :, 0].astype(jnp.int32)
    wge = w_expert_gate.reshape(N_EXPERTS, D, D_EXPERT)
    wue = w_expert_up.reshape(N_EXPERTS, D, D_EXPERT)
    wde = w_expert_down.reshape(N_EXPERTS, D_EXPERT, D)
    nct = tb // MOE_COMBINE_TILE
    nst = tb // MOE_SCATTER_TILE
    slots =(2 * tb + N_EXPERTS + MOE_CHUNK + 7) // 8 * 8
    hidx = lambda sb, j, off, cn: (sb * nst + jnp.minimum(j, nst - 1), 0)
    eidx = lambda sb, j, off, cn: (jnp.clip(j - (nst + 1), 0, N_EXPERTS - 1), 0, 0)
    tidx = lambda sb, j, off, cn: (sb * nct + jnp.clip(j - (nst + 1 + N_EXPERTS) - 1, 0, nct - 1), 0)
    out = pl.pallas_call(
        functools.partial(_moe_kernel, tb),
        grid_spec=pltpu.PrefetchScalarGridSpec(
            num_scalar_prefetch=2,
            grid=(nsb, (nst + 1) + N_EXPERTS + (nct + 1)),
            in_specs=[pl.BlockSpec((MOE_SCATTER_TILE, D), hidx),
                      pl.BlockSpec((2 * tb,), lambda sb, j, off, cn: (sb,), memory_space=pltpu.SMEM),
                      pl.BlockSpec((1, D, D_EXPERT), eidx), pl.BlockSpec((1, D, D_EXPERT), eidx),
                      pl.BlockSpec((1, D_EXPERT, D), eidx),
                      pl.BlockSpec((MOE_COMBINE_TILE, D), tidx),
                      pl.BlockSpec((8, MOE_COMBINE_TILE), lambda sb, j, off, cn: (0, tidx(sb, j, off, cn)[0]))],
            out_specs=pl.BlockSpec((MOE_COMBINE_TILE, D), tidx),
            scratch_shapes=[pltpu.VMEM((slots * PACK_ROWS, LANES), U32),
                            pltpu.VMEM((MOE_SCATTER_TILE * PACK_ROWS, LANES), U32),
                            pltpu.VMEM((MOE_SCATTER_TILE * PACK_ROWS, LANES), U32)]),
        out_shape=jax.ShapeDtypeStruct((T, D), F32),
        compiler_params=pltpu.CompilerParams(dimension_semantics=("arbitrary", "arbitrary"),
                                             vmem_limit_bytes=MOE_VMEM_LIMIT),
        name="moe",
    )(flat_off, counts.reshape(-1), h2, pos, wge, wue, wde, x1, route)
    return out


def kernel(x, positions, norm1_gain, w_in, gla_gk_fwd_w, gla_gk_fwd_b, gla_gk_bwd_w, gla_gk_bwd_b, gla_out_gain, mla_q_gain, mla_w_qb, mla_kv_gain, mla_w_kvb, q_norm_gain, k_norm_gain, w_out, norm2_gain, w_router_group, b_router_group, w_router_expert, b_router_expert, w_expert_gate, w_expert_up, w_expert_down):
    batch, seq, d = x.shape
    x2 = x.reshape(batch * seq, d)
    pos2 = positions.reshape(batch * seq, 1)
    for l in range(norm1_gain.shape[0]):
        x2 = _layer(x2, pos2, batch, seq, norm1_gain[l], w_in[l], gla_gk_fwd_w[l], gla_gk_fwd_b[l],
                    gla_gk_bwd_w[l], gla_gk_bwd_b[l], gla_out_gain[l], mla_q_gain[l], mla_w_qb[l],
                    mla_kv_gain[l], mla_w_kvb[l], q_norm_gain[l], k_norm_gain[l], w_out[l], norm2_gain[l],
                    w_router_group[l], b_router_group[l], w_router_expert[l], b_router_expert[l],
                    w_expert_gate[l], w_expert_up[l], w_expert_down[l])
    return x2.reshape(batch, seq, d)
```

```python
import functools

import numpy as np
import jax
import jax.numpy as jnp
from jax import lax
from jax.experimental import pallas as pl
from jax.experimental.pallas import tpu as pltpu

F32 = jnp.float32
BF16 = jnp.bfloat16

D_MODEL = 1024
GLA_HEADS = 4
GLA_DK = 64
GLA_DV = 128
GLA_GATE_RANK = 16
GLA_GATE_NORMALIZER = 16.0
GLA_CHUNK = 64
GLA_CHUNK_UNROLL = 16
MLA_HEADS = 8
MLA_NOPE = 64
MLA_ROPE = 32
MLA_QK = MLA_NOPE + MLA_ROPE
MLA_V = 64
MLA_Q_RANK = 256
MLA_KV_RANK = 128
ROPE_BASE = 10000.0
GLA_QK_WIDTH = GLA_HEADS * GLA_DK
GLA_WIDTH = GLA_HEADS * GLA_DV
MLA_WIDTH = MLA_HEADS * MLA_V
N_GROUPS = 4
EXPERTS_PER_GROUP = 8
N_EXPERTS = N_GROUPS * EXPERTS_PER_GROUP
D_EXPERT = 256
EPS = 1e-6

LANES = 128
HEAD_TILE = LANES
ROPE_LO = MLA_NOPE
ROPE_HALF = MLA_ROPE // 2
LOG2_E = 1.4426950408889634
ATTN_SCORES_AHEAD = 3
ONES_ROWS = 16

COL_GQ = 0
COL_GK = COL_GQ + GLA_QK_WIDTH
COL_GV = COL_GK + GLA_QK_WIDTH
COL_GG = COL_GV + GLA_WIDTH
COL_GLA_END = COL_GG + GLA_WIDTH
TAIL_MQ = 0
TAIL_MKV = TAIL_MQ + MLA_Q_RANK
TAIL_MISC = TAIL_MKV + MLA_KV_RANK
TAIL_WIDTH = TAIL_MISC + LANES

ROUTER_ROWS = LANES
ROUTER_EXPERT_ROW = 8

U32 = jnp.uint32
HI_HALF_MASK = np.uint32(0xFFFF0000)
PACK_ROWS = D_MODEL // (2 * LANES)
MOE_SUPER_BLOCK = 8192
MOE_CHUNK = 576
MOE_SCATTER_TILE = 1024
MOE_COMBINE_TILE = 512
PROJ_TILE = 1024
OUTPROJ_TILE = 1024
assert 2 * MOE_COMBINE_TILE == MOE_SCATTER_TILE

VMEM_LIMIT = 56 * 1024 * 1024
MOE_VMEM_LIMIT = 59900 * 1024


def _dot(a, b):
    return jnp.dot(a, b, preferred_element_type=F32)


def _dot_nt(a, b):
    return lax.dot_general(a, b, (((1,), (1,)), ((), ())), preferred_element_type=F32)


def _dot_tn(a, b):
    return lax.dot_general(a, b, (((0,), (0,)), ((), ())), preferred_element_type=F32)


def _split_bf16(x):
    hi = x.astype(BF16)
    lo = (x - hi.astype(F32)).astype(BF16)
    return hi, lo


def _proj_kernel(x_ref, pos_ref, n1_ref, win_ref, wtail_ref, wg_ref, bg_ref, qgain_ref, wqb_ref, wqbr_ref, kvgain_ref,
                 wkvk_ref, wkvv_ref, qng_ref, qngr_ref, kng_ref, kngr_ref, freq_ref, place_ref, ones_ref,
                 gq_ref, gk_ref, gv_ref, gg_ref, lg_ref, mq_ref, mk_ref, mv_ref):
    x = x_ref[...]
    h = x * lax.rsqrt(jnp.mean(x * x, axis=-1, keepdims=True) + EPS) * n1_ref[...]
    hb = h.astype(BF16)
    proj = _dot(hb, win_ref[...])
    tail = _dot(hb, wtail_ref[...])

    gq_ref[...] = (proj[:, COL_GQ:COL_GK] * GLA_DK ** -0.5).astype(BF16)
    gk_ref[...] = proj[:, COL_GK:COL_GV].astype(BF16)
    gv_ref[...] = proj[:, COL_GV:COL_GG].astype(BF16)
    gg_ref[...] = proj[:, COL_GG:COL_GLA_END].astype(BF16)

    misc = tail[:, TAIL_MISC:TAIL_WIDTH]
    z = _dot(misc.astype(BF16), wg_ref[...]) + bg_ref[...]
    log_sig = -(jnp.maximum(-z, 0.0) + jnp.log(1.0 + jnp.exp(-jnp.abs(z))))
    lg_ref[...] = log_sig / GLA_GATE_NORMALIZER

    lane = lax.broadcasted_iota(jnp.int32, (1, LANES), 1)
    ang_t = freq_ref[...] * pos_ref[0].astype(F32)
    cos_hi, cos_lo = _split_bf16(jnp.cos(ang_t))
    sin_hi, sin_lo = _split_bf16(jnp.sin(ang_t))
    tabs = _dot_tn(jnp.concatenate([cos_hi, cos_lo, sin_hi, sin_lo], axis=0), place_ref[...])
    in_rope = (lane >= ROPE_LO) & (lane < ROPE_LO + MLA_ROPE)
    c_tab = jnp.where(lane < ROPE_LO, 1.0, tabs[:, 0:LANES])
    s_tab = tabs[:, LANES:2 * LANES]
    ones_bd = ones_ref[...]

    def norm_rope_pair(t2, r2, gc, gs, on_mxu):
        if on_mxu:
            ss = _dot((t2 * t2).astype(BF16), ones_bd) * (1.0 / MLA_QK)
        outs = []
        for i in range(2):
            sl = slice(i * LANES, (i + 1) * LANES)
            t = t2[:, sl]
            ssi = ss[:, sl] if on_mxu else jnp.sum(t * t, axis=-1, keepdims=True) * (1.0 / MLA_QK)
            outs.append(lax.rsqrt(ssi + EPS) * (t * gc + r2[:, sl] * gs))
        return outs

    qa = tail[:, TAIL_MQ:TAIL_MKV]
    qn = qa * lax.rsqrt(jnp.mean(qa * qa, axis=-1, keepdims=True) + EPS) * qgain_ref[...]
    qn = qn.astype(BF16)
    mq = _dot(qn, wqb_ref[...])
    mq_rot = _dot(qn, wqbr_ref[...])
    kva = tail[:, TAIL_MKV:TAIL_MISC]
    kvn = (kva * lax.rsqrt(jnp.mean(kva * kva, axis=-1, keepdims=True) + EPS) * kvgain_ref[...]).astype(BF16)
    kn = _dot(kvn, wkvk_ref[...])
    mv_ref[0] = _dot_nt(wkvv_ref[...], kvn).astype(BF16)
    rope_tile = jnp.where(in_rope, misc, 0.0)
    up = (lane >= ROPE_LO + ROPE_HALF) & (lane < ROPE_LO + MLA_ROPE)
    dn = (lane >= ROPE_LO) & (lane < ROPE_LO + ROPE_HALF)
    rope_rot = (jnp.where(up, pltpu.roll(rope_tile, ROPE_HALF, 1), 0.0)
                - jnp.where(dn, pltpu.roll(rope_tile, LANES - ROPE_HALF, 1), 0.0))
    rope2 = jnp.concatenate([rope_tile, rope_tile], axis=1)
    rope_rot2 = jnp.concatenate([rope_rot, rope_rot], axis=1)
    scale = MLA_QK ** -0.5 * LOG2_E
    q_gc, q_gs = (qng_ref[...] * scale) * c_tab, (qngr_ref[...] * scale) * s_tab
    k_gc, k_gs = kng_ref[...] * c_tab, kngr_ref[...] * s_tab
    for hp in range(MLA_HEADS // 2):
        sl = slice(hp * 2 * HEAD_TILE, (hp + 1) * 2 * HEAD_TILE)
        q2 = norm_rope_pair(mq[:, sl], mq_rot[:, sl], q_gc, q_gs, on_mxu=True)
        k2 = norm_rope_pair(kn[:, sl] + rope2, rope_rot2, k_gc, k_gs, on_mxu=False)
        for i in range(2):
            mq_ref[0, 2 * hp + i] = q2[i].astype(BF16)
            mk_ref[0, 2 * hp + i] = k2[i].astype(BF16)


def _gla_kernel(q_ref, k_ref, v_ref, lg_ref, gate_ref, gain_ref, o_ref, accf_ref, accb_ref, stf_ref, stb_ref):
    seq = q_ref.shape[0]
    C = GLA_CHUNK
    n_chunks = seq // C
    HK = GLA_QK_WIDTH

    lane_head = lax.broadcasted_iota(jnp.int32, (1, HK), 1) // GLA_DK
    rowi = lax.broadcasted_iota(jnp.int32, (C, 1), 0)
    srow = lax.broadcasted_iota(jnp.int32, (GLA_HEADS * C, C), 0) % C
    scol = lax.broadcasted_iota(jnp.int32, (GLA_HEADS * C, C), 1)
    gain = gain_ref[...]
    fwd_cfg = (True, C // 2 - 1, C - 1, scol <= srow, 0)
    bwd_cfg = (False, C // 2, 0, scol > srow, HK)

    def stack_heads(t):
        return jnp.concatenate([jnp.where(lane_head == hd, t, 0.0) for hd in range(GLA_HEADS)], axis=0)

    def chunk_step(c, st, cfg, acc_ref, other_acc_ref=None):
        prefix, ref_i, last_i, keep, lg_off = cfg
        rows = pl.ds(pl.multiple_of(c * C, C), C)
        b = lg_ref[rows, lg_off:lg_off + HK]
        shift = 1
        while shift < C:
            if prefix:
                b = b + jnp.where(rowi >= shift, pltpu.roll(b, shift, 0), 0.0)
            else:
                b = b + jnp.where(rowi < C - shift, pltpu.roll(b, C - shift, 0), 0.0)
            shift *= 2
        b_ref = b[ref_i:ref_i + 1, :]
        b_last = b[last_i:last_i + 1, :]
        q = q_ref[rows, :].astype(F32)
        k = k_ref[rows, :].astype(F32)
        v = v_ref[rows, :]
        q_rel = q * jnp.exp(b - b_ref)
        k_rel = (k * jnp.exp(b_ref - b)).astype(BF16)
        k_dec = (k * jnp.exp(b_last - b)).astype(BF16)
        q_dec = q * jnp.exp(b)
        sc = _dot_nt(stack_heads(q_rel).astype(BF16), k_rel)
        sc = jnp.where(keep, sc, 0.0).astype(BF16)
        o_inter = _dot_nt(stack_heads(q_dec).astype(BF16), st.astype(BF16))
        kv_t = _dot_tn(v, k_dec)
        new_st = jnp.exp(b_last) * st
        for hd in range(GLA_HEADS):
            vs = slice(hd * GLA_DV, (hd + 1) * GLA_DV)
            rs = slice(hd * C, (hd + 1) * C)
            o_h = _dot(sc[rs, :], v[:, vs]) + o_inter[rs, :]
            if other_acc_ref is None:
                acc_ref[rows, vs] = o_h
            else:
                o = o_h + other_acc_ref[rows, vs]
                on = o * lax.rsqrt(jnp.mean(o * o, axis=-1, keepdims=True) + EPS) * gain
                gt = gate_ref[rows, vs].astype(F32)
                o_ref[rows, vs] = (on * (gt * jax.nn.sigmoid(gt))).astype(BF16)
            new_st = new_st + jnp.where(lane_head == hd, kv_t[vs, :], 0.0)
        return new_st

    stf_ref[...] = jnp.zeros_like(stf_ref)
    stb_ref[...] = jnp.zeros_like(stb_ref)

    unroll = min(GLA_CHUNK_UNROLL, n_chunks // 2)
    n_iter = n_chunks // unroll

    def make_body(finalize):
        def body(i, carry):
            st_f = stf_ref[...]
            st_b = stb_ref[...]
            for u in range(unroll):
                cf = i * unroll + u
                st_f = chunk_step(cf, st_f, fwd_cfg, accf_ref, accb_ref if finalize else None)
                st_b = chunk_step(n_chunks - 1 - cf, st_b, bwd_cfg, accb_ref, accf_ref if finalize else None)
            stf_ref[...] = st_f
            stb_ref[...] = st_b
            return carry
        return body

    lax.fori_loop(0, n_iter // 2, make_body(False), 0)
    lax.fori_loop(n_iter // 2, n_iter, make_body(True), 0)


def _attn_kernel(q_ref, k_ref, vt_ref, o_ref):
    seq = k_ref.shape[2]
    ones = jnp.ones((ONES_ROWS, seq), BF16)
    outs = []
    scores = [_dot_nt(q_ref[0, h], k_ref[0, h]) for h in range(ATTN_SCORES_AHEAD)]
    for hd in range(MLA_HEADS):
        if hd + ATTN_SCORES_AHEAD < MLA_HEADS:
            scores.append(_dot_nt(q_ref[0, hd + ATTN_SCORES_AHEAD], k_ref[0, hd + ATTN_SCORES_AHEAD]))
        s = scores[hd]
        p = jnp.exp2(s - jnp.max(s, axis=-1, keepdims=True)).astype(BF16)
        vt_aug = jnp.concatenate([vt_ref[0, hd * MLA_V:(hd + 1) * MLA_V, :], ones], axis=0)
        ot = _dot_nt(vt_aug, p)
        outs.append(ot[0:MLA_V, :] / ot[MLA_V:MLA_V + 1, :])
    o_ref[0] = jnp.concatenate(outs, axis=0).astype(BF16)


def _unpack_bf16_pair(w):
    lo = pltpu.unpack_elementwise(w, index=0, packed_dtype=BF16, unpacked_dtype=F32)
    hi = pltpu.unpack_elementwise(w, index=1, packed_dtype=BF16, unpacked_dtype=F32)
    return lo, hi


def _pack_bf16_pair(lo, hi):
    return pltpu.pack_elementwise([lo, hi], packed_dtype=BF16)


def _outproj_kernel(tiles_per_sb, x_ref, gla_ref, mla_ref, wo_ref, n2_ref, wr_ref, br_ref, tri_ref,
                    x1_ref, h2_ref, route_ref, cnt_ref, carry_ref, wob_ref):
    @pl.when(pl.program_id(0) == 0)
    def _():
        wob_ref[...] = wo_ref[...].astype(BF16)

    mix = _dot(gla_ref[...], wob_ref[0:GLA_WIDTH, :]) + _dot_tn(mla_ref[0], wob_ref[GLA_WIDTH:, :])
    x1 = x_ref[...] + mix
    x1_ref[...] = x1
    h2 = x1 * lax.rsqrt(jnp.mean(x1 * x1, axis=-1, keepdims=True) + EPS) * n2_ref[...]
    h2b = h2.astype(BF16)
    h2_ref[...] = h2b

    logits = _dot_nt(wr_ref[...], h2b) + br_ref[...]
    tm = logits.shape[1]
    gl = logits[0:N_GROUPS, :]
    ge = jnp.exp(gl - jnp.max(gl, axis=0, keepdims=True))
    pg = ge / jnp.sum(ge, axis=0, keepdims=True)
    p_top = jnp.max(pg, axis=0, keepdims=True)
    gi = lax.broadcasted_iota(jnp.int32, (N_GROUPS, tm), 0)
    g_idx = jnp.min(jnp.where(pg == p_top, gi, N_GROUPS), axis=0, keepdims=True)
    sel = jnp.zeros((EXPERTS_PER_GROUP, tm), F32)
    for g in range(N_GROUPS):
        r0 = ROUTER_EXPERT_ROW + g * EXPERTS_PER_GROUP
        sel = sel + jnp.where(g_idx == g, logits[r0:r0 + EXPERTS_PER_GROUP, :], 0.0)
    se = jnp.exp(sel - jnp.max(sel, axis=0, keepdims=True))
    pe = se / jnp.sum(se, axis=0, keepdims=True)
    ei = lax.broadcasted_iota(jnp.int32, (EXPERTS_PER_GROUP, tm), 0)
    m1 = jnp.max(pe, axis=0, keepdims=True)
    i1 = jnp.min(jnp.where(pe == m1, ei, EXPERTS_PER_GROUP), axis=0, keepdims=True)
    pe2 = jnp.where(ei == i1, -1.0, pe)
    m2 = jnp.max(pe2, axis=0, keepdims=True)
    i2 = jnp.min(jnp.where(pe2 == m2, ei, EXPERTS_PER_GROUP), axis=0, keepdims=True)
    den = m1 + m2
    e1 = g_idx * EXPERTS_PER_GROUP + i1
    e2 = g_idx * EXPERTS_PER_GROUP + i2

    @pl.when(pl.program_id(0) % tiles_per_sb == 0)
    def _():
        carry_ref[...] = jnp.zeros_like(carry_ref)

    eall = lax.broadcasted_iota(jnp.int32, (N_EXPERTS, tm), 0)
    tri = tri_ref[...]
    carry = carry_ref[...]
    ranks = []
    for eid in (e1, e2):
        hot = eall == eid
        prefix = _dot(hot.astype(BF16), tri)
        ranks.append(jnp.sum(jnp.where(hot, carry + prefix - 1.0, 0.0), axis=0, keepdims=True))
        carry = carry + prefix[:, tm - 1:tm]
    carry_ref[...] = carry
    cnt_ref[0] = jnp.broadcast_to(carry, (N_EXPERTS, LANES))
    zeros = jnp.zeros((2, tm), F32)
    route_ref[...] = jnp.concatenate([e1.astype(F32), e2.astype(F32), p_top * (m1 / den), p_top * (m2 / den),
                                      ranks[0], ranks[1], zeros], axis=0)


def _slots_kernel(route_ref, cnt_ref, strict_ref, pos_ref, off_ref):
    counts = cnt_ref[0]
    padded = 2.0 * jnp.floor(counts * 0.5 + 0.5)
    hi = jnp.floor(padded * (1.0 / 256.0))
    lo = padded - 256.0 * hi
    strict = strict_ref[...]
    off = 256.0 * _dot(strict, hi.astype(BF16)) + _dot(strict, lo.astype(BF16))
    off_ref[0] = off.astype(jnp.int32)
    tb = route_ref.shape[1]
    eall = lax.broadcasted_iota(jnp.int32, (N_EXPERTS, tb), 0)
    off_col = off[:, 0:1]
    for s in range(2):
        hot = eall == route_ref[s:s + 1, :].astype(jnp.int32)
        base = jnp.sum(jnp.where(hot, off_col, 0.0), axis=0, keepdims=True)
        pos_ref[0, s:s + 1, :] = ((base + route_ref[4 + s:5 + s, :]) * PACK_ROWS).astype(jnp.int32)


def _moe_kernel(tb, off_ref, cnt_ref, h2_ref, pos_ref, wg_ref, wu_ref, wd_ref, x1_ref, wt_ref, o_ref,
                xy_ref, cba_ref, cbb_ref):
    sb = pl.program_id(0)
    j = pl.program_id(1)
    n_scatter = tb // MOE_SCATTER_TILE
    n_combine = tb // MOE_COMBINE_TILE
    first_expert_step = n_scatter + 1
    first_combine_step = first_expert_step + N_EXPERTS

    def pack(dst_ref):
        h = h2_ref[...].astype(F32)
        for r in range(PACK_ROWS):
            c0 = r * 2 * LANES
            dst_ref[pl.ds(r, MOE_SCATTER_TILE, stride=PACK_ROWS), :] = _pack_bf16_pair(
                h[:, c0:c0 + LANES], h[:, c0 + LANES:c0 + 2 * LANES])

    def scatter(src_ref, tile):
        t0 = tile * MOE_SCATTER_TILE
        for pair in range(MOE_SCATTER_TILE // 2):
            v = src_ref[pair * 2 * PACK_ROWS:(pair + 1) * 2 * PACK_ROWS, :]
            for half in range(2):
                piece = v[half * PACK_ROWS:(half + 1) * PACK_ROWS, :]
                for slot in range(2):
                    p = pos_ref[slot * tb + t0 + pair * 2 + half]
                    xy_ref[pl.ds(pl.multiple_of(p, PACK_ROWS), PACK_ROWS), :] = piece

    def gather(dst_ref, tile):
        t0 = tile * MOE_COMBINE_TILE
        for t in range(MOE_COMBINE_TILE):
            for slot in range(2):
                p = pos_ref[slot * tb + t0 + t]
                r0 = (slot * MOE_COMBINE_TILE + t) * PACK_ROWS
                dst_ref[r0:r0 + PACK_ROWS, :] = xy_ref[pl.ds(pl.multiple_of(p, PACK_ROWS), PACK_ROWS), :]

    def weighted_sum(src_ref):
        wt = wt_ref[...].T
        w1 = wt[:, 2:3]
        w2 = wt[:, 3:4]
        second = MOE_COMBINE_TILE * PACK_ROWS
        for r in range(PACK_ROWS):
            lo1, hi1 = _unpack_bf16_pair(src_ref[pl.ds(r, MOE_COMBINE_TILE, stride=PACK_ROWS), :])
            lo2, hi2 = _unpack_bf16_pair(src_ref[pl.ds(second + r, MOE_COMBINE_TILE, stride=PACK_ROWS), :])
            c0 = r * 2 * LANES
            o_ref[:, c0:c0 + LANES] = x1_ref[:, c0:c0 + LANES] + w1 * lo1 + w2 * lo2
            o_ref[:, c0 + LANES:c0 + 2 * LANES] = x1_ref[:, c0 + LANES:c0 + 2 * LANES] + w1 * hi1 + w2 * hi2

    bufs = (cba_ref, cbb_ref)

    @pl.when(j == 0)
    def _first_pack():
        zero_slot = jnp.zeros((PACK_ROWS, LANES), U32)
        for e in range(N_EXPERTS):
            p = (off_ref[sb * N_EXPERTS + e] + cnt_ref[sb * N_EXPERTS + e]) * PACK_ROWS
            xy_ref[pl.ds(pl.multiple_of(p, PACK_ROWS), PACK_ROWS), :] = zero_slot
        tail0 = 2 * tb * PACK_ROWS
        xy_ref[tail0:, :] = jnp.zeros((xy_ref.shape[0] - tail0, LANES), U32)
        pack(bufs[0])

    for parity in range(2):
        @pl.when((j >= 1) & (j < n_scatter) & (j % 2 == parity))
        def _scatter_and_pack():
            pack(bufs[parity])
            scatter(bufs[1 - parity], j - 1)

    @pl.when(j == n_scatter)
    def _last_scatter():
        scatter(bufs[(n_scatter - 1) % 2], n_scatter - 1)

    @pl.when((j >= first_expert_step) & (j < first_combine_step))
    def _experts():
        e = sb * N_EXPERTS + j - first_expert_step
        n = cnt_ref[e]
        base = off_ref[e]

        def body(i, c):
            r0 = base + i * MOE_CHUNK
            row0 = pl.multiple_of(r0 * PACK_ROWS, 2 * PACK_ROWS)
            halves = [_unpack_bf16_pair(xy_ref[pl.ds(row0 + r, MOE_CHUNK, stride=PACK_ROWS), :])
                      for r in range(PACK_ROWS)]
            a = jnp.zeros((MOE_CHUNK, D_EXPERT), F32)
            u = jnp.zeros((MOE_CHUNK, D_EXPERT), F32)
            for r in range(PACK_ROWS):
                lo, hi = halves[r]
                xb = jnp.concatenate([lo, hi], axis=1).astype(BF16)
                a = a + _dot(xb, wg_ref[0, r * 2 * LANES:(r + 1) * 2 * LANES, :].astype(BF16))
                u = u + _dot(xb, wu_ref[0, r * 2 * LANES:(r + 1) * 2 * LANES, :].astype(BF16))
            hid = ((a * jax.nn.sigmoid(a)) * u).astype(BF16)
            y = _dot(hid, wd_ref[0].astype(BF16))
            valid = (r0 + lax.broadcasted_iota(jnp.int32, (MOE_CHUNK, 1), 0)) < (base + n)
            for r in range(PACK_ROWS):
                c0 = r * 2 * LANES
                lo, hi = halves[r]
                xy_ref[pl.ds(row0 + r, MOE_CHUNK, stride=PACK_ROWS), :] = _pack_bf16_pair(
                    jnp.where(valid, y[:, c0:c0 + LANES], lo), jnp.where(valid, y[:, c0 + LANES:c0 + 2 * LANES], hi))
            return c

        lax.fori_loop(0, (n + MOE_CHUNK - 1) // MOE_CHUNK, body, 0)

    cstep = j - first_combine_step

    @pl.when(cstep == 0)
    def _first_gather():
        gather(bufs[0], 0)

    for parity in range(2):
        @pl.when((cstep >= 1) & (cstep < n_combine) & (cstep % 2 == parity))
        def _gather_and_sum():
            gather(bufs[parity], cstep)
            weighted_sum(bufs[1 - parity])

    @pl.when(cstep == n_combine)
    def _last_sum():
        weighted_sum(bufs[(n_combine - 1) % 2])


def _full(shape):
    return pl.BlockSpec(shape, lambda *_: (0,) * len(shape))


def _prep_weights(w_in, gk_fwd_w, gk_fwd_b, gk_bwd_w, gk_bwd_b, mla_w_qb, mla_w_kvb, q_norm_gain, k_norm_gain):
    splits = np.cumsum([0, GLA_QK_WIDTH, GLA_QK_WIDTH, GLA_WIDTH, GLA_WIDTH, GLA_GATE_RANK, GLA_GATE_RANK,
                        MLA_Q_RANK, MLA_KV_RANK, MLA_ROPE])
    d = w_in.shape[0]
    win = w_in[:, :splits[4]].astype(BF16)
    wt = w_in[:, splits[4]:]
    rel = splits - splits[4]
    gap = jnp.zeros((d, ROPE_LO - 2 * GLA_GATE_RANK), F32)
    end = jnp.zeros((d, LANES - ROPE_LO - MLA_ROPE), F32)
    wtail = jnp.concatenate([wt[:, rel[6]:rel[8]], wt[:, rel[4]:rel[6]], gap, wt[:, rel[8]:rel[9]], end],
                            axis=1).astype(BF16)

    zq = jnp.zeros((GLA_GATE_RANK, GLA_QK_WIDTH), F32)
    wg = jnp.concatenate([jnp.concatenate([gk_fwd_w, zq], axis=1), jnp.concatenate([zq, gk_bwd_w], axis=1),
                          jnp.zeros((LANES - 2 * GLA_GATE_RANK, 2 * GLA_QK_WIDTH), F32)], axis=0)
    bg = jnp.concatenate([gk_fwd_b, gk_bwd_b])[None, :]

    pad = HEAD_TILE - MLA_QK
    wqb = jnp.pad(mla_w_qb.reshape(MLA_Q_RANK, MLA_HEADS, MLA_QK), ((0, 0), (0, 0), (0, pad)))
    wqb = wqb.reshape(MLA_Q_RANK, MLA_HEADS * HEAD_TILE)
    wkv = mla_w_kvb.reshape(MLA_KV_RANK, MLA_HEADS, MLA_NOPE + MLA_V)
    wkvk = jnp.pad(wkv[:, :, :MLA_NOPE], ((0, 0), (0, 0), (0, HEAD_TILE - MLA_NOPE)))
    wkvk = wkvk.reshape(MLA_KV_RANK, MLA_HEADS * HEAD_TILE).astype(BF16)
    wkvv = wkv[:, :, MLA_NOPE:].reshape(MLA_KV_RANK, MLA_WIDTH).T.astype(BF16)
    qng = jnp.pad(q_norm_gain, (0, pad))[None, :]
    kng = jnp.pad(k_norm_gain, (0, pad))[None, :]
    def partner(w, sign):
        first, second = w[..., ROPE_LO:ROPE_LO + ROPE_HALF], w[..., ROPE_LO + ROPE_HALF:ROPE_LO + MLA_ROPE]
        return jnp.concatenate([jnp.zeros_like(w[..., :ROPE_LO]), sign * second, first,
                                jnp.zeros_like(w[..., ROPE_LO + MLA_ROPE:])], axis=-1)

    wqb_rot = partner(wqb.reshape(MLA_Q_RANK, MLA_HEADS, HEAD_TILE), -1.0).reshape(MLA_Q_RANK, -1).astype(BF16)
    qng_rot = partner(qng, 1.0)
    kng_rot = partner(kng, 1.0)
    return win, wtail, wg.astype(BF16), bg, wqb.astype(BF16), wqb_rot, wkvk, wkvv, qng, qng_rot, kng, kng_rot


def _rope_consts():
    inv = ROPE_BASE ** (-np.arange(0, MLA_ROPE, 2, dtype=np.float32) / MLA_ROPE)
    one = np.zeros((ROPE_HALF, LANES), np.float32)
    one[np.arange(ROPE_HALF), ROPE_LO + np.arange(ROPE_HALF)] = 1.0
    one[np.arange(ROPE_HALF), ROPE_LO + ROPE_HALF + np.arange(ROPE_HALF)] = 1.0
    zero = np.zeros_like(one)
    place = np.block([[one, zero], [one, zero], [zero, one], [zero, one]])
    ones_bd = np.kron(np.eye(2, dtype=np.float32), np.ones((LANES, LANES), np.float32))
    return jnp.asarray(inv[:, None]), jnp.asarray(place, BF16), jnp.asarray(ones_bd, BF16)


def _layer(x2, pos2, batch, seq, norm1_gain, w_in, gk_fwd_w, gk_fwd_b, gk_bwd_w, gk_bwd_b, gla_out_gain,
           mla_q_gain, mla_w_qb, mla_kv_gain, mla_w_kvb, q_norm_gain, k_norm_gain, w_out, norm2_gain,
           w_router_group, b_router_group, w_router_expert, b_router_expert,
           w_expert_gate, w_expert_up, w_expert_down):
    T, D = x2.shape
    win, wtail, wg, bg, wqb, wqb_rot, wkvk, wkvv, qng, qng_rot, kng, kng_rot = _prep_weights(
        w_in, gk_fwd_w, gk_fwd_b, gk_bwd_w, gk_bwd_b, mla_w_qb, mla_w_kvb, q_norm_gain, k_norm_gain)

    TM = min(PROJ_TILE, seq)
    spb = seq // TM
    tok = lambda w: pl.BlockSpec((TM, w), lambda i: (i, 0))
    head_spec = pl.BlockSpec((1, MLA_HEADS, TM, HEAD_TILE), lambda i: (i // spb, 0, i % spb, 0))
    gq, gk, gv, gg, lg, mq, mk, mv = pl.pallas_call(
        _proj_kernel,
        grid=(T // TM,),
        in_specs=[tok(D), pl.BlockSpec((1, 1, TM), lambda i: (i, 0, 0)), _full((1, D)), _full((D, COL_GLA_END)),
                  _full((D, TAIL_WIDTH)),
                  _full((LANES, 2 * GLA_QK_WIDTH)), _full((1, 2 * GLA_QK_WIDTH)), _full((1, MLA_Q_RANK)),
                  _full((MLA_Q_RANK, MLA_HEADS * HEAD_TILE)), _full((MLA_Q_RANK, MLA_HEADS * HEAD_TILE)),
                  _full((1, MLA_KV_RANK)), _full((MLA_KV_RANK, MLA_HEADS * HEAD_TILE)),
                  _full((MLA_WIDTH, MLA_KV_RANK)), _full((1, HEAD_TILE)), _full((1, HEAD_TILE)),
                  _full((1, HEAD_TILE)), _full((1, HEAD_TILE)),
                  _full((ROPE_HALF, 1)), _full((4 * ROPE_HALF, 2 * LANES)), _full((2 * LANES, 2 * LANES))],
        out_specs=[tok(GLA_QK_WIDTH), tok(GLA_QK_WIDTH), tok(GLA_WIDTH), tok(GLA_WIDTH), tok(2 * GLA_QK_WIDTH),
                   head_spec, head_spec, pl.BlockSpec((1, MLA_WIDTH, TM), lambda i: (i // spb, 0, i % spb))],
        out_shape=[jax.ShapeDtypeStruct((T, GLA_QK_WIDTH), BF16), jax.ShapeDtypeStruct((T, GLA_QK_WIDTH), BF16),
                   jax.ShapeDtypeStruct((T, GLA_WIDTH), BF16), jax.ShapeDtypeStruct((T, GLA_WIDTH), BF16),
                   jax.ShapeDtypeStruct((T, 2 * GLA_QK_WIDTH), F32),
                   jax.ShapeDtypeStruct((batch, MLA_HEADS, seq, HEAD_TILE), BF16),
                   jax.ShapeDtypeStruct((batch, MLA_HEADS, seq, HEAD_TILE), BF16),
                   jax.ShapeDtypeStruct((batch, MLA_WIDTH, seq), BF16)],
        compiler_params=pltpu.CompilerParams(dimension_semantics=("parallel",), vmem_limit_bytes=VMEM_LIMIT),
        name="proj",
    )(x2, pos2.reshape(T // TM, 1, TM), norm1_gain[None, :], win, wtail, wg, bg, mla_q_gain[None, :], wqb, wqb_rot,
      mla_kv_gain[None, :], wkvk, wkvv, qng, qng_rot, kng, kng_rot, *_rope_consts())

    seqspec = lambda w: pl.BlockSpec((seq, w), lambda b: (b, 0))
    gla_out = pl.pallas_call(
        _gla_kernel,
        grid=(batch,),
        in_specs=[seqspec(GLA_QK_WIDTH), seqspec(GLA_QK_WIDTH), seqspec(GLA_WIDTH), seqspec(2 * GLA_QK_WIDTH),
                  seqspec(GLA_WIDTH), _full((1, GLA_DV))],
        out_specs=seqspec(GLA_WIDTH),
        out_shape=jax.ShapeDtypeStruct((T, GLA_WIDTH), BF16),
        scratch_shapes=[pltpu.VMEM((seq, GLA_WIDTH), F32), pltpu.VMEM((seq, GLA_WIDTH), F32),
                        pltpu.VMEM((GLA_DV, GLA_QK_WIDTH), F32), pltpu.VMEM((GLA_DV, GLA_QK_WIDTH), F32)],
        compiler_params=pltpu.CompilerParams(dimension_semantics=("parallel",), vmem_limit_bytes=VMEM_LIMIT),
        name="gla",
    )(gq, gk, gv, lg, gg, gla_out_gain[None, :])

    TQ = 256
    nq = seq // TQ
    mla_out = pl.pallas_call(
        _attn_kernel,
        grid=(batch, nq),
        in_specs=[pl.BlockSpec((1, MLA_HEADS, TQ, HEAD_TILE), lambda b, i: (b, 0, i, 0)),
                  pl.BlockSpec((1, MLA_HEADS, seq, HEAD_TILE), lambda b, i: (b, 0, 0, 0)),
                  pl.BlockSpec((1, MLA_WIDTH, seq), lambda b, i: (b, 0, 0))],
        out_specs=pl.BlockSpec((1, MLA_WIDTH, TQ), lambda b, i: (b, 0, i)),
        out_shape=jax.ShapeDtypeStruct((batch, MLA_WIDTH, seq), BF16),
        compiler_params=pltpu.CompilerParams(dimension_semantics=("parallel", "parallel"),
                                             vmem_limit_bytes=VMEM_LIMIT),
        name="attn",
    )(mq, mk, mv)

    row_gap = ROUTER_EXPERT_ROW - N_GROUPS
    row_tail = ROUTER_ROWS - ROUTER_EXPERT_ROW - N_EXPERTS
    wr = jnp.concatenate([w_router_group, jnp.zeros((D, row_gap), F32), w_router_expert,
                          jnp.zeros((D, row_tail), F32)], axis=1).T
    br = jnp.concatenate([b_router_group, jnp.zeros((row_gap,), F32), b_router_expert,
                          jnp.zeros((row_tail,), F32)])[:, None]
    tb = min(MOE_SUPER_BLOCK, T)
    nsb = T // tb
    TO = min(OUTPROJ_TILE, seq)
    tiles_per_sb = tb // TO
    tok = lambda w: pl.BlockSpec((TO, w), lambda i: (i, 0))
    tri = jnp.asarray(np.triu(np.ones((TO, TO), np.float32)), BF16)
    x1, h2, route, cnt = pl.pallas_call(
        functools.partial(_outproj_kernel, tiles_per_sb),
        grid=(T // TO,),
        in_specs=[tok(D), tok(GLA_WIDTH),
                  pl.BlockSpec((1, MLA_WIDTH, TO), lambda i: (i // (seq // TO), 0, i % (seq // TO))),
                  _full((D, D)), _full((1, D)),
                  _full((ROUTER_ROWS, D)), _full((ROUTER_ROWS, 1)), _full((TO, TO))],
        out_specs=[tok(D), tok(D), pl.BlockSpec((8, TO), lambda i: (0, i)),
                   pl.BlockSpec((1, N_EXPERTS, LANES), lambda i: (i // tiles_per_sb, 0, 0))],
        out_shape=[jax.ShapeDtypeStruct((T, D), F32), jax.ShapeDtypeStruct((T, D), BF16),
                   jax.ShapeDtypeStruct((8, T), F32), jax.ShapeDtypeStruct((nsb, N_EXPERTS, LANES), F32)],
        scratch_shapes=[pltpu.VMEM((N_EXPERTS, 1), F32), pltpu.VMEM((D, D), BF16)],
        compiler_params=pltpu.CompilerParams(dimension_semantics=("arbitrary",), vmem_limit_bytes=VMEM_LIMIT),
        name="outproj",
    )(x2, gla_out, mla_out, w_out, norm2_gain[None, :], wr.astype(BF16), br, tri)

    strict = jnp.asarray(np.tril(np.ones((N_EXPERTS, N_EXPERTS), np.float32), -1), BF16)
    pos, offs = pl.pallas_call(
        _slots_kernel,
        grid=(nsb,),
        in_specs=[pl.BlockSpec((8, tb), lambda s: (0, s)), pl.BlockSpec((1, N_EXPERTS, LANES), lambda s: (s, 0, 0)),
                  _full((N_EXPERTS, N_EXPERTS))],
        out_specs=[pl.BlockSpec((1, 2, tb), lambda s: (s, 0, 0)),
                   pl.BlockSpec((1, N_EXPERTS, LANES), lambda s: (s, 0, 0))],
        out_shape=[jax.ShapeDtypeStruct((nsb, 2, tb), jnp.int32),
                   jax.ShapeDtypeStruct((nsb, N_EXPERTS, LANES), jnp.int32)],
        compiler_params=pltpu.CompilerParams(dimension_semantics=("parallel",)),
        name="slots",
    )(route, cnt, strict)
    pos = pos.reshape(-1)
    flat_off = offs[:, :, 0].reshape(-1)
    counts = cnt[:, :, 0].astype(jnp.int32)
    wge = w_expert_gate.reshape(N_EXPERTS, D, D_EXPERT)
    wue = w_expert_up.reshape(N_EXPERTS, D, D_EXPERT)
    wde = w_expert_down.reshape(N_EXPERTS, D_EXPERT, D)
    nct = tb // MOE_COMBINE_TILE
    nst = tb // MOE_SCATTER_TILE
    slots =(2 * tb + N_EXPERTS + MOE_CHUNK + 7) // 8 * 8
    hidx = lambda sb, j, off, cn: (sb * nst + jnp.minimum(j, nst - 1), 0)
    eidx = lambda sb, j, off, cn: (jnp.clip(j - (nst + 1), 0, N_EXPERTS - 1), 0, 0)
    tidx = lambda sb, j, off, cn: (sb * nct + jnp.clip(j - (nst + 1 + N_EXPERTS) - 1, 0, nct - 1), 0)
    out = pl.pallas_call(
        functools.partial(_moe_kernel, tb),
        grid_spec=pltpu.PrefetchScalarGridSpec(
            num_scalar_prefetch=2,
            grid=(nsb, (nst + 1) + N_EXPERTS + (nct + 1)),
            in_specs=[pl.BlockSpec((MOE_SCATTER_TILE, D), hidx),
                      pl.BlockSpec((2 * tb,), lambda sb, j, off, cn: (sb,), memory_space=pltpu.SMEM),
                      pl.BlockSpec((1, D, D_EXPERT), eidx), pl.BlockSpec((1, D, D_EXPERT), eidx),
                      pl.BlockSpec((1, D_EXPERT, D), eidx),
                      pl.BlockSpec((MOE_COMBINE_TILE, D), tidx),
                      pl.BlockSpec((8, MOE_COMBINE_TILE), lambda sb, j, off, cn: (0, tidx(sb, j, off, cn)[0]))],
            out_specs=pl.BlockSpec((MOE_COMBINE_TILE, D), tidx),
            scratch_shapes=[pltpu.VMEM((slots * PACK_ROWS, LANES), U32),
                            pltpu.VMEM((MOE_SCATTER_TILE * PACK_ROWS, LANES), U32),
                            pltpu.VMEM((MOE_SCATTER_TILE * PACK_ROWS, LANES), U32)]),
        out_shape=jax.ShapeDtypeStruct((T, D), F32),
        compiler_params=pltpu.CompilerParams(dimension_semantics=("arbitrary", "arbitrary"),
                                             vmem_limit_bytes=MOE_VMEM_LIMIT),
        name="moe",
    )(flat_off, counts.reshape(-1), h2, pos, wge, wue, wde, x1, route)
    return out


def kernel(x, positions, norm1_gain, w_in, gla_gk_fwd_w, gla_gk_fwd_b, gla_gk_bwd_w, gla_gk_bwd_b, gla_out_gain, mla_q_gain, mla_w_qb, mla_kv_gain, mla_w_kvb, q_norm_gain, k_norm_gain, w_out, norm2_gain, w_router_group, b_router_group, w_router_expert, b_router_expert, w_expert_gate, w_expert_up, w_expert_down):
    batch, seq, d = x.shape
    x2 = x.reshape(batch * seq, d)
    pos2 = positions.reshape(batch * seq, 1)
    for l in range(norm1_gain.shape[0]):
        x2 = _layer(x2, pos2, batch, seq, norm1_gain[l], w_in[l], gla_gk_fwd_w[l], gla_gk_fwd_b[l],
                    gla_gk_bwd_w[l], gla_gk_bwd_b[l], gla_out_gain[l], mla_q_gain[l], mla_w_qb[l],
                    mla_kv_gain[l], mla_w_kvb[l], q_norm_gain[l], k_norm_gain[l], w_out[l], norm2_gain[l],
                    w_router_group[l], b_router_group[l], w_router_expert[l], b_router_expert[l],
                    w_expert_gate[l], w_expert_up[l], w_expert_down[l])
    return x2.reshape(batch, seq, d)
```

```python
import functools

import numpy as np
import jax
import jax.numpy as jnp
from jax import lax
from jax.experimental import pallas as pl
from jax.experimental.pallas import tpu as pltpu

F32 = jnp.float32
BF16 = jnp.bfloat16

D_MODEL = 1024
GLA_HEADS = 4
GLA_DK = 64
GLA_DV = 128
GLA_GATE_RANK = 16
GLA_GATE_NORMALIZER = 16.0
GLA_CHUNK = 64
GLA_CHUNK_UNROLL = 16
MLA_HEADS = 8
MLA_NOPE = 64
MLA_ROPE = 32
MLA_QK = MLA_NOPE + MLA_ROPE
MLA_V = 64
MLA_Q_RANK = 256
MLA_KV_RANK = 128
ROPE_BASE = 10000.0
GLA_QK_WIDTH = GLA_HEADS * GLA_DK
GLA_WIDTH = GLA_HEADS * GLA_DV
MLA_WIDTH = MLA_HEADS * MLA_V
N_GROUPS = 4
EXPERTS_PER_GROUP = 8
N_EXPERTS = N_GROUPS * EXPERTS_PER_GROUP
D_EXPERT = 256
EPS = 1e-6

LANES = 128
HEAD_TILE = LANES
ROPE_LO = MLA_NOPE
ROPE_HALF = MLA_ROPE // 2
LOG2_E = 1.4426950408889634
ATTN_SCORES_AHEAD = 3
ONES_ROWS = 16

COL_GQ = 0
COL_GK = COL_GQ + GLA_QK_WIDTH
COL_GV = COL_GK + GLA_QK_WIDTH
COL_GG = COL_GV + GLA_WIDTH
COL_GLA_END = COL_GG + GLA_WIDTH
TAIL_MQ = 0
TAIL_MKV = TAIL_MQ + MLA_Q_RANK
TAIL_MISC = TAIL_MKV + MLA_KV_RANK
TAIL_WIDTH = TAIL_MISC + LANES

ROUTER_ROWS = LANES
ROUTER_EXPERT_ROW = 8

U32 = jnp.uint32
HI_HALF_MASK = np.uint32(0xFFFF0000)
PACK_ROWS = D_MODEL // (2 * LANES)
MOE_SUPER_BLOCK = 8192
MOE_CHUNK = 576
MOE_SCATTER_TILE = 1024
MOE_COMBINE_TILE = 512
PROJ_TILE = 1024
OUTPROJ_TILE = 1024
assert 2 * MOE_COMBINE_TILE == MOE_SCATTER_TILE

VMEM_LIMIT = 56 * 1024 * 1024
MOE_VMEM_LIMIT = 59900 * 1024


def _dot(a, b):
    return jnp.dot(a, b, preferred_element_type=F32)


def _dot_nt(a, b):
    return lax.dot_general(a, b, (((1,), (1,)), ((), ())), preferred_element_type=F32)


def _dot_tn(a, b):
    return lax.dot_general(a, b, (((0,), (0,)), ((), ())), preferred_element_type=F32)


def _split_bf16(x):
    hi = x.astype(BF16)
    lo = (x - hi.astype(F32)).astype(BF16)
    return hi, lo


def _proj_kernel(x_ref, pos_ref, n1_ref, win_ref, wtail_ref, wg_ref, bg_ref, qgain_ref, wqb_ref, wqbr_ref, kvgain_ref,
                 wkvk_ref, wkvv_ref, qng_ref, qngr_ref, kng_ref, kngr_ref, freq_ref, place_ref, ones_ref,
                 gq_ref, gk_ref, gv_ref, gg_ref, lg_ref, mq_ref, mk_ref, mv_ref):
    x = x_ref[...]
    h = x * lax.rsqrt(jnp.mean(x * x, axis=-1, keepdims=True) + EPS) * n1_ref[...]
    hb = h.astype(BF16)
    proj = _dot(hb, win_ref[...])
    tail = _dot(hb, wtail_ref[...])

    gq_ref[...] = (proj[:, COL_GQ:COL_GK] * GLA_DK ** -0.5).astype(BF16)
    gk_ref[...] = proj[:, COL_GK:COL_GV].astype(BF16)
    gv_ref[...] = proj[:, COL_GV:COL_GG].astype(BF16)
    gg_ref[...] = proj[:, COL_GG:COL_GLA_END].astype(BF16)

    misc = tail[:, TAIL_MISC:TAIL_WIDTH]
    z = _dot(misc.astype(BF16), wg_ref[...]) + bg_ref[...]
    log_sig = -(jnp.maximum(-z, 0.0) + jnp.log(1.0 + jnp.exp(-jnp.abs(z))))
    lg_ref[...] = log_sig / GLA_GATE_NORMALIZER

    lane = lax.broadcasted_iota(jnp.int32, (1, LANES), 1)
    ang_t = freq_ref[...] * pos_ref[0].astype(F32)
    cos_hi, cos_lo = _split_bf16(jnp.cos(ang_t))
    sin_hi, sin_lo = _split_bf16(jnp.sin(ang_t))
    tabs = _dot_tn(jnp.concatenate([cos_hi, cos_lo, sin_hi, sin_lo], axis=0), place_ref[...])
    in_rope = (lane >= ROPE_LO) & (lane < ROPE_LO + MLA_ROPE)
    c_tab = jnp.where(lane < ROPE_LO, 1.0, tabs[:, 0:LANES])
    s_tab = tabs[:, LANES:2 * LANES]
    ones_bd = ones_ref[...]

    def norm_rope_pair(t2, r2, gc, gs, on_mxu):
        if on_mxu:
            ss = _dot((t2 * t2).astype(BF16), ones_bd) * (1.0 / MLA_QK)
        outs = []
        for i in range(2):
            sl = slice(i * LANES, (i + 1) * LANES)
            t = t2[:, sl]
            ssi = ss[:, sl] if on_mxu else jnp.sum(t * t, axis=-1, keepdims=True) * (1.0 / MLA_QK)
            outs.append(lax.rsqrt(ssi + EPS) * (t * gc + r2[:, sl] * gs))
        return outs

    qa = tail[:, TAIL_MQ:TAIL_MKV]
    qn = qa * lax.rsqrt(jnp.mean(qa * qa, axis=-1, keepdims=True) + EPS) * qgain_ref[...]
    qn = qn.astype(BF16)
    mq = _dot(qn, wqb_ref[...])
    mq_rot = _dot(qn, wqbr_ref[...])
    kva = tail[:, TAIL_MKV:TAIL_MISC]
    kvn = (kva * lax.rsqrt(jnp.mean(kva * kva, axis=-1, keepdims=True) + EPS) * kvgain_ref[...]).astype(BF16)
    kn = _dot(kvn, wkvk_ref[...])
    mv_ref[0] = _dot_nt(wkvv_ref[...], kvn).astype(BF16)
    rope_tile = jnp.where(in_rope, misc, 0.0)
    up = (lane >= ROPE_LO + ROPE_HALF) & (lane < ROPE_LO + MLA_ROPE)
    dn = (lane >= ROPE_LO) & (lane < ROPE_LO + ROPE_HALF)
    rope_rot = (jnp.where(up, pltpu.roll(rope_tile, ROPE_HALF, 1), 0.0)
                - jnp.where(dn, pltpu.roll(rope_tile, LANES - ROPE_HALF, 1), 0.0))
    rope2 = jnp.concatenate([rope_tile, rope_tile], axis=1)
    rope_rot2 = jnp.concatenate([rope_rot, rope_rot], axis=1)
    scale = MLA_QK ** -0.5 * LOG2_E
    q_gc, q_gs = (qng_ref[...] * scale) * c_tab, (qngr_ref[...] * scale) * s_tab
    k_gc, k_gs = kng_ref[...] * c_tab, kngr_ref[...] * s_tab
    for hp in range(MLA_HEADS // 2):
        sl = slice(hp * 2 * HEAD_TILE, (hp + 1) * 2 * HEAD_TILE)
        q2 = norm_rope_pair(mq[:, sl], mq_rot[:, sl], q_gc, q_gs, on_mxu=True)
        k2 = norm_rope_pair(kn[:, sl] + rope2, rope_rot2, k_gc, k_gs, on_mxu=False)
        for i in range(2):
            mq_ref[0, 2 * hp + i] = q2[i].astype(BF16)
            mk_ref[0, 2 * hp + i] = k2[i].astype(BF16)


def _gla_kernel(q_ref, k_ref, v_ref, lg_ref, gate_ref, gain_ref, o_ref, accf_ref, accb_ref, stf_ref, stb_ref):
    seq = q_ref.shape[0]
    C = GLA_CHUNK
    n_chunks = seq // C
    HK = GLA_QK_WIDTH

    lane_head = lax.broadcasted_iota(jnp.int32, (1, HK), 1) // GLA_DK
    rowi = lax.broadcasted_iota(jnp.int32, (C, 1), 0)
    srow = lax.broadcasted_iota(jnp.int32, (GLA_HEADS * C, C), 0) % C
    scol = lax.broadcasted_iota(jnp.int32, (GLA_HEADS * C, C), 1)
    gain = gain_ref[...]
    fwd_cfg = (True, C // 2 - 1, C - 1, scol <= srow, 0)
    bwd_cfg = (False, C // 2, 0, scol > srow, HK)

    def stack_heads(t):
        return jnp.concatenate([jnp.where(lane_head == hd, t, 0.0) for hd in range(GLA_HEADS)], axis=0)

    def chunk_step(c, st, cfg, acc_ref, other_acc_ref=None):
        prefix, ref_i, last_i, keep, lg_off = cfg
        rows = pl.ds(pl.multiple_of(c * C, C), C)
        b = lg_ref[rows, lg_off:lg_off + HK]
        shift = 1
        while shift < C:
            if prefix:
                b = b + jnp.where(rowi >= shift, pltpu.roll(b, shift, 0), 0.0)
            else:
                b = b + jnp.where(rowi < C - shift, pltpu.roll(b, C - shift, 0), 0.0)
            shift *= 2
        b_ref = b[ref_i:ref_i + 1, :]
        b_last = b[last_i:last_i + 1, :]
        q = q_ref[rows, :].astype(F32)
        k = k_ref[rows, :].astype(F32)
        v = v_ref[rows, :]
        q_rel = q * jnp.exp(b - b_ref)
        k_rel = (k * jnp.exp(b_ref - b)).astype(BF16)
        k_dec = (k * jnp.exp(b_last - b)).astype(BF16)
        q_dec = q * jnp.exp(b)
        sc = _dot_nt(stack_heads(q_rel).astype(BF16), k_rel)
        sc = jnp.where(keep, sc, 0.0).astype(BF16)
        o_inter = _dot_nt(stack_heads(q_dec).astype(BF16), st.astype(BF16))
        kv_t = _dot_tn(v, k_dec)
        new_st = jnp.exp(b_last) * st
        for hd in range(GLA_HEADS):
            vs = slice(hd * GLA_DV, (hd + 1) * GLA_DV)
            rs = slice(hd * C, (hd + 1) * C)
            o_h = _dot(sc[rs, :], v[:, vs]) + o_inter[rs, :]
            if other_acc_ref is None:
                acc_ref[rows, vs] = o_h
            else:
                o = o_h + other_acc_ref[rows, vs]
                on = o * lax.rsqrt(jnp.mean(o * o, axis=-1, keepdims=True) + EPS) * gain
                gt = gate_ref[rows, vs].astype(F32)
                o_ref[rows, vs] = (on * (gt * jax.nn.sigmoid(gt))).astype(BF16)
            new_st = new_st + jnp.where(lane_head == hd, kv_t[vs, :], 0.0)
        return new_st

    stf_ref[...] = jnp.zeros_like(stf_ref)
    stb_ref[...] = jnp.zeros_like(stb_ref)

    unroll = min(GLA_CHUNK_UNROLL, n_chunks // 2)
    n_iter = n_chunks // unroll

    def make_body(finalize):
        def body(i, carry):
            st_f = stf_ref[...]
            st_b = stb_ref[...]
            for u in range(unroll):
                cf = i * unroll + u
                st_f = chunk_step(cf, st_f, fwd_cfg, accf_ref, accb_ref if finalize else None)
                st_b = chunk_step(n_chunks - 1 - cf, st_b, bwd_cfg, accb_ref, accf_ref if finalize else None)
            stf_ref[...] = st_f
            stb_ref[...] = st_b
            return carry
        return body

    lax.fori_loop(0, n_iter // 2, make_body(False), 0)
    lax.fori_loop(n_iter // 2, n_iter, make_body(True), 0)


def _attn_kernel(q_ref, k_ref, vt_ref, o_ref):
    seq = k_ref.shape[2]
    ones = jnp.ones((ONES_ROWS, seq), BF16)
    outs = []
    scores = [_dot_nt(q_ref[0, h], k_ref[0, h]) for h in range(ATTN_SCORES_AHEAD)]
    for hd in range(MLA_HEADS):
        if hd + ATTN_SCORES_AHEAD < MLA_HEADS:
            scores.append(_dot_nt(q_ref[0, hd + ATTN_SCORES_AHEAD], k_ref[0, hd + ATTN_SCORES_AHEAD]))
        s = scores[hd]
        p = jnp.exp2(s - jnp.max(s, axis=-1, keepdims=True)).astype(BF16)
        vt_aug = jnp.concatenate([vt_ref[0, hd * MLA_V:(hd + 1) * MLA_V, :], ones], axis=0)
        ot = _dot_nt(vt_aug, p)
        outs.append(ot[0:MLA_V, :] / ot[MLA_V:MLA_V + 1, :])
    o_ref[0] = jnp.concatenate(outs, axis=0).astype(BF16)


def _unpack_bf16_pair(w):
    lo = pltpu.unpack_elementwise(w, index=0, packed_dtype=BF16, unpacked_dtype=F32)
    hi = pltpu.unpack_elementwise(w, index=1, packed_dtype=BF16, unpacked_dtype=F32)
    return lo, hi


def _pack_bf16_pair(lo, hi):
    return pltpu.pack_elementwise([lo, hi], packed_dtype=BF16)


def _outproj_kernel(tiles_per_sb, x_ref, gla_ref, mla_ref, wo_ref, n2_ref, wr_ref, br_ref, tri_ref,
                    x1_ref, h2_ref, route_ref, cnt_ref, carry_ref, wob_ref):
    @pl.when(pl.program_id(0) == 0)
    def _():
        wob_ref[...] = wo_ref[...].astype(BF16)

    mix = _dot(gla_ref[...], wob_ref[0:GLA_WIDTH, :]) + _dot_tn(mla_ref[0], wob_ref[GLA_WIDTH:, :])
    x1 = x_ref[...] + mix
    x1_ref[...] = x1
    h2 = x1 * lax.rsqrt(jnp.mean(x1 * x1, axis=-1, keepdims=True) + EPS) * n2_ref[...]
    h2b = h2.astype(BF16)
    h2_ref[...] = h2b

    logits = _dot_nt(wr_ref[...], h2b) + br_ref[...]
    tm = logits.shape[1]
    gl = logits[0:N_GROUPS, :]
    ge = jnp.exp(gl - jnp.max(gl, axis=0, keepdims=True))
    pg = ge / jnp.sum(ge, axis=0, keepdims=True)
    p_top = jnp.max(pg, axis=0, keepdims=True)
    gi = lax.broadcasted_iota(jnp.int32, (N_GROUPS, tm), 0)
    g_idx = jnp.min(jnp.where(pg == p_top, gi, N_GROUPS), axis=0, keepdims=True)
    sel = jnp.zeros((EXPERTS_PER_GROUP, tm), F32)
    for g in range(N_GROUPS):
        r0 = ROUTER_EXPERT_ROW + g * EXPERTS_PER_GROUP
        sel = sel + jnp.where(g_idx == g, logits[r0:r0 + EXPERTS_PER_GROUP, :], 0.0)
    se = jnp.exp(sel - jnp.max(sel, axis=0, keepdims=True))
    pe = se / jnp.sum(se, axis=0, keepdims=True)
    ei = lax.broadcasted_iota(jnp.int32, (EXPERTS_PER_GROUP, tm), 0)
    m1 = jnp.max(pe, axis=0, keepdims=True)
    i1 = jnp.min(jnp.where(pe == m1, ei, EXPERTS_PER_GROUP), axis=0, keepdims=True)
    pe2 = jnp.where(ei == i1, -1.0, pe)
    m2 = jnp.max(pe2, axis=0, keepdims=True)
    i2 = jnp.min(jnp.where(pe2 == m2, ei, EXPERTS_PER_GROUP), axis=0, keepdims=True)
    den = m1 + m2
    e1 = g_idx * EXPERTS_PER_GROUP + i1
    e2 = g_idx * EXPERTS_PER_GROUP + i2

    @pl.when(pl.program_id(0) % tiles_per_sb == 0)
    def _():
        carry_ref[...] = jnp.zeros_like(carry_ref)

    eall = lax.broadcasted_iota(jnp.int32, (N_EXPERTS, tm), 0)
    tri = tri_ref[...]
    carry = carry_ref[...]
    ranks = []
    for eid in (e1, e2):
        hot = eall == eid
        prefix = _dot(hot.astype(BF16), tri)
        ranks.append(jnp.sum(jnp.where(hot, carry + prefix - 1.0, 0.0), axis=0, keepdims=True))
        carry = carry + prefix[:, tm - 1:tm]
    carry_ref[...] = carry
    cnt_ref[0] = jnp.broadcast_to(carry, (N_EXPERTS, LANES))
    zeros = jnp.zeros((2, tm), F32)
    route_ref[...] = jnp.concatenate([e1.astype(F32), e2.astype(F32), p_top * (m1 / den), p_top * (m2 / den),
                                      ranks[0], ranks[1], zeros], axis=0)


def _slots_kernel(route_ref, cnt_ref, strict_ref, pos_ref, off_ref):
    counts = cnt_ref[0]
    padded = 2.0 * jnp.floor(counts * 0.5 + 0.5)
    hi = jnp.floor(padded * (1.0 / 256.0))
    lo = padded - 256.0 * hi
    strict = strict_ref[...]
    off = 256.0 * _dot(strict, hi.astype(BF16)) + _dot(strict, lo.astype(BF16))
    off_ref[0] = off.astype(jnp.int32)
    tb = route_ref.shape[1]
    eall = lax.broadcasted_iota(jnp.int32, (N_EXPERTS, tb), 0)
    off_col = off[:, 0:1]
    for s in range(2):
        hot = eall == route_ref[s:s + 1, :].astype(jnp.int32)
        base = jnp.sum(jnp.where(hot, off_col, 0.0), axis=0, keepdims=True)
        pos_ref[0, s:s + 1, :] = ((base + route_ref[4 + s:5 + s, :]) * PACK_ROWS).astype(jnp.int32)


def _moe_kernel(tb, off_ref, cnt_ref, h2_ref, pos_ref, wg_ref, wu_ref, wd_ref, x1_ref, wt_ref, o_ref,
                xy_ref, cba_ref, cbb_ref):
    sb = pl.program_id(0)
    j = pl.program_id(1)
    n_scatter = tb // MOE_SCATTER_TILE
    n_combine = tb // MOE_COMBINE_TILE
    first_expert_step = n_scatter + 1
    first_combine_step = first_expert_step + N_EXPERTS

    def pack(dst_ref):
        h = h2_ref[...].astype(F32)
        for r in range(PACK_ROWS):
            c0 = r * 2 * LANES
            dst_ref[pl.ds(r, MOE_SCATTER_TILE, stride=PACK_ROWS), :] = _pack_bf16_pair(
                h[:, c0:c0 + LANES], h[:, c0 + LANES:c0 + 2 * LANES])

    def scatter(src_ref, tile):
        t0 = tile * MOE_SCATTER_TILE
        for pair in range(MOE_SCATTER_TILE // 2):
            v = src_ref[pair * 2 * PACK_ROWS:(pair + 1) * 2 * PACK_ROWS, :]
            for half in range(2):
                piece = v[half * PACK_ROWS:(half + 1) * PACK_ROWS, :]
                for slot in range(2):
                    p = pos_ref[slot * tb + t0 + pair * 2 + half]
                    xy_ref[pl.ds(pl.multiple_of(p, PACK_ROWS), PACK_ROWS), :] = piece

    def gather(dst_ref, tile):
        t0 = tile * MOE_COMBINE_TILE
        for t in range(MOE_COMBINE_TILE):
            for slot in range(2):
                p = pos_ref[slot * tb + t0 + t]
                r0 = (slot * MOE_COMBINE_TILE + t) * PACK_ROWS
                dst_ref[r0:r0 + PACK_ROWS, :] = xy_ref[pl.ds(pl.multiple_of(p, PACK_ROWS), PACK_ROWS), :]

    def weighted_sum(src_ref):
        wt = wt_ref[...].T
        w1 = wt[:, 2:3]
        w2 = wt[:, 3:4]
        second = MOE_COMBINE_TILE * PACK_ROWS
        for r in range(PACK_ROWS):
            lo1, hi1 = _unpack_bf16_pair(src_ref[pl.ds(r, MOE_COMBINE_TILE, stride=PACK_ROWS), :])
            lo2, hi2 = _unpack_bf16_pair(src_ref[pl.ds(second + r, MOE_COMBINE_TILE, stride=PACK_ROWS), :])
            c0 = r * 2 * LANES
            o_ref[:, c0:c0 + LANES] = x1_ref[:, c0:c0 + LANES] + w1 * lo1 + w2 * lo2
            o_ref[:, c0 + LANES:c0 + 2 * LANES] = x1_ref[:, c0 + LANES:c0 + 2 * LANES] + w1 * hi1 + w2 * hi2

    bufs = (cba_ref, cbb_ref)

    @pl.when(j == 0)
    def _first_pack():
        zero_slot = jnp.zeros((PACK_ROWS, LANES), U32)
        for e in range(N_EXPERTS):
            p = (off_ref[sb * N_EXPERTS + e] + cnt_ref[sb * N_EXPERTS + e]) * PACK_ROWS
            xy_ref[pl.ds(pl.multiple_of(p, PACK_ROWS), PACK_ROWS), :] = zero_slot
        tail0 = 2 * tb * PACK_ROWS
        xy_ref[tail0:, :] = jnp.zeros((xy_ref.shape[0] - tail0, LANES), U32)
        pack(bufs[0])

    for parity in range(2):
        @pl.when((j >= 1) & (j < n_scatter) & (j % 2 == parity))
        def _scatter_and_pack():
            pack(bufs[parity])
            scatter(bufs[1 - parity], j - 1)

    @pl.when(j == n_scatter)
    def _last_scatter():
        scatter(bufs[(n_scatter - 1) % 2], n_scatter - 1)

    @pl.when((j >= first_expert_step) & (j < first_combine_step))
    def _experts():
        e = sb * N_EXPERTS + j - first_expert_step
        n = cnt_ref[e]
        base = off_ref[e]

        def body(i, c):
            r0 = base + i * MOE_CHUNK
            row0 = pl.multiple_of(r0 * PACK_ROWS, 2 * PACK_ROWS)
            halves = [_unpack_bf16_pair(xy_ref[pl.ds(row0 + r, MOE_CHUNK, stride=PACK_ROWS), :])
                      for r in range(PACK_ROWS)]
            a = jnp.zeros((MOE_CHUNK, D_EXPERT), F32)
            u = jnp.zeros((MOE_CHUNK, D_EXPERT), F32)
            for r in range(PACK_ROWS):
                lo, hi = halves[r]
                xb = jnp.concatenate([lo, hi], axis=1).astype(BF16)
                a = a + _dot(xb, wg_ref[0, r * 2 * LANES:(r + 1) * 2 * LANES, :].astype(BF16))
                u = u + _dot(xb, wu_ref[0, r * 2 * LANES:(r + 1) * 2 * LANES, :].astype(BF16))
            hid = ((a * jax.nn.sigmoid(a)) * u).astype(BF16)
            y = _dot(hid, wd_ref[0].astype(BF16))
            valid = (r0 + lax.broadcasted_iota(jnp.int32, (MOE_CHUNK, 1), 0)) < (base + n)
            for r in range(PACK_ROWS):
                c0 = r * 2 * LANES
                lo, hi = halves[r]
                xy_ref[pl.ds(row0 + r, MOE_CHUNK, stride=PACK_ROWS), :] = _pack_bf16_pair(
                    jnp.where(valid, y[:, c0:c0 + LANES], lo), jnp.where(valid, y[:, c0 + LANES:c0 + 2 * LANES], hi))
            return c

        lax.fori_loop(0, (n + MOE_CHUNK - 1) // MOE_CHUNK, body, 0)

    cstep = j - first_combine_step

    @pl.when(cstep == 0)
    def _first_gather():
        gather(bufs[0], 0)

    for parity in range(2):
        @pl.when((cstep >= 1) & (cstep < n_combine) & (cstep % 2 == parity))
        def _gather_and_sum():
            gather(bufs[parity], cstep)
            weighted_sum(bufs[1 - parity])

    @pl.when(cstep == n_combine)
    def _last_sum():
        weighted_sum(bufs[(n_combine - 1) % 2])


def _full(shape):
    return pl.BlockSpec(shape, lambda *_: (0,) * len(shape))


def _prep_weights(w_in, gk_fwd_w, gk_fwd_b, gk_bwd_w, gk_bwd_b, mla_w_qb, mla_w_kvb, q_norm_gain, k_norm_gain):
    splits = np.cumsum([0, GLA_QK_WIDTH, GLA_QK_WIDTH, GLA_WIDTH, GLA_WIDTH, GLA_GATE_RANK, GLA_GATE_RANK,
                        MLA_Q_RANK, MLA_KV_RANK, MLA_ROPE])
    d = w_in.shape[0]
    win = w_in[:, :splits[4]].astype(BF16)
    wt = w_in[:, splits[4]:]
    rel = splits - splits[4]
    gap = jnp.zeros((d, ROPE_LO - 2 * GLA_GATE_RANK), F32)
    end = jnp.zeros((d, LANES - ROPE_LO - MLA_ROPE), F32)
    wtail = jnp.concatenate([wt[:, rel[6]:rel[8]], wt[:, rel[4]:rel[6]], gap, wt[:, rel[8]:rel[9]], end],
                            axis=1).astype(BF16)

    zq = jnp.zeros((GLA_GATE_RANK, GLA_QK_WIDTH), F32)
    wg = jnp.concatenate([jnp.concatenate([gk_fwd_w, zq], axis=1), jnp.concatenate([zq, gk_bwd_w], axis=1),
                          jnp.zeros((LANES - 2 * GLA_GATE_RANK, 2 * GLA_QK_WIDTH), F32)], axis=0)
    bg = jnp.concatenate([gk_fwd_b, gk_bwd_b])[None, :]

    pad = HEAD_TILE - MLA_QK
    wqb = jnp.pad(mla_w_qb.reshape(MLA_Q_RANK, MLA_HEADS, MLA_QK), ((0, 0), (0, 0), (0, pad)))
    wqb = wqb.reshape(MLA_Q_RANK, MLA_HEADS * HEAD_TILE)
    wkv = mla_w_kvb.reshape(MLA_KV_RANK, MLA_HEADS, MLA_NOPE + MLA_V)
    wkvk = jnp.pad(wkv[:, :, :MLA_NOPE], ((0, 0), (0, 0), (0, HEAD_TILE - MLA_NOPE)))
    wkvk = wkvk.reshape(MLA_KV_RANK, MLA_HEADS * HEAD_TILE).astype(BF16)
    wkvv = wkv[:, :, MLA_NOPE:].reshape(MLA_KV_RANK, MLA_WIDTH).T.astype(BF16)
    qng = jnp.pad(q_norm_gain, (0, pad))[None, :]
    kng = jnp.pad(k_norm_gain, (0, pad))[None, :]
    def partner(w, sign):
        first, second = w[..., ROPE_LO:ROPE_LO + ROPE_HALF], w[..., ROPE_LO + ROPE_HALF:ROPE_LO + MLA_ROPE]
        return jnp.concatenate([jnp.zeros_like(w[..., :ROPE_LO]), sign * second, first,
                                jnp.zeros_like(w[..., ROPE_LO + MLA_ROPE:])], axis=-1)

    wqb_rot = partner(wqb.reshape(MLA_Q_RANK, MLA_HEADS, HEAD_TILE), -1.0).reshape(MLA_Q_RANK, -1).astype(BF16)
    qng_rot = partner(qng, 1.0)
    kng_rot = partner(kng, 1.0)
    return win, wtail, wg.astype(BF16), bg, wqb.astype(BF16), wqb_rot, wkvk, wkvv, qng, qng_rot, kng, kng_rot


def _rope_consts():
    inv = ROPE_BASE ** (-np.arange(0, MLA_ROPE, 2, dtype=np.float32) / MLA_ROPE)
    one = np.zeros((ROPE_HALF, LANES), np.float32)
    one[np.arange(ROPE_HALF), ROPE_LO + np.arange(ROPE_HALF)] = 1.0
    one[np.arange(ROPE_HALF), ROPE_LO + ROPE_HALF + np.arange(ROPE_HALF)] = 1.0
    zero = np.zeros_like(one)
    place = np.block([[one, zero], [one, zero], [zero, one], [zero, one]])
    ones_bd = np.kron(np.eye(2, dtype=np.float32), np.ones((LANES, LANES), np.float32))
    return jnp.asarray(inv[:, None]), jnp.asarray(place, BF16), jnp.asarray(ones_bd, BF16)


def _layer(x2, pos2, batch, seq, norm1_gain, w_in, gk_fwd_w, gk_fwd_b, gk_bwd_w, gk_bwd_b, gla_out_gain,
           mla_q_gain, mla_w_qb, mla_kv_gain, mla_w_kvb, q_norm_gain, k_norm_gain, w_out, norm2_gain,
           w_router_group, b_router_group, w_router_expert, b_router_expert,
           w_expert_gate, w_expert_up, w_expert_down):
    T, D = x2.shape
    win, wtail, wg, bg, wqb, wqb_rot, wkvk, wkvv, qng, qng_rot, kng, kng_rot = _prep_weights(
        w_in, gk_fwd_w, gk_fwd_b, gk_bwd_w, gk_bwd_b, mla_w_qb, mla_w_kvb, q_norm_gain, k_norm_gain)

    TM = min(PROJ_TILE, seq)
    spb = seq // TM
    tok = lambda w: pl.BlockSpec((TM, w), lambda i: (i, 0))
    head_spec = pl.BlockSpec((1, MLA_HEADS, TM, HEAD_TILE), lambda i: (i // spb, 0, i % spb, 0))
    gq, gk, gv, gg, lg, mq, mk, mv = pl.pallas_call(
        _proj_kernel,
        grid=(T // TM,),
        in_specs=[tok(D), pl.BlockSpec((1, 1, TM), lambda i: (i, 0, 0)), _full((1, D)), _full((D, COL_GLA_END)),
                  _full((D, TAIL_WIDTH)),
                  _full((LANES, 2 * GLA_QK_WIDTH)), _full((1, 2 * GLA_QK_WIDTH)), _full((1, MLA_Q_RANK)),
                  _full((MLA_Q_RANK, MLA_HEADS * HEAD_TILE)), _full((MLA_Q_RANK, MLA_HEADS * HEAD_TILE)),
                  _full((1, MLA_KV_RANK)), _full((MLA_KV_RANK, MLA_HEADS * HEAD_TILE)),
                  _full((MLA_WIDTH, MLA_KV_RANK)), _full((1, HEAD_TILE)), _full((1, HEAD_TILE)),
                  _full((1, HEAD_TILE)), _full((1, HEAD_TILE)),
                  _full((ROPE_HALF, 1)), _full((4 * ROPE_HALF, 2 * LANES)), _full((2 * LANES, 2 * LANES))],
        out_specs=[tok(GLA_QK_WIDTH), tok(GLA_QK_WIDTH), tok(GLA_WIDTH), tok(GLA_WIDTH), tok(2 * GLA_QK_WIDTH),
                   head_spec, head_spec, pl.BlockSpec((1, MLA_WIDTH, TM), lambda i: (i // spb, 0, i % spb))],
        out_shape=[jax.ShapeDtypeStruct((T, GLA_QK_WIDTH), BF16), jax.ShapeDtypeStruct((T, GLA_QK_WIDTH), BF16),
                   jax.ShapeDtypeStruct((T, GLA_WIDTH), BF16), jax.ShapeDtypeStruct((T, GLA_WIDTH), BF16),
                   jax.ShapeDtypeStruct((T, 2 * GLA_QK_WIDTH), F32),
                   jax.ShapeDtypeStruct((batch, MLA_HEADS, seq, HEAD_TILE), BF16),
                   jax.ShapeDtypeStruct((batch, MLA_HEADS, seq, HEAD_TILE), BF16),
                   jax.ShapeDtypeStruct((batch, MLA_WIDTH, seq), BF16)],
        compiler_params=pltpu.CompilerParams(dimension_semantics=("parallel",), vmem_limit_bytes=VMEM_LIMIT),
        name="proj",
    )(x2, pos2.reshape(T // TM, 1, TM), norm1_gain[None, :], win, wtail, wg, bg, mla_q_gain[None, :], wqb, wqb_rot,
      mla_kv_gain[None, :], wkvk, wkvv, qng, qng_rot, kng, kng_rot, *_rope_consts())

    seqspec = lambda w: pl.BlockSpec((seq, w), lambda b: (b, 0))
    gla_out = pl.pallas_call(
        _gla_kernel,
        grid=(batch,),
        in_specs=[seqspec(GLA_QK_WIDTH), seqspec(GLA_QK_WIDTH), seqspec(GLA_WIDTH), seqspec(2 * GLA_QK_WIDTH),
                  seqspec(GLA_WIDTH), _full((1, GLA_DV))],
        out_specs=seqspec(GLA_WIDTH),
        out_shape=jax.ShapeDtypeStruct((T, GLA_WIDTH), BF16),
        scratch_shapes=[pltpu.VMEM((seq, GLA_WIDTH), F32), pltpu.VMEM((seq, GLA_WIDTH), F32),
                        pltpu.VMEM((GLA_DV, GLA_QK_WIDTH), F32), pltpu.VMEM((GLA_DV, GLA_QK_WIDTH), F32)],
        compiler_params=pltpu.CompilerParams(dimension_semantics=("parallel",), vmem_limit_bytes=VMEM_LIMIT),
        name="gla",
    )(gq, gk, gv, lg, gg, gla_out_gain[None, :])

    TQ = 256
    nq = seq // TQ
    mla_out = pl.pallas_call(
        _attn_kernel,
        grid=(batch, nq),
        in_specs=[pl.BlockSpec((1, MLA_HEADS, TQ, HEAD_TILE), lambda b, i: (b, 0, i, 0)),
                  pl.BlockSpec((1, MLA_HEADS, seq, HEAD_TILE), lambda b, i: (b, 0, 0, 0)),
                  pl.BlockSpec((1, MLA_WIDTH, seq), lambda b, i: (b, 0, 0))],
        out_specs=pl.BlockSpec((1, MLA_WIDTH, TQ), lambda b, i: (b, 0, i)),
        out_shape=jax.ShapeDtypeStruct((batch, MLA_WIDTH, seq), BF16),
        compiler_params=pltpu.CompilerParams(dimension_semantics=("parallel", "parallel"),
                                             vmem_limit_bytes=VMEM_LIMIT),
        name="attn",
    )(mq, mk, mv)

    row_gap = ROUTER_EXPERT_ROW - N_GROUPS
    row_tail = ROUTER_ROWS - ROUTER_EXPERT_ROW - N_EXPERTS
    wr = jnp.concatenate([w_router_group, jnp.zeros((D, row_gap), F32), w_router_expert,
                          jnp.zeros((D, row_tail), F32)], axis=1).T
    br = jnp.concatenate([b_router_group, jnp.zeros((row_gap,), F32), b_router_expert,
                          jnp.zeros((row_tail,), F32)])[:, None]
    tb = min(MOE_SUPER_BLOCK, T)
    nsb = T // tb
    TO = min(OUTPROJ_TILE, seq)
    tiles_per_sb = tb // TO
    tok = lambda w: pl.BlockSpec((TO, w), lambda i: (i, 0))
    tri = jnp.asarray(np.triu(np.ones((TO, TO), np.float32)), BF16)
    x1, h2, route, cnt = pl.pallas_call(
        functools.partial(_outproj_kernel, tiles_per_sb),
        grid=(T // TO,),
        in_specs=[tok(D), tok(GLA_WIDTH),
                  pl.BlockSpec((1, MLA_WIDTH, TO), lambda i: (i // (seq // TO), 0, i % (seq // TO))),
                  _full((D, D)), _full((1, D)),
                  _full((ROUTER_ROWS, D)), _full((ROUTER_ROWS, 1)), _full((TO, TO))],
        out_specs=[tok(D), tok(D), pl.BlockSpec((8, TO), lambda i: (0, i)),
                   pl.BlockSpec((1, N_EXPERTS, LANES), lambda i: (i // tiles_per_sb, 0, 0))],
        out_shape=[jax.ShapeDtypeStruct((T, D), F32), jax.ShapeDtypeStruct((T, D), BF16),
                   jax.ShapeDtypeStruct((8, T), F32), jax.ShapeDtypeStruct((nsb, N_EXPERTS, LANES), F32)],
        scratch_shapes=[pltpu.VMEM((N_EXPERTS, 1), F32), pltpu.VMEM((D, D), BF16)],
        compiler_params=pltpu.CompilerParams(dimension_semantics=("arbitrary",), vmem_limit_bytes=VMEM_LIMIT),
        name="outproj",
    )(x2, gla_out, mla_out, w_out, norm2_gain[None, :], wr.astype(BF16), br, tri)

    strict = jnp.asarray(np.tril(np.ones((N_EXPERTS, N_EXPERTS), np.float32), -1), BF16)
    pos, offs = pl.pallas_call(
        _slots_kernel,
        grid=(nsb,),
        in_specs=[pl.BlockSpec((8, tb), lambda s: (0, s)), pl.BlockSpec((1, N_EXPERTS, LANES), lambda s: (s, 0, 0)),
                  _full((N_EXPERTS, N_EXPERTS))],
        out_specs=[pl.BlockSpec((1, 2, tb), lambda s: (s, 0, 0)),
                   pl.BlockSpec((1, N_EXPERTS, LANES), lambda s: (s, 0, 0))],
        out_shape=[jax.ShapeDtypeStruct((nsb, 2, tb), jnp.int32),
                   jax.ShapeDtypeStruct((nsb, N_EXPERTS, LANES), jnp.int32)],
        compiler_params=pltpu.CompilerParams(dimension_semantics=("parallel",)),
        name="slots",
    )(route, cnt, strict)
    pos = pos.reshape(-1)
    flat_off = offs[:, :, 0].reshape(-1)
    counts = cnt[:, :, 0].astype(jnp.int32)
    wge = w_expert_gate.reshape(N_EXPERTS, D, D_EXPERT).astype(BF16)
    wue = w_expert_up.reshape(N_EXPERTS, D, D_EXPERT).astype(BF16)
    wde = w_expert_down.reshape(N_EXPERTS, D_EXPERT, D).astype(BF16)
    nct = tb // MOE_COMBINE_TILE
    nst = tb // MOE_SCATTER_TILE
    slots =(2 * tb + N_EXPERTS + MOE_CHUNK + 7) // 8 * 8
    hidx = lambda sb, j, off, cn: (sb * nst + jnp.minimum(j, nst - 1), 0)
    eidx = lambda sb, j, off, cn: (jnp.clip(j - (nst + 1), 0, N_EXPERTS - 1), 0, 0)
    tidx = lambda sb, j, off, cn: (sb * nct + jnp.clip(j - (nst + 1 + N_EXPERTS) - 1, 0, nct - 1), 0)
    out = pl.pallas_call(
        functools.partial(_moe_kernel, tb),
        grid_spec=pltpu.PrefetchScalarGridSpec(
            num_scalar_prefetch=2,
            grid=(nsb, (nst + 1) + N_EXPERTS + (nct + 1)),
            in_specs=[pl.BlockSpec((MOE_SCATTER_TILE, D), hidx),
                      pl.BlockSpec((2 * tb,), lambda sb, j, off, cn: (sb,), memory_space=pltpu.SMEM),
                      pl.BlockSpec((1, D, D_EXPERT), eidx), pl.BlockSpec((1, D, D_EXPERT), eidx),
                      pl.BlockSpec((1, D_EXPERT, D), eidx),
                      pl.BlockSpec((MOE_COMBINE_TILE, D), tidx),
                      pl.BlockSpec((8, MOE_COMBINE_TILE), lambda sb, j, off, cn: (0, tidx(sb, j, off, cn)[0]))],
            out_specs=pl.BlockSpec((MOE_COMBINE_TILE, D), tidx),
            scratch_shapes=[pltpu.VMEM((slots * PACK_ROWS, LANES), U32),
                            pltpu.VMEM((MOE_SCATTER_TILE * PACK_ROWS, LANES), U32),
                            pltpu.VMEM((MOE_SCATTER_TILE * PACK_ROWS, LANES), U32)]),
        out_shape=jax.ShapeDtypeStruct((T, D), F32),
        compiler_params=pltpu.CompilerParams(dimension_semantics=("arbitrary", "arbitrary"),
                                             vmem_limit_bytes=MOE_VMEM_LIMIT),
        name="moe",
    )(flat_off, counts.reshape(-1), h2, pos, wge, wue, wde, x1, route)
    return out


def kernel(x, positions, norm1_gain, w_in, gla_gk_fwd_w, gla_gk_fwd_b, gla_gk_bwd_w, gla_gk_bwd_b, gla_out_gain, mla_q_gain, mla_w_qb, mla_kv_gain, mla_w_kvb, q_norm_gain, k_norm_gain, w_out, norm2_gain, w_router_group, b_router_group, w_router_expert, b_router_expert, w_expert_gate, w_expert_up, w_expert_down):
    batch, seq, d = x.shape
    x2 = x.reshape(batch * seq, d)
    pos2 = positions.reshape(batch * seq, 1)
    for l in range(norm1_gain.shape[0]):
        x2 = _layer(x2, pos2, batch, seq, norm1_gain[l], w_in[l], gla_gk_fwd_w[l], gla_gk_fwd_b[l],
                    gla_gk_bwd_w[l], gla_gk_bwd_b[l], gla_out_gain[l], mla_q_gain[l], mla_w_qb[l],
                    mla_kv_gain[l], mla_w_kvb[l], q_norm_gain[l], k_norm_gain[l], w_out[l], norm2_gain[l],
                    w_router_group[l], b_router_group[l], w_router_expert[l], b_router_expert[l],
                    w_expert_gate[l], w_expert_up[l], w_expert_down[l])
    return x2.reshape(batch, seq, d)
```

```python
import functools

import numpy as np
import jax
import jax.numpy as jnp
from jax import lax
from jax.experimental import pallas as pl
from jax.experimental.pallas import tpu as pltpu

F32 = jnp.float32
BF16 = jnp.bfloat16

D_MODEL = 1024
GLA_HEADS = 4
GLA_DK = 64
GLA_DV = 128
GLA_GATE_RANK = 16
GLA_GATE_NORMALIZER = 16.0
GLA_CHUNK = 64
GLA_CHUNK_UNROLL = 16
MLA_HEADS = 8
MLA_NOPE = 64
MLA_ROPE = 32
MLA_QK = MLA_NOPE + MLA_ROPE
MLA_V = 64
MLA_Q_RANK = 256
MLA_KV_RANK = 128
ROPE_BASE = 10000.0
GLA_QK_WIDTH = GLA_HEADS * GLA_DK
GLA_WIDTH = GLA_HEADS * GLA_DV
MLA_WIDTH = MLA_HEADS * MLA_V
N_GROUPS = 4
EXPERTS_PER_GROUP = 8
N_EXPERTS = N_GROUPS * EXPERTS_PER_GROUP
D_EXPERT = 256
EPS = 1e-6

LANES = 128
HEAD_TILE = LANES
ROPE_LO = MLA_NOPE
ROPE_HALF = MLA_ROPE // 2
LOG2_E = 1.4426950408889634
ATTN_SCORES_AHEAD = 3
ONES_ROWS = 16

COL_GQ = 0
COL_GK = COL_GQ + GLA_QK_WIDTH
COL_GV = COL_GK + GLA_QK_WIDTH
COL_GG = COL_GV + GLA_WIDTH
COL_GLA_END = COL_GG + GLA_WIDTH
TAIL_MQ = 0
TAIL_MKV = TAIL_MQ + MLA_Q_RANK
TAIL_MISC = TAIL_MKV + MLA_KV_RANK
TAIL_WIDTH = TAIL_MISC + LANES

ROUTER_ROWS = LANES
ROUTER_EXPERT_ROW = 8

U32 = jnp.uint32
HI_HALF_MASK = np.uint32(0xFFFF0000)
PACK_ROWS = D_MODEL // (2 * LANES)
MOE_SUPER_BLOCK = 8192
MOE_CHUNK = 576
MOE_SCATTER_TILE = 1024
MOE_COMBINE_TILE = 512
PROJ_TILE = 1024
OUTPROJ_TILE = 1024
assert 2 * MOE_COMBINE_TILE == MOE_SCATTER_TILE

VMEM_LIMIT = 56 * 1024 * 1024
MOE_VMEM_LIMIT = 59900 * 1024


def _dot(a, b):
    return jnp.dot(a, b, preferred_element_type=F32)


def _dot_nt(a, b):
    return lax.dot_general(a, b, (((1,), (1,)), ((), ())), preferred_element_type=F32)


def _dot_tn(a, b):
    return lax.dot_general(a, b, (((0,), (0,)), ((), ())), preferred_element_type=F32)


def _split_bf16(x):
    hi = x.astype(BF16)
    lo = (x - hi.astype(F32)).astype(BF16)
    return hi, lo


def _proj_kernel(x_ref, pos_ref, n1_ref, win_ref, wtail_ref, wg_ref, bg_ref, qgain_ref, wqb_ref, wqbr_ref, kvgain_ref,
                 wkvk_ref, wkvv_ref, qng_ref, qngr_ref, kng_ref, kngr_ref, freq_ref, place_ref, ones_ref,
                 gq_ref, gk_ref, gv_ref, gg_ref, lg_ref, mq_ref, mk_ref, mv_ref):
    x = x_ref[...]
    h = x * lax.rsqrt(jnp.mean(x * x, axis=-1, keepdims=True) + EPS) * n1_ref[...]
    hb = h.astype(BF16)
    proj = _dot(hb, win_ref[...])
    tail = _dot(hb, wtail_ref[...])

    gq_ref[...] = (proj[:, COL_GQ:COL_GK] * GLA_DK ** -0.5).astype(BF16)
    gk_ref[...] = proj[:, COL_GK:COL_GV].astype(BF16)
    gv_ref[...] = proj[:, COL_GV:COL_GG].astype(BF16)
    gg_ref[...] = proj[:, COL_GG:COL_GLA_END].astype(BF16)

    misc = tail[:, TAIL_MISC:TAIL_WIDTH]
    z = _dot(misc.astype(BF16), wg_ref[...]) + bg_ref[...]
    log_sig = -(jnp.maximum(-z, 0.0) + jnp.log(1.0 + jnp.exp(-jnp.abs(z))))
    lg_ref[...] = log_sig / GLA_GATE_NORMALIZER

    lane = lax.broadcasted_iota(jnp.int32, (1, LANES), 1)
    ang_t = freq_ref[...] * pos_ref[0].astype(F32)
    cos_hi, cos_lo = _split_bf16(jnp.cos(ang_t))
    sin_hi, sin_lo = _split_bf16(jnp.sin(ang_t))
    tabs = _dot_tn(jnp.concatenate([cos_hi, cos_lo, sin_hi, sin_lo], axis=0), place_ref[...])
    in_rope = (lane >= ROPE_LO) & (lane < ROPE_LO + MLA_ROPE)
    c_tab = jnp.where(lane < ROPE_LO, 1.0, tabs[:, 0:LANES])
    s_tab = tabs[:, LANES:2 * LANES]
    ones_bd = ones_ref[...]

    def norm_rope_pair(t2, r2, gc, gs, on_mxu):
        if on_mxu:
            ss = _dot((t2 * t2).astype(BF16), ones_bd) * (1.0 / MLA_QK)
        outs = []
        for i in range(2):
            sl = slice(i * LANES, (i + 1) * LANES)
            t = t2[:, sl]
            ssi = ss[:, sl] if on_mxu else jnp.sum(t * t, axis=-1, keepdims=True) * (1.0 / MLA_QK)
            outs.append(lax.rsqrt(ssi + EPS) * (t * gc + r2[:, sl] * gs))
        return outs

    qa = tail[:, TAIL_MQ:TAIL_MKV]
    qn = qa * lax.rsqrt(jnp.mean(qa * qa, axis=-1, keepdims=True) + EPS) * qgain_ref[...]
    qn = qn.astype(BF16)
    mq = _dot(qn, wqb_ref[...])
    mq_rot = _dot(qn, wqbr_ref[...])
    kva = tail[:, TAIL_MKV:TAIL_MISC]
    kvn = (kva * lax.rsqrt(jnp.mean(kva * kva, axis=-1, keepdims=True) + EPS) * kvgain_ref[...]).astype(BF16)
    kn = _dot(kvn, wkvk_ref[...])
    mv_ref[0] = _dot_nt(wkvv_ref[...], kvn).astype(BF16)
    rope_tile = jnp.where(in_rope, misc, 0.0)
    up = (lane >= ROPE_LO + ROPE_HALF) & (lane < ROPE_LO + MLA_ROPE)
    dn = (lane >= ROPE_LO) & (lane < ROPE_LO + ROPE_HALF)
    rope_rot = (jnp.where(up, pltpu.roll(rope_tile, ROPE_HALF, 1), 0.0)
                - jnp.where(dn, pltpu.roll(rope_tile, LANES - ROPE_HALF, 1), 0.0))
    rope2 = jnp.concatenate([rope_tile, rope_tile], axis=1)
    rope_rot2 = jnp.concatenate([rope_rot, rope_rot], axis=1)
    scale = MLA_QK ** -0.5 * LOG2_E
    q_gc, q_gs = (qng_ref[...] * scale) * c_tab, (qngr_ref[...] * scale) * s_tab
    k_gc, k_gs = kng_ref[...] * c_tab, kngr_ref[...] * s_tab
    for hp in range(MLA_HEADS // 2):
        sl = slice(hp * 2 * HEAD_TILE, (hp + 1) * 2 * HEAD_TILE)
        q2 = norm_rope_pair(mq[:, sl], mq_rot[:, sl], q_gc, q_gs, on_mxu=True)
        k2 = norm_rope_pair(kn[:, sl] + rope2, rope_rot2, k_gc, k_gs, on_mxu=False)
        for i in range(2):
            mq_ref[0, 2 * hp + i] = q2[i].astype(BF16)
            mk_ref[0, 2 * hp + i] = k2[i].astype(BF16)


def _gla_kernel(q_ref, k_ref, v_ref, lg_ref, gate_ref, gain_ref, o_ref, accf_ref, accb_ref, stf_ref, stb_ref):
    seq = q_ref.shape[0]
    C = GLA_CHUNK
    n_chunks = seq // C
    HK = GLA_QK_WIDTH

    lane_head = lax.broadcasted_iota(jnp.int32, (1, HK), 1) // GLA_DK
    rowi = lax.broadcasted_iota(jnp.int32, (C, 1), 0)
    srow = lax.broadcasted_iota(jnp.int32, (GLA_HEADS * C, C), 0) % C
    scol = lax.broadcasted_iota(jnp.int32, (GLA_HEADS * C, C), 1)
    gain = gain_ref[...]
    fwd_cfg = (True, C // 2 - 1, C - 1, scol <= srow, 0)
    bwd_cfg = (False, C // 2, 0, scol > srow, HK)

    def stack_heads(t):
        return jnp.concatenate([jnp.where(lane_head == hd, t, 0.0) for hd in range(GLA_HEADS)], axis=0)

    def chunk_step(c, st, cfg, acc_ref, other_acc_ref=None):
        prefix, ref_i, last_i, keep, lg_off = cfg
        rows = pl.ds(pl.multiple_of(c * C, C), C)
        b = lg_ref[rows, lg_off:lg_off + HK]
        shift = 1
        while shift < C:
            if prefix:
                b = b + jnp.where(rowi >= shift, pltpu.roll(b, shift, 0), 0.0)
            else:
                b = b + jnp.where(rowi < C - shift, pltpu.roll(b, C - shift, 0), 0.0)
            shift *= 2
        b_ref = b[ref_i:ref_i + 1, :]
        b_last = b[last_i:last_i + 1, :]
        q = q_ref[rows, :].astype(F32)
        k = k_ref[rows, :].astype(F32)
        v = v_ref[rows, :]
        q_rel = q * jnp.exp(b - b_ref)
        k_rel = (k * jnp.exp(b_ref - b)).astype(BF16)
        k_dec = (k * jnp.exp(b_last - b)).astype(BF16)
        q_dec = q * jnp.exp(b)
        sc = _dot_nt(stack_heads(q_rel).astype(BF16), k_rel)
        sc = jnp.where(keep, sc, 0.0).astype(BF16)
        o_inter = _dot_nt(stack_heads(q_dec).astype(BF16), st.astype(BF16))
        kv_t = _dot_tn(v, k_dec)
        new_st = jnp.exp(b_last) * st
        for hd in range(GLA_HEADS):
            vs = slice(hd * GLA_DV, (hd + 1) * GLA_DV)
            rs = slice(hd * C, (hd + 1) * C)
            o_h = _dot(sc[rs, :], v[:, vs]) + o_inter[rs, :]
            if other_acc_ref is None:
                acc_ref[rows, vs] = o_h
            else:
                o = o_h + other_acc_ref[rows, vs]
                on = o * lax.rsqrt(jnp.mean(o * o, axis=-1, keepdims=True) + EPS) * gain
                gt = gate_ref[rows, vs].astype(F32)
                o_ref[rows, vs] = (on * (gt * jax.nn.sigmoid(gt))).astype(BF16)
            new_st = new_st + jnp.where(lane_head == hd, kv_t[vs, :], 0.0)
        return new_st

    stf_ref[...] = jnp.zeros_like(stf_ref)
    stb_ref[...] = jnp.zeros_like(stb_ref)

    unroll = min(GLA_CHUNK_UNROLL, n_chunks // 2)
    n_iter = n_chunks // unroll

    def make_body(finalize):
        def body(i, carry):
            st_f = stf_ref[...]
            st_b = stb_ref[...]
            for u in range(unroll):
                cf = i * unroll + u
                st_f = chunk_step(cf, st_f, fwd_cfg, accf_ref, accb_ref if finalize else None)
                st_b = chunk_step(n_chunks - 1 - cf, st_b, bwd_cfg, accb_ref, accf_ref if finalize else None)
            stf_ref[...] = st_f
            stb_ref[...] = st_b
            return carry
        return body

    lax.fori_loop(0, n_iter // 2, make_body(False), 0)
    lax.fori_loop(n_iter // 2, n_iter, make_body(True), 0)


def _attn_kernel(q_ref, k_ref, vt_ref, wg_ref, wu_ref, wd_ref, o_ref, wgb_ref, wub_ref, wdb_ref):
    wgb_ref[...] = wg_ref[...].astype(BF16)
    wub_ref[...] = wu_ref[...].astype(BF16)
    wdb_ref[...] = wd_ref[...].astype(BF16)
    seq = k_ref.shape[2]
    ones = jnp.ones((ONES_ROWS, seq), BF16)
    outs = []
    scores = [_dot_nt(q_ref[0, h], k_ref[0, h]) for h in range(ATTN_SCORES_AHEAD)]
    for hd in range(MLA_HEADS):
        if hd + ATTN_SCORES_AHEAD < MLA_HEADS:
            scores.append(_dot_nt(q_ref[0, hd + ATTN_SCORES_AHEAD], k_ref[0, hd + ATTN_SCORES_AHEAD]))
        s = scores[hd]
        p = jnp.exp2(s - jnp.max(s, axis=-1, keepdims=True)).astype(BF16)
        vt_aug = jnp.concatenate([vt_ref[0, hd * MLA_V:(hd + 1) * MLA_V, :], ones], axis=0)
        ot = _dot_nt(vt_aug, p)
        outs.append(ot[0:MLA_V, :] / ot[MLA_V:MLA_V + 1, :])
    o_ref[0] = jnp.concatenate(outs, axis=0).astype(BF16)


def _unpack_bf16_pair(w):
    lo = pltpu.unpack_elementwise(w, index=0, packed_dtype=BF16, unpacked_dtype=F32)
    hi = pltpu.unpack_elementwise(w, index=1, packed_dtype=BF16, unpacked_dtype=F32)
    return lo, hi


def _pack_bf16_pair(lo, hi):
    return pltpu.pack_elementwise([lo, hi], packed_dtype=BF16)


def _outproj_kernel(tiles_per_sb, x_ref, gla_ref, mla_ref, wo_ref, n2_ref, wr_ref, br_ref, tri_ref,
                    x1_ref, h2_ref, route_ref, cnt_ref, carry_ref, wob_ref):
    @pl.when(pl.program_id(0) == 0)
    def _():
        wob_ref[...] = wo_ref[...].astype(BF16)

    mix = _dot(gla_ref[...], wob_ref[0:GLA_WIDTH, :]) + _dot_tn(mla_ref[0], wob_ref[GLA_WIDTH:, :])
    x1 = x_ref[...] + mix
    x1_ref[...] = x1
    h2 = x1 * lax.rsqrt(jnp.mean(x1 * x1, axis=-1, keepdims=True) + EPS) * n2_ref[...]
    h2b = h2.astype(BF16)
    h2_ref[...] = h2b

    logits = _dot_nt(wr_ref[...], h2b) + br_ref[...]
    tm = logits.shape[1]
    gl = logits[0:N_GROUPS, :]
    ge = jnp.exp(gl - jnp.max(gl, axis=0, keepdims=True))
    pg = ge / jnp.sum(ge, axis=0, keepdims=True)
    p_top = jnp.max(pg, axis=0, keepdims=True)
    gi = lax.broadcasted_iota(jnp.int32, (N_GROUPS, tm), 0)
    g_idx = jnp.min(jnp.where(pg == p_top, gi, N_GROUPS), axis=0, keepdims=True)
    sel = jnp.zeros((EXPERTS_PER_GROUP, tm), F32)
    for g in range(N_GROUPS):
        r0 = ROUTER_EXPERT_ROW + g * EXPERTS_PER_GROUP
        sel = sel + jnp.where(g_idx == g, logits[r0:r0 + EXPERTS_PER_GROUP, :], 0.0)
    se = jnp.exp(sel - jnp.max(sel, axis=0, keepdims=True))
    pe = se / jnp.sum(se, axis=0, keepdims=True)
    ei = lax.broadcasted_iota(jnp.int32, (EXPERTS_PER_GROUP, tm), 0)
    m1 = jnp.max(pe, axis=0, keepdims=True)
    i1 = jnp.min(jnp.where(pe == m1, ei, EXPERTS_PER_GROUP), axis=0, keepdims=True)
    pe2 = jnp.where(ei == i1, -1.0, pe)
    m2 = jnp.max(pe2, axis=0, keepdims=True)
    i2 = jnp.min(jnp.where(pe2 == m2, ei, EXPERTS_PER_GROUP), axis=0, keepdims=True)
    den = m1 + m2
    e1 = g_idx * EXPERTS_PER_GROUP + i1
    e2 = g_idx * EXPERTS_PER_GROUP + i2

    @pl.when(pl.program_id(0) % tiles_per_sb == 0)
    def _():
        carry_ref[...] = jnp.zeros_like(carry_ref)

    eall = lax.broadcasted_iota(jnp.int32, (N_EXPERTS, tm), 0)
    tri = tri_ref[...]
    carry = carry_ref[...]
    ranks = []
    for eid in (e1, e2):
        hot = eall == eid
        prefix = _dot(hot.astype(BF16), tri)
        ranks.append(jnp.sum(jnp.where(hot, carry + prefix - 1.0, 0.0), axis=0, keepdims=True))
        carry = carry + prefix[:, tm - 1:tm]
    carry_ref[...] = carry
    cnt_ref[0] = jnp.broadcast_to(carry, (N_EXPERTS, LANES))
    zeros = jnp.zeros((2, tm), F32)
    route_ref[...] = jnp.concatenate([e1.astype(F32), e2.astype(F32), p_top * (m1 / den), p_top * (m2 / den),
                                      ranks[0], ranks[1], zeros], axis=0)


def _slots_kernel(route_ref, cnt_ref, strict_ref, pos_ref, off_ref):
    counts = cnt_ref[0]
    padded = 2.0 * jnp.floor(counts * 0.5 + 0.5)
    hi = jnp.floor(padded * (1.0 / 256.0))
    lo = padded - 256.0 * hi
    strict = strict_ref[...]
    off = 256.0 * _dot(strict, hi.astype(BF16)) + _dot(strict, lo.astype(BF16))
    off_ref[0] = off.astype(jnp.int32)
    tb = route_ref.shape[1]
    eall = lax.broadcasted_iota(jnp.int32, (N_EXPERTS, tb), 0)
    off_col = off[:, 0:1]
    for s in range(2):
        hot = eall == route_ref[s:s + 1, :].astype(jnp.int32)
        base = jnp.sum(jnp.where(hot, off_col, 0.0), axis=0, keepdims=True)
        pos_ref[0, s:s + 1, :] = ((base + route_ref[4 + s:5 + s, :]) * PACK_ROWS).astype(jnp.int32)


def _moe_kernel(tb, off_ref, cnt_ref, h2_ref, pos_ref, wg_ref, wu_ref, wd_ref, x1_ref, wt_ref, o_ref,
                xy_ref, cba_ref, cbb_ref):
    sb = pl.program_id(0)
    j = pl.program_id(1)
    n_scatter = tb // MOE_SCATTER_TILE
    n_combine = tb // MOE_COMBINE_TILE
    first_expert_step = n_scatter + 1
    first_combine_step = first_expert_step + N_EXPERTS

    def pack(dst_ref):
        h = h2_ref[...].astype(F32)
        for r in range(PACK_ROWS):
            c0 = r * 2 * LANES
            dst_ref[pl.ds(r, MOE_SCATTER_TILE, stride=PACK_ROWS), :] = _pack_bf16_pair(
                h[:, c0:c0 + LANES], h[:, c0 + LANES:c0 + 2 * LANES])

    def scatter(src_ref, tile):
        t0 = tile * MOE_SCATTER_TILE
        for pair in range(MOE_SCATTER_TILE // 2):
            v = src_ref[pair * 2 * PACK_ROWS:(pair + 1) * 2 * PACK_ROWS, :]
            for half in range(2):
                piece = v[half * PACK_ROWS:(half + 1) * PACK_ROWS, :]
                for slot in range(2):
                    p = pos_ref[slot * tb + t0 + pair * 2 + half]
                    xy_ref[pl.ds(pl.multiple_of(p, PACK_ROWS), PACK_ROWS), :] = piece

    def gather(dst_ref, tile):
        t0 = tile * MOE_COMBINE_TILE
        for t in range(MOE_COMBINE_TILE):
            for slot in range(2):
                p = pos_ref[slot * tb + t0 + t]
                r0 = (slot * MOE_COMBINE_TILE + t) * PACK_ROWS
                dst_ref[r0:r0 + PACK_ROWS, :] = xy_ref[pl.ds(pl.multiple_of(p, PACK_ROWS), PACK_ROWS), :]

    def weighted_sum(src_ref):
        wt = wt_ref[...].T
        w1 = wt[:, 2:3]
        w2 = wt[:, 3:4]
        second = MOE_COMBINE_TILE * PACK_ROWS
        for r in range(PACK_ROWS):
            lo1, hi1 = _unpack_bf16_pair(src_ref[pl.ds(r, MOE_COMBINE_TILE, stride=PACK_ROWS), :])
            lo2, hi2 = _unpack_bf16_pair(src_ref[pl.ds(second + r, MOE_COMBINE_TILE, stride=PACK_ROWS), :])
            c0 = r * 2 * LANES
            o_ref[:, c0:c0 + LANES] = x1_ref[:, c0:c0 + LANES] + w1 * lo1 + w2 * lo2
            o_ref[:, c0 + LANES:c0 + 2 * LANES] = x1_ref[:, c0 + LANES:c0 + 2 * LANES] + w1 * hi1 + w2 * hi2

    bufs = (cba_ref, cbb_ref)

    @pl.when(j == 0)
    def _first_pack():
        zero_slot = jnp.zeros((PACK_ROWS, LANES), U32)
        for e in range(N_EXPERTS):
            p = (off_ref[sb * N_EXPERTS + e] + cnt_ref[sb * N_EXPERTS + e]) * PACK_ROWS
            xy_ref[pl.ds(pl.multiple_of(p, PACK_ROWS), PACK_ROWS), :] = zero_slot
        tail0 = 2 * tb * PACK_ROWS
        xy_ref[tail0:, :] = jnp.zeros((xy_ref.shape[0] - tail0, LANES), U32)
        pack(bufs[0])

    for parity in range(2):
        @pl.when((j >= 1) & (j < n_scatter) & (j % 2 == parity))
        def _scatter_and_pack():
            pack(bufs[parity])
            scatter(bufs[1 - parity], j - 1)

    @pl.when(j == n_scatter)
    def _last_scatter():
        scatter(bufs[(n_scatter - 1) % 2], n_scatter - 1)

    @pl.when((j >= first_expert_step) & (j < first_combine_step))
    def _experts():
        e = sb * N_EXPERTS + j - first_expert_step
        n = cnt_ref[e]
        base = off_ref[e]

        def body(i, c):
            r0 = base + i * MOE_CHUNK
            row0 = pl.multiple_of(r0 * PACK_ROWS, 2 * PACK_ROWS)
            halves = [_unpack_bf16_pair(xy_ref[pl.ds(row0 + r, MOE_CHUNK, stride=PACK_ROWS), :])
                      for r in range(PACK_ROWS)]
            a = jnp.zeros((MOE_CHUNK, D_EXPERT), F32)
            u = jnp.zeros((MOE_CHUNK, D_EXPERT), F32)
            for r in range(PACK_ROWS):
                lo, hi = halves[r]
                xb = jnp.concatenate([lo, hi], axis=1).astype(BF16)
                a = a + _dot(xb, wg_ref[0, r * 2 * LANES:(r + 1) * 2 * LANES, :])
                u = u + _dot(xb, wu_ref[0, r * 2 * LANES:(r + 1) * 2 * LANES, :])
            hid = ((a * jax.nn.sigmoid(a)) * u).astype(BF16)
            y = _dot(hid, wd_ref[0])
            valid = (r0 + lax.broadcasted_iota(jnp.int32, (MOE_CHUNK, 1), 0)) < (base + n)
            for r in range(PACK_ROWS):
                c0 = r * 2 * LANES
                lo, hi = halves[r]
                xy_ref[pl.ds(row0 + r, MOE_CHUNK, stride=PACK_ROWS), :] = _pack_bf16_pair(
                    jnp.where(valid, y[:, c0:c0 + LANES], lo), jnp.where(valid, y[:, c0 + LANES:c0 + 2 * LANES], hi))
            return c

        lax.fori_loop(0, (n + MOE_CHUNK - 1) // MOE_CHUNK, body, 0)

    cstep = j - first_combine_step

    @pl.when(cstep == 0)
    def _first_gather():
        gather(bufs[0], 0)

    for parity in range(2):
        @pl.when((cstep >= 1) & (cstep < n_combine) & (cstep % 2 == parity))
        def _gather_and_sum():
            gather(bufs[parity], cstep)
            weighted_sum(bufs[1 - parity])

    @pl.when(cstep == n_combine)
    def _last_sum():
        weighted_sum(bufs[(n_combine - 1) % 2])


def _full(shape):
    return pl.BlockSpec(shape, lambda *_: (0,) * len(shape))


def _prep_weights(w_in, gk_fwd_w, gk_fwd_b, gk_bwd_w, gk_bwd_b, mla_w_qb, mla_w_kvb, q_norm_gain, k_norm_gain):
    splits = np.cumsum([0, GLA_QK_WIDTH, GLA_QK_WIDTH, GLA_WIDTH, GLA_WIDTH, GLA_GATE_RANK, GLA_GATE_RANK,
                        MLA_Q_RANK, MLA_KV_RANK, MLA_ROPE])
    d = w_in.shape[0]
    win = w_in[:, :splits[4]].astype(BF16)
    wt = w_in[:, splits[4]:]
    rel = splits - splits[4]
    gap = jnp.zeros((d, ROPE_LO - 2 * GLA_GATE_RANK), F32)
    end = jnp.zeros((d, LANES - ROPE_LO - MLA_ROPE), F32)
    wtail = jnp.concatenate([wt[:, rel[6]:rel[8]], wt[:, rel[4]:rel[6]], gap, wt[:, rel[8]:rel[9]], end],
                            axis=1).astype(BF16)

    zq = jnp.zeros((GLA_GATE_RANK, GLA_QK_WIDTH), F32)
    wg = jnp.concatenate([jnp.concatenate([gk_fwd_w, zq], axis=1), jnp.concatenate([zq, gk_bwd_w], axis=1),
                          jnp.zeros((LANES - 2 * GLA_GATE_RANK, 2 * GLA_QK_WIDTH), F32)], axis=0)
    bg = jnp.concatenate([gk_fwd_b, gk_bwd_b])[None, :]

    pad = HEAD_TILE - MLA_QK
    wqb = jnp.pad(mla_w_qb.reshape(MLA_Q_RANK, MLA_HEADS, MLA_QK), ((0, 0), (0, 0), (0, pad)))
    wqb = wqb.reshape(MLA_Q_RANK, MLA_HEADS * HEAD_TILE)
    wkv = mla_w_kvb.reshape(MLA_KV_RANK, MLA_HEADS, MLA_NOPE + MLA_V)
    wkvk = jnp.pad(wkv[:, :, :MLA_NOPE], ((0, 0), (0, 0), (0, HEAD_TILE - MLA_NOPE)))
    wkvk = wkvk.reshape(MLA_KV_RANK, MLA_HEADS * HEAD_TILE).astype(BF16)
    wkvv = wkv[:, :, MLA_NOPE:].reshape(MLA_KV_RANK, MLA_WIDTH).T.astype(BF16)
    qng = jnp.pad(q_norm_gain, (0, pad))[None, :]
    kng = jnp.pad(k_norm_gain, (0, pad))[None, :]
    def partner(w, sign):
        first, second = w[..., ROPE_LO:ROPE_LO + ROPE_HALF], w[..., ROPE_LO + ROPE_HALF:ROPE_LO + MLA_ROPE]
        return jnp.concatenate([jnp.zeros_like(w[..., :ROPE_LO]), sign * second, first,
                                jnp.zeros_like(w[..., ROPE_LO + MLA_ROPE:])], axis=-1)

    wqb_rot = partner(wqb.reshape(MLA_Q_RANK, MLA_HEADS, HEAD_TILE), -1.0).reshape(MLA_Q_RANK, -1).astype(BF16)
    qng_rot = partner(qng, 1.0)
    kng_rot = partner(kng, 1.0)
    return win, wtail, wg.astype(BF16), bg, wqb.astype(BF16), wqb_rot, wkvk, wkvv, qng, qng_rot, kng, kng_rot


def _rope_consts():
    inv = ROPE_BASE ** (-np.arange(0, MLA_ROPE, 2, dtype=np.float32) / MLA_ROPE)
    one = np.zeros((ROPE_HALF, LANES), np.float32)
    one[np.arange(ROPE_HALF), ROPE_LO + np.arange(ROPE_HALF)] = 1.0
    one[np.arange(ROPE_HALF), ROPE_LO + ROPE_HALF + np.arange(ROPE_HALF)] = 1.0
    zero = np.zeros_like(one)
    place = np.block([[one, zero], [one, zero], [zero, one], [zero, one]])
    ones_bd = np.kron(np.eye(2, dtype=np.float32), np.ones((LANES, LANES), np.float32))
    return jnp.asarray(inv[:, None]), jnp.asarray(place, BF16), jnp.asarray(ones_bd, BF16)


def _layer(x2, pos2, batch, seq, norm1_gain, w_in, gk_fwd_w, gk_fwd_b, gk_bwd_w, gk_bwd_b, gla_out_gain,
           mla_q_gain, mla_w_qb, mla_kv_gain, mla_w_kvb, q_norm_gain, k_norm_gain, w_out, norm2_gain,
           w_router_group, b_router_group, w_router_expert, b_router_expert,
           w_expert_gate, w_expert_up, w_expert_down):
    T, D = x2.shape
    win, wtail, wg, bg, wqb, wqb_rot, wkvk, wkvv, qng, qng_rot, kng, kng_rot = _prep_weights(
        w_in, gk_fwd_w, gk_fwd_b, gk_bwd_w, gk_bwd_b, mla_w_qb, mla_w_kvb, q_norm_gain, k_norm_gain)

    TM = min(PROJ_TILE, seq)
    spb = seq // TM
    tok = lambda w: pl.BlockSpec((TM, w), lambda i: (i, 0))
    head_spec = pl.BlockSpec((1, MLA_HEADS, TM, HEAD_TILE), lambda i: (i // spb, 0, i % spb, 0))
    gq, gk, gv, gg, lg, mq, mk, mv = pl.pallas_call(
        _proj_kernel,
        grid=(T // TM,),
        in_specs=[tok(D), pl.BlockSpec((1, 1, TM), lambda i: (i, 0, 0)), _full((1, D)), _full((D, COL_GLA_END)),
                  _full((D, TAIL_WIDTH)),
                  _full((LANES, 2 * GLA_QK_WIDTH)), _full((1, 2 * GLA_QK_WIDTH)), _full((1, MLA_Q_RANK)),
                  _full((MLA_Q_RANK, MLA_HEADS * HEAD_TILE)), _full((MLA_Q_RANK, MLA_HEADS * HEAD_TILE)),
                  _full((1, MLA_KV_RANK)), _full((MLA_KV_RANK, MLA_HEADS * HEAD_TILE)),
                  _full((MLA_WIDTH, MLA_KV_RANK)), _full((1, HEAD_TILE)), _full((1, HEAD_TILE)),
                  _full((1, HEAD_TILE)), _full((1, HEAD_TILE)),
                  _full((ROPE_HALF, 1)), _full((4 * ROPE_HALF, 2 * LANES)), _full((2 * LANES, 2 * LANES))],
        out_specs=[tok(GLA_QK_WIDTH), tok(GLA_QK_WIDTH), tok(GLA_WIDTH), tok(GLA_WIDTH), tok(2 * GLA_QK_WIDTH),
                   head_spec, head_spec, pl.BlockSpec((1, MLA_WIDTH, TM), lambda i: (i // spb, 0, i % spb))],
        out_shape=[jax.ShapeDtypeStruct((T, GLA_QK_WIDTH), BF16), jax.ShapeDtypeStruct((T, GLA_QK_WIDTH), BF16),
                   jax.ShapeDtypeStruct((T, GLA_WIDTH), BF16), jax.ShapeDtypeStruct((T, GLA_WIDTH), BF16),
                   jax.ShapeDtypeStruct((T, 2 * GLA_QK_WIDTH), F32),
                   jax.ShapeDtypeStruct((batch, MLA_HEADS, seq, HEAD_TILE), BF16),
                   jax.ShapeDtypeStruct((batch, MLA_HEADS, seq, HEAD_TILE), BF16),
                   jax.ShapeDtypeStruct((batch, MLA_WIDTH, seq), BF16)],
        compiler_params=pltpu.CompilerParams(dimension_semantics=("parallel",), vmem_limit_bytes=VMEM_LIMIT),
        name="proj",
    )(x2, pos2.reshape(T // TM, 1, TM), norm1_gain[None, :], win, wtail, wg, bg, mla_q_gain[None, :], wqb, wqb_rot,
      mla_kv_gain[None, :], wkvk, wkvv, qng, qng_rot, kng, kng_rot, *_rope_consts())

    seqspec = lambda w: pl.BlockSpec((seq, w), lambda b: (b, 0))
    gla_out = pl.pallas_call(
        _gla_kernel,
        grid=(batch,),
        in_specs=[seqspec(GLA_QK_WIDTH), seqspec(GLA_QK_WIDTH), seqspec(GLA_WIDTH), seqspec(2 * GLA_QK_WIDTH),
                  seqspec(GLA_WIDTH), _full((1, GLA_DV))],
        out_specs=seqspec(GLA_WIDTH),
        out_shape=jax.ShapeDtypeStruct((T, GLA_WIDTH), BF16),
        scratch_shapes=[pltpu.VMEM((seq, GLA_WIDTH), F32), pltpu.VMEM((seq, GLA_WIDTH), F32),
                        pltpu.VMEM((GLA_DV, GLA_QK_WIDTH), F32), pltpu.VMEM((GLA_DV, GLA_QK_WIDTH), F32)],
        compiler_params=pltpu.CompilerParams(dimension_semantics=("parallel",), vmem_limit_bytes=VMEM_LIMIT),
        name="gla",
    )(gq, gk, gv, lg, gg, gla_out_gain[None, :])

    TQ = 256
    nq = seq // TQ
    n_steps = batch * nq
    up_rows = N_EXPERTS * D // n_steps
    down_rows = N_EXPERTS * D_EXPERT // n_steps
    wslice = lambda rows, cols: pl.BlockSpec((rows, cols), lambda b, i: (b * nq + i, 0))
    mla_out, wge, wue, wde = pl.pallas_call(
        _attn_kernel,
        grid=(batch, nq),
        in_specs=[pl.BlockSpec((1, MLA_HEADS, TQ, HEAD_TILE), lambda b, i: (b, 0, i, 0)),
                  pl.BlockSpec((1, MLA_HEADS, seq, HEAD_TILE), lambda b, i: (b, 0, 0, 0)),
                  pl.BlockSpec((1, MLA_WIDTH, seq), lambda b, i: (b, 0, 0)),
                  wslice(up_rows, D_EXPERT), wslice(up_rows, D_EXPERT), wslice(down_rows, D)],
        out_specs=[pl.BlockSpec((1, MLA_WIDTH, TQ), lambda b, i: (b, 0, i)),
                   wslice(up_rows, D_EXPERT), wslice(up_rows, D_EXPERT), wslice(down_rows, D)],
        out_shape=[jax.ShapeDtypeStruct((batch, MLA_WIDTH, seq), BF16),
                   jax.ShapeDtypeStruct((N_EXPERTS * D, D_EXPERT), BF16),
                   jax.ShapeDtypeStruct((N_EXPERTS * D, D_EXPERT), BF16),
                   jax.ShapeDtypeStruct((N_EXPERTS * D_EXPERT, D), BF16)],
        compiler_params=pltpu.CompilerParams(dimension_semantics=("parallel", "parallel"),
                                             vmem_limit_bytes=VMEM_LIMIT),
        name="attn",
    )(mq, mk, mv, w_expert_gate.reshape(N_EXPERTS * D, D_EXPERT), w_expert_up.reshape(N_EXPERTS * D, D_EXPERT),
      w_expert_down.reshape(N_EXPERTS * D_EXPERT, D))

    row_gap = ROUTER_EXPERT_ROW - N_GROUPS
    row_tail = ROUTER_ROWS - ROUTER_EXPERT_ROW - N_EXPERTS
    wr = jnp.concatenate([w_router_group, jnp.zeros((D, row_gap), F32), w_router_expert,
                          jnp.zeros((D, row_tail), F32)], axis=1).T
    br = jnp.concatenate([b_router_group, jnp.zeros((row_gap,), F32), b_router_expert,
                          jnp.zeros((row_tail,), F32)])[:, None]
    tb = min(MOE_SUPER_BLOCK, T)
    nsb = T // tb
    TO = min(OUTPROJ_TILE, seq)
    tiles_per_sb = tb // TO
    tok = lambda w: pl.BlockSpec((TO, w), lambda i: (i, 0))
    tri = jnp.asarray(np.triu(np.ones((TO, TO), np.float32)), BF16)
    x1, h2, route, cnt = pl.pallas_call(
        functools.partial(_outproj_kernel, tiles_per_sb),
        grid=(T // TO,),
        in_specs=[tok(D), tok(GLA_WIDTH),
                  pl.BlockSpec((1, MLA_WIDTH, TO), lambda i: (i // (seq // TO), 0, i % (seq // TO))),
                  _full((D, D)), _full((1, D)),
                  _full((ROUTER_ROWS, D)), _full((ROUTER_ROWS, 1)), _full((TO, TO))],
        out_specs=[tok(D), tok(D), pl.BlockSpec((8, TO), lambda i: (0, i)),
                   pl.BlockSpec((1, N_EXPERTS, LANES), lambda i: (i // tiles_per_sb, 0, 0))],
        out_shape=[jax.ShapeDtypeStruct((T, D), F32), jax.ShapeDtypeStruct((T, D), BF16),
                   jax.ShapeDtypeStruct((8, T), F32), jax.ShapeDtypeStruct((nsb, N_EXPERTS, LANES), F32)],
        scratch_shapes=[pltpu.VMEM((N_EXPERTS, 1), F32), pltpu.VMEM((D, D), BF16)],
        compiler_params=pltpu.CompilerParams(dimension_semantics=("arbitrary",), vmem_limit_bytes=VMEM_LIMIT),
        name="outproj",
    )(x2, gla_out, mla_out, w_out, norm2_gain[None, :], wr.astype(BF16), br, tri)

    strict = jnp.asarray(np.tril(np.ones((N_EXPERTS, N_EXPERTS), np.float32), -1), BF16)
    pos, offs = pl.pallas_call(
        _slots_kernel,
        grid=(nsb,),
        in_specs=[pl.BlockSpec((8, tb), lambda s: (0, s)), pl.BlockSpec((1, N_EXPERTS, LANES), lambda s: (s, 0, 0)),
                  _full((N_EXPERTS, N_EXPERTS))],
        out_specs=[pl.BlockSpec((1, 2, tb), lambda s: (s, 0, 0)),
                   pl.BlockSpec((1, N_EXPERTS, LANES), lambda s: (s, 0, 0))],
        out_shape=[jax.ShapeDtypeStruct((nsb, 2, tb), jnp.int32),
                   jax.ShapeDtypeStruct((nsb, N_EXPERTS, LANES), jnp.int32)],
        compiler_params=pltpu.CompilerParams(dimension_semantics=("parallel",)),
        name="slots",
    )(route, cnt, strict)
    pos = pos.reshape(-1)
    flat_off = offs[:, :, 0].reshape(-1)
    counts = cnt[:, :, 0].astype(jnp.int32)
    wge = wge.reshape(N_EXPERTS, D, D_EXPERT)
    wue = wue.reshape(N_EXPERTS, D, D_EXPERT)
    wde = wde.reshape(N_EXPERTS, D_EXPERT, D)
    nct = tb // MOE_COMBINE_TILE
    nst = tb // MOE_SCATTER_TILE
    slots =(2 * tb + N_EXPERTS + MOE_CHUNK + 7) // 8 * 8
    hidx = lambda sb, j, off, cn: (sb * nst + jnp.minimum(j, nst - 1), 0)
    eidx = lambda sb, j, off, cn: (jnp.clip(j - (nst + 1), 0, N_EXPERTS - 1), 0, 0)
    tidx = lambda sb, j, off, cn: (sb * nct + jnp.clip(j - (nst + 1 + N_EXPERTS) - 1, 0, nct - 1), 0)
    out = pl.pallas_call(
        functools.partial(_moe_kernel, tb),
        grid_spec=pltpu.PrefetchScalarGridSpec(
            num_scalar_prefetch=2,
            grid=(nsb, (nst + 1) + N_EXPERTS + (nct + 1)),
            in_specs=[pl.BlockSpec((MOE_SCATTER_TILE, D), hidx),
                      pl.BlockSpec((2 * tb,), lambda sb, j, off, cn: (sb,), memory_space=pltpu.SMEM),
                      pl.BlockSpec((1, D, D_EXPERT), eidx), pl.BlockSpec((1, D, D_EXPERT), eidx),
                      pl.BlockSpec((1, D_EXPERT, D), eidx),
                      pl.BlockSpec((MOE_COMBINE_TILE, D), tidx),
                      pl.BlockSpec((8, MOE_COMBINE_TILE), lambda sb, j, off, cn: (0, tidx(sb, j, off, cn)[0]))],
            out_specs=pl.BlockSpec((MOE_COMBINE_TILE, D), tidx),
            scratch_shapes=[pltpu.VMEM((slots * PACK_ROWS, LANES), U32),
                            pltpu.VMEM((MOE_SCATTER_TILE * PACK_ROWS, LANES), U32),
                            pltpu.VMEM((MOE_SCATTER_TILE * PACK_ROWS, LANES), U32)]),
        out_shape=jax.ShapeDtypeStruct((T, D), F32),
        compiler_params=pltpu.CompilerParams(dimension_semantics=("arbitrary", "arbitrary"),
                                             vmem_limit_bytes=MOE_VMEM_LIMIT),
        name="moe",
    )(flat_off, counts.reshape(-1), h2, pos, wge, wue, wde, x1, route)
    return out


def kernel(x, positions, norm1_gain, w_in, gla_gk_fwd_w, gla_gk_fwd_b, gla_gk_bwd_w, gla_gk_bwd_b, gla_out_gain, mla_q_gain, mla_w_qb, mla_kv_gain, mla_w_kvb, q_norm_gain, k_norm_gain, w_out, norm2_gain, w_router_group, b_router_group, w_router_expert, b_router_expert, w_expert_gate, w_expert_up, w_expert_down):
    batch, seq, d = x.shape
    x2 = x.reshape(batch * seq, d)
    pos2 = positions.reshape(batch * seq, 1)
    for l in range(norm1_gain.shape[0]):
        x2 = _layer(x2, pos2, batch, seq, norm1_gain[l], w_in[l], gla_gk_fwd_w[l], gla_gk_fwd_b[l],
                    gla_gk_bwd_w[l], gla_gk_bwd_b[l], gla_out_gain[l], mla_q_gain[l], mla_w_qb[l],
                    mla_kv_gain[l], mla_w_kvb[l], q_norm_gain[l], k_norm_gain[l], w_out[l], norm2_gain[l],
                    w_router_group[l], b_router_group[l], w_router_expert[l], b_router_expert[l],
                    w_expert_gate[l], w_expert_up[l], w_expert_down[l])
    return x2.reshape(batch, seq, d)
```

```python
import functools

import numpy as np
import jax
import jax.numpy as jnp
from jax import lax
from jax.experimental import pallas as pl
from jax.experimental.pallas import tpu as pltpu

F32 = jnp.float32
BF16 = jnp.bfloat16

D_MODEL = 1024
GLA_HEADS = 4
GLA_DK = 64
GLA_DV = 128
GLA_GATE_RANK = 16
GLA_GATE_NORMALIZER = 16.0
GLA_CHUNK = 64
GLA_CHUNK_UNROLL = 16
MLA_HEADS = 8
MLA_NOPE = 64
MLA_ROPE = 32
MLA_QK = MLA_NOPE + MLA_ROPE
MLA_V = 64
MLA_Q_RANK = 256
MLA_KV_RANK = 128
ROPE_BASE = 10000.0
GLA_QK_WIDTH = GLA_HEADS * GLA_DK
GLA_WIDTH = GLA_HEADS * GLA_DV
MLA_WIDTH = MLA_HEADS * MLA_V
N_GROUPS = 4
EXPERTS_PER_GROUP = 8
N_EXPERTS = N_GROUPS * EXPERTS_PER_GROUP
D_EXPERT = 256
EPS = 1e-6

LANES = 128
HEAD_TILE = LANES
ROPE_LO = MLA_NOPE
ROPE_HALF = MLA_ROPE // 2
LOG2_E = 1.4426950408889634
ATTN_SCORES_AHEAD = 3
ONES_ROWS = 16

COL_GQ = 0
COL_GK = COL_GQ + GLA_QK_WIDTH
COL_GV = COL_GK + GLA_QK_WIDTH
COL_GG = COL_GV + GLA_WIDTH
COL_GLA_END = COL_GG + GLA_WIDTH
TAIL_MQ = 0
TAIL_MKV = TAIL_MQ + MLA_Q_RANK
TAIL_MISC = TAIL_MKV + MLA_KV_RANK
TAIL_WIDTH = TAIL_MISC + LANES

ROUTER_ROWS = LANES
ROUTER_EXPERT_ROW = 8

U32 = jnp.uint32
HI_HALF_MASK = np.uint32(0xFFFF0000)
PACK_ROWS = D_MODEL // (2 * LANES)
MOE_SUPER_BLOCK = 8192
MOE_CHUNK = 576
MOE_SCATTER_TILE = 1024
MOE_COMBINE_TILE = 512
PROJ_TILE = 1024
OUTPROJ_TILE = 1024
assert 2 * MOE_COMBINE_TILE == MOE_SCATTER_TILE

VMEM_LIMIT = 56 * 1024 * 1024
MOE_VMEM_LIMIT = 59900 * 1024


def _dot(a, b):
    return jnp.dot(a, b, preferred_element_type=F32)


def _dot_nt(a, b):
    return lax.dot_general(a, b, (((1,), (1,)), ((), ())), preferred_element_type=F32)


def _dot_tn(a, b):
    return lax.dot_general(a, b, (((0,), (0,)), ((), ())), preferred_element_type=F32)


def _split_bf16(x):
    hi = x.astype(BF16)
    lo = (x - hi.astype(F32)).astype(BF16)
    return hi, lo


def _proj_kernel(x_ref, pos_ref, n1_ref, winf_ref, wtail_ref, wg_ref, bg_ref, qgain_ref, wqb_ref, wqbr_ref, kvgain_ref,
                 wkvk_ref, wkvv_ref, qng_ref, qngr_ref, kng_ref, kngr_ref, freq_ref, place_ref, ones_ref,
                 gq_ref, gk_ref, gv_ref, gg_ref, lg_ref, mq_ref, mk_ref, mv_ref, win_ref):
    @pl.when(pl.program_id(0) == 0)
    def _():
        win_ref[...] = winf_ref[:, 0:COL_GLA_END].astype(BF16)

    x = x_ref[...]
    h = x * lax.rsqrt(jnp.mean(x * x, axis=-1, keepdims=True) + EPS) * n1_ref[...]
    hb = h.astype(BF16)
    proj = _dot(hb, win_ref[...])
    tail = _dot(hb, wtail_ref[...])

    gq_ref[...] = (proj[:, COL_GQ:COL_GK] * GLA_DK ** -0.5).astype(BF16)
    gk_ref[...] = proj[:, COL_GK:COL_GV].astype(BF16)
    gv_ref[...] = proj[:, COL_GV:COL_GG].astype(BF16)
    gg_ref[...] = proj[:, COL_GG:COL_GLA_END].astype(BF16)

    misc = tail[:, TAIL_MISC:TAIL_WIDTH]
    z = _dot(misc.astype(BF16), wg_ref[...]) + bg_ref[...]
    log_sig = -(jnp.maximum(-z, 0.0) + jnp.log(1.0 + jnp.exp(-jnp.abs(z))))
    lg_ref[...] = log_sig / GLA_GATE_NORMALIZER

    lane = lax.broadcasted_iota(jnp.int32, (1, LANES), 1)
    ang_t = freq_ref[...] * pos_ref[0].astype(F32)
    cos_hi, cos_lo = _split_bf16(jnp.cos(ang_t))
    sin_hi, sin_lo = _split_bf16(jnp.sin(ang_t))
    tabs = _dot_tn(jnp.concatenate([cos_hi, cos_lo, sin_hi, sin_lo], axis=0), place_ref[...])
    in_rope = (lane >= ROPE_LO) & (lane < ROPE_LO + MLA_ROPE)
    c_tab = jnp.where(lane < ROPE_LO, 1.0, tabs[:, 0:LANES])
    s_tab = tabs[:, LANES:2 * LANES]
    ones_bd = ones_ref[...]

    def norm_rope_pair(t2, r2, gc, gs, on_mxu):
        if on_mxu:
            ss = _dot((t2 * t2).astype(BF16), ones_bd) * (1.0 / MLA_QK)
        outs = []
        for i in range(2):
            sl = slice(i * LANES, (i + 1) * LANES)
            t = t2[:, sl]
            ssi = ss[:, sl] if on_mxu else jnp.sum(t * t, axis=-1, keepdims=True) * (1.0 / MLA_QK)
            outs.append(lax.rsqrt(ssi + EPS) * (t * gc + r2[:, sl] * gs))
        return outs

    qa = tail[:, TAIL_MQ:TAIL_MKV]
    qn = qa * lax.rsqrt(jnp.mean(qa * qa, axis=-1, keepdims=True) + EPS) * qgain_ref[...]
    qn = qn.astype(BF16)
    mq = _dot(qn, wqb_ref[...])
    mq_rot = _dot(qn, wqbr_ref[...])
    kva = tail[:, TAIL_MKV:TAIL_MISC]
    kvn = (kva * lax.rsqrt(jnp.mean(kva * kva, axis=-1, keepdims=True) + EPS) * kvgain_ref[...]).astype(BF16)
    kn = _dot(kvn, wkvk_ref[...])
    mv_ref[0] = _dot_nt(wkvv_ref[...], kvn).astype(BF16)
    rope_tile = jnp.where(in_rope, misc, 0.0)
    up = (lane >= ROPE_LO + ROPE_HALF) & (lane < ROPE_LO + MLA_ROPE)
    dn = (lane >= ROPE_LO) & (lane < ROPE_LO + ROPE_HALF)
    rope_rot = (jnp.where(up, pltpu.roll(rope_tile, ROPE_HALF, 1), 0.0)
                - jnp.where(dn, pltpu.roll(rope_tile, LANES - ROPE_HALF, 1), 0.0))
    rope2 = jnp.concatenate([rope_tile, rope_tile], axis=1)
    rope_rot2 = jnp.concatenate([rope_rot, rope_rot], axis=1)
    scale = MLA_QK ** -0.5 * LOG2_E
    q_gc, q_gs = (qng_ref[...] * scale) * c_tab, (qngr_ref[...] * scale) * s_tab
    k_gc, k_gs = kng_ref[...] * c_tab, kngr_ref[...] * s_tab
    for hp in range(MLA_HEADS // 2):
        sl = slice(hp * 2 * HEAD_TILE, (hp + 1) * 2 * HEAD_TILE)
        q2 = norm_rope_pair(mq[:, sl], mq_rot[:, sl], q_gc, q_gs, on_mxu=True)
        k2 = norm_rope_pair(kn[:, sl] + rope2, rope_rot2, k_gc, k_gs, on_mxu=False)
        for i in range(2):
            mq_ref[0, 2 * hp + i] = q2[i].astype(BF16)
            mk_ref[0, 2 * hp + i] = k2[i].astype(BF16)


def _gla_kernel(q_ref, k_ref, v_ref, lg_ref, gate_ref, gain_ref, o_ref, accf_ref, accb_ref, stf_ref, stb_ref):
    seq = q_ref.shape[0]
    C = GLA_CHUNK
    n_chunks = seq // C
    HK = GLA_QK_WIDTH

    lane_head = lax.broadcasted_iota(jnp.int32, (1, HK), 1) // GLA_DK
    rowi = lax.broadcasted_iota(jnp.int32, (C, 1), 0)
    srow = lax.broadcasted_iota(jnp.int32, (GLA_HEADS * C, C), 0) % C
    scol = lax.broadcasted_iota(jnp.int32, (GLA_HEADS * C, C), 1)
    gain = gain_ref[...]
    fwd_cfg = (True, C // 2 - 1, C - 1, scol <= srow, 0)
    bwd_cfg = (False, C // 2, 0, scol > srow, HK)

    def stack_heads(t):
        return jnp.concatenate([jnp.where(lane_head == hd, t, 0.0) for hd in range(GLA_HEADS)], axis=0)

    def chunk_step(c, st, cfg, acc_ref, other_acc_ref=None):
        prefix, ref_i, last_i, keep, lg_off = cfg
        rows = pl.ds(pl.multiple_of(c * C, C), C)
        b = lg_ref[rows, lg_off:lg_off + HK]
        shift = 1
        while shift < C:
            if prefix:
                b = b + jnp.where(rowi >= shift, pltpu.roll(b, shift, 0), 0.0)
            else:
                b = b + jnp.where(rowi < C - shift, pltpu.roll(b, C - shift, 0), 0.0)
            shift *= 2
        b_ref = b[ref_i:ref_i + 1, :]
        b_last = b[last_i:last_i + 1, :]
        q = q_ref[rows, :].astype(F32)
        k = k_ref[rows, :].astype(F32)
        v = v_ref[rows, :]
        q_rel = q * jnp.exp(b - b_ref)
        k_rel = (k * jnp.exp(b_ref - b)).astype(BF16)
        k_dec = (k * jnp.exp(b_last - b)).astype(BF16)
        q_dec = q * jnp.exp(b)
        sc = _dot_nt(stack_heads(q_rel).astype(BF16), k_rel)
        sc = jnp.where(keep, sc, 0.0).astype(BF16)
        o_inter = _dot_nt(stack_heads(q_dec).astype(BF16), st.astype(BF16))
        kv_t = _dot_tn(v, k_dec)
        new_st = jnp.exp(b_last) * st
        for hd in range(GLA_HEADS):
            vs = slice(hd * GLA_DV, (hd + 1) * GLA_DV)
            rs = slice(hd * C, (hd + 1) * C)
            o_h = _dot(sc[rs, :], v[:, vs]) + o_inter[rs, :]
            if other_acc_ref is None:
                acc_ref[rows, vs] = o_h
            else:
                o = o_h + other_acc_ref[rows, vs]
                on = o * lax.rsqrt(jnp.mean(o * o, axis=-1, keepdims=True) + EPS) * gain
                gt = gate_ref[rows, vs].astype(F32)
                o_ref[rows, vs] = (on * (gt * jax.nn.sigmoid(gt))).astype(BF16)
            new_st = new_st + jnp.where(lane_head == hd, kv_t[vs, :], 0.0)
        return new_st

    stf_ref[...] = jnp.zeros_like(stf_ref)
    stb_ref[...] = jnp.zeros_like(stb_ref)

    unroll = min(GLA_CHUNK_UNROLL, n_chunks // 2)
    n_iter = n_chunks // unroll

    def make_body(finalize):
        def body(i, carry):
            st_f = stf_ref[...]
            st_b = stb_ref[...]
            for u in range(unroll):
                cf = i * unroll + u
                st_f = chunk_step(cf, st_f, fwd_cfg, accf_ref, accb_ref if finalize else None)
                st_b = chunk_step(n_chunks - 1 - cf, st_b, bwd_cfg, accb_ref, accf_ref if finalize else None)
            stf_ref[...] = st_f
            stb_ref[...] = st_b
            return carry
        return body

    lax.fori_loop(0, n_iter // 2, make_body(False), 0)
    lax.fori_loop(n_iter // 2, n_iter, make_body(True), 0)


def _attn_kernel(q_ref, k_ref, vt_ref, wg_ref, wu_ref, wd_ref, o_ref, wgb_ref, wub_ref, wdb_ref):
    wgb_ref[...] = wg_ref[...].astype(BF16)
    wub_ref[...] = wu_ref[...].astype(BF16)
    wdb_ref[...] = wd_ref[...].astype(BF16)
    seq = k_ref.shape[2]
    ones = jnp.ones((ONES_ROWS, seq), BF16)
    outs = []
    scores = [_dot_nt(q_ref[0, h], k_ref[0, h]) for h in range(ATTN_SCORES_AHEAD)]
    for hd in range(MLA_HEADS):
        if hd + ATTN_SCORES_AHEAD < MLA_HEADS:
            scores.append(_dot_nt(q_ref[0, hd + ATTN_SCORES_AHEAD], k_ref[0, hd + ATTN_SCORES_AHEAD]))
        s = scores[hd]
        p = jnp.exp2(s - jnp.max(s, axis=-1, keepdims=True)).astype(BF16)
        vt_aug = jnp.concatenate([vt_ref[0, hd * MLA_V:(hd + 1) * MLA_V, :], ones], axis=0)
        ot = _dot_nt(vt_aug, p)
        outs.append(ot[0:MLA_V, :] / ot[MLA_V:MLA_V + 1, :])
    o_ref[0] = jnp.concatenate(outs, axis=0).astype(BF16)


def _unpack_bf16_pair(w):
    lo = pltpu.unpack_elementwise(w, index=0, packed_dtype=BF16, unpacked_dtype=F32)
    hi = pltpu.unpack_elementwise(w, index=1, packed_dtype=BF16, unpacked_dtype=F32)
    return lo, hi


def _pack_bf16_pair(lo, hi):
    return pltpu.pack_elementwise([lo, hi], packed_dtype=BF16)


def _outproj_kernel(tiles_per_sb, x_ref, gla_ref, mla_ref, wo_ref, n2_ref, wr_ref, br_ref, tri_ref,
                    x1_ref, h2_ref, route_ref, cnt_ref, carry_ref, wob_ref):
    @pl.when(pl.program_id(0) == 0)
    def _():
        wob_ref[...] = wo_ref[...].astype(BF16)

    mix = _dot(gla_ref[...], wob_ref[0:GLA_WIDTH, :]) + _dot_tn(mla_ref[0], wob_ref[GLA_WIDTH:, :])
    x1 = x_ref[...] + mix
    x1_ref[...] = x1
    h2 = x1 * lax.rsqrt(jnp.mean(x1 * x1, axis=-1, keepdims=True) + EPS) * n2_ref[...]
    h2b = h2.astype(BF16)
    h2_ref[...] = h2b

    logits = _dot_nt(wr_ref[...], h2b) + br_ref[...]
    tm = logits.shape[1]
    gl = logits[0:N_GROUPS, :]
    ge = jnp.exp(gl - jnp.max(gl, axis=0, keepdims=True))
    pg = ge / jnp.sum(ge, axis=0, keepdims=True)
    p_top = jnp.max(pg, axis=0, keepdims=True)
    gi = lax.broadcasted_iota(jnp.int32, (N_GROUPS, tm), 0)
    g_idx = jnp.min(jnp.where(pg == p_top, gi, N_GROUPS), axis=0, keepdims=True)
    sel = jnp.zeros((EXPERTS_PER_GROUP, tm), F32)
    for g in range(N_GROUPS):
        r0 = ROUTER_EXPERT_ROW + g * EXPERTS_PER_GROUP
        sel = sel + jnp.where(g_idx == g, logits[r0:r0 + EXPERTS_PER_GROUP, :], 0.0)
    se = jnp.exp(sel - jnp.max(sel, axis=0, keepdims=True))
    pe = se / jnp.sum(se, axis=0, keepdims=True)
    ei = lax.broadcasted_iota(jnp.int32, (EXPERTS_PER_GROUP, tm), 0)
    m1 = jnp.max(pe, axis=0, keepdims=True)
    i1 = jnp.min(jnp.where(pe == m1, ei, EXPERTS_PER_GROUP), axis=0, keepdims=True)
    pe2 = jnp.where(ei == i1, -1.0, pe)
    m2 = jnp.max(pe2, axis=0, keepdims=True)
    i2 = jnp.min(jnp.where(pe2 == m2, ei, EXPERTS_PER_GROUP), axis=0, keepdims=True)
    den = m1 + m2
    e1 = g_idx * EXPERTS_PER_GROUP + i1
    e2 = g_idx * EXPERTS_PER_GROUP + i2

    @pl.when(pl.program_id(0) % tiles_per_sb == 0)
    def _():
        carry_ref[...] = jnp.zeros_like(carry_ref)

    eall = lax.broadcasted_iota(jnp.int32, (N_EXPERTS, tm), 0)
    tri = tri_ref[...]
    carry = carry_ref[...]
    ranks = []
    for eid in (e1, e2):
        hot = eall == eid
        prefix = _dot(hot.astype(BF16), tri)
        ranks.append(jnp.sum(jnp.where(hot, carry + prefix - 1.0, 0.0), axis=0, keepdims=True))
        carry = carry + prefix[:, tm - 1:tm]
    carry_ref[...] = carry
    cnt_ref[0] = jnp.broadcast_to(carry, (N_EXPERTS, LANES))
    zeros = jnp.zeros((2, tm), F32)
    route_ref[...] = jnp.concatenate([e1.astype(F32), e2.astype(F32), p_top * (m1 / den), p_top * (m2 / den),
                                      ranks[0], ranks[1], zeros], axis=0)


def _slots_kernel(route_ref, cnt_ref, strict_ref, pos_ref, off_ref):
    counts = cnt_ref[0]
    padded = 2.0 * jnp.floor(counts * 0.5 + 0.5)
    hi = jnp.floor(padded * (1.0 / 256.0))
    lo = padded - 256.0 * hi
    strict = strict_ref[...]
    off = 256.0 * _dot(strict, hi.astype(BF16)) + _dot(strict, lo.astype(BF16))
    off_ref[0] = off.astype(jnp.int32)
    tb = route_ref.shape[1]
    eall = lax.broadcasted_iota(jnp.int32, (N_EXPERTS, tb), 0)
    off_col = off[:, 0:1]
    for s in range(2):
        hot = eall == route_ref[s:s + 1, :].astype(jnp.int32)
        base = jnp.sum(jnp.where(hot, off_col, 0.0), axis=0, keepdims=True)
        pos_ref[0, s:s + 1, :] = ((base + route_ref[4 + s:5 + s, :]) * PACK_ROWS).astype(jnp.int32)


def _moe_kernel(tb, off_ref, cnt_ref, h2_ref, pos_ref, wg_ref, wu_ref, wd_ref, x1_ref, wt_ref, o_ref,
                xy_ref, cba_ref, cbb_ref):
    sb = pl.program_id(0)
    j = pl.program_id(1)
    n_scatter = tb // MOE_SCATTER_TILE
    n_combine = tb // MOE_COMBINE_TILE
    first_expert_step = n_scatter + 1
    first_combine_step = first_expert_step + N_EXPERTS

    def pack(dst_ref):
        h = h2_ref[...].astype(F32)
        for r in range(PACK_ROWS):
            c0 = r * 2 * LANES
            dst_ref[pl.ds(r, MOE_SCATTER_TILE, stride=PACK_ROWS), :] = _pack_bf16_pair(
                h[:, c0:c0 + LANES], h[:, c0 + LANES:c0 + 2 * LANES])

    def scatter(src_ref, tile):
        t0 = tile * MOE_SCATTER_TILE
        for pair in range(MOE_SCATTER_TILE // 2):
            v = src_ref[pair * 2 * PACK_ROWS:(pair + 1) * 2 * PACK_ROWS, :]
            for half in range(2):
                piece = v[half * PACK_ROWS:(half + 1) * PACK_ROWS, :]
                for slot in range(2):
                    p = pos_ref[slot * tb + t0 + pair * 2 + half]
                    xy_ref[pl.ds(pl.multiple_of(p, PACK_ROWS), PACK_ROWS), :] = piece

    def gather(dst_ref, tile):
        t0 = tile * MOE_COMBINE_TILE
        for t in range(MOE_COMBINE_TILE):
            for slot in range(2):
                p = pos_ref[slot * tb + t0 + t]
                r0 = (slot * MOE_COMBINE_TILE + t) * PACK_ROWS
                dst_ref[r0:r0 + PACK_ROWS, :] = xy_ref[pl.ds(pl.multiple_of(p, PACK_ROWS), PACK_ROWS), :]

    def weighted_sum(src_ref):
        wt = wt_ref[...].T
        w1 = wt[:, 2:3]
        w2 = wt[:, 3:4]
        second = MOE_COMBINE_TILE * PACK_ROWS
        for r in range(PACK_ROWS):
            lo1, hi1 = _unpack_bf16_pair(src_ref[pl.ds(r, MOE_COMBINE_TILE, stride=PACK_ROWS), :])
            lo2, hi2 = _unpack_bf16_pair(src_ref[pl.ds(second + r, MOE_COMBINE_TILE, stride=PACK_ROWS), :])
            c0 = r * 2 * LANES
            o_ref[:, c0:c0 + LANES] = x1_ref[:, c0:c0 + LANES] + w1 * lo1 + w2 * lo2
            o_ref[:, c0 + LANES:c0 + 2 * LANES] = x1_ref[:, c0 + LANES:c0 + 2 * LANES] + w1 * hi1 + w2 * hi2

    bufs = (cba_ref, cbb_ref)

    @pl.when(j == 0)
    def _first_pack():
        zero_slot = jnp.zeros((PACK_ROWS, LANES), U32)
        for e in range(N_EXPERTS):
            p = (off_ref[sb * N_EXPERTS + e] + cnt_ref[sb * N_EXPERTS + e]) * PACK_ROWS
            xy_ref[pl.ds(pl.multiple_of(p, PACK_ROWS), PACK_ROWS), :] = zero_slot
        tail0 = 2 * tb * PACK_ROWS
        xy_ref[tail0:, :] = jnp.zeros((xy_ref.shape[0] - tail0, LANES), U32)
        pack(bufs[0])

    for parity in range(2):
        @pl.when((j >= 1) & (j < n_scatter) & (j % 2 == parity))
        def _scatter_and_pack():
            pack(bufs[parity])
            scatter(bufs[1 - parity], j - 1)

    @pl.when(j == n_scatter)
    def _last_scatter():
        scatter(bufs[(n_scatter - 1) % 2], n_scatter - 1)

    @pl.when((j >= first_expert_step) & (j < first_combine_step))
    def _experts():
        e = sb * N_EXPERTS + j - first_expert_step
        n = cnt_ref[e]
        base = off_ref[e]

        def body(i, c):
            r0 = base + i * MOE_CHUNK
            row0 = pl.multiple_of(r0 * PACK_ROWS, 2 * PACK_ROWS)
            halves = [_unpack_bf16_pair(xy_ref[pl.ds(row0 + r, MOE_CHUNK, stride=PACK_ROWS), :])
                      for r in range(PACK_ROWS)]
            a = jnp.zeros((MOE_CHUNK, D_EXPERT), F32)
            u = jnp.zeros((MOE_CHUNK, D_EXPERT), F32)
            for r in range(PACK_ROWS):
                lo, hi = halves[r]
                xb = jnp.concatenate([lo, hi], axis=1).astype(BF16)
                a = a + _dot(xb, wg_ref[0, r * 2 * LANES:(r + 1) * 2 * LANES, :])
                u = u + _dot(xb, wu_ref[0, r * 2 * LANES:(r + 1) * 2 * LANES, :])
            hid = ((a * jax.nn.sigmoid(a)) * u).astype(BF16)
            y = _dot(hid, wd_ref[0])
            valid = (r0 + lax.broadcasted_iota(jnp.int32, (MOE_CHUNK, 1), 0)) < (base + n)
            for r in range(PACK_ROWS):
                c0 = r * 2 * LANES
                lo, hi = halves[r]
                xy_ref[pl.ds(row0 + r, MOE_CHUNK, stride=PACK_ROWS), :] = _pack_bf16_pair(
                    jnp.where(valid, y[:, c0:c0 + LANES], lo), jnp.where(valid, y[:, c0 + LANES:c0 + 2 * LANES], hi))
            return c

        lax.fori_loop(0, (n + MOE_CHUNK - 1) // MOE_CHUNK, body, 0)

    cstep = j - first_combine_step

    @pl.when(cstep == 0)
    def _first_gather():
        gather(bufs[0], 0)

    for parity in range(2):
        @pl.when((cstep >= 1) & (cstep < n_combine) & (cstep % 2 == parity))
        def _gather_and_sum():
            gather(bufs[parity], cstep)
            weighted_sum(bufs[1 - parity])

    @pl.when(cstep == n_combine)
    def _last_sum():
        weighted_sum(bufs[(n_combine - 1) % 2])


def _full(shape):
    return pl.BlockSpec(shape, lambda *_: (0,) * len(shape))


def _prep_weights(w_in, gk_fwd_w, gk_fwd_b, gk_bwd_w, gk_bwd_b, mla_w_qb, mla_w_kvb, q_norm_gain, k_norm_gain):
    splits = np.cumsum([0, GLA_QK_WIDTH, GLA_QK_WIDTH, GLA_WIDTH, GLA_WIDTH, GLA_GATE_RANK, GLA_GATE_RANK,
                        MLA_Q_RANK, MLA_KV_RANK, MLA_ROPE])
    d = w_in.shape[0]
    win = w_in
    wt = w_in[:, splits[4]:]
    rel = splits - splits[4]
    gap = jnp.zeros((d, ROPE_LO - 2 * GLA_GATE_RANK), F32)
    end = jnp.zeros((d, LANES - ROPE_LO - MLA_ROPE), F32)
    wtail = jnp.concatenate([wt[:, rel[6]:rel[8]], wt[:, rel[4]:rel[6]], gap, wt[:, rel[8]:rel[9]], end],
                            axis=1).astype(BF16)

    zq = jnp.zeros((GLA_GATE_RANK, GLA_QK_WIDTH), F32)
    wg = jnp.concatenate([jnp.concatenate([gk_fwd_w, zq], axis=1), jnp.concatenate([zq, gk_bwd_w], axis=1),
                          jnp.zeros((LANES - 2 * GLA_GATE_RANK, 2 * GLA_QK_WIDTH), F32)], axis=0)
    bg = jnp.concatenate([gk_fwd_b, gk_bwd_b])[None, :]

    pad = HEAD_TILE - MLA_QK
    wqb = jnp.pad(mla_w_qb.reshape(MLA_Q_RANK, MLA_HEADS, MLA_QK), ((0, 0), (0, 0), (0, pad)))
    wqb = wqb.reshape(MLA_Q_RANK, MLA_HEADS * HEAD_TILE)
    wkv = mla_w_kvb.reshape(MLA_KV_RANK, MLA_HEADS, MLA_NOPE + MLA_V)
    wkvk = jnp.pad(wkv[:, :, :MLA_NOPE], ((0, 0), (0, 0), (0, HEAD_TILE - MLA_NOPE)))
    wkvk = wkvk.reshape(MLA_KV_RANK, MLA_HEADS * HEAD_TILE).astype(BF16)
    wkvv = wkv[:, :, MLA_NOPE:].reshape(MLA_KV_RANK, MLA_WIDTH).T.astype(BF16)
    qng = jnp.pad(q_norm_gain, (0, pad))[None, :]
    kng = jnp.pad(k_norm_gain, (0, pad))[None, :]
    def partner(w, sign):
        first, second = w[..., ROPE_LO:ROPE_LO + ROPE_HALF], w[..., ROPE_LO + ROPE_HALF:ROPE_LO + MLA_ROPE]
        return jnp.concatenate([jnp.zeros_like(w[..., :ROPE_LO]), sign * second, first,
                                jnp.zeros_like(w[..., ROPE_LO + MLA_ROPE:])], axis=-1)

    wqb_rot = partner(wqb.reshape(MLA_Q_RANK, MLA_HEADS, HEAD_TILE), -1.0).reshape(MLA_Q_RANK, -1).astype(BF16)
    qng_rot = partner(qng, 1.0)
    kng_rot = partner(kng, 1.0)
    return win, wtail, wg.astype(BF16), bg, wqb.astype(BF16), wqb_rot, wkvk, wkvv, qng, qng_rot, kng, kng_rot


def _rope_consts():
    inv = ROPE_BASE ** (-np.arange(0, MLA_ROPE, 2, dtype=np.float32) / MLA_ROPE)
    one = np.zeros((ROPE_HALF, LANES), np.float32)
    one[np.arange(ROPE_HALF), ROPE_LO + np.arange(ROPE_HALF)] = 1.0
    one[np.arange(ROPE_HALF), ROPE_LO + ROPE_HALF + np.arange(ROPE_HALF)] = 1.0
    zero = np.zeros_like(one)
    place = np.block([[one, zero], [one, zero], [zero, one], [zero, one]])
    ones_bd = np.kron(np.eye(2, dtype=np.float32), np.ones((LANES, LANES), np.float32))
    return jnp.asarray(inv[:, None]), jnp.asarray(place, BF16), jnp.asarray(ones_bd, BF16)


def _layer(x2, pos2, batch, seq, norm1_gain, w_in, gk_fwd_w, gk_fwd_b, gk_bwd_w, gk_bwd_b, gla_out_gain,
           mla_q_gain, mla_w_qb, mla_kv_gain, mla_w_kvb, q_norm_gain, k_norm_gain, w_out, norm2_gain,
           w_router_group, b_router_group, w_router_expert, b_router_expert,
           w_expert_gate, w_expert_up, w_expert_down):
    T, D = x2.shape
    win, wtail, wg, bg, wqb, wqb_rot, wkvk, wkvv, qng, qng_rot, kng, kng_rot = _prep_weights(
        w_in, gk_fwd_w, gk_fwd_b, gk_bwd_w, gk_bwd_b, mla_w_qb, mla_w_kvb, q_norm_gain, k_norm_gain)

    TM = min(PROJ_TILE, seq)
    spb = seq // TM
    tok = lambda w: pl.BlockSpec((TM, w), lambda i: (i, 0))
    head_spec = pl.BlockSpec((1, MLA_HEADS, TM, HEAD_TILE), lambda i: (i // spb, 0, i % spb, 0))
    gq, gk, gv, gg, lg, mq, mk, mv = pl.pallas_call(
        _proj_kernel,
        grid=(T // TM,),
        in_specs=[tok(D), pl.BlockSpec((1, 1, TM), lambda i: (i, 0, 0)), _full((1, D)),
                  pl.BlockSpec(w_in.shape, lambda i: (0, 0), pipeline_mode=pl.Buffered(1)),
                  _full((D, TAIL_WIDTH)),
                  _full((LANES, 2 * GLA_QK_WIDTH)), _full((1, 2 * GLA_QK_WIDTH)), _full((1, MLA_Q_RANK)),
                  _full((MLA_Q_RANK, MLA_HEADS * HEAD_TILE)), _full((MLA_Q_RANK, MLA_HEADS * HEAD_TILE)),
                  _full((1, MLA_KV_RANK)), _full((MLA_KV_RANK, MLA_HEADS * HEAD_TILE)),
                  _full((MLA_WIDTH, MLA_KV_RANK)), _full((1, HEAD_TILE)), _full((1, HEAD_TILE)),
                  _full((1, HEAD_TILE)), _full((1, HEAD_TILE)),
                  _full((ROPE_HALF, 1)), _full((4 * ROPE_HALF, 2 * LANES)), _full((2 * LANES, 2 * LANES))],
        out_specs=[tok(GLA_QK_WIDTH), tok(GLA_QK_WIDTH), tok(GLA_WIDTH), tok(GLA_WIDTH), tok(2 * GLA_QK_WIDTH),
                   head_spec, head_spec, pl.BlockSpec((1, MLA_WIDTH, TM), lambda i: (i // spb, 0, i % spb))],
        out_shape=[jax.ShapeDtypeStruct((T, GLA_QK_WIDTH), BF16), jax.ShapeDtypeStruct((T, GLA_QK_WIDTH), BF16),
                   jax.ShapeDtypeStruct((T, GLA_WIDTH), BF16), jax.ShapeDtypeStruct((T, GLA_WIDTH), BF16),
                   jax.ShapeDtypeStruct((T, 2 * GLA_QK_WIDTH), F32),
                   jax.ShapeDtypeStruct((batch, MLA_HEADS, seq, HEAD_TILE), BF16),
                   jax.ShapeDtypeStruct((batch, MLA_HEADS, seq, HEAD_TILE), BF16),
                   jax.ShapeDtypeStruct((batch, MLA_WIDTH, seq), BF16)],
        scratch_shapes=[pltpu.VMEM((D, COL_GLA_END), BF16)],
        compiler_params=pltpu.CompilerParams(dimension_semantics=("arbitrary",), vmem_limit_bytes=VMEM_LIMIT),
        name="proj",
    )(x2, pos2.reshape(T // TM, 1, TM), norm1_gain[None, :], win, wtail, wg, bg, mla_q_gain[None, :], wqb, wqb_rot,
      mla_kv_gain[None, :], wkvk, wkvv, qng, qng_rot, kng, kng_rot, *_rope_consts())

    seqspec = lambda w: pl.BlockSpec((seq, w), lambda b: (b, 0))
    gla_out = pl.pallas_call(
        _gla_kernel,
        grid=(batch,),
        in_specs=[seqspec(GLA_QK_WIDTH), seqspec(GLA_QK_WIDTH), seqspec(GLA_WIDTH), seqspec(2 * GLA_QK_WIDTH),
                  seqspec(GLA_WIDTH), _full((1, GLA_DV))],
        out_specs=seqspec(GLA_WIDTH),
        out_shape=jax.ShapeDtypeStruct((T, GLA_WIDTH), BF16),
        scratch_shapes=[pltpu.VMEM((seq, GLA_WIDTH), F32), pltpu.VMEM((seq, GLA_WIDTH), F32),
                        pltpu.VMEM((GLA_DV, GLA_QK_WIDTH), F32), pltpu.VMEM((GLA_DV, GLA_QK_WIDTH), F32)],
        compiler_params=pltpu.CompilerParams(dimension_semantics=("parallel",), vmem_limit_bytes=VMEM_LIMIT),
        name="gla",
    )(gq, gk, gv, lg, gg, gla_out_gain[None, :])

    TQ = 256
    nq = seq // TQ
    n_steps = batch * nq
    up_rows = N_EXPERTS * D // n_steps
    down_rows = N_EXPERTS * D_EXPERT // n_steps
    wslice = lambda rows, cols: pl.BlockSpec((rows, cols), lambda b, i: (b * nq + i, 0))
    mla_out, wge, wue, wde = pl.pallas_call(
        _attn_kernel,
        grid=(batch, nq),
        in_specs=[pl.BlockSpec((1, MLA_HEADS, TQ, HEAD_TILE), lambda b, i: (b, 0, i, 0)),
                  pl.BlockSpec((1, MLA_HEADS, seq, HEAD_TILE), lambda b, i: (b, 0, 0, 0)),
                  pl.BlockSpec((1, MLA_WIDTH, seq), lambda b, i: (b, 0, 0)),
                  wslice(up_rows, D_EXPERT), wslice(up_rows, D_EXPERT), wslice(down_rows, D)],
        out_specs=[pl.BlockSpec((1, MLA_WIDTH, TQ), lambda b, i: (b, 0, i)),
                   wslice(up_rows, D_EXPERT), wslice(up_rows, D_EXPERT), wslice(down_rows, D)],
        out_shape=[jax.ShapeDtypeStruct((batch, MLA_WIDTH, seq), BF16),
                   jax.ShapeDtypeStruct((N_EXPERTS * D, D_EXPERT), BF16),
                   jax.ShapeDtypeStruct((N_EXPERTS * D, D_EXPERT), BF16),
                   jax.ShapeDtypeStruct((N_EXPERTS * D_EXPERT, D), BF16)],
        compiler_params=pltpu.CompilerParams(dimension_semantics=("parallel", "parallel"),
                                             vmem_limit_bytes=VMEM_LIMIT),
        name="attn",
    )(mq, mk, mv, w_expert_gate.reshape(N_EXPERTS * D, D_EXPERT), w_expert_up.reshape(N_EXPERTS * D, D_EXPERT),
      w_expert_down.reshape(N_EXPERTS * D_EXPERT, D))

    row_gap = ROUTER_EXPERT_ROW - N_GROUPS
    row_tail = ROUTER_ROWS - ROUTER_EXPERT_ROW - N_EXPERTS
    wr = jnp.concatenate([w_router_group, jnp.zeros((D, row_gap), F32), w_router_expert,
                          jnp.zeros((D, row_tail), F32)], axis=1).T
    br = jnp.concatenate([b_router_group, jnp.zeros((row_gap,), F32), b_router_expert,
                          jnp.zeros((row_tail,), F32)])[:, None]
    tb = min(MOE_SUPER_BLOCK, T)
    nsb = T // tb
    TO = min(OUTPROJ_TILE, seq)
    tiles_per_sb = tb // TO
    tok = lambda w: pl.BlockSpec((TO, w), lambda i: (i, 0))
    tri = jnp.asarray(np.triu(np.ones((TO, TO), np.float32)), BF16)
    x1, h2, route, cnt = pl.pallas_call(
        functools.partial(_outproj_kernel, tiles_per_sb),
        grid=(T // TO,),
        in_specs=[tok(D), tok(GLA_WIDTH),
                  pl.BlockSpec((1, MLA_WIDTH, TO), lambda i: (i // (seq // TO), 0, i % (seq // TO))),
                  _full((D, D)), _full((1, D)),
                  _full((ROUTER_ROWS, D)), _full((ROUTER_ROWS, 1)), _full((TO, TO))],
        out_specs=[tok(D), tok(D), pl.BlockSpec((8, TO), lambda i: (0, i)),
                   pl.BlockSpec((1, N_EXPERTS, LANES), lambda i: (i // tiles_per_sb, 0, 0))],
        out_shape=[jax.ShapeDtypeStruct((T, D), F32), jax.ShapeDtypeStruct((T, D), BF16),
                   jax.ShapeDtypeStruct((8, T), F32), jax.ShapeDtypeStruct((nsb, N_EXPERTS, LANES), F32)],
        scratch_shapes=[pltpu.VMEM((N_EXPERTS, 1), F32), pltpu.VMEM((D, D), BF16)],
        compiler_params=pltpu.CompilerParams(dimension_semantics=("arbitrary",), vmem_limit_bytes=VMEM_LIMIT),
        name="outproj",
    )(x2, gla_out, mla_out, w_out, norm2_gain[None, :], wr.astype(BF16), br, tri)

    strict = jnp.asarray(np.tril(np.ones((N_EXPERTS, N_EXPERTS), np.float32), -1), BF16)
    pos, offs = pl.pallas_call(
        _slots_kernel,
        grid=(nsb,),
        in_specs=[pl.BlockSpec((8, tb), lambda s: (0, s)), pl.BlockSpec((1, N_EXPERTS, LANES), lambda s: (s, 0, 0)),
                  _full((N_EXPERTS, N_EXPERTS))],
        out_specs=[pl.BlockSpec((1, 2, tb), lambda s: (s, 0, 0)),
                   pl.BlockSpec((1, N_EXPERTS, LANES), lambda s: (s, 0, 0))],
        out_shape=[jax.ShapeDtypeStruct((nsb, 2, tb), jnp.int32),
                   jax.ShapeDtypeStruct((nsb, N_EXPERTS, LANES), jnp.int32)],
        compiler_params=pltpu.CompilerParams(dimension_semantics=("parallel",)),
        name="slots",
    )(route, cnt, strict)
    pos = pos.reshape(-1)
    flat_off = offs[:, :, 0].reshape(-1)
    counts = cnt[:, :, 0].astype(jnp.int32)
    wge = wge.reshape(N_EXPERTS, D, D_EXPERT)
    wue = wue.reshape(N_EXPERTS, D, D_EXPERT)
    wde = wde.reshape(N_EXPERTS, D_EXPERT, D)
    nct = tb // MOE_COMBINE_TILE
    nst = tb // MOE_SCATTER_TILE
    slots =(2 * tb + N_EXPERTS + MOE_CHUNK + 7) // 8 * 8
    hidx = lambda sb, j, off, cn: (sb * nst + jnp.minimum(j, nst - 1), 0)
    eidx = lambda sb, j, off, cn: (jnp.clip(j - (nst + 1), 0, N_EXPERTS - 1), 0, 0)
    tidx = lambda sb, j, off, cn: (sb * nct + jnp.clip(j - (nst + 1 + N_EXPERTS) - 1, 0, nct - 1), 0)
    out = pl.pallas_call(
        functools.partial(_moe_kernel, tb),
        grid_spec=pltpu.PrefetchScalarGridSpec(
            num_scalar_prefetch=2,
            grid=(nsb, (nst + 1) + N_EXPERTS + (nct + 1)),
            in_specs=[pl.BlockSpec((MOE_SCATTER_TILE, D), hidx),
                      pl.BlockSpec((2 * tb,), lambda sb, j, off, cn: (sb,), memory_space=pltpu.SMEM),
                      pl.BlockSpec((1, D, D_EXPERT), eidx), pl.BlockSpec((1, D, D_EXPERT), eidx),
                      pl.BlockSpec((1, D_EXPERT, D), eidx),
                      pl.BlockSpec((MOE_COMBINE_TILE, D), tidx),
                      pl.BlockSpec((8, MOE_COMBINE_TILE), lambda sb, j, off, cn: (0, tidx(sb, j, off, cn)[0]))],
            out_specs=pl.BlockSpec((MOE_COMBINE_TILE, D), tidx),
            scratch_shapes=[pltpu.VMEM((slots * PACK_ROWS, LANES), U32),
                            pltpu.VMEM((MOE_SCATTER_TILE * PACK_ROWS, LANES), U32),
                            pltpu.VMEM((MOE_SCATTER_TILE * PACK_ROWS, LANES), U32)]),
        out_shape=jax.ShapeDtypeStruct((T, D), F32),
        compiler_params=pltpu.CompilerParams(dimension_semantics=("arbitrary", "arbitrary"),
                                             vmem_limit_bytes=MOE_VMEM_LIMIT),
        name="moe",
    )(flat_off, counts.reshape(-1), h2, pos, wge, wue, wde, x1, route)
    return out


def kernel(x, positions, norm1_gain, w_in, gla_gk_fwd_w, gla_gk_fwd_b, gla_gk_bwd_w, gla_gk_bwd_b, gla_out_gain, mla_q_gain, mla_w_qb, mla_kv_gain, mla_w_kvb, q_norm_gain, k_norm_gain, w_out, norm2_gain, w_router_group, b_router_group, w_router_expert, b_router_expert, w_expert_gate, w_expert_up, w_expert_down):
    batch, seq, d = x.shape
    x2 = x.reshape(batch * seq, d)
    pos2 = positions.reshape(batch * seq, 1)
    for l in range(norm1_gain.shape[0]):
        x2 = _layer(x2, pos2, batch, seq, norm1_gain[l], w_in[l], gla_gk_fwd_w[l], gla_gk_fwd_b[l],
                    gla_gk_bwd_w[l], gla_gk_bwd_b[l], gla_out_gain[l], mla_q_gain[l], mla_w_qb[l],
                    mla_kv_gain[l], mla_w_kvb[l], q_norm_gain[l], k_norm_gain[l], w_out[l], norm2_gain[l],
                    w_router_group[l], b_router_group[l], w_router_expert[l], b_router_expert[l],
                    w_expert_gate[l], w_expert_up[l], w_expert_down[l])
    return x2.reshape(batch, seq, d)
```

```python
import functools

import numpy as np
import jax
import jax.numpy as jnp
from jax import lax
from jax.experimental import pallas as pl
from jax.experimental.pallas import tpu as pltpu

F32 = jnp.float32
BF16 = jnp.bfloat16

D_MODEL = 1024
GLA_HEADS = 4
GLA_DK = 64
GLA_DV = 128
GLA_GATE_RANK = 16
GLA_GATE_NORMALIZER = 16.0
GLA_CHUNK = 64
GLA_CHUNK_UNROLL = 16
MLA_HEADS = 8
MLA_NOPE = 64
MLA_ROPE = 32
MLA_QK = MLA_NOPE + MLA_ROPE
MLA_V = 64
MLA_Q_RANK = 256
MLA_KV_RANK = 128
ROPE_BASE = 10000.0
GLA_QK_WIDTH = GLA_HEADS * GLA_DK
GLA_WIDTH = GLA_HEADS * GLA_DV
MLA_WIDTH = MLA_HEADS * MLA_V
N_GROUPS = 4
EXPERTS_PER_GROUP = 8
N_EXPERTS = N_GROUPS * EXPERTS_PER_GROUP
D_EXPERT = 256
EPS = 1e-6

LANES = 128
HEAD_TILE = LANES
ROPE_LO = MLA_NOPE
ROPE_HALF = MLA_ROPE // 2
LOG2_E = 1.4426950408889634
ATTN_SCORES_AHEAD = 3
ONES_ROWS = 16

COL_GQ = 0
COL_GK = COL_GQ + GLA_QK_WIDTH
COL_GV = COL_GK + GLA_QK_WIDTH
COL_GG = COL_GV + GLA_WIDTH
COL_GLA_END = COL_GG + GLA_WIDTH
TAIL_MQ = 0
TAIL_MKV = TAIL_MQ + MLA_Q_RANK
TAIL_MISC = TAIL_MKV + MLA_KV_RANK
TAIL_WIDTH = TAIL_MISC + LANES

ROUTER_ROWS = LANES
ROUTER_EXPERT_ROW = 8

U32 = jnp.uint32
HI_HALF_MASK = np.uint32(0xFFFF0000)
PACK_ROWS = D_MODEL // (2 * LANES)
MOE_SUPER_BLOCK = 8192
MOE_CHUNK = 576
MOE_EXPERTS_PER_STEP = 2
MOE_SCATTER_TILE = 1024
MOE_COMBINE_TILE = 512
PROJ_TILE = 1024
OUTPROJ_TILE = 1024
assert 2 * MOE_COMBINE_TILE == MOE_SCATTER_TILE

VMEM_LIMIT = 56 * 1024 * 1024
MOE_VMEM_LIMIT = 59900 * 1024


def _dot(a, b):
    return jnp.dot(a, b, preferred_element_type=F32)


def _dot_nt(a, b):
    return lax.dot_general(a, b, (((1,), (1,)), ((), ())), preferred_element_type=F32)


def _dot_tn(a, b):
    return lax.dot_general(a, b, (((0,), (0,)), ((), ())), preferred_element_type=F32)


def _split_bf16(x):
    hi = x.astype(BF16)
    lo = (x - hi.astype(F32)).astype(BF16)
    return hi, lo


def _proj_kernel(x_ref, pos_ref, n1_ref, win_ref, wtail_ref, wg_ref, bg_ref, qgain_ref, wqb_ref, wqbr_ref, kvgain_ref,
                 wkvk_ref, wkvv_ref, qng_ref, qngr_ref, kng_ref, kngr_ref, freq_ref, place_ref, ones_ref,
                 gq_ref, gk_ref, gv_ref, gg_ref, lg_ref, mq_ref, mk_ref, mv_ref):
    x = x_ref[...]
    h = x * lax.rsqrt(jnp.mean(x * x, axis=-1, keepdims=True) + EPS) * n1_ref[...]
    hb = h.astype(BF16)
    proj = _dot(hb, win_ref[...])
    tail = _dot(hb, wtail_ref[...])

    gq_ref[...] = (proj[:, COL_GQ:COL_GK] * GLA_DK ** -0.5).astype(BF16)
    gk_ref[...] = proj[:, COL_GK:COL_GV].astype(BF16)
    gv_ref[...] = proj[:, COL_GV:COL_GG].astype(BF16)
    gg_ref[...] = proj[:, COL_GG:COL_GLA_END].astype(BF16)

    misc = tail[:, TAIL_MISC:TAIL_WIDTH]
    z = _dot(misc.astype(BF16), wg_ref[...]) + bg_ref[...]
    log_sig = -(jnp.maximum(-z, 0.0) + jnp.log(1.0 + jnp.exp(-jnp.abs(z))))
    lg_ref[...] = log_sig / GLA_GATE_NORMALIZER

    lane = lax.broadcasted_iota(jnp.int32, (1, LANES), 1)
    ang_t = freq_ref[...] * pos_ref[0].astype(F32)
    cos_hi, cos_lo = _split_bf16(jnp.cos(ang_t))
    sin_hi, sin_lo = _split_bf16(jnp.sin(ang_t))
    tabs = _dot_tn(jnp.concatenate([cos_hi, cos_lo, sin_hi, sin_lo], axis=0), place_ref[...])
    in_rope = (lane >= ROPE_LO) & (lane < ROPE_LO + MLA_ROPE)
    c_tab = jnp.where(lane < ROPE_LO, 1.0, tabs[:, 0:LANES])
    s_tab = tabs[:, LANES:2 * LANES]
    ones_bd = ones_ref[...]

    def norm_rope_pair(t2, r2, gc, gs, on_mxu):
        if on_mxu:
            ss = _dot((t2 * t2).astype(BF16), ones_bd) * (1.0 / MLA_QK)
        outs = []
        for i in range(2):
            sl = slice(i * LANES, (i + 1) * LANES)
            t = t2[:, sl]
            ssi = ss[:, sl] if on_mxu else jnp.sum(t * t, axis=-1, keepdims=True) * (1.0 / MLA_QK)
            outs.append(lax.rsqrt(ssi + EPS) * (t * gc + r2[:, sl] * gs))
        return outs

    qa = tail[:, TAIL_MQ:TAIL_MKV]
    qn = qa * lax.rsqrt(jnp.mean(qa * qa, axis=-1, keepdims=True) + EPS) * qgain_ref[...]
    qn = qn.astype(BF16)
    mq = _dot(qn, wqb_ref[...])
    mq_rot = _dot(qn, wqbr_ref[...])
    kva = tail[:, TAIL_MKV:TAIL_MISC]
    kvn = (kva * lax.rsqrt(jnp.mean(kva * kva, axis=-1, keepdims=True) + EPS) * kvgain_ref[...]).astype(BF16)
    kn = _dot(kvn, wkvk_ref[...])
    mv_ref[0] = _dot_nt(wkvv_ref[...], kvn).astype(BF16)
    rope_tile = jnp.where(in_rope, misc, 0.0)
    up = (lane >= ROPE_LO + ROPE_HALF) & (lane < ROPE_LO + MLA_ROPE)
    dn = (lane >= ROPE_LO) & (lane < ROPE_LO + ROPE_HALF)
    rope_rot = (jnp.where(up, pltpu.roll(rope_tile, ROPE_HALF, 1), 0.0)
                - jnp.where(dn, pltpu.roll(rope_tile, LANES - ROPE_HALF, 1), 0.0))
    rope2 = jnp.concatenate([rope_tile, rope_tile], axis=1)
    rope_rot2 = jnp.concatenate([rope_rot, rope_rot], axis=1)
    scale = MLA_QK ** -0.5 * LOG2_E
    q_gc, q_gs = (qng_ref[...] * scale) * c_tab, (qngr_ref[...] * scale) * s_tab
    k_gc, k_gs = kng_ref[...] * c_tab, kngr_ref[...] * s_tab
    for hp in range(MLA_HEADS // 2):
        sl = slice(hp * 2 * HEAD_TILE, (hp + 1) * 2 * HEAD_TILE)
        q2 = norm_rope_pair(mq[:, sl], mq_rot[:, sl], q_gc, q_gs, on_mxu=True)
        k2 = norm_rope_pair(kn[:, sl] + rope2, rope_rot2, k_gc, k_gs, on_mxu=False)
        for i in range(2):
            mq_ref[0, 2 * hp + i] = q2[i].astype(BF16)
            mk_ref[0, 2 * hp + i] = k2[i].astype(BF16)


def _gla_kernel(q_ref, k_ref, v_ref, lg_ref, gate_ref, gain_ref, o_ref, accf_ref, accb_ref, stf_ref, stb_ref):
    seq = q_ref.shape[0]
    C = GLA_CHUNK
    n_chunks = seq // C
    HK = GLA_QK_WIDTH

    lane_head = lax.broadcasted_iota(jnp.int32, (1, HK), 1) // GLA_DK
    rowi = lax.broadcasted_iota(jnp.int32, (C, 1), 0)
    srow = lax.broadcasted_iota(jnp.int32, (GLA_HEADS * C, C), 0) % C
    scol = lax.broadcasted_iota(jnp.int32, (GLA_HEADS * C, C), 1)
    gain = gain_ref[...]
    fwd_cfg = (True, C // 2 - 1, C - 1, scol <= srow, 0)
    bwd_cfg = (False, C // 2, 0, scol > srow, HK)

    def stack_heads(t):
        return jnp.concatenate([jnp.where(lane_head == hd, t, 0.0) for hd in range(GLA_HEADS)], axis=0)

    def chunk_step(c, st, cfg, acc_ref, other_acc_ref=None):
        prefix, ref_i, last_i, keep, lg_off = cfg
        rows = pl.ds(pl.multiple_of(c * C, C), C)
        b = lg_ref[rows, lg_off:lg_off + HK]
        shift = 1
        while shift < C:
            if prefix:
                b = b + jnp.where(rowi >= shift, pltpu.roll(b, shift, 0), 0.0)
            else:
                b = b + jnp.where(rowi < C - shift, pltpu.roll(b, C - shift, 0), 0.0)
            shift *= 2
        b_ref = b[ref_i:ref_i + 1, :]
        b_last = b[last_i:last_i + 1, :]
        q = q_ref[rows, :].astype(F32)
        k = k_ref[rows, :].astype(F32)
        v = v_ref[rows, :]
        q_rel = q * jnp.exp(b - b_ref)
        k_rel = (k * jnp.exp(b_ref - b)).astype(BF16)
        k_dec = (k * jnp.exp(b_last - b)).astype(BF16)
        q_dec = q * jnp.exp(b)
        sc = _dot_nt(stack_heads(q_rel).astype(BF16), k_rel)
        sc = jnp.where(keep, sc, 0.0).astype(BF16)
        o_inter = _dot_nt(stack_heads(q_dec).astype(BF16), st.astype(BF16))
        kv_t = _dot_tn(v, k_dec)
        new_st = jnp.exp(b_last) * st
        for hd in range(GLA_HEADS):
            vs = slice(hd * GLA_DV, (hd + 1) * GLA_DV)
            rs = slice(hd * C, (hd + 1) * C)
            o_h = _dot(sc[rs, :], v[:, vs]) + o_inter[rs, :]
            if other_acc_ref is None:
                acc_ref[rows, vs] = o_h
            else:
                o = o_h + other_acc_ref[rows, vs]
                on = o * lax.rsqrt(jnp.mean(o * o, axis=-1, keepdims=True) + EPS) * gain
                gt = gate_ref[rows, vs].astype(F32)
                o_ref[rows, vs] = (on * (gt * jax.nn.sigmoid(gt))).astype(BF16)
            new_st = new_st + jnp.where(lane_head == hd, kv_t[vs, :], 0.0)
        return new_st

    stf_ref[...] = jnp.zeros_like(stf_ref)
    stb_ref[...] = jnp.zeros_like(stb_ref)

    unroll = min(GLA_CHUNK_UNROLL, n_chunks // 2)
    n_iter = n_chunks // unroll

    def make_body(finalize):
        def body(i, carry):
            st_f = stf_ref[...]
            st_b = stb_ref[...]
            for u in range(unroll):
                cf = i * unroll + u
                st_f = chunk_step(cf, st_f, fwd_cfg, accf_ref, accb_ref if finalize else None)
                st_b = chunk_step(n_chunks - 1 - cf, st_b, bwd_cfg, accb_ref, accf_ref if finalize else None)
            stf_ref[...] = st_f
            stb_ref[...] = st_b
            return carry
        return body

    lax.fori_loop(0, n_iter // 2, make_body(False), 0)
    lax.fori_loop(n_iter // 2, n_iter, make_body(True), 0)


def _attn_kernel(q_ref, k_ref, vt_ref, wg_ref, wu_ref, wd_ref, o_ref, wgb_ref, wub_ref, wdb_ref):
    wgb_ref[...] = wg_ref[...].astype(BF16)
    wub_ref[...] = wu_ref[...].astype(BF16)
    wdb_ref[...] = wd_ref[...].astype(BF16)
    seq = k_ref.shape[2]
    ones = jnp.ones((ONES_ROWS, seq), BF16)
    outs = []
    scores = [_dot_nt(q_ref[0, h], k_ref[0, h]) for h in range(ATTN_SCORES_AHEAD)]
    for hd in range(MLA_HEADS):
        if hd + ATTN_SCORES_AHEAD < MLA_HEADS:
            scores.append(_dot_nt(q_ref[0, hd + ATTN_SCORES_AHEAD], k_ref[0, hd + ATTN_SCORES_AHEAD]))
        s = scores[hd]
        p = jnp.exp2(s - jnp.max(s, axis=-1, keepdims=True)).astype(BF16)
        vt_aug = jnp.concatenate([vt_ref[0, hd * MLA_V:(hd + 1) * MLA_V, :], ones], axis=0)
        ot = _dot_nt(vt_aug, p)
        outs.append(ot[0:MLA_V, :] / ot[MLA_V:MLA_V + 1, :])
    o_ref[0] = jnp.concatenate(outs, axis=0).astype(BF16)


def _unpack_bf16_pair(w):
    lo = pltpu.unpack_elementwise(w, index=0, packed_dtype=BF16, unpacked_dtype=F32)
    hi = pltpu.unpack_elementwise(w, index=1, packed_dtype=BF16, unpacked_dtype=F32)
    return lo, hi


def _pack_bf16_pair(lo, hi):
    return pltpu.pack_elementwise([lo, hi], packed_dtype=BF16)


def _outproj_kernel(tiles_per_sb, x_ref, gla_ref, mla_ref, wo_ref, n2_ref, wr_ref, br_ref, tri_ref,
                    x1_ref, h2_ref, route_ref, cnt_ref, carry_ref, wob_ref):
    @pl.when(pl.program_id(0) == 0)
    def _():
        wob_ref[...] = wo_ref[...].astype(BF16)

    mix = _dot(gla_ref[...], wob_ref[0:GLA_WIDTH, :]) + _dot_tn(mla_ref[0], wob_ref[GLA_WIDTH:, :])
    x1 = x_ref[...] + mix
    x1_ref[...] = x1
    h2 = x1 * lax.rsqrt(jnp.mean(x1 * x1, axis=-1, keepdims=True) + EPS) * n2_ref[...]
    h2b = h2.astype(BF16)
    h2_ref[...] = h2b

    logits = _dot_nt(wr_ref[...], h2b) + br_ref[...]
    tm = logits.shape[1]
    gl = logits[0:N_GROUPS, :]
    ge = jnp.exp(gl - jnp.max(gl, axis=0, keepdims=True))
    pg = ge / jnp.sum(ge, axis=0, keepdims=True)
    p_top = jnp.max(pg, axis=0, keepdims=True)
    gi = lax.broadcasted_iota(jnp.int32, (N_GROUPS, tm), 0)
    g_idx = jnp.min(jnp.where(pg == p_top, gi, N_GROUPS), axis=0, keepdims=True)
    sel = jnp.zeros((EXPERTS_PER_GROUP, tm), F32)
    for g in range(N_GROUPS):
        r0 = ROUTER_EXPERT_ROW + g * EXPERTS_PER_GROUP
        sel = sel + jnp.where(g_idx == g, logits[r0:r0 + EXPERTS_PER_GROUP, :], 0.0)
    se = jnp.exp(sel - jnp.max(sel, axis=0, keepdims=True))
    pe = se / jnp.sum(se, axis=0, keepdims=True)
    ei = lax.broadcasted_iota(jnp.int32, (EXPERTS_PER_GROUP, tm), 0)
    m1 = jnp.max(pe, axis=0, keepdims=True)
    i1 = jnp.min(jnp.where(pe == m1, ei, EXPERTS_PER_GROUP), axis=0, keepdims=True)
    pe2 = jnp.where(ei == i1, -1.0, pe)
    m2 = jnp.max(pe2, axis=0, keepdims=True)
    i2 = jnp.min(jnp.where(pe2 == m2, ei, EXPERTS_PER_GROUP), axis=0, keepdims=True)
    den = m1 + m2
    e1 = g_idx * EXPERTS_PER_GROUP + i1
    e2 = g_idx * EXPERTS_PER_GROUP + i2

    @pl.when(pl.program_id(0) % tiles_per_sb == 0)
    def _():
        carry_ref[...] = jnp.zeros_like(carry_ref)

    eall = lax.broadcasted_iota(jnp.int32, (N_EXPERTS, tm), 0)
    tri = tri_ref[...]
    carry = carry_ref[...]
    ranks = []
    for eid in (e1, e2):
        hot = eall == eid
        prefix = _dot(hot.astype(BF16), tri)
        ranks.append(jnp.sum(jnp.where(hot, carry + prefix - 1.0, 0.0), axis=0, keepdims=True))
        carry = carry + prefix[:, tm - 1:tm]
    carry_ref[...] = carry
    cnt_ref[0] = jnp.broadcast_to(carry, (N_EXPERTS, LANES))
    zeros = jnp.zeros((2, tm), F32)
    route_ref[...] = jnp.concatenate([e1.astype(F32), e2.astype(F32), p_top * (m1 / den), p_top * (m2 / den),
                                      ranks[0], ranks[1], zeros], axis=0)


def _slots_kernel(route_ref, cnt_ref, strict_ref, pos_ref, off_ref):
    counts = cnt_ref[0]
    padded = 2.0 * jnp.floor(counts * 0.5 + 0.5)
    hi = jnp.floor(padded * (1.0 / 256.0))
    lo = padded - 256.0 * hi
    strict = strict_ref[...]
    off = 256.0 * _dot(strict, hi.astype(BF16)) + _dot(strict, lo.astype(BF16))
    off_ref[0] = off.astype(jnp.int32)
    tb = route_ref.shape[1]
    eall = lax.broadcasted_iota(jnp.int32, (N_EXPERTS, tb), 0)
    off_col = off[:, 0:1]
    for s in range(2):
        hot = eall == route_ref[s:s + 1, :].astype(jnp.int32)
        base = jnp.sum(jnp.where(hot, off_col, 0.0), axis=0, keepdims=True)
        pos_ref[0, s:s + 1, :] = ((base + route_ref[4 + s:5 + s, :]) * PACK_ROWS).astype(jnp.int32)


def _moe_kernel(tb, off_ref, cnt_ref, h2_ref, pos_ref, wg_ref, wu_ref, wd_ref, x1_ref, wt_ref, o_ref,
                xy_ref, cba_ref, cbb_ref):
    sb = pl.program_id(0)
    j = pl.program_id(1)
    n_scatter = tb // MOE_SCATTER_TILE
    n_combine = tb // MOE_COMBINE_TILE
    first_expert_step = n_scatter + 1
    first_combine_step = first_expert_step + N_EXPERTS // MOE_EXPERTS_PER_STEP

    def pack(dst_ref):
        h = h2_ref[...].astype(F32)
        for r in range(PACK_ROWS):
            c0 = r * 2 * LANES
            dst_ref[pl.ds(r, MOE_SCATTER_TILE, stride=PACK_ROWS), :] = _pack_bf16_pair(
                h[:, c0:c0 + LANES], h[:, c0 + LANES:c0 + 2 * LANES])

    def scatter(src_ref, tile):
        t0 = tile * MOE_SCATTER_TILE
        for pair in range(MOE_SCATTER_TILE // 2):
            v = src_ref[pair * 2 * PACK_ROWS:(pair + 1) * 2 * PACK_ROWS, :]
            for half in range(2):
                piece = v[half * PACK_ROWS:(half + 1) * PACK_ROWS, :]
                for slot in range(2):
                    p = pos_ref[slot * tb + t0 + pair * 2 + half]
                    xy_ref[pl.ds(pl.multiple_of(p, PACK_ROWS), PACK_ROWS), :] = piece

    def gather(dst_ref, tile):
        t0 = tile * MOE_COMBINE_TILE
        for t in range(MOE_COMBINE_TILE):
            for slot in range(2):
                p = pos_ref[slot * tb + t0 + t]
                r0 = (slot * MOE_COMBINE_TILE + t) * PACK_ROWS
                dst_ref[r0:r0 + PACK_ROWS, :] = xy_ref[pl.ds(pl.multiple_of(p, PACK_ROWS), PACK_ROWS), :]

    def weighted_sum(src_ref):
        wt = wt_ref[...].T
        w1 = wt[:, 2:3]
        w2 = wt[:, 3:4]
        second = MOE_COMBINE_TILE * PACK_ROWS
        for r in range(PACK_ROWS):
            lo1, hi1 = _unpack_bf16_pair(src_ref[pl.ds(r, MOE_COMBINE_TILE, stride=PACK_ROWS), :])
            lo2, hi2 = _unpack_bf16_pair(src_ref[pl.ds(second + r, MOE_COMBINE_TILE, stride=PACK_ROWS), :])
            c0 = r * 2 * LANES
            o_ref[:, c0:c0 + LANES] = x1_ref[:, c0:c0 + LANES] + w1 * lo1 + w2 * lo2
            o_ref[:, c0 + LANES:c0 + 2 * LANES] = x1_ref[:, c0 + LANES:c0 + 2 * LANES] + w1 * hi1 + w2 * hi2

    bufs = (cba_ref, cbb_ref)

    @pl.when(j == 0)
    def _first_pack():
        zero_slot = jnp.zeros((PACK_ROWS, LANES), U32)
        for e in range(N_EXPERTS):
            p = (off_ref[sb * N_EXPERTS + e] + cnt_ref[sb * N_EXPERTS + e]) * PACK_ROWS
            xy_ref[pl.ds(pl.multiple_of(p, PACK_ROWS), PACK_ROWS), :] = zero_slot
        tail0 = 2 * tb * PACK_ROWS
        xy_ref[tail0:, :] = jnp.zeros((xy_ref.shape[0] - tail0, LANES), U32)
        pack(bufs[0])

    for parity in range(2):
        @pl.when((j >= 1) & (j < n_scatter) & (j % 2 == parity))
        def _scatter_and_pack():
            pack(bufs[parity])
            scatter(bufs[1 - parity], j - 1)

    @pl.when(j == n_scatter)
    def _last_scatter():
        scatter(bufs[(n_scatter - 1) % 2], n_scatter - 1)

    def run_expert(k, e):
        n = cnt_ref[e]
        base = off_ref[e]

        def body(i, c):
            r0 = base + i * MOE_CHUNK
            row0 = pl.multiple_of(r0 * PACK_ROWS, 2 * PACK_ROWS)
            halves = [_unpack_bf16_pair(xy_ref[pl.ds(row0 + r, MOE_CHUNK, stride=PACK_ROWS), :])
                      for r in range(PACK_ROWS)]
            a = jnp.zeros((MOE_CHUNK, D_EXPERT), F32)
            u = jnp.zeros((MOE_CHUNK, D_EXPERT), F32)
            for r in range(PACK_ROWS):
                lo, hi = halves[r]
                xb = jnp.concatenate([lo, hi], axis=1).astype(BF16)
                a = a + _dot(xb, wg_ref[k, r * 2 * LANES:(r + 1) * 2 * LANES, :])
                u = u + _dot(xb, wu_ref[k, r * 2 * LANES:(r + 1) * 2 * LANES, :])
            hid = ((a * jax.nn.sigmoid(a)) * u).astype(BF16)
            y = _dot(hid, wd_ref[k])
            valid = (r0 + lax.broadcasted_iota(jnp.int32, (MOE_CHUNK, 1), 0)) < (base + n)
            for r in range(PACK_ROWS):
                c0 = r * 2 * LANES
                lo, hi = halves[r]
                xy_ref[pl.ds(row0 + r, MOE_CHUNK, stride=PACK_ROWS), :] = _pack_bf16_pair(
                    jnp.where(valid, y[:, c0:c0 + LANES], lo), jnp.where(valid, y[:, c0 + LANES:c0 + 2 * LANES], hi))
            return c

        lax.fori_loop(0, (n + MOE_CHUNK - 1) // MOE_CHUNK, body, 0)

    @pl.when((j >= first_expert_step) & (j < first_combine_step))
    def _experts():
        for k in range(MOE_EXPERTS_PER_STEP):
            run_expert(k, sb * N_EXPERTS + (j - first_expert_step) * MOE_EXPERTS_PER_STEP + k)

    cstep = j - first_combine_step

    @pl.when(cstep == 0)
    def _first_gather():
        gather(bufs[0], 0)

    for parity in range(2):
        @pl.when((cstep >= 1) & (cstep < n_combine) & (cstep % 2 == parity))
        def _gather_and_sum():
            gather(bufs[parity], cstep)
            weighted_sum(bufs[1 - parity])

    @pl.when(cstep == n_combine)
    def _last_sum():
        weighted_sum(bufs[(n_combine - 1) % 2])


def _full(shape):
    return pl.BlockSpec(shape, lambda *_: (0,) * len(shape))


def _prep_weights(w_in, gk_fwd_w, gk_fwd_b, gk_bwd_w, gk_bwd_b, mla_w_qb, mla_w_kvb, q_norm_gain, k_norm_gain):
    splits = np.cumsum([0, GLA_QK_WIDTH, GLA_QK_WIDTH, GLA_WIDTH, GLA_WIDTH, GLA_GATE_RANK, GLA_GATE_RANK,
                        MLA_Q_RANK, MLA_KV_RANK, MLA_ROPE])
    d = w_in.shape[0]
    win = w_in[:, :splits[4]].astype(BF16)
    wt = w_in[:, splits[4]:]
    rel = splits - splits[4]
    gap = jnp.zeros((d, ROPE_LO - 2 * GLA_GATE_RANK), F32)
    end = jnp.zeros((d, LANES - ROPE_LO - MLA_ROPE), F32)
    wtail = jnp.concatenate([wt[:, rel[6]:rel[8]], wt[:, rel[4]:rel[6]], gap, wt[:, rel[8]:rel[9]], end],
                            axis=1).astype(BF16)

    zq = jnp.zeros((GLA_GATE_RANK, GLA_QK_WIDTH), F32)
    wg = jnp.concatenate([jnp.concatenate([gk_fwd_w, zq], axis=1), jnp.concatenate([zq, gk_bwd_w], axis=1),
                          jnp.zeros((LANES - 2 * GLA_GATE_RANK, 2 * GLA_QK_WIDTH), F32)], axis=0)
    bg = jnp.concatenate([gk_fwd_b, gk_bwd_b])[None, :]

    pad = HEAD_TILE - MLA_QK
    wqb = jnp.pad(mla_w_qb.reshape(MLA_Q_RANK, MLA_HEADS, MLA_QK), ((0, 0), (0, 0), (0, pad)))
    wqb = wqb.reshape(MLA_Q_RANK, MLA_HEADS * HEAD_TILE)
    wkv = mla_w_kvb.reshape(MLA_KV_RANK, MLA_HEADS, MLA_NOPE + MLA_V)
    wkvk = jnp.pad(wkv[:, :, :MLA_NOPE], ((0, 0), (0, 0), (0, HEAD_TILE - MLA_NOPE)))
    wkvk = wkvk.reshape(MLA_KV_RANK, MLA_HEADS * HEAD_TILE).astype(BF16)
    wkvv = wkv[:, :, MLA_NOPE:].reshape(MLA_KV_RANK, MLA_WIDTH).T.astype(BF16)
    qng = jnp.pad(q_norm_gain, (0, pad))[None, :]
    kng = jnp.pad(k_norm_gain, (0, pad))[None, :]
    def partner(w, sign):
        first, second = w[..., ROPE_LO:ROPE_LO + ROPE_HALF], w[..., ROPE_LO + ROPE_HALF:ROPE_LO + MLA_ROPE]
        return jnp.concatenate([jnp.zeros_like(w[..., :ROPE_LO]), sign * second, first,
                                jnp.zeros_like(w[..., ROPE_LO + MLA_ROPE:])], axis=-1)

    wqb_rot = partner(wqb.reshape(MLA_Q_RANK, MLA_HEADS, HEAD_TILE), -1.0).reshape(MLA_Q_RANK, -1).astype(BF16)
    qng_rot = partner(qng, 1.0)
    kng_rot = partner(kng, 1.0)
    return win, wtail, wg.astype(BF16), bg, wqb.astype(BF16), wqb_rot, wkvk, wkvv, qng, qng_rot, kng, kng_rot


def _rope_consts():
    inv = ROPE_BASE ** (-np.arange(0, MLA_ROPE, 2, dtype=np.float32) / MLA_ROPE)
    one = np.zeros((ROPE_HALF, LANES), np.float32)
    one[np.arange(ROPE_HALF), ROPE_LO + np.arange(ROPE_HALF)] = 1.0
    one[np.arange(ROPE_HALF), ROPE_LO + ROPE_HALF + np.arange(ROPE_HALF)] = 1.0
    zero = np.zeros_like(one)
    place = np.block([[one, zero], [one, zero], [zero, one], [zero, one]])
    ones_bd = np.kron(np.eye(2, dtype=np.float32), np.ones((LANES, LANES), np.float32))
    return jnp.asarray(inv[:, None]), jnp.asarray(place, BF16), jnp.asarray(ones_bd, BF16)


def _layer(x2, pos2, batch, seq, norm1_gain, w_in, gk_fwd_w, gk_fwd_b, gk_bwd_w, gk_bwd_b, gla_out_gain,
           mla_q_gain, mla_w_qb, mla_kv_gain, mla_w_kvb, q_norm_gain, k_norm_gain, w_out, norm2_gain,
           w_router_group, b_router_group, w_router_expert, b_router_expert,
           w_expert_gate, w_expert_up, w_expert_down):
    T, D = x2.shape
    win, wtail, wg, bg, wqb, wqb_rot, wkvk, wkvv, qng, qng_rot, kng, kng_rot = _prep_weights(
        w_in, gk_fwd_w, gk_fwd_b, gk_bwd_w, gk_bwd_b, mla_w_qb, mla_w_kvb, q_norm_gain, k_norm_gain)

    TM = min(PROJ_TILE, seq)
    spb = seq // TM
    tok = lambda w: pl.BlockSpec((TM, w), lambda i: (i, 0))
    head_spec = pl.BlockSpec((1, MLA_HEADS, TM, HEAD_TILE), lambda i: (i // spb, 0, i % spb, 0))
    gq, gk, gv, gg, lg, mq, mk, mv = pl.pallas_call(
        _proj_kernel,
        grid=(T // TM,),
        in_specs=[tok(D), pl.BlockSpec((1, 1, TM), lambda i: (i, 0, 0)), _full((1, D)), _full((D, COL_GLA_END)),
                  _full((D, TAIL_WIDTH)),
                  _full((LANES, 2 * GLA_QK_WIDTH)), _full((1, 2 * GLA_QK_WIDTH)), _full((1, MLA_Q_RANK)),
                  _full((MLA_Q_RANK, MLA_HEADS * HEAD_TILE)), _full((MLA_Q_RANK, MLA_HEADS * HEAD_TILE)),
                  _full((1, MLA_KV_RANK)), _full((MLA_KV_RANK, MLA_HEADS * HEAD_TILE)),
                  _full((MLA_WIDTH, MLA_KV_RANK)), _full((1, HEAD_TILE)), _full((1, HEAD_TILE)),
                  _full((1, HEAD_TILE)), _full((1, HEAD_TILE)),
                  _full((ROPE_HALF, 1)), _full((4 * ROPE_HALF, 2 * LANES)), _full((2 * LANES, 2 * LANES))],
        out_specs=[tok(GLA_QK_WIDTH), tok(GLA_QK_WIDTH), tok(GLA_WIDTH), tok(GLA_WIDTH), tok(2 * GLA_QK_WIDTH),
                   head_spec, head_spec, pl.BlockSpec((1, MLA_WIDTH, TM), lambda i: (i // spb, 0, i % spb))],
        out_shape=[jax.ShapeDtypeStruct((T, GLA_QK_WIDTH), BF16), jax.ShapeDtypeStruct((T, GLA_QK_WIDTH), BF16),
                   jax.ShapeDtypeStruct((T, GLA_WIDTH), BF16), jax.ShapeDtypeStruct((T, GLA_WIDTH), BF16),
                   jax.ShapeDtypeStruct((T, 2 * GLA_QK_WIDTH), F32),
                   jax.ShapeDtypeStruct((batch, MLA_HEADS, seq, HEAD_TILE), BF16),
                   jax.ShapeDtypeStruct((batch, MLA_HEADS, seq, HEAD_TILE), BF16),
                   jax.ShapeDtypeStruct((batch, MLA_WIDTH, seq), BF16)],
        compiler_params=pltpu.CompilerParams(dimension_semantics=("parallel",), vmem_limit_bytes=VMEM_LIMIT),
        name="proj",
    )(x2, pos2.reshape(T // TM, 1, TM), norm1_gain[None, :], win, wtail, wg, bg, mla_q_gain[None, :], wqb, wqb_rot,
      mla_kv_gain[None, :], wkvk, wkvv, qng, qng_rot, kng, kng_rot, *_rope_consts())

    seqspec = lambda w: pl.BlockSpec((seq, w), lambda b: (b, 0))
    gla_out = pl.pallas_call(
        _gla_kernel,
        grid=(batch,),
        in_specs=[seqspec(GLA_QK_WIDTH), seqspec(GLA_QK_WIDTH), seqspec(GLA_WIDTH), seqspec(2 * GLA_QK_WIDTH),
                  seqspec(GLA_WIDTH), _full((1, GLA_DV))],
        out_specs=seqspec(GLA_WIDTH),
        out_shape=jax.ShapeDtypeStruct((T, GLA_WIDTH), BF16),
        scratch_shapes=[pltpu.VMEM((seq, GLA_WIDTH), F32), pltpu.VMEM((seq, GLA_WIDTH), F32),
                        pltpu.VMEM((GLA_DV, GLA_QK_WIDTH), F32), pltpu.VMEM((GLA_DV, GLA_QK_WIDTH), F32)],
        compiler_params=pltpu.CompilerParams(dimension_semantics=("parallel",), vmem_limit_bytes=VMEM_LIMIT),
        name="gla",
    )(gq, gk, gv, lg, gg, gla_out_gain[None, :])

    TQ = 256
    nq = seq // TQ
    n_steps = batch * nq
    up_rows = N_EXPERTS * D // n_steps
    down_rows = N_EXPERTS * D_EXPERT // n_steps
    wslice = lambda rows, cols: pl.BlockSpec((rows, cols), lambda b, i: (b * nq + i, 0))
    mla_out, wge, wue, wde = pl.pallas_call(
        _attn_kernel,
        grid=(batch, nq),
        in_specs=[pl.BlockSpec((1, MLA_HEADS, TQ, HEAD_TILE), lambda b, i: (b, 0, i, 0)),
                  pl.BlockSpec((1, MLA_HEADS, seq, HEAD_TILE), lambda b, i: (b, 0, 0, 0)),
                  pl.BlockSpec((1, MLA_WIDTH, seq), lambda b, i: (b, 0, 0)),
                  wslice(up_rows, D_EXPERT), wslice(up_rows, D_EXPERT), wslice(down_rows, D)],
        out_specs=[pl.BlockSpec((1, MLA_WIDTH, TQ), lambda b, i: (b, 0, i)),
                   wslice(up_rows, D_EXPERT), wslice(up_rows, D_EXPERT), wslice(down_rows, D)],
        out_shape=[jax.ShapeDtypeStruct((batch, MLA_WIDTH, seq), BF16),
                   jax.ShapeDtypeStruct((N_EXPERTS * D, D_EXPERT), BF16),
                   jax.ShapeDtypeStruct((N_EXPERTS * D, D_EXPERT), BF16),
                   jax.ShapeDtypeStruct((N_EXPERTS * D_EXPERT, D), BF16)],
        compiler_params=pltpu.CompilerParams(dimension_semantics=("parallel", "parallel"),
                                             vmem_limit_bytes=VMEM_LIMIT),
        name="attn",
    )(mq, mk, mv, w_expert_gate.reshape(N_EXPERTS * D, D_EXPERT), w_expert_up.reshape(N_EXPERTS * D, D_EXPERT),
      w_expert_down.reshape(N_EXPERTS * D_EXPERT, D))

    row_gap = ROUTER_EXPERT_ROW - N_GROUPS
    row_tail = ROUTER_ROWS - ROUTER_EXPERT_ROW - N_EXPERTS
    wr = jnp.concatenate([w_router_group, jnp.zeros((D, row_gap), F32), w_router_expert,
                          jnp.zeros((D, row_tail), F32)], axis=1).T
    br = jnp.concatenate([b_router_group, jnp.zeros((row_gap,), F32), b_router_expert,
                          jnp.zeros((row_tail,), F32)])[:, None]
    tb = min(MOE_SUPER_BLOCK, T)
    nsb = T // tb
    TO = min(OUTPROJ_TILE, seq)
    tiles_per_sb = tb // TO
    tok = lambda w: pl.BlockSpec((TO, w), lambda i: (i, 0))
    tri = jnp.asarray(np.triu(np.ones((TO, TO), np.float32)), BF16)
    x1, h2, route, cnt = pl.pallas_call(
        functools.partial(_outproj_kernel, tiles_per_sb),
        grid=(T // TO,),
        in_specs=[tok(D), tok(GLA_WIDTH),
                  pl.BlockSpec((1, MLA_WIDTH, TO), lambda i: (i // (seq // TO), 0, i % (seq // TO))),
                  _full((D, D)), _full((1, D)),
                  _full((ROUTER_ROWS, D)), _full((ROUTER_ROWS, 1)), _full((TO, TO))],
        out_specs=[tok(D), tok(D), pl.BlockSpec((8, TO), lambda i: (0, i)),
                   pl.BlockSpec((1, N_EXPERTS, LANES), lambda i: (i // tiles_per_sb, 0, 0))],
        out_shape=[jax.ShapeDtypeStruct((T, D), F32), jax.ShapeDtypeStruct((T, D), BF16),
                   jax.ShapeDtypeStruct((8, T), F32), jax.ShapeDtypeStruct((nsb, N_EXPERTS, LANES), F32)],
        scratch_shapes=[pltpu.VMEM((N_EXPERTS, 1), F32), pltpu.VMEM((D, D), BF16)],
        compiler_params=pltpu.CompilerParams(dimension_semantics=("arbitrary",), vmem_limit_bytes=VMEM_LIMIT),
        name="outproj",
    )(x2, gla_out, mla_out, w_out, norm2_gain[None, :], wr.astype(BF16), br, tri)

    strict = jnp.asarray(np.tril(np.ones((N_EXPERTS, N_EXPERTS), np.float32), -1), BF16)
    pos, offs = pl.pallas_call(
        _slots_kernel,
        grid=(nsb,),
        in_specs=[pl.BlockSpec((8, tb), lambda s: (0, s)), pl.BlockSpec((1, N_EXPERTS, LANES), lambda s: (s, 0, 0)),
                  _full((N_EXPERTS, N_EXPERTS))],
        out_specs=[pl.BlockSpec((1, 2, tb), lambda s: (s, 0, 0)),
                   pl.BlockSpec((1, N_EXPERTS, LANES), lambda s: (s, 0, 0))],
        out_shape=[jax.ShapeDtypeStruct((nsb, 2, tb), jnp.int32),
                   jax.ShapeDtypeStruct((nsb, N_EXPERTS, LANES), jnp.int32)],
        compiler_params=pltpu.CompilerParams(dimension_semantics=("parallel",)),
        name="slots",
    )(route, cnt, strict)
    pos = pos.reshape(-1)
    flat_off = offs[:, :, 0].reshape(-1)
    counts = cnt[:, :, 0].astype(jnp.int32)
    wge = wge.reshape(N_EXPERTS, D, D_EXPERT)
    wue = wue.reshape(N_EXPERTS, D, D_EXPERT)
    wde = wde.reshape(N_EXPERTS, D_EXPERT, D)
    nct = tb // MOE_COMBINE_TILE
    nst = tb // MOE_SCATTER_TILE
    slots =(2 * tb + N_EXPERTS + MOE_CHUNK + 7) // 8 * 8
    hidx = lambda sb, j, off, cn: (sb * nst + jnp.minimum(j, nst - 1), 0)
    nes = N_EXPERTS // MOE_EXPERTS_PER_STEP
    eidx = lambda sb, j, off, cn: (jnp.clip(j - (nst + 1), 0, nes - 1), 0, 0)
    tidx = lambda sb, j, off, cn: (sb * nct + jnp.clip(j - (nst + 1 + nes) - 1, 0, nct - 1), 0)
    out = pl.pallas_call(
        functools.partial(_moe_kernel, tb),
        grid_spec=pltpu.PrefetchScalarGridSpec(
            num_scalar_prefetch=2,
            grid=(nsb, (nst + 1) + nes + (nct + 1)),
            in_specs=[pl.BlockSpec((MOE_SCATTER_TILE, D), hidx),
                      pl.BlockSpec((2 * tb,), lambda sb, j, off, cn: (sb,), memory_space=pltpu.SMEM),
                      pl.BlockSpec((MOE_EXPERTS_PER_STEP, D, D_EXPERT), eidx),
                      pl.BlockSpec((MOE_EXPERTS_PER_STEP, D, D_EXPERT), eidx),
                      pl.BlockSpec((MOE_EXPERTS_PER_STEP, D_EXPERT, D), eidx),
                      pl.BlockSpec((MOE_COMBINE_TILE, D), tidx),
                      pl.BlockSpec((8, MOE_COMBINE_TILE), lambda sb, j, off, cn: (0, tidx(sb, j, off, cn)[0]))],
            out_specs=pl.BlockSpec((MOE_COMBINE_TILE, D), tidx),
            scratch_shapes=[pltpu.VMEM((slots * PACK_ROWS, LANES), U32),
                            pltpu.VMEM((MOE_SCATTER_TILE * PACK_ROWS, LANES), U32),
                            pltpu.VMEM((MOE_SCATTER_TILE * PACK_ROWS, LANES), U32)]),
        out_shape=jax.ShapeDtypeStruct((T, D), F32),
        compiler_params=pltpu.CompilerParams(dimension_semantics=("arbitrary", "arbitrary"),
                                             vmem_limit_bytes=MOE_VMEM_LIMIT),
        name="moe",
    )(flat_off, counts.reshape(-1), h2, pos, wge, wue, wde, x1, route)
    return out


def kernel(x, positions, norm1_gain, w_in, gla_gk_fwd_w, gla_gk_fwd_b, gla_gk_bwd_w, gla_gk_bwd_b, gla_out_gain, mla_q_gain, mla_w_qb, mla_kv_gain, mla_w_kvb, q_norm_gain, k_norm_gain, w_out, norm2_gain, w_router_group, b_router_group, w_router_expert, b_router_expert, w_expert_gate, w_expert_up, w_expert_down):
    batch, seq, d = x.shape
    x2 = x.reshape(batch * seq, d)
    pos2 = positions.reshape(batch * seq, 1)
    for l in range(norm1_gain.shape[0]):
        x2 = _layer(x2, pos2, batch, seq, norm1_gain[l], w_in[l], gla_gk_fwd_w[l], gla_gk_fwd_b[l],
                    gla_gk_bwd_w[l], gla_gk_bwd_b[l], gla_out_gain[l], mla_q_gain[l], mla_w_qb[l],
                    mla_kv_gain[l], mla_w_kvb[l], q_norm_gain[l], k_norm_gain[l], w_out[l], norm2_gain[l],
                    w_router_group[l], b_router_group[l], w_router_expert[l], b_router_expert[l],
                    w_expert_gate[l], w_expert_up[l], w_expert_down[l])
    return x2.reshape(batch, seq, d)
```

```python
import functools

import numpy as np
import jax
import jax.numpy as jnp
from jax import lax
from jax.experimental import pallas as pl
from jax.experimental.pallas import tpu as pltpu

F32 = jnp.float32
BF16 = jnp.bfloat16

D_MODEL = 1024
GLA_HEADS = 4
GLA_DK = 64
GLA_DV = 128
GLA_GATE_RANK = 16
GLA_GATE_NORMALIZER = 16.0
GLA_CHUNK = 64
GLA_CHUNK_UNROLL = 16
MLA_HEADS = 8
MLA_NOPE = 64
MLA_ROPE = 32
MLA_QK = MLA_NOPE + MLA_ROPE
MLA_V = 64
MLA_Q_RANK = 256
MLA_KV_RANK = 128
ROPE_BASE = 10000.0
GLA_QK_WIDTH = GLA_HEADS * GLA_DK
GLA_WIDTH = GLA_HEADS * GLA_DV
MLA_WIDTH = MLA_HEADS * MLA_V
N_GROUPS = 4
EXPERTS_PER_GROUP = 8
N_EXPERTS = N_GROUPS * EXPERTS_PER_GROUP
D_EXPERT = 256
EPS = 1e-6

LANES = 128
HEAD_TILE = LANES
ROPE_LO = MLA_NOPE
ROPE_HALF = MLA_ROPE // 2
LOG2_E = 1.4426950408889634
ATTN_SCORES_AHEAD = 3
ONES_ROWS = 16

COL_GQ = 0
COL_GK = COL_GQ + GLA_QK_WIDTH
COL_GV = COL_GK + GLA_QK_WIDTH
COL_GG = COL_GV + GLA_WIDTH
COL_GLA_END = COL_GG + GLA_WIDTH
TAIL_MQ = 0
TAIL_MKV = TAIL_MQ + MLA_Q_RANK
TAIL_MISC = TAIL_MKV + MLA_KV_RANK
TAIL_WIDTH = TAIL_MISC + LANES

ROUTER_ROWS = LANES
ROUTER_EXPERT_ROW = 8

U32 = jnp.uint32
HI_HALF_MASK = np.uint32(0xFFFF0000)
PACK_ROWS = D_MODEL // (2 * LANES)
MOE_SUPER_BLOCK = 8192
MOE_CHUNK = 576
MOE_EXPERTS_PER_STEP = 2
MOE_SCATTER_TILE = 1024
MOE_COMBINE_TILE = 512
PROJ_TILE = 1024
X_RING_DEPTH = 3
OUTPROJ_TILE = 1024
assert 2 * MOE_COMBINE_TILE == MOE_SCATTER_TILE

VMEM_LIMIT = 56 * 1024 * 1024
MOE_VMEM_LIMIT = 59900 * 1024


def _dot(a, b):
    return jnp.dot(a, b, preferred_element_type=F32)


def _dot_nt(a, b):
    return lax.dot_general(a, b, (((1,), (1,)), ((), ())), preferred_element_type=F32)


def _dot_tn(a, b):
    return lax.dot_general(a, b, (((0,), (0,)), ((), ())), preferred_element_type=F32)


def _split_bf16(x):
    hi = x.astype(BF16)
    lo = (x - hi.astype(F32)).astype(BF16)
    return hi, lo


def _proj_kernel(x_ref, pos_ref, n1_ref, win_ref, wtail_ref, wg_ref, bg_ref, qgain_ref, wqb_ref, wqbr_ref, kvgain_ref,
                 wkvk_ref, wkvv_ref, qng_ref, qngr_ref, kng_ref, kngr_ref, freq_ref, place_ref, ones_ref,
                 gq_ref, gk_ref, gv_ref, gg_ref, lg_ref, mq_ref, mk_ref, mv_ref):
    x = x_ref[...]
    h = x * lax.rsqrt(jnp.mean(x * x, axis=-1, keepdims=True) + EPS) * n1_ref[...]
    hb = h.astype(BF16)
    proj = _dot(hb, win_ref[...])
    tail = _dot(hb, wtail_ref[...])

    gq_ref[...] = (proj[:, COL_GQ:COL_GK] * GLA_DK ** -0.5).astype(BF16)
    gk_ref[...] = proj[:, COL_GK:COL_GV].astype(BF16)
    gv_ref[...] = proj[:, COL_GV:COL_GG].astype(BF16)
    gg_ref[...] = proj[:, COL_GG:COL_GLA_END].astype(BF16)

    misc = tail[:, TAIL_MISC:TAIL_WIDTH]
    z = _dot(misc.astype(BF16), wg_ref[...]) + bg_ref[...]
    log_sig = -(jnp.maximum(-z, 0.0) + jnp.log(1.0 + jnp.exp(-jnp.abs(z))))
    lg_ref[...] = log_sig / GLA_GATE_NORMALIZER

    lane = lax.broadcasted_iota(jnp.int32, (1, LANES), 1)
    ang_t = freq_ref[...] * pos_ref[0].astype(F32)
    cos_hi, cos_lo = _split_bf16(jnp.cos(ang_t))
    sin_hi, sin_lo = _split_bf16(jnp.sin(ang_t))
    tabs = _dot_tn(jnp.concatenate([cos_hi, cos_lo, sin_hi, sin_lo], axis=0), place_ref[...])
    in_rope = (lane >= ROPE_LO) & (lane < ROPE_LO + MLA_ROPE)
    c_tab = jnp.where(lane < ROPE_LO, 1.0, tabs[:, 0:LANES])
    s_tab = tabs[:, LANES:2 * LANES]
    ones_bd = ones_ref[...]

    def norm_rope_pair(t2, r2, gc, gs, on_mxu):
        if on_mxu:
            ss = _dot((t2 * t2).astype(BF16), ones_bd) * (1.0 / MLA_QK)
        outs = []
        for i in range(2):
            sl = slice(i * LANES, (i + 1) * LANES)
            t = t2[:, sl]
            ssi = ss[:, sl] if on_mxu else jnp.sum(t * t, axis=-1, keepdims=True) * (1.0 / MLA_QK)
            outs.append(lax.rsqrt(ssi + EPS) * (t * gc + r2[:, sl] * gs))
        return outs

    qa = tail[:, TAIL_MQ:TAIL_MKV]
    qn = qa * lax.rsqrt(jnp.mean(qa * qa, axis=-1, keepdims=True) + EPS) * qgain_ref[...]
    qn = qn.astype(BF16)
    mq = _dot(qn, wqb_ref[...])
    mq_rot = _dot(qn, wqbr_ref[...])
    kva = tail[:, TAIL_MKV:TAIL_MISC]
    kvn = (kva * lax.rsqrt(jnp.mean(kva * kva, axis=-1, keepdims=True) + EPS) * kvgain_ref[...]).astype(BF16)
    kn = _dot(kvn, wkvk_ref[...])
    mv_ref[0] = _dot_nt(wkvv_ref[...], kvn).astype(BF16)
    rope_tile = jnp.where(in_rope, misc, 0.0)
    up = (lane >= ROPE_LO + ROPE_HALF) & (lane < ROPE_LO + MLA_ROPE)
    dn = (lane >= ROPE_LO) & (lane < ROPE_LO + ROPE_HALF)
    rope_rot = (jnp.where(up, pltpu.roll(rope_tile, ROPE_HALF, 1), 0.0)
                - jnp.where(dn, pltpu.roll(rope_tile, LANES - ROPE_HALF, 1), 0.0))
    rope2 = jnp.concatenate([rope_tile, rope_tile], axis=1)
    rope_rot2 = jnp.concatenate([rope_rot, rope_rot], axis=1)
    scale = MLA_QK ** -0.5 * LOG2_E
    q_gc, q_gs = (qng_ref[...] * scale) * c_tab, (qngr_ref[...] * scale) * s_tab
    k_gc, k_gs = kng_ref[...] * c_tab, kngr_ref[...] * s_tab
    for hp in range(MLA_HEADS // 2):
        sl = slice(hp * 2 * HEAD_TILE, (hp + 1) * 2 * HEAD_TILE)
        q2 = norm_rope_pair(mq[:, sl], mq_rot[:, sl], q_gc, q_gs, on_mxu=True)
        k2 = norm_rope_pair(kn[:, sl] + rope2, rope_rot2, k_gc, k_gs, on_mxu=False)
        for i in range(2):
            mq_ref[0, 2 * hp + i] = q2[i].astype(BF16)
            mk_ref[0, 2 * hp + i] = k2[i].astype(BF16)


def _gla_kernel(q_ref, k_ref, v_ref, lg_ref, gate_ref, gain_ref, o_ref, accf_ref, accb_ref, stf_ref, stb_ref):
    seq = q_ref.shape[0]
    C = GLA_CHUNK
    n_chunks = seq // C
    HK = GLA_QK_WIDTH

    lane_head = lax.broadcasted_iota(jnp.int32, (1, HK), 1) // GLA_DK
    rowi = lax.broadcasted_iota(jnp.int32, (C, 1), 0)
    srow = lax.broadcasted_iota(jnp.int32, (GLA_HEADS * C, C), 0) % C
    scol = lax.broadcasted_iota(jnp.int32, (GLA_HEADS * C, C), 1)
    gain = gain_ref[...]
    fwd_cfg = (True, C // 2 - 1, C - 1, scol <= srow, 0)
    bwd_cfg = (False, C // 2, 0, scol > srow, HK)

    def stack_heads(t):
        return jnp.concatenate([jnp.where(lane_head == hd, t, 0.0) for hd in range(GLA_HEADS)], axis=0)

    def chunk_step(c, st, cfg, acc_ref, other_acc_ref=None):
        prefix, ref_i, last_i, keep, lg_off = cfg
        rows = pl.ds(pl.multiple_of(c * C, C), C)
        b = lg_ref[rows, lg_off:lg_off + HK]
        shift = 1
        while shift < C:
            if prefix:
                b = b + jnp.where(rowi >= shift, pltpu.roll(b, shift, 0), 0.0)
            else:
                b = b + jnp.where(rowi < C - shift, pltpu.roll(b, C - shift, 0), 0.0)
            shift *= 2
        b_ref = b[ref_i:ref_i + 1, :]
        b_last = b[last_i:last_i + 1, :]
        q = q_ref[rows, :].astype(F32)
        k = k_ref[rows, :].astype(F32)
        v = v_ref[rows, :]
        q_rel = q * jnp.exp(b - b_ref)
        k_rel = (k * jnp.exp(b_ref - b)).astype(BF16)
        k_dec = (k * jnp.exp(b_last - b)).astype(BF16)
        q_dec = q * jnp.exp(b)
        sc = _dot_nt(stack_heads(q_rel).astype(BF16), k_rel)
        sc = jnp.where(keep, sc, 0.0).astype(BF16)
        o_inter = _dot_nt(stack_heads(q_dec).astype(BF16), st.astype(BF16))
        kv_t = _dot_tn(v, k_dec)
        new_st = jnp.exp(b_last) * st
        for hd in range(GLA_HEADS):
            vs = slice(hd * GLA_DV, (hd + 1) * GLA_DV)
            rs = slice(hd * C, (hd + 1) * C)
            o_h = _dot(sc[rs, :], v[:, vs]) + o_inter[rs, :]
            if other_acc_ref is None:
                acc_ref[rows, vs] = o_h
            else:
                o = o_h + other_acc_ref[rows, vs]
                on = o * lax.rsqrt(jnp.mean(o * o, axis=-1, keepdims=True) + EPS) * gain
                gt = gate_ref[rows, vs].astype(F32)
                o_ref[rows, vs] = (on * (gt * jax.nn.sigmoid(gt))).astype(BF16)
            new_st = new_st + jnp.where(lane_head == hd, kv_t[vs, :], 0.0)
        return new_st

    stf_ref[...] = jnp.zeros_like(stf_ref)
    stb_ref[...] = jnp.zeros_like(stb_ref)

    unroll = min(GLA_CHUNK_UNROLL, n_chunks // 2)
    n_iter = n_chunks // unroll

    def make_body(finalize):
        def body(i, carry):
            st_f = stf_ref[...]
            st_b = stb_ref[...]
            for u in range(unroll):
                cf = i * unroll + u
                st_f = chunk_step(cf, st_f, fwd_cfg, accf_ref, accb_ref if finalize else None)
                st_b = chunk_step(n_chunks - 1 - cf, st_b, bwd_cfg, accb_ref, accf_ref if finalize else None)
            stf_ref[...] = st_f
            stb_ref[...] = st_b
            return carry
        return body

    lax.fori_loop(0, n_iter // 2, make_body(False), 0)
    lax.fori_loop(n_iter // 2, n_iter, make_body(True), 0)


def _attn_kernel(q_ref, k_ref, vt_ref, wg_ref, wu_ref, wd_ref, o_ref, wgb_ref, wub_ref, wdb_ref):
    wgb_ref[...] = wg_ref[...].astype(BF16)
    wub_ref[...] = wu_ref[...].astype(BF16)
    wdb_ref[...] = wd_ref[...].astype(BF16)
    seq = k_ref.shape[2]
    ones = jnp.ones((ONES_ROWS, seq), BF16)
    outs = []
    scores = [_dot_nt(q_ref[0, h], k_ref[0, h]) for h in range(ATTN_SCORES_AHEAD)]
    for hd in range(MLA_HEADS):
        if hd + ATTN_SCORES_AHEAD < MLA_HEADS:
            scores.append(_dot_nt(q_ref[0, hd + ATTN_SCORES_AHEAD], k_ref[0, hd + ATTN_SCORES_AHEAD]))
        s = scores[hd]
        p = jnp.exp2(s - jnp.max(s, axis=-1, keepdims=True)).astype(BF16)
        vt_aug = jnp.concatenate([vt_ref[0, hd * MLA_V:(hd + 1) * MLA_V, :], ones], axis=0)
        ot = _dot_nt(vt_aug, p)
        outs.append(ot[0:MLA_V, :] / ot[MLA_V:MLA_V + 1, :])
    o_ref[0] = jnp.concatenate(outs, axis=0).astype(BF16)


def _unpack_bf16_pair(w):
    lo = pltpu.unpack_elementwise(w, index=0, packed_dtype=BF16, unpacked_dtype=F32)
    hi = pltpu.unpack_elementwise(w, index=1, packed_dtype=BF16, unpacked_dtype=F32)
    return lo, hi


def _pack_bf16_pair(lo, hi):
    return pltpu.pack_elementwise([lo, hi], packed_dtype=BF16)


def _outproj_kernel(tiles_per_sb, n_steps, x_hbm, gla_ref, mla_ref, wo_ref, n2_ref, wr_ref, br_ref, tri_ref,
                    x1_ref, h2_ref, route_ref, cnt_ref, carry_ref, wob_ref, xbuf_ref, xsem):
    step = pl.program_id(0)
    rows = xbuf_ref.shape[1]

    def x_copy(s, slot):
        return pltpu.make_async_copy(x_hbm.at[pl.ds(s * rows, rows), :], xbuf_ref.at[slot], xsem.at[slot])

    @pl.when(step == 0)
    def _():
        wob_ref[...] = wo_ref[...].astype(BF16)
        for s in range(min(X_RING_DEPTH - 1, n_steps)):
            x_copy(s, s).start()

    @pl.when(step + (X_RING_DEPTH - 1) < n_steps)
    def _():
        nxt = step + (X_RING_DEPTH - 1)
        x_copy(nxt, nxt % X_RING_DEPTH).start()

    slot = step % X_RING_DEPTH
    x_copy(step, slot).wait()

    mix = _dot(gla_ref[...], wob_ref[0:GLA_WIDTH, :]) + _dot_tn(mla_ref[0], wob_ref[GLA_WIDTH:, :])
    x1 = xbuf_ref[slot] + mix
    x1_ref[...] = x1
    h2 = x1 * lax.rsqrt(jnp.mean(x1 * x1, axis=-1, keepdims=True) + EPS) * n2_ref[...]
    h2b = h2.astype(BF16)
    h2_ref[...] = h2b

    logits = _dot_nt(wr_ref[...], h2b) + br_ref[...]
    tm = logits.shape[1]
    gl = logits[0:N_GROUPS, :]
    ge = jnp.exp(gl - jnp.max(gl, axis=0, keepdims=True))
    pg = ge / jnp.sum(ge, axis=0, keepdims=True)
    p_top = jnp.max(pg, axis=0, keepdims=True)
    gi = lax.broadcasted_iota(jnp.int32, (N_GROUPS, tm), 0)
    g_idx = jnp.min(jnp.where(pg == p_top, gi, N_GROUPS), axis=0, keepdims=True)
    sel = jnp.zeros((EXPERTS_PER_GROUP, tm), F32)
    for g in range(N_GROUPS):
        r0 = ROUTER_EXPERT_ROW + g * EXPERTS_PER_GROUP
        sel = sel + jnp.where(g_idx == g, logits[r0:r0 + EXPERTS_PER_GROUP, :], 0.0)
    se = jnp.exp(sel - jnp.max(sel, axis=0, keepdims=True))
    pe = se / jnp.sum(se, axis=0, keepdims=True)
    ei = lax.broadcasted_iota(jnp.int32, (EXPERTS_PER_GROUP, tm), 0)
    m1 = jnp.max(pe, axis=0, keepdims=True)
    i1 = jnp.min(jnp.where(pe == m1, ei, EXPERTS_PER_GROUP), axis=0, keepdims=True)
    pe2 = jnp.where(ei == i1, -1.0, pe)
    m2 = jnp.max(pe2, axis=0, keepdims=True)
    i2 = jnp.min(jnp.where(pe2 == m2, ei, EXPERTS_PER_GROUP), axis=0, keepdims=True)
    den = m1 + m2
    e1 = g_idx * EXPERTS_PER_GROUP + i1
    e2 = g_idx * EXPERTS_PER_GROUP + i2

    @pl.when(pl.program_id(0) % tiles_per_sb == 0)
    def _():
        carry_ref[...] = jnp.zeros_like(carry_ref)

    eall = lax.broadcasted_iota(jnp.int32, (N_EXPERTS, tm), 0)
    tri = tri_ref[...]
    carry = carry_ref[...]
    ranks = []
    for eid in (e1, e2):
        hot = eall == eid
        prefix = _dot(hot.astype(BF16), tri)
        ranks.append(jnp.sum(jnp.where(hot, carry + prefix - 1.0, 0.0), axis=0, keepdims=True))
        carry = carry + prefix[:, tm - 1:tm]
    carry_ref[...] = carry
    cnt_ref[0] = jnp.broadcast_to(carry, (N_EXPERTS, LANES))
    zeros = jnp.zeros((2, tm), F32)
    route_ref[...] = jnp.concatenate([e1.astype(F32), e2.astype(F32), p_top * (m1 / den), p_top * (m2 / den),
                                      ranks[0], ranks[1], zeros], axis=0)


def _slots_kernel(route_ref, cnt_ref, strict_ref, pos_ref, off_ref):
    counts = cnt_ref[0]
    padded = 2.0 * jnp.floor(counts * 0.5 + 0.5)
    hi = jnp.floor(padded * (1.0 / 256.0))
    lo = padded - 256.0 * hi
    strict = strict_ref[...]
    off = 256.0 * _dot(strict, hi.astype(BF16)) + _dot(strict, lo.astype(BF16))
    off_ref[0] = off.astype(jnp.int32)
    tb = route_ref.shape[1]
    eall = lax.broadcasted_iota(jnp.int32, (N_EXPERTS, tb), 0)
    off_col = off[:, 0:1]
    for s in range(2):
        hot = eall == route_ref[s:s + 1, :].astype(jnp.int32)
        base = jnp.sum(jnp.where(hot, off_col, 0.0), axis=0, keepdims=True)
        pos_ref[0, s:s + 1, :] = ((base + route_ref[4 + s:5 + s, :]) * PACK_ROWS).astype(jnp.int32)


def _moe_kernel(tb, off_ref, cnt_ref, h2_ref, pos_ref, wg_ref, wu_ref, wd_ref, x1_ref, wt_ref, o_ref,
                xy_ref, cba_ref, cbb_ref):
    sb = pl.program_id(0)
    j = pl.program_id(1)
    n_scatter = tb // MOE_SCATTER_TILE
    n_combine = tb // MOE_COMBINE_TILE
    first_expert_step = n_scatter + 1
    first_combine_step = first_expert_step + N_EXPERTS // MOE_EXPERTS_PER_STEP

    def pack(dst_ref):
        h = h2_ref[...].astype(F32)
        for r in range(PACK_ROWS):
            c0 = r * 2 * LANES
            dst_ref[pl.ds(r, MOE_SCATTER_TILE, stride=PACK_ROWS), :] = _pack_bf16_pair(
                h[:, c0:c0 + LANES], h[:, c0 + LANES:c0 + 2 * LANES])

    def scatter(src_ref, tile):
        t0 = tile * MOE_SCATTER_TILE
        for pair in range(MOE_SCATTER_TILE // 2):
            v = src_ref[pair * 2 * PACK_ROWS:(pair + 1) * 2 * PACK_ROWS, :]
            for half in range(2):
                piece = v[half * PACK_ROWS:(half + 1) * PACK_ROWS, :]
                for slot in range(2):
                    p = pos_ref[slot * tb + t0 + pair * 2 + half]
                    xy_ref[pl.ds(pl.multiple_of(p, PACK_ROWS), PACK_ROWS), :] = piece

    def gather(dst_ref, tile):
        t0 = tile * MOE_COMBINE_TILE
        for t in range(MOE_COMBINE_TILE):
            for slot in range(2):
                p = pos_ref[slot * tb + t0 + t]
                r0 = (slot * MOE_COMBINE_TILE + t) * PACK_ROWS
                dst_ref[r0:r0 + PACK_ROWS, :] = xy_ref[pl.ds(pl.multiple_of(p, PACK_ROWS), PACK_ROWS), :]

    def weighted_sum(src_ref):
        wt = wt_ref[...].T
        w1 = wt[:, 2:3]
        w2 = wt[:, 3:4]
        second = MOE_COMBINE_TILE * PACK_ROWS
        for r in range(PACK_ROWS):
            lo1, hi1 = _unpack_bf16_pair(src_ref[pl.ds(r, MOE_COMBINE_TILE, stride=PACK_ROWS), :])
            lo2, hi2 = _unpack_bf16_pair(src_ref[pl.ds(second + r, MOE_COMBINE_TILE, stride=PACK_ROWS), :])
            c0 = r * 2 * LANES
            o_ref[:, c0:c0 + LANES] = x1_ref[:, c0:c0 + LANES] + w1 * lo1 + w2 * lo2
            o_ref[:, c0 + LANES:c0 + 2 * LANES] = x1_ref[:, c0 + LANES:c0 + 2 * LANES] + w1 * hi1 + w2 * hi2

    bufs = (cba_ref, cbb_ref)

    @pl.when(j == 0)
    def _first_pack():
        zero_slot = jnp.zeros((PACK_ROWS, LANES), U32)
        for e in range(N_EXPERTS):
            p = (off_ref[sb * N_EXPERTS + e] + cnt_ref[sb * N_EXPERTS + e]) * PACK_ROWS
            xy_ref[pl.ds(pl.multiple_of(p, PACK_ROWS), PACK_ROWS), :] = zero_slot
        tail0 = 2 * tb * PACK_ROWS
        xy_ref[tail0:, :] = jnp.zeros((xy_ref.shape[0] - tail0, LANES), U32)
        pack(bufs[0])

    for parity in range(2):
        @pl.when((j >= 1) & (j < n_scatter) & (j % 2 == parity))
        def _scatter_and_pack():
            pack(bufs[parity])
            scatter(bufs[1 - parity], j - 1)

    @pl.when(j == n_scatter)
    def _last_scatter():
        scatter(bufs[(n_scatter - 1) % 2], n_scatter - 1)

    def run_expert(k, e):
        n = cnt_ref[e]
        base = off_ref[e]

        def body(i, c):
            r0 = base + i * MOE_CHUNK
            row0 = pl.multiple_of(r0 * PACK_ROWS, 2 * PACK_ROWS)
            halves = [_unpack_bf16_pair(xy_ref[pl.ds(row0 + r, MOE_CHUNK, stride=PACK_ROWS), :])
                      for r in range(PACK_ROWS)]
            a = jnp.zeros((MOE_CHUNK, D_EXPERT), F32)
            u = jnp.zeros((MOE_CHUNK, D_EXPERT), F32)
            for r in range(PACK_ROWS):
                lo, hi = halves[r]
                xb = jnp.concatenate([lo, hi], axis=1).astype(BF16)
                a = a + _dot(xb, wg_ref[k, r * 2 * LANES:(r + 1) * 2 * LANES, :])
                u = u + _dot(xb, wu_ref[k, r * 2 * LANES:(r + 1) * 2 * LANES, :])
            hid = ((a * jax.nn.sigmoid(a)) * u).astype(BF16)
            y = _dot(hid, wd_ref[k])
            valid = (r0 + lax.broadcasted_iota(jnp.int32, (MOE_CHUNK, 1), 0)) < (base + n)
            for r in range(PACK_ROWS):
                c0 = r * 2 * LANES
                lo, hi = halves[r]
                xy_ref[pl.ds(row0 + r, MOE_CHUNK, stride=PACK_ROWS), :] = _pack_bf16_pair(
                    jnp.where(valid, y[:, c0:c0 + LANES], lo), jnp.where(valid, y[:, c0 + LANES:c0 + 2 * LANES], hi))
            return c

        lax.fori_loop(0, (n + MOE_CHUNK - 1) // MOE_CHUNK, body, 0)

    @pl.when((j >= first_expert_step) & (j < first_combine_step))
    def _experts():
        for k in range(MOE_EXPERTS_PER_STEP):
            run_expert(k, sb * N_EXPERTS + (j - first_expert_step) * MOE_EXPERTS_PER_STEP + k)

    cstep = j - first_combine_step

    @pl.when(cstep == 0)
    def _first_gather():
        gather(bufs[0], 0)

    for parity in range(2):
        @pl.when((cstep >= 1) & (cstep < n_combine) & (cstep % 2 == parity))
        def _gather_and_sum():
            gather(bufs[parity], cstep)
            weighted_sum(bufs[1 - parity])

    @pl.when(cstep == n_combine)
    def _last_sum():
        weighted_sum(bufs[(n_combine - 1) % 2])


def _full(shape):
    return pl.BlockSpec(shape, lambda *_: (0,) * len(shape))


def _prep_weights(w_in, gk_fwd_w, gk_fwd_b, gk_bwd_w, gk_bwd_b, mla_w_qb, mla_w_kvb, q_norm_gain, k_norm_gain):
    splits = np.cumsum([0, GLA_QK_WIDTH, GLA_QK_WIDTH, GLA_WIDTH, GLA_WIDTH, GLA_GATE_RANK, GLA_GATE_RANK,
                        MLA_Q_RANK, MLA_KV_RANK, MLA_ROPE])
    d = w_in.shape[0]
    win = w_in[:, :splits[4]].astype(BF16)
    wt = w_in[:, splits[4]:]
    rel = splits - splits[4]
    gap = jnp.zeros((d, ROPE_LO - 2 * GLA_GATE_RANK), F32)
    end = jnp.zeros((d, LANES - ROPE_LO - MLA_ROPE), F32)
    wtail = jnp.concatenate([wt[:, rel[6]:rel[8]], wt[:, rel[4]:rel[6]], gap, wt[:, rel[8]:rel[9]], end],
                            axis=1).astype(BF16)

    zq = jnp.zeros((GLA_GATE_RANK, GLA_QK_WIDTH), F32)
    wg = jnp.concatenate([jnp.concatenate([gk_fwd_w, zq], axis=1), jnp.concatenate([zq, gk_bwd_w], axis=1),
                          jnp.zeros((LANES - 2 * GLA_GATE_RANK, 2 * GLA_QK_WIDTH), F32)], axis=0)
    bg = jnp.concatenate([gk_fwd_b, gk_bwd_b])[None, :]

    pad = HEAD_TILE - MLA_QK
    wqb = jnp.pad(mla_w_qb.reshape(MLA_Q_RANK, MLA_HEADS, MLA_QK), ((0, 0), (0, 0), (0, pad)))
    wqb = wqb.reshape(MLA_Q_RANK, MLA_HEADS * HEAD_TILE)
    wkv = mla_w_kvb.reshape(MLA_KV_RANK, MLA_HEADS, MLA_NOPE + MLA_V)
    wkvk = jnp.pad(wkv[:, :, :MLA_NOPE], ((0, 0), (0, 0), (0, HEAD_TILE - MLA_NOPE)))
    wkvk = wkvk.reshape(MLA_KV_RANK, MLA_HEADS * HEAD_TILE).astype(BF16)
    wkvv = wkv[:, :, MLA_NOPE:].reshape(MLA_KV_RANK, MLA_WIDTH).T.astype(BF16)
    qng = jnp.pad(q_norm_gain, (0, pad))[None, :]
    kng = jnp.pad(k_norm_gain, (0, pad))[None, :]
    def partner(w, sign):
        first, second = w[..., ROPE_LO:ROPE_LO + ROPE_HALF], w[..., ROPE_LO + ROPE_HALF:ROPE_LO + MLA_ROPE]
        return jnp.concatenate([jnp.zeros_like(w[..., :ROPE_LO]), sign * second, first,
                                jnp.zeros_like(w[..., ROPE_LO + MLA_ROPE:])], axis=-1)

    wqb_rot = partner(wqb.reshape(MLA_Q_RANK, MLA_HEADS, HEAD_TILE), -1.0).reshape(MLA_Q_RANK, -1).astype(BF16)
    qng_rot = partner(qng, 1.0)
    kng_rot = partner(kng, 1.0)
    return win, wtail, wg.astype(BF16), bg, wqb.astype(BF16), wqb_rot, wkvk, wkvv, qng, qng_rot, kng, kng_rot


def _rope_consts():
    inv = ROPE_BASE ** (-np.arange(0, MLA_ROPE, 2, dtype=np.float32) / MLA_ROPE)
    one = np.zeros((ROPE_HALF, LANES), np.float32)
    one[np.arange(ROPE_HALF), ROPE_LO + np.arange(ROPE_HALF)] = 1.0
    one[np.arange(ROPE_HALF), ROPE_LO + ROPE_HALF + np.arange(ROPE_HALF)] = 1.0
    zero = np.zeros_like(one)
    place = np.block([[one, zero], [one, zero], [zero, one], [zero, one]])
    ones_bd = np.kron(np.eye(2, dtype=np.float32), np.ones((LANES, LANES), np.float32))
    return jnp.asarray(inv[:, None]), jnp.asarray(place, BF16), jnp.asarray(ones_bd, BF16)


def _layer(x2, pos2, batch, seq, norm1_gain, w_in, gk_fwd_w, gk_fwd_b, gk_bwd_w, gk_bwd_b, gla_out_gain,
           mla_q_gain, mla_w_qb, mla_kv_gain, mla_w_kvb, q_norm_gain, k_norm_gain, w_out, norm2_gain,
           w_router_group, b_router_group, w_router_expert, b_router_expert,
           w_expert_gate, w_expert_up, w_expert_down):
    T, D = x2.shape
    win, wtail, wg, bg, wqb, wqb_rot, wkvk, wkvv, qng, qng_rot, kng, kng_rot = _prep_weights(
        w_in, gk_fwd_w, gk_fwd_b, gk_bwd_w, gk_bwd_b, mla_w_qb, mla_w_kvb, q_norm_gain, k_norm_gain)

    TM = min(PROJ_TILE, seq)
    spb = seq // TM
    tok = lambda w: pl.BlockSpec((TM, w), lambda i: (i, 0))
    head_spec = pl.BlockSpec((1, MLA_HEADS, TM, HEAD_TILE), lambda i: (i // spb, 0, i % spb, 0))
    gq, gk, gv, gg, lg, mq, mk, mv = pl.pallas_call(
        _proj_kernel,
        grid=(T // TM,),
        in_specs=[tok(D), pl.BlockSpec((1, 1, TM), lambda i: (i, 0, 0)), _full((1, D)), _full((D, COL_GLA_END)),
                  _full((D, TAIL_WIDTH)),
                  _full((LANES, 2 * GLA_QK_WIDTH)), _full((1, 2 * GLA_QK_WIDTH)), _full((1, MLA_Q_RANK)),
                  _full((MLA_Q_RANK, MLA_HEADS * HEAD_TILE)), _full((MLA_Q_RANK, MLA_HEADS * HEAD_TILE)),
                  _full((1, MLA_KV_RANK)), _full((MLA_KV_RANK, MLA_HEADS * HEAD_TILE)),
                  _full((MLA_WIDTH, MLA_KV_RANK)), _full((1, HEAD_TILE)), _full((1, HEAD_TILE)),
                  _full((1, HEAD_TILE)), _full((1, HEAD_TILE)),
                  _full((ROPE_HALF, 1)), _full((4 * ROPE_HALF, 2 * LANES)), _full((2 * LANES, 2 * LANES))],
        out_specs=[tok(GLA_QK_WIDTH), tok(GLA_QK_WIDTH), tok(GLA_WIDTH), tok(GLA_WIDTH), tok(2 * GLA_QK_WIDTH),
                   head_spec, head_spec, pl.BlockSpec((1, MLA_WIDTH, TM), lambda i: (i // spb, 0, i % spb))],
        out_shape=[jax.ShapeDtypeStruct((T, GLA_QK_WIDTH), BF16), jax.ShapeDtypeStruct((T, GLA_QK_WIDTH), BF16),
                   jax.ShapeDtypeStruct((T, GLA_WIDTH), BF16), jax.ShapeDtypeStruct((T, GLA_WIDTH), BF16),
                   jax.ShapeDtypeStruct((T, 2 * GLA_QK_WIDTH), F32),
                   jax.ShapeDtypeStruct((batch, MLA_HEADS, seq, HEAD_TILE), BF16),
                   jax.ShapeDtypeStruct((batch, MLA_HEADS, seq, HEAD_TILE), BF16),
                   jax.ShapeDtypeStruct((batch, MLA_WIDTH, seq), BF16)],
        compiler_params=pltpu.CompilerParams(dimension_semantics=("parallel",), vmem_limit_bytes=VMEM_LIMIT),
        name="proj",
    )(x2, pos2.reshape(T // TM, 1, TM), norm1_gain[None, :], win, wtail, wg, bg, mla_q_gain[None, :], wqb, wqb_rot,
      mla_kv_gain[None, :], wkvk, wkvv, qng, qng_rot, kng, kng_rot, *_rope_consts())

    seqspec = lambda w: pl.BlockSpec((seq, w), lambda b: (b, 0))
    gla_out = pl.pallas_call(
        _gla_kernel,
        grid=(batch,),
        in_specs=[seqspec(GLA_QK_WIDTH), seqspec(GLA_QK_WIDTH), seqspec(GLA_WIDTH), seqspec(2 * GLA_QK_WIDTH),
                  seqspec(GLA_WIDTH), _full((1, GLA_DV))],
        out_specs=seqspec(GLA_WIDTH),
        out_shape=jax.ShapeDtypeStruct((T, GLA_WIDTH), BF16),
        scratch_shapes=[pltpu.VMEM((seq, GLA_WIDTH), F32), pltpu.VMEM((seq, GLA_WIDTH), F32),
                        pltpu.VMEM((GLA_DV, GLA_QK_WIDTH), F32), pltpu.VMEM((GLA_DV, GLA_QK_WIDTH), F32)],
        compiler_params=pltpu.CompilerParams(dimension_semantics=("parallel",), vmem_limit_bytes=VMEM_LIMIT),
        name="gla",
    )(gq, gk, gv, lg, gg, gla_out_gain[None, :])

    TQ = 256
    nq = seq // TQ
    n_steps = batch * nq
    up_rows = N_EXPERTS * D // n_steps
    down_rows = N_EXPERTS * D_EXPERT // n_steps
    wslice = lambda rows, cols: pl.BlockSpec((rows, cols), lambda b, i: (b * nq + i, 0))
    mla_out, wge, wue, wde = pl.pallas_call(
        _attn_kernel,
        grid=(batch, nq),
        in_specs=[pl.BlockSpec((1, MLA_HEADS, TQ, HEAD_TILE), lambda b, i: (b, 0, i, 0)),
                  pl.BlockSpec((1, MLA_HEADS, seq, HEAD_TILE), lambda b, i: (b, 0, 0, 0)),
                  pl.BlockSpec((1, MLA_WIDTH, seq), lambda b, i: (b, 0, 0)),
                  wslice(up_rows, D_EXPERT), wslice(up_rows, D_EXPERT), wslice(down_rows, D)],
        out_specs=[pl.BlockSpec((1, MLA_WIDTH, TQ), lambda b, i: (b, 0, i)),
                   wslice(up_rows, D_EXPERT), wslice(up_rows, D_EXPERT), wslice(down_rows, D)],
        out_shape=[jax.ShapeDtypeStruct((batch, MLA_WIDTH, seq), BF16),
                   jax.ShapeDtypeStruct((N_EXPERTS * D, D_EXPERT), BF16),
                   jax.ShapeDtypeStruct((N_EXPERTS * D, D_EXPERT), BF16),
                   jax.ShapeDtypeStruct((N_EXPERTS * D_EXPERT, D), BF16)],
        compiler_params=pltpu.CompilerParams(dimension_semantics=("parallel", "parallel"),
                                             vmem_limit_bytes=VMEM_LIMIT),
        name="attn",
    )(mq, mk, mv, w_expert_gate.reshape(N_EXPERTS * D, D_EXPERT), w_expert_up.reshape(N_EXPERTS * D, D_EXPERT),
      w_expert_down.reshape(N_EXPERTS * D_EXPERT, D))

    row_gap = ROUTER_EXPERT_ROW - N_GROUPS
    row_tail = ROUTER_ROWS - ROUTER_EXPERT_ROW - N_EXPERTS
    wr = jnp.concatenate([w_router_group, jnp.zeros((D, row_gap), F32), w_router_expert,
                          jnp.zeros((D, row_tail), F32)], axis=1).T
    br = jnp.concatenate([b_router_group, jnp.zeros((row_gap,), F32), b_router_expert,
                          jnp.zeros((row_tail,), F32)])[:, None]
    tb = min(MOE_SUPER_BLOCK, T)
    nsb = T // tb
    TO = min(OUTPROJ_TILE, seq)
    tiles_per_sb = tb // TO
    tok = lambda w: pl.BlockSpec((TO, w), lambda i: (i, 0))
    tri = jnp.asarray(np.triu(np.ones((TO, TO), np.float32)), BF16)
    x1, h2, route, cnt = pl.pallas_call(
        functools.partial(_outproj_kernel, tiles_per_sb, T // TO),
        grid=(T // TO,),
        in_specs=[pl.BlockSpec(memory_space=pl.ANY), tok(GLA_WIDTH),
                  pl.BlockSpec((1, MLA_WIDTH, TO), lambda i: (i // (seq // TO), 0, i % (seq // TO))),
                  _full((D, D)), _full((1, D)),
                  _full((ROUTER_ROWS, D)), _full((ROUTER_ROWS, 1)), _full((TO, TO))],
        out_specs=[tok(D), tok(D), pl.BlockSpec((8, TO), lambda i: (0, i)),
                   pl.BlockSpec((1, N_EXPERTS, LANES), lambda i: (i // tiles_per_sb, 0, 0))],
        out_shape=[jax.ShapeDtypeStruct((T, D), F32), jax.ShapeDtypeStruct((T, D), BF16),
                   jax.ShapeDtypeStruct((8, T), F32), jax.ShapeDtypeStruct((nsb, N_EXPERTS, LANES), F32)],
        scratch_shapes=[pltpu.VMEM((N_EXPERTS, 1), F32), pltpu.VMEM((D, D), BF16),
                        pltpu.VMEM((X_RING_DEPTH, TO, D), F32), pltpu.SemaphoreType.DMA((X_RING_DEPTH,))],
        compiler_params=pltpu.CompilerParams(dimension_semantics=("arbitrary",), vmem_limit_bytes=VMEM_LIMIT),
        name="outproj",
    )(x2, gla_out, mla_out, w_out, norm2_gain[None, :], wr.astype(BF16), br, tri)

    strict = jnp.asarray(np.tril(np.ones((N_EXPERTS, N_EXPERTS), np.float32), -1), BF16)
    pos, offs = pl.pallas_call(
        _slots_kernel,
        grid=(nsb,),
        in_specs=[pl.BlockSpec((8, tb), lambda s: (0, s)), pl.BlockSpec((1, N_EXPERTS, LANES), lambda s: (s, 0, 0)),
                  _full((N_EXPERTS, N_EXPERTS))],
        out_specs=[pl.BlockSpec((1, 2, tb), lambda s: (s, 0, 0)),
                   pl.BlockSpec((1, N_EXPERTS, LANES), lambda s: (s, 0, 0))],
        out_shape=[jax.ShapeDtypeStruct((nsb, 2, tb), jnp.int32),
                   jax.ShapeDtypeStruct((nsb, N_EXPERTS, LANES), jnp.int32)],
        compiler_params=pltpu.CompilerParams(dimension_semantics=("parallel",)),
        name="slots",
    )(route, cnt, strict)
    pos = pos.reshape(-1)
    flat_off = offs[:, :, 0].reshape(-1)
    counts = cnt[:, :, 0].astype(jnp.int32)
    wge = wge.reshape(N_EXPERTS, D, D_EXPERT)
    wue = wue.reshape(N_EXPERTS, D, D_EXPERT)
    wde = wde.reshape(N_EXPERTS, D_EXPERT, D)
    nct = tb // MOE_COMBINE_TILE
    nst = tb // MOE_SCATTER_TILE
    slots =(2 * tb + N_EXPERTS + MOE_CHUNK + 7) // 8 * 8
    hidx = lambda sb, j, off, cn: (sb * nst + jnp.minimum(j, nst - 1), 0)
    nes = N_EXPERTS // MOE_EXPERTS_PER_STEP
    eidx = lambda sb, j, off, cn: (jnp.clip(j - (nst + 1), 0, nes - 1), 0, 0)
    tidx = lambda sb, j, off, cn: (sb * nct + jnp.clip(j - (nst + 1 + nes) - 1, 0, nct - 1), 0)
    out = pl.pallas_call(
        functools.partial(_moe_kernel, tb),
        grid_spec=pltpu.PrefetchScalarGridSpec(
            num_scalar_prefetch=2,
            grid=(nsb, (nst + 1) + nes + (nct + 1)),
            in_specs=[pl.BlockSpec((MOE_SCATTER_TILE, D), hidx),
                      pl.BlockSpec((2 * tb,), lambda sb, j, off, cn: (sb,), memory_space=pltpu.SMEM),
                      pl.BlockSpec((MOE_EXPERTS_PER_STEP, D, D_EXPERT), eidx),
                      pl.BlockSpec((MOE_EXPERTS_PER_STEP, D, D_EXPERT), eidx),
                      pl.BlockSpec((MOE_EXPERTS_PER_STEP, D_EXPERT, D), eidx),
                      pl.BlockSpec((MOE_COMBINE_TILE, D), tidx),
                      pl.BlockSpec((8, MOE_COMBINE_TILE), lambda sb, j, off, cn: (0, tidx(sb, j, off, cn)[0]))],
            out_specs=pl.BlockSpec((MOE_COMBINE_TILE, D), tidx),
            scratch_shapes=[pltpu.VMEM((slots * PACK_ROWS, LANES), U32),
                            pltpu.VMEM((MOE_SCATTER_TILE * PACK_ROWS, LANES), U32),
                            pltpu.VMEM((MOE_SCATTER_TILE * PACK_ROWS, LANES), U32)]),
        out_shape=jax.ShapeDtypeStruct((T, D), F32),
        compiler_params=pltpu.CompilerParams(dimension_semantics=("arbitrary", "arbitrary"),
                                             vmem_limit_bytes=MOE_VMEM_LIMIT),
        name="moe",
    )(flat_off, counts.reshape(-1), h2, pos, wge, wue, wde, x1, route)
    return out


def kernel(x, positions, norm1_gain, w_in, gla_gk_fwd_w, gla_gk_fwd_b, gla_gk_bwd_w, gla_gk_bwd_b, gla_out_gain, mla_q_gain, mla_w_qb, mla_kv_gain, mla_w_kvb, q_norm_gain, k_norm_gain, w_out, norm2_gain, w_router_group, b_router_group, w_router_expert, b_router_expert, w_expert_gate, w_expert_up, w_expert_down):
    batch, seq, d = x.shape
    x2 = x.reshape(batch * seq, d)
    pos2 = positions.reshape(batch * seq, 1)
    for l in range(norm1_gain.shape[0]):
        x2 = _layer(x2, pos2, batch, seq, norm1_gain[l], w_in[l], gla_gk_fwd_w[l], gla_gk_fwd_b[l],
                    gla_gk_bwd_w[l], gla_gk_bwd_b[l], gla_out_gain[l], mla_q_gain[l], mla_w_qb[l],
                    mla_kv_gain[l], mla_w_kvb[l], q_norm_gain[l], k_norm_gain[l], w_out[l], norm2_gain[l],
                    w_router_group[l], b_router_group[l], w_router_expert[l], b_router_expert[l],
                    w_expert_gate[l], w_expert_up[l], w_expert_down[l])
    return x2.reshape(batch, seq, d)
```
